```python
import jax, jax.numpy as jnp
from jax import lax
import numpy as np

D_MODEL = 1024
BATCH = 32
SEQ = 256
DEPTH = 2
DEC_BATCH = 2
DEC_SEQ = 2048
PAST_LEN = 512

GRID_W = 64
HEAD_DIM = 64
GROUP_WIDTH = D_MODEL // 4
CONV_CH = GROUP_WIDTH
MLA_HEADS = GROUP_WIDTH // HEAD_DIM
MLA_NOPE = 64
MLA_ROPE = 32
MLA_V = GROUP_WIDTH // MLA_HEADS
MLA_Q_LORA = 192
MLA_KV_LORA = 128
GQA_HEADS = GROUP_WIDTH // HEAD_DIM
GQA_KV_HEADS = 2
SWA_HEADS = GROUP_WIDTH // HEAD_DIM
SWA_KV_HEADS = 2
WINDOW = 128
Q_BLOCK = 128
D_FF = 2816
ROPE_THETA = 10000.0
EPS = 1e-6
NEG_INF = -1e30
ATTN_SCALE = HEAD_DIM ** -0.5
MLA_SCALE = (MLA_NOPE + MLA_ROPE) ** -0.5
IN_SIZES = (3 * CONV_CH, MLA_Q_LORA, MLA_KV_LORA, MLA_ROPE, GQA_HEADS * HEAD_DIM,
            2 * GQA_KV_HEADS * HEAD_DIM, SWA_HEADS * HEAD_DIM, 2 * SWA_KV_HEADS * HEAD_DIM)
IN_COLS = sum(IN_SIZES)
IN_SPLITS = tuple(sum(IN_SIZES[:i + 1]) for i in range(len(IN_SIZES) - 1))

kernel_name = 'hybrid_parallel_group_diffusion_step'


def rmsnorm(x, g):
    xf = x.astype(jnp.float32)
    y = xf * lax.rsqrt(jnp.mean(xf * xf, axis=-1, keepdims=True) + EPS)
    return (y * g.astype(jnp.float32)).astype(x.dtype)


def conv3(x, w):
    xp = jnp.pad(x, ((0, 0), (1, 1), (0, 0)))
    return xp[:, :-2] * w[0] + xp[:, 1:-1] * w[1] + xp[:, 2:] * w[2]


def axial_tables(T, dim):
    rows = T // GRID_W
    row = jnp.repeat(jnp.arange(rows, dtype=jnp.float32), GRID_W)
    col = jnp.tile(jnp.arange(GRID_W, dtype=jnp.float32), rows)
    half = dim // 2
    inv = jnp.power(ROPE_THETA, -jnp.arange(0, half, 2, dtype=jnp.float32) / half)
    ar = row[:, None] * inv
    ac = col[:, None] * inv
    return (jnp.cos(ar)[:, None, :], jnp.sin(ar)[:, None, :], jnp.cos(ac)[:, None, :], jnp.sin(ac)[:, None, :])


def rope_half(x, c, s):
    x1, x2 = jnp.split(x, 2, axis=-1)
    return jnp.concatenate([x1 * c - x2 * s, x2 * c + x1 * s], axis=-1)


def axial_rope(x, tables):
    cr, sr, cc, sc = tables
    xr, xc = jnp.split(x.astype(jnp.float32), 2, axis=-1)
    return jnp.concatenate([rope_half(xr, cr, sr), rope_half(xc, cc, sc)], axis=-1).astype(x.dtype)


def group_heads(q, kvh):
    B, T, H, d = q.shape
    return q.reshape(B, T, kvh, H // kvh, d)


def blocked_attention(qs, ks, vs, scale, sink=None):
    B, T, KVH, G, _ = qs[0].shape
    nb = T // Q_BLOCK
    qb = tuple(jnp.moveaxis(q.reshape(B, nb, Q_BLOCK, *q.shape[2:]), 1, 0) for q in qs)
    sizes = [k.shape[1] for k in ks]

    def one(qblk):
        logits = [jnp.einsum('bqkgd,bskd->bkgqs', qi, ki).astype(jnp.float32) * scale for qi, ki in zip(qblk, ks)]
        if sink is not None:
            logits.append(jnp.broadcast_to(sink.reshape(KVH, G)[None, :, :, None, None].astype(jnp.float32),
                                           (B, KVH, G, Q_BLOCK, 1)))
        p = jax.nn.softmax(jnp.concatenate(logits, axis=-1), axis=-1)
        outs = []
        off = 0
        for vi, n in zip(vs, sizes):
            outs.append(jnp.einsum('bkgqs,bskd->bqkgd', p[..., off:off + n].astype(vi.dtype), vi))
            off += n
        return sum(outs[1:], outs[0])

    o = lax.map(one, qb)
    return jnp.moveaxis(o, 0, 1).reshape(B, T, -1)


def windowed_attention(q, k, v, q_ctx, k_ctx, v_ctx, sink, scale):
    B, T, KVH, G, Dh = q.shape
    nb = T // Q_BLOCK
    W3 = 3 * Q_BLOCK

    def band(a):
        ab = jnp.pad(a, ((0, 0), (Q_BLOCK, Q_BLOCK), (0, 0), (0, 0))).reshape(B, nb + 2, Q_BLOCK, KVH, a.shape[-1])
        return jnp.concatenate([ab[:, :-2], ab[:, 1:-1], ab[:, 2:]], axis=2)

    kw, vw = band(k), band(v)
    qb = q.reshape(B, nb, Q_BLOCK, KVH, G, Dh)
    qcb = q_ctx.reshape(B, nb, Q_BLOCK, KVH, G, Dh)
    qpos = jnp.arange(nb)[:, None, None] * Q_BLOCK + jnp.arange(Q_BLOCK)[None, :, None]
    kpos = jnp.arange(nb)[:, None, None] * Q_BLOCK - Q_BLOCK + jnp.arange(W3)[None, None, :]
    valid = (jnp.abs(kpos - qpos) <= WINDOW) & (kpos >= 0) & (kpos < T)
    s_loc = jnp.einsum('bnqkgd,bnskd->bnkgqs', qb, kw).astype(jnp.float32) * scale
    s_loc = jnp.where(valid[None, :, None, None], s_loc, NEG_INF)
    s_ctx = jnp.einsum('bnqkgd,bskd->bnkgqs', qcb, k_ctx).astype(jnp.float32) * scale
    s_sink = jnp.broadcast_to(sink.reshape(KVH, G)[None, None, :, :, None, None].astype(jnp.float32),
                              (B, nb, KVH, G, Q_BLOCK, 1))
    p = jax.nn.softmax(jnp.concatenate([s_loc, s_ctx, s_sink], axis=-1), axis=-1)
    L = k_ctx.shape[1]
    out = (jnp.einsum('bnkgqs,bnskd->bnqkgd', p[..., :W3].astype(vw.dtype), vw)
           + jnp.einsum('bnkgqs,bskd->bnqkgd', p[..., W3:W3 + L].astype(v_ctx.dtype), v_ctx))
    return out.reshape(B, T, KVH * G * Dh)


def mla_kv(ckv, kpe, w_kv_b):
    B, S, _ = ckv.shape
    kv = (ckv @ w_kv_b).reshape(B, S, MLA_HEADS, MLA_NOPE + MLA_V)
    k_nope, v = jnp.split(kv, [MLA_NOPE], axis=-1)
    k = jnp.concatenate([k_nope, jnp.broadcast_to(kpe, (B, S, MLA_HEADS, MLA_ROPE)).astype(k_nope.dtype)], axis=-1)
    return k, v


def layer(x, cond, lp, cache=None, ropes=None):
    B, T, _ = x.shape
    mod = (jax.nn.silu(cond) @ lp['w_ada'] + lp['b_ada'])[:, None, :]
    sh1, sc1, g1, sh2, sc2, g2 = jnp.split(mod, 6, axis=-1)
    h = rmsnorm(x, lp['norm1']) * (1 + sc1) + sh1
    proj = h @ lp['w_in']
    pa, pcq, pckv, pkpe, pq_c, pkv_c, pq_d, pkv_d = jnp.split(proj, IN_SPLITS, axis=-1)
    xa, gb, gc = jnp.split(pa, 3, axis=-1)
    y_a = gb * conv3(gc * xa, lp['conv_a'])
    cq = rmsnorm(pcq, lp['mla_q_norm'])
    q = (cq @ lp['mla_wq_b']).reshape(B, T, MLA_HEADS, MLA_NOPE + MLA_ROPE)
    q_nope, q_pe = jnp.split(q, [MLA_NOPE], axis=-1)
    ckv = rmsnorm(pckv, lp['mla_kv_norm'])
    kpe = pkpe[:, :, None, :]
    qc = rmsnorm(pq_c.reshape(B, T, GQA_HEADS, HEAD_DIM), lp['gqa_q_norm'])
    kc, vc = jnp.split(pkv_c.reshape(B, T, 2 * GQA_KV_HEADS, HEAD_DIM), 2, axis=2)
    kc = rmsnorm(kc, lp['gqa_k_norm'])
    qd = pq_d.reshape(B, T, SWA_HEADS, HEAD_DIM)
    kd, vd = jnp.split(pkv_d.reshape(B, T, 2 * SWA_KV_HEADS, HEAD_DIM), 2, axis=2)
    q_mla_raw = jnp.concatenate([q_nope, q_pe], axis=-1)[:, :, :, None, :]
    if cache is None:
        k_m, v_m = mla_kv(ckv, kpe, lp['mla_wkv_b'])
        y_b = blocked_attention((q_mla_raw,), (k_m,), (v_m,), MLA_SCALE)
        y_c = blocked_attention((group_heads(qc, GQA_KV_HEADS),), (kc,), (vc,), ATTN_SCALE)
        y_d = blocked_attention((group_heads(qd, SWA_KV_HEADS),), (kd,), (vd,), ATTN_SCALE, lp['swa_sink'])
        new = (ckv, pkpe, kc, vc, kd, vd)
    else:
        ckv_x, kpe_x, kc_x, vc_x, kd_x, vd_x = cache
        rope64, rope32 = ropes
        q_mla_rot = jnp.concatenate([q_nope, axial_rope(q_pe, rope32)], axis=-1)[:, :, :, None, :]
        k_l, v_l = mla_kv(ckv, axial_rope(kpe, rope32), lp['mla_wkv_b'])
        k_x, v_x = mla_kv(ckv_x, kpe_x[:, :, None, :], lp['mla_wkv_b'])
        y_b = blocked_attention((q_mla_rot, q_mla_raw), (k_l, k_x), (v_l, v_x), MLA_SCALE)
        y_c = blocked_attention((group_heads(axial_rope(qc, rope64), GQA_KV_HEADS), group_heads(qc, GQA_KV_HEADS)),
                                (axial_rope(kc, rope64), kc_x), (vc, vc_x), ATTN_SCALE)
        y_d = windowed_attention(group_heads(axial_rope(qd, rope64), SWA_KV_HEADS), axial_rope(kd, rope64), vd,
                                 group_heads(qd, SWA_KV_HEADS), kd_x, vd_x, lp['swa_sink'], ATTN_SCALE)
        new = None
    y = jnp.concatenate([y_a, y_b, y_c, y_d], axis=-1) @ lp['w_out']
    x = x + g1 * y
    h = rmsnorm(x, lp['norm2']) * (1 + sc2) + sh2
    u = conv3(h @ lp['w_up'], lp['conv_ff'])
    ua, ub = jnp.split(u, 2, axis=-1)
    x = x + g2 * ((jax.nn.silu(ua) * ub) @ lp['w_down'])
    return x, new


def setup_inputs(seed: int = 0) -> dict:
    key = jax.random.key(seed)
    ks = iter(jax.random.split(key, 32))
    nrm = lambda shape, s: jax.random.normal(next(ks), shape, jnp.float32) * s
    gain = lambda shape: 1.0 + nrm(shape, 0.05)
    L = DEPTH
    return {
        'x_prompt': nrm((BATCH, SEQ, D_MODEL), 1.0),
        'x_sample': nrm((DEC_BATCH, DEC_SEQ, D_MODEL), 1.0),
        'cache_mla_ckv': nrm((DEC_BATCH, L, PAST_LEN, MLA_KV_LORA), 1.0),
        'cache_mla_kpe': nrm((DEC_BATCH, L, PAST_LEN, MLA_ROPE), 1.0),
        'cache_gqa_k': nrm((DEC_BATCH, L, PAST_LEN, GQA_KV_HEADS, HEAD_DIM), 1.0),
        'cache_gqa_v': nrm((DEC_BATCH, L, PAST_LEN, GQA_KV_HEADS, HEAD_DIM), 1.0),
        'cache_swa_k': nrm((DEC_BATCH, L, PAST_LEN, SWA_KV_HEADS, HEAD_DIM), 1.0),
        'cache_swa_v': nrm((DEC_BATCH, L, PAST_LEN, SWA_KV_HEADS, HEAD_DIM), 1.0),
        'c': nrm((DEC_BATCH, D_MODEL), 1.0),
        'c_ctx': nrm((D_MODEL,), 1.0),
        'w_ada': nrm((L, D_MODEL, 6 * D_MODEL), D_MODEL ** -0.5),
        'b_ada': nrm((L, 6 * D_MODEL), 0.01),
        'norm1': gain((L, D_MODEL)),
        'w_in': nrm((L, D_MODEL, IN_COLS), D_MODEL ** -0.5),
        'conv_a': nrm((L, 3, CONV_CH), 0.5),
        'mla_q_norm': gain((L, MLA_Q_LORA)),
        'mla_wq_b': nrm((L, MLA_Q_LORA, MLA_HEADS * (MLA_NOPE + MLA_ROPE)), MLA_Q_LORA ** -0.5),
        'mla_kv_norm': gain((L, MLA_KV_LORA)),
        'mla_wkv_b': nrm((L, MLA_KV_LORA, MLA_HEADS * (MLA_NOPE + MLA_V)), MLA_KV_LORA ** -0.5),
        'gqa_q_norm': gain((L, HEAD_DIM)),
        'gqa_k_norm': gain((L, HEAD_DIM)),
        'swa_sink': nrm((L, SWA_HEADS), 0.5),
        'w_out': nrm((L, D_MODEL, D_MODEL), D_MODEL ** -0.5),
        'norm2': gain((L, D_MODEL)),
        'w_up': nrm((L, D_MODEL, 2 * D_FF), D_MODEL ** -0.5),
        'conv_ff': nrm((L, 3, 2 * D_FF), 0.5),
        'w_down': nrm((L, D_FF, D_MODEL), D_FF ** -0.5),
        'final_norm': gain((D_MODEL,)),
    }


def reference(x_prompt, x_sample, cache_mla_ckv, cache_mla_kpe, cache_gqa_k, cache_gqa_v, cache_swa_k, cache_swa_v,
              c, c_ctx, w_ada, b_ada, norm1, w_in, conv_a, mla_q_norm, mla_wq_b, mla_kv_norm, mla_wkv_b,
              gqa_q_norm, gqa_k_norm, swa_sink, w_out, norm2, w_up, conv_ff, w_down, final_norm):
    T_lat = x_sample.shape[1]
    ropes = (axial_tables(T_lat, HEAD_DIM), axial_tables(T_lat, MLA_ROPE))
    xp, xs = x_prompt, x_sample
    states = []
    for l in range(DEPTH):
        lp = {'w_ada': w_ada[l], 'b_ada': b_ada[l], 'norm1': norm1[l], 'w_in': w_in[l], 'conv_a': conv_a[l],
              'mla_q_norm': mla_q_norm[l], 'mla_wq_b': mla_wq_b[l], 'mla_kv_norm': mla_kv_norm[l],
              'mla_wkv_b': mla_wkv_b[l], 'gqa_q_norm': gqa_q_norm[l], 'gqa_k_norm': gqa_k_norm[l],
              'swa_sink': swa_sink[l], 'w_out': w_out[l], 'norm2': norm2[l], 'w_up': w_up[l],
              'conv_ff': conv_ff[l], 'w_down': w_down[l]}
        xp, st = layer(xp, c_ctx[None, :], lp)
        states.append(st)
        cache_l = (cache_mla_ckv[:, l], cache_mla_kpe[:, l], cache_gqa_k[:, l], cache_gqa_v[:, l],
                   cache_swa_k[:, l], cache_swa_v[:, l])
        xs, _ = layer(xs, c, lp, cache_l, ropes)
    new_mla_ckv, new_mla_kpe, new_gqa_k, new_gqa_v, new_swa_k, new_swa_v = [
        jnp.stack([s[i] for s in states], axis=1) for i in range(6)]
    y_prompt = rmsnorm(xp, final_norm)
    y_sample = rmsnorm(xs, final_norm)
    return (y_prompt, y_sample, new_mla_ckv, new_mla_kpe, new_gqa_k, new_gqa_v, new_swa_k, new_swa_v)
```

```python
import functools

import jax
import jax.numpy as jnp
from jax import lax
from jax.experimental import pallas as pl
from jax.experimental.pallas import tpu as pltpu

F32 = jnp.float32
BF16 = jnp.bfloat16

D_MODEL = 1024
DEPTH = 2
GRID_W = 64
HEAD_DIM = 64
GROUP_WIDTH = D_MODEL // 4
MLA_HEADS = 4
MLA_NOPE = 64
MLA_ROPE = 32
MLA_V = 64
MLA_Q_LORA = 192
MLA_KV_LORA = 128
WINDOW = 128
D_FF = 2816
ROPE_THETA = 10000.0
EPS = 1e-6
NEG_INF = -1e30
ATTN_SCALE = HEAD_DIM ** -0.5
MLA_SCALE = (MLA_NOPE + MLA_ROPE) ** -0.5

LANES = 128
BF16_ROWS = 16
ROW_TILE = 512
FF_TILE = 256
N_FF_TILES = D_FF // FF_TILE
IN_COLS_PACKED = 2304
VMEM_LIMIT = 56 * 1024 * 1024

_NT = (((1,), (1,)), ((), ()))


def _dot(a, b):
    return jnp.dot(a, b, preferred_element_type=F32)


def _dot_nt(a, b):
    return lax.dot_general(a, b, _NT, preferred_element_type=F32)


def _rms(x, g, n):
    ms = jnp.sum(x * x, axis=-1, keepdims=True) * (1.0 / n)
    return x * lax.rsqrt(ms + EPS) * g


def _lane(shape):
    return lax.broadcasted_iota(jnp.int32, shape, len(shape) - 1)


def _rope(x, c, s, half):
    w = x.shape[-1]
    lo = (_lane(x.shape) % (2 * half)) < half
    sw = jnp.where(lo, pltpu.roll(x, w - half, 1), pltpu.roll(x, half, 1))
    return x * c + sw * s


def _shift_rows(zz, halo, rows):
    n = zz.shape[0]
    prev = pltpu.roll(zz, 1, 0)[halo:halo + rows]
    nxt = pltpu.roll(zz, n - 1, 0)[halo:halo + rows]
    return prev, nxt


def _seq_pos(tile, rows, seq_len):
    r = lax.broadcasted_iota(jnp.int32, (rows, 1), 0)
    return (tile * rows + r) % seq_len


def _ada_body(c_ref, w_ref, b_ref, o_ref):
    c = c_ref[...]
    s = c * jax.nn.sigmoid(c)
    o_ref[...] = jnp.dot(s, w_ref[...], precision=lax.Precision.HIGHEST,
                         preferred_element_type=F32) + b_ref[...]


def _ada(cond8, w_ada, b_ada):
    tn = 1536
    n = 6 * D_MODEL
    return pl.pallas_call(
        _ada_body,
        grid=(DEPTH, n // tn),
        in_specs=[pl.BlockSpec((8, D_MODEL), lambda l, j: (0, 0)),
                  pl.BlockSpec((None, D_MODEL, tn), lambda l, j: (l, 0, j)),
                  pl.BlockSpec((None, 1, tn), lambda l, j: (l, 0, j))],
        out_specs=pl.BlockSpec((None, 8, tn), lambda l, j: (l, 0, j)),
        out_shape=jax.ShapeDtypeStruct((DEPTH, 8, n), F32),
        compiler_params=pltpu.CompilerParams(
            dimension_semantics=("arbitrary", "arbitrary"), vmem_limit_bytes=VMEM_LIMIT),
        name="ada",
    )(cond8, w_ada, b_ada.reshape(DEPTH, 1, n))


def _ctx_body(ckv_ref, kpe_ref, w_ref, kx_ref, vx_ref):
    kv = _dot(ckv_ref[...].astype(BF16), w_ref[...])
    kpe = kpe_ref[...]
    kx_ref[...] = (kv[:, :4 * LANES] + jnp.concatenate([kpe] * MLA_HEADS, axis=1)).astype(BF16)
    vx_ref[...] = kv[:, 4 * LANES:].astype(BF16)


def _ctx_mla(cache_ckv, cache_kpe_pad, wkv_p):
    b, _, s, _ = cache_ckv.shape
    return pl.pallas_call(
        _ctx_body,
        grid=(DEPTH, b),
        in_specs=[pl.BlockSpec((None, None, s, MLA_KV_LORA), lambda l, i: (i, l, 0, 0)),
                  pl.BlockSpec((None, None, s, LANES), lambda l, i: (i, l, 0, 0)),
                  pl.BlockSpec((None, MLA_KV_LORA, 6 * LANES), lambda l, i: (l, 0, 0))],
        out_specs=[pl.BlockSpec((None, None, s, 4 * LANES), lambda l, i: (l, i, 0, 0)),
                   pl.BlockSpec((None, None, s, 2 * LANES), lambda l, i: (l, i, 0, 0))],
        out_shape=[jax.ShapeDtypeStruct((DEPTH, b, s, 4 * LANES), BF16),
                   jax.ShapeDtypeStruct((DEPTH, b, s, 2 * LANES), BF16)],
        compiler_params=pltpu.CompilerParams(
            dimension_semantics=("arbitrary", "arbitrary"), vmem_limit_bytes=VMEM_LIMIT),
        name="ctx_mla",
    )(cache_ckv, cache_kpe_pad, wkv_p)


def _inproj_body(rope, x_ref, mod_ref, n1_ref, w_ref, wq_ref, wkv_ref, gq_ref, gkv_ref, gqc_ref, gkc_ref,
                 hm_ref, *rest):
    if rope:
        c64_ref, s64_ref, c32_ref, s32_ref = rest[:4]
        outs = rest[4:]
    else:
        outs = rest
    x = x_ref[...]
    sh1 = mod_ref[0:1, :]
    sc1 = mod_ref[1:2, :]
    h = _rms(x, n1_ref[...], D_MODEL) * (1.0 + sc1) + sh1
    acc = _dot(h.astype(BF16), w_ref[...])

    xa, gb, gc = acc[:, 0:256], acc[:, 256:512], acc[:, 512:768]
    z = gc * xa
    cq = acc[:, 768:1024]
    cqn = _rms(cq, gq_ref[...], MLA_Q_LORA)
    qm = _dot(cqn.astype(BF16), wq_ref[...]) * MLA_SCALE
    ckvn = _rms(acc[:, 1024:1152], gkv_ref[...], MLA_KV_LORA)
    kv = _dot(ckvn.astype(BF16), wkv_ref[...])
    kpe = acc[:, 1152:1280]
    vm = kv[:, 4 * LANES:]

    hm = hm_ref[...]

    def head_rms(v, g):
        v2 = v * v
        hi = v2.astype(BF16)
        lo = (v2 - hi.astype(F32)).astype(BF16)
        w = v.shape[-1]
        ss = _dot(hi, hm[:w, :w]) + _dot(lo, hm[:w, :w])
        return v * lax.rsqrt(ss * (1.0 / HEAD_DIM) + EPS) * g

    qc = head_rms(acc[:, 1280:1536], gqc_ref[...]) * ATTN_SCALE
    kc = head_rms(acc[:, 1536:1664], gkc_ref[...])
    vc = acc[:, 1664:1792]
    qd = acc[:, 1792:2048] * ATTN_SCALE
    kd = acc[:, 2048:2176]
    vd = acc[:, 2176:2304]

    if not rope:
        (z_o, gb_o, qm_o, km_o, vm_o, qc_o, kc_o, vc_o, qd_o, kd_o, vd_o,
         ckv_c, kpe_c, kc_c, vc_c, kd_c, vd_c) = outs
        km = kv[:, :4 * LANES] + jnp.concatenate([kpe] * MLA_HEADS, axis=1)
        z_o[...] = z.astype(BF16)
        gb_o[...] = gb.astype(BF16)
        qm_o[...] = qm.astype(BF16)
        km_o[...] = km.astype(BF16)
        vm_o[...] = vm.astype(BF16)
        qc_o[...] = qc.astype(BF16)
        kc_o[...] = kc.astype(BF16)
        vc_o[...] = vc.astype(BF16)
        qd_o[...] = qd.astype(BF16)
        kd_o[...] = kd.astype(BF16)
        vd_o[...] = vd.astype(BF16)
        ckv_c[...] = ckvn
        kpe_c[...] = kpe[:, :MLA_ROPE]
        kc_c[...] = kc
        vc_c[...] = vc
        kd_c[...] = kd
        vd_c[...] = vd
    else:
        (z_o, gb_o, qmr_o, qm_o, km_o, vm_o, qcr_o, qc_o, kc_o, vc_o, qdr_o, qd_o, kd_o, vd_o) = outs
        c64, s64, c32, s32 = c64_ref[...], s64_ref[...], c32_ref[...], s32_ref[...]
        rope64 = lambda v: jnp.concatenate(
            [_rope(v[:, i:i + LANES], c64, s64, 16) for i in range(0, v.shape[-1], LANES)], axis=1)
        rope32 = lambda v: jnp.concatenate(
            [_rope(v[:, i:i + LANES], c32, s32, 8) for i in range(0, v.shape[-1], LANES)], axis=1)
        km = kv[:, :4 * LANES] + jnp.concatenate([_rope(kpe, c32, s32, 8)] * MLA_HEADS, axis=1)
        z_o[...] = z.astype(BF16)
        gb_o[...] = gb.astype(BF16)
        qmr_o[...] = rope32(qm).astype(BF16)
        qm_o[...] = qm.astype(BF16)
        km_o[...] = km.astype(BF16)
        vm_o[...] = vm.astype(BF16)
        qcr_o[...] = rope64(qc).astype(BF16)
        qc_o[...] = qc.astype(BF16)
        kc_o[...] = rope64(kc).astype(BF16)
        vc_o[...] = vc.astype(BF16)
        qdr_o[...] = rope64(qd).astype(BF16)
        qd_o[...] = qd.astype(BF16)
        kd_o[...] = rope64(kd).astype(BF16)
        vd_o[...] = vd.astype(BF16)


def _inproj(x, mod, lw, l, seq_len, cond_base, rope_tabs):
    rows = x.shape[0]
    r = ROW_TILE
    rope = rope_tabs is not None
    tiles_per_seq = max(seq_len // r, 1)
    step = 1 if seq_len >= r else 0

    def cond_map(i):
        return (l, cond_base + step * (i // tiles_per_seq), 0, 0)

    row = lambda w: pl.BlockSpec((r, w), lambda i: (i, 0))
    full = lambda a: pl.BlockSpec((None,) + a.shape[1:], lambda i: (l,) + (0,) * (a.ndim - 1))
    in_specs = [row(D_MODEL),
                pl.BlockSpec((None, None, 6, D_MODEL), cond_map),
                full(lw['norm1']), full(lw['w_in']), full(lw['wq']), full(lw['wkv']),
                full(lw['gq']), full(lw['gkv']), full(lw['gqc']), full(lw['gkc']),
                pl.BlockSpec((2 * LANES, 2 * LANES), lambda i: (0, 0))]
    args = [x, mod, lw['norm1'], lw['w_in'], lw['wq'], lw['wkv'], lw['gq'], lw['gkv'], lw['gqc'], lw['gkc'],
            lw['hm']]
    bf = lambda w: jax.ShapeDtypeStruct((rows, w), BF16)
    f32 = lambda w: jax.ShapeDtypeStruct((rows, w), F32)
    if rope:
        tab = pl.BlockSpec((r, LANES), lambda i: (i % tiles_per_seq, 0))
        in_specs += [tab] * 4
        args += list(rope_tabs)
        widths = [256, 256, 512, 512, 512, 256, 256, 256, 128, 128, 256, 256, 128, 128]
        out_shape = [bf(w) for w in widths]
        out_specs = [row(w) for w in widths]
    else:
        widths = [256, 256, 512, 512, 256, 256, 128, 128, 256, 128, 128]
        cwidths = [128, MLA_ROPE, 128, 128, 128, 128]
        out_shape = [bf(w) for w in widths] + [f32(w) for w in cwidths]
        out_specs = [row(w) for w in widths + cwidths]
    return pl.pallas_call(
        functools.partial(_inproj_body, rope),
        grid=(rows // r,),
        in_specs=in_specs, out_specs=out_specs, out_shape=out_shape,
        compiler_params=pltpu.CompilerParams(
            dimension_semantics=("arbitrary",), vmem_limit_bytes=VMEM_LIMIT),
        name="inproj_rope" if rope else "inproj",
    )(*args)


def _softmax_pv(s_list, v_list, extra=None):
    m = s_list[0].max(axis=-1, keepdims=True)
    for s in s_list[1:]:
        m = jnp.maximum(m, s.max(axis=-1, keepdims=True))
    if extra is not None:
        m = jnp.maximum(m, extra)
    den = jnp.exp(extra - m) if extra is not None else 0.0
    out = None
    for s, v in zip(s_list, v_list):
        p = jnp.exp(s - m)
        den = den + p.sum(axis=-1, keepdims=True)
        pv = _dot(p.astype(BF16), v)
        out = pv if out is None else out + pv
    return out / den


def _half_masks(q):
    lo = _lane(q.shape) < HEAD_DIM
    zero = jnp.zeros_like(q)
    return jnp.where(lo, q, zero), jnp.where(lo, zero, q)


def _merge_halves(a, b):
    return jnp.where(_lane(a.shape) < HEAD_DIM, a, b)


def _sink_col(sink_ref, l, h0, h1, tq):
    r = lax.broadcasted_iota(jnp.int32, (2 * tq, 1), 0)
    return jnp.where(r < tq, sink_ref[l, h0], sink_ref[l, h1])


def _attn_ctx_body(l, nb, t, sink_ref, qm_ref, km_ref, vm_ref, qc_ref, kc_ref, vc_ref, qd_ref, kd_ref, vd_ref,
                   yb_ref, yc_ref, yd_ref):
    for b in range(nb):
        rs = slice(b * t, (b + 1) * t)
        o = []
        for h in range(MLA_HEADS):
            cs = slice(h * LANES, (h + 1) * LANES)
            vs = slice((h // 2) * LANES, (h // 2 + 1) * LANES)
            s = _dot_nt(qm_ref[rs, cs], km_ref[rs, cs])
            o.append(_softmax_pv([s], [vm_ref[rs, vs]]))
        yb_ref[rs, 0:LANES] = _merge_halves(o[0], o[1]).astype(BF16)
        yb_ref[rs, LANES:] = _merge_halves(o[2], o[3]).astype(BF16)

        for q_ref, k_ref, v_ref, y_ref, use_sink in ((qc_ref, kc_ref, vc_ref, yc_ref, False),
                                                      (qd_ref, kd_ref, vd_ref, yd_ref, True)):
            qa_lo, qa_hi = _half_masks(q_ref[rs, 0:LANES])
            qb_lo, qb_hi = _half_masks(q_ref[rs, LANES:])
            k = k_ref[rs, :]
            v = v_ref[rs, :]
            res = []
            for j, (qa, qb) in enumerate(((qa_lo, qb_lo), (qa_hi, qb_hi))):
                s = _dot_nt(jnp.concatenate([qa, qb], axis=0), k)
                extra = _sink_col(sink_ref, l, 2 * j, 2 * j + 1, t) if use_sink else None
                res.append(_softmax_pv([s], [v], extra))
            y_ref[rs, 0:LANES] = _merge_halves(res[0][:t], res[1][:t]).astype(BF16)
            y_ref[rs, LANES:] = _merge_halves(res[0][t:], res[1][t:]).astype(BF16)


def _attn_ctx(l, t, sink, qm, km, vm, qc, kc, vc, qd, kd, vd):
    rows = qm.shape[0]
    nb = 2
    r = nb * t
    row = lambda a: pl.BlockSpec((r, a.shape[1]), lambda i: (i, 0))
    ins = [qm, km, vm, qc, kc, vc, qd, kd, vd]
    return pl.pallas_call(
        functools.partial(_attn_ctx_body, l, nb, t),
        grid=(rows // r,),
        in_specs=[pl.BlockSpec(memory_space=pltpu.SMEM)] + [row(a) for a in ins],
        out_specs=[pl.BlockSpec((r, 2 * LANES), lambda i: (i, 0))] * 3,
        out_shape=[jax.ShapeDtypeStruct((rows, 2 * LANES), BF16)] * 3,
        compiler_params=pltpu.CompilerParams(
            dimension_semantics=("arbitrary",), vmem_limit_bytes=VMEM_LIMIT),
        name="attn_ctx",
    )(sink, *ins)


def _attn_lat_body(l, tq, t, sink_ref, qmr_ref, qm_ref, km_ref, vm_ref, kx_ref, vx_ref,
                   qcr_ref, qc_ref, kc_ref, vc_ref, kcx_ref, vcx_ref,
                   qdr_ref, qd_ref, kd_ref, vd_ref, kdx_ref, vdx_ref,
                   yb_ref, yc_ref, yd_ref):
    i = pl.program_id(1)
    o = []
    for h in range(MLA_HEADS):
        cs = slice(h * LANES, (h + 1) * LANES)
        vs = slice((h // 2) * LANES, (h // 2 + 1) * LANES)
        s1 = _dot_nt(qmr_ref[:, cs], km_ref[:, cs])
        s2 = _dot_nt(qm_ref[:, cs], kx_ref[:, cs])
        o.append(_softmax_pv([s1, s2], [vm_ref[:, vs], vx_ref[:, vs]]))
    yb_ref[:, 0:LANES] = _merge_halves(o[0], o[1]).astype(BF16)
    yb_ref[:, LANES:] = _merge_halves(o[2], o[3]).astype(BF16)

    ra_lo, ra_hi = _half_masks(qcr_ref[:, 0:LANES])
    rb_lo, rb_hi = _half_masks(qcr_ref[:, LANES:])
    qa_lo, qa_hi = _half_masks(qc_ref[:, 0:LANES])
    qb_lo, qb_hi = _half_masks(qc_ref[:, LANES:])
    res = []
    for (ra, rb, qa, qb) in ((ra_lo, rb_lo, qa_lo, qb_lo), (ra_hi, rb_hi, qa_hi, qb_hi)):
        s1 = _dot_nt(jnp.concatenate([ra, rb], axis=0), kc_ref[...])
        s2 = _dot_nt(jnp.concatenate([qa, qb], axis=0), kcx_ref[...])
        res.append(_softmax_pv([s1, s2], [vc_ref[...], vcx_ref[...]]))
    yc_ref[:, 0:LANES] = _merge_halves(res[0][:tq], res[1][:tq]).astype(BF16)
    yc_ref[:, LANES:] = _merge_halves(res[0][tq:], res[1][tq:]).astype(BF16)

    wk = tq + 2 * WINDOW
    start = pl.multiple_of(jnp.clip(i * tq - WINDOW, 0, t - wk), WINDOW)
    kw = kd_ref[pl.ds(start, wk), :]
    vw = vd_ref[pl.ds(start, wk), :]
    qpos = i * tq + lax.broadcasted_iota(jnp.int32, (2 * tq, wk), 0) % tq
    kpos = start + lax.broadcasted_iota(jnp.int32, (2 * tq, wk), 1)
    valid = jnp.abs(kpos - qpos) <= WINDOW
    ra_lo, ra_hi = _half_masks(qdr_ref[:, 0:LANES])
    rb_lo, rb_hi = _half_masks(qdr_ref[:, LANES:])
    qa_lo, qa_hi = _half_masks(qd_ref[:, 0:LANES])
    qb_lo, qb_hi = _half_masks(qd_ref[:, LANES:])
    res = []
    for j, (ra, rb, qa, qb) in enumerate(((ra_lo, rb_lo, qa_lo, qb_lo), (ra_hi, rb_hi, qa_hi, qb_hi))):
        s1 = jnp.where(valid, _dot_nt(jnp.concatenate([ra, rb], axis=0), kw), NEG_INF)
        s2 = _dot_nt(jnp.concatenate([qa, qb], axis=0), kdx_ref[...])
        extra = _sink_col(sink_ref, l, 2 * j, 2 * j + 1, tq)
        res.append(_softmax_pv([s1, s2], [vw, vdx_ref[...]], extra))
    yd_ref[:, 0:LANES] = _merge_halves(res[0][:tq], res[1][:tq]).astype(BF16)
    yd_ref[:, LANES:] = _merge_halves(res[0][tq:], res[1][tq:]).astype(BF16)


def _attn_lat(l, nbatch, t, sink, qmr, qm, km, vm, kx, vx, qcr, qc, kc, vc, kcx, vcx, qdr, qd, kd, vd, kdx, vdx):
    tq = 256
    nq = t // tq
    rows = qm.shape[0]
    qs = lambda a: pl.BlockSpec((tq, a.shape[1]), lambda b, i: (b * nq + i, 0))
    ks = lambda a: pl.BlockSpec((t, a.shape[1]), lambda b, i: (b, 0))
    xs = lambda a: pl.BlockSpec((None, None) + a.shape[2:], lambda b, i: (l, b, 0, 0))
    cs = lambda a: pl.BlockSpec((None, None) + a.shape[2:], lambda b, i: (b, l, 0, 0))
    in_specs = [pl.BlockSpec(memory_space=pltpu.SMEM),
                qs(qmr), qs(qm), ks(km), ks(vm), xs(kx), xs(vx),
                qs(qcr), qs(qc), ks(kc), ks(vc), cs(kcx), cs(vcx),
                qs(qdr), qs(qd), ks(kd), ks(vd), cs(kdx), cs(vdx)]
    return pl.pallas_call(
        functools.partial(_attn_lat_body, l, tq, t),
        grid=(nbatch, nq),
        in_specs=in_specs,
        out_specs=[pl.BlockSpec((tq, 2 * LANES), lambda b, i: (b * nq + i, 0))] * 3,
        out_shape=[jax.ShapeDtypeStruct((rows, 2 * LANES), BF16)] * 3,
        compiler_params=pltpu.CompilerParams(
            dimension_semantics=("arbitrary", "arbitrary"), vmem_limit_bytes=VMEM_LIMIT),
        name="attn_lat",
    )(sink, qmr, qm, km, vm, kx, vx, qcr, qc, kc, vc, kcx, vcx, qdr, qd, kd, vd, kdx, vdx)


def _outproj_body(seq_len, x_ref, mod_ref, n2_ref, cw_ref, wo_ref, zp_ref, z_ref, zn_ref, gb_ref,
                  yb_ref, yc_ref, yd_ref, x1_ref, h2_ref):
    r = x_ref.shape[0]
    i = pl.program_id(0)
    zz = jnp.concatenate([zp_ref[...], z_ref[...], zn_ref[...]], axis=0).astype(F32)
    prev, nxt = _shift_rows(zz, BF16_ROWS, r)
    pos = _seq_pos(i, r, seq_len)
    prev = jnp.where(pos == 0, 0.0, prev)
    nxt = jnp.where(pos == seq_len - 1, 0.0, nxt)
    cw = cw_ref[...]
    ya = gb_ref[...].astype(F32) * (prev * cw[0:1] + z_ref[...].astype(F32) * cw[1:2] + nxt * cw[2:3])
    ycat = jnp.concatenate([ya.astype(BF16), yb_ref[...], yc_ref[...], yd_ref[...]], axis=1)
    y = _dot(ycat, wo_ref[...])
    g1 = mod_ref[2:3, :]
    sh2 = mod_ref[3:4, :]
    sc2 = mod_ref[4:5, :]
    x1 = x_ref[...] + g1 * y
    x1_ref[...] = x1
    h2_ref[...] = (_rms(x1, n2_ref[...], D_MODEL) * (1.0 + sc2) + sh2).astype(BF16)


def _halo_specs(r, w, rows):
    nblk = rows // BF16_ROWS
    per = r // BF16_ROWS
    prev = pl.BlockSpec((BF16_ROWS, w), lambda i: (jnp.maximum(i * per - 1, 0), 0))
    cur = pl.BlockSpec((r, w), lambda i: (i, 0))
    nxt = pl.BlockSpec((BF16_ROWS, w), lambda i: (jnp.minimum((i + 1) * per, nblk - 1), 0))
    return prev, cur, nxt


def _cond_spec(l, seq_len, cond_base):
    r = ROW_TILE
    tiles_per_seq = max(seq_len // r, 1)
    step = 1 if seq_len >= r else 0
    return pl.BlockSpec((None, None, 6, D_MODEL),
                        lambda i: (l, cond_base + step * (i // tiles_per_seq), 0, 0))


def _outproj(x, mod, lw, l, seq_len, cond_base, z, gb, yb, yc, yd):
    rows = x.shape[0]
    r = ROW_TILE
    row = lambda w: pl.BlockSpec((r, w), lambda i: (i, 0))
    full = lambda a: pl.BlockSpec((None,) + a.shape[1:], lambda i: (l,) + (0,) * (a.ndim - 1))
    zp, zc, zn = _halo_specs(r, 2 * LANES, rows)
    return pl.pallas_call(
        functools.partial(_outproj_body, seq_len),
        grid=(rows // r,),
        in_specs=[row(D_MODEL), _cond_spec(l, seq_len, cond_base), full(lw['norm2']), full(lw['conv_a']),
                  full(lw['w_out']), zp, zc, zn, row(256), row(256), row(256), row(256)],
        out_specs=[row(D_MODEL), row(D_MODEL)],
        out_shape=[jax.ShapeDtypeStruct((rows, D_MODEL), F32), jax.ShapeDtypeStruct((rows, D_MODEL), BF16)],
        compiler_params=pltpu.CompilerParams(
            dimension_semantics=("arbitrary",), vmem_limit_bytes=VMEM_LIMIT),
        name="outproj",
    )(x, mod, lw['norm2'], lw['conv_a'], lw['w_out'], z, z, z, gb, yb, yc, yd)


def _mlp_body(seq_len, final, x1_ref, mod_ref, cw_ref, wup_ref, wdn_ref, fn_ref, hp_ref, h_ref, hn_ref,
              o_ref, hbuf):
    r = x1_ref.shape[0]
    halo = BF16_ROWS
    i = pl.program_id(0)
    hbuf[0:halo, :] = hp_ref[...]
    hbuf[halo:halo + r, :] = h_ref[...]
    hbuf[halo + r:, :] = hn_ref[...]
    pos = _seq_pos(i, r, seq_len)
    first = pos == 0
    last = pos == seq_len - 1
    acc = jnp.zeros((r, D_MODEL), F32)
    for j in range(N_FF_TILES):
        cs = slice(j * 2 * FF_TILE, (j + 1) * 2 * FF_TILE)
        u = _dot(hbuf[...], wup_ref[:, cs])
        prev, nxt = _shift_rows(u, halo, r)
        cw = cw_ref[:, cs]
        uc = (jnp.where(first, 0.0, prev) * cw[0:1] + u[halo:halo + r] * cw[1:2]
              + jnp.where(last, 0.0, nxt) * cw[2:3])
        ua, ub = uc[:, :FF_TILE], uc[:, FF_TILE:]
        g = ua * jax.nn.sigmoid(ua) * ub
        acc = acc + _dot(g.astype(BF16), wdn_ref[j * FF_TILE:(j + 1) * FF_TILE, :])
    g2 = mod_ref[5:6, :]
    x2 = x1_ref[...] + g2 * acc
    if final:
        x2 = _rms(x2, fn_ref[...], D_MODEL)
    o_ref[...] = x2


def _mlp(x1, h2, mod, lw, l, seq_len, cond_base, final_norm, final):
    rows = x1.shape[0]
    r = ROW_TILE
    row = lambda w: pl.BlockSpec((r, w), lambda i: (i, 0))
    full = lambda a: pl.BlockSpec((None,) + a.shape[1:], lambda i: (l,) + (0,) * (a.ndim - 1))
    hp, hc, hn = _halo_specs(r, D_MODEL, rows)
    return pl.pallas_call(
        functools.partial(_mlp_body, seq_len, final),
        grid=(rows // r,),
        in_specs=[row(D_MODEL), _cond_spec(l, seq_len, cond_base), full(lw['conv_ff']),
                  full(lw['w_up']), full(lw['w_down']),
                  pl.BlockSpec((1, D_MODEL), lambda i: (0, 0)), hp, hc, hn],
        out_specs=row(D_MODEL),
        out_shape=jax.ShapeDtypeStruct((rows, D_MODEL), F32),
        scratch_shapes=[pltpu.VMEM((r + 2 * BF16_ROWS, D_MODEL), BF16)],
        compiler_params=pltpu.CompilerParams(
            dimension_semantics=("arbitrary",), vmem_limit_bytes=VMEM_LIMIT),
        name="mlp",
    )(x1, mod, lw['conv_ff'], lw['w_up'], lw['w_down'], final_norm, h2, h2, h2)


def _pad_cols(a, n):
    return jnp.pad(a, [(0, 0)] * (a.ndim - 1) + [(0, n - a.shape[-1])])


def _perm_heads(a, axis):
    shp = a.shape
    a = a.reshape(shp[:axis] + (4, HEAD_DIM) + shp[axis + 1:])
    a = jnp.take(a, jnp.array([0, 2, 1, 3]), axis=axis)
    return a.reshape(shp)


def _prep_weights(w_in, mla_wq_b, mla_wkv_b, w_out, w_up, conv_ff, w_down, norm1, norm2, conv_a,
                  mla_q_norm, mla_kv_norm, gqa_q_norm, gqa_k_norm):
    L = DEPTH
    a = w_in[..., 0:768]
    cq = _pad_cols(w_in[..., 768:960], 256)
    ckv = w_in[..., 960:1088]
    kpe = _pad_cols(w_in[..., 1088:1120], LANES)
    qc = _perm_heads(w_in[..., 1120:1376], 2)
    kvc = w_in[..., 1376:1632]
    qd = _perm_heads(w_in[..., 1632:1888], 2)
    kvd = w_in[..., 1888:2144]
    w_in_p = jnp.concatenate([a, cq, ckv, kpe, qc, kvc, qd, kvd], axis=-1).astype(BF16)

    wq = mla_wq_b.reshape(L, MLA_Q_LORA, MLA_HEADS, MLA_NOPE + MLA_ROPE)
    wq = jnp.concatenate([wq[..., MLA_NOPE:], wq[..., :MLA_NOPE],
                          jnp.zeros((L, MLA_Q_LORA, MLA_HEADS, LANES - MLA_NOPE - MLA_ROPE), F32)], axis=-1)
    wq = jnp.pad(wq.reshape(L, MLA_Q_LORA, MLA_HEADS * LANES), ((0, 0), (0, 256 - MLA_Q_LORA), (0, 0)))

    wkv = mla_wkv_b.reshape(L, MLA_KV_LORA, MLA_HEADS, MLA_NOPE + MLA_V)
    zk = jnp.zeros((L, MLA_KV_LORA, MLA_HEADS, MLA_ROPE), F32)
    wk = jnp.concatenate([zk, wkv[..., :MLA_NOPE], zk], axis=-1).reshape(L, MLA_KV_LORA, MLA_HEADS * LANES)
    wv = wkv[..., MLA_NOPE:].reshape(L, MLA_KV_LORA, MLA_HEADS * MLA_V)
    wkv_p = jnp.concatenate([wk, wv], axis=-1)

    wo = jnp.concatenate([w_out[:, 0:512], _perm_heads(w_out[:, 512:768], 1),
                          _perm_heads(w_out[:, 768:1024], 1)], axis=1)

    wu = w_up.reshape(L, D_MODEL, 2, N_FF_TILES, FF_TILE).transpose(0, 1, 3, 2, 4).reshape(L, D_MODEL, 2 * D_FF)
    cf = conv_ff.reshape(L, 3, 2, N_FF_TILES, FF_TILE).transpose(0, 1, 3, 2, 4).reshape(L, 3, 2 * D_FF)

    hm = jnp.kron(jnp.eye(2 * LANES // HEAD_DIM, dtype=F32), jnp.ones((HEAD_DIM, HEAD_DIM), F32)).astype(BF16)
    return {
        'w_in': w_in_p, 'wq': wq.astype(BF16), 'wkv': wkv_p.astype(BF16), 'w_out': wo.astype(BF16),
        'w_up': wu.astype(BF16), 'conv_ff': cf, 'w_down': w_down.astype(BF16),
        'norm1': norm1.reshape(L, 1, D_MODEL), 'norm2': norm2.reshape(L, 1, D_MODEL), 'conv_a': conv_a,
        'gq': _pad_cols(mla_q_norm, 256).reshape(L, 1, 256), 'gkv': mla_kv_norm.reshape(L, 1, MLA_KV_LORA),
        'gqc': jnp.tile(gqa_q_norm, (1, 4)).reshape(L, 1, 256), 'gkc': jnp.tile(gqa_k_norm, (1, 2)).reshape(L, 1, 128),
        'hm': hm,
    }


def _rope_tables(t):
    rows = t // GRID_W
    row = jnp.repeat(jnp.arange(rows, dtype=F32), GRID_W)
    col = jnp.tile(jnp.arange(GRID_W, dtype=F32), rows)

    def tabs(dim):
        half = dim // 2
        inv = jnp.power(ROPE_THETA, -jnp.arange(0, half, 2, dtype=F32) / half)
        ar = row[:, None] * inv
        ac = col[:, None] * inv
        c = jnp.concatenate([jnp.cos(ar), jnp.cos(ar), jnp.cos(ac), jnp.cos(ac)], axis=1)
        s = jnp.concatenate([-jnp.sin(ar), jnp.sin(ar), -jnp.sin(ac), jnp.sin(ac)], axis=1)
        return c, s

    c64, s64 = tabs(HEAD_DIM)
    c32, s32 = tabs(MLA_ROPE)
    c64 = jnp.tile(c64, (1, 2))
    s64 = jnp.tile(s64, (1, 2))
    c32 = jnp.concatenate([c32, jnp.ones((t, LANES - MLA_ROPE), F32)], axis=1)
    s32 = _pad_cols(s32, LANES)
    return c64, s64, c32, s32


def kernel(x_prompt, x_sample, cache_mla_ckv, cache_mla_kpe, cache_gqa_k, cache_gqa_v, cache_swa_k, cache_swa_v,
           c, c_ctx, w_ada, b_ada, norm1, w_in, conv_a, mla_q_norm, mla_wq_b, mla_kv_norm, mla_wkv_b,
           gqa_q_norm, gqa_k_norm, swa_sink, w_out, norm2, w_up, conv_ff, w_down, final_norm):
    B, T, _ = x_prompt.shape
    DB, DT, _ = x_sample.shape
    past = cache_mla_ckv.shape[2]

    lw = _prep_weights(w_in, mla_wq_b, mla_wkv_b, w_out, w_up, conv_ff, w_down, norm1, norm2, conv_a,
                       mla_q_norm, mla_kv_norm, gqa_q_norm, gqa_k_norm)
    rope_tabs = _rope_tables(DT)
    fnorm = final_norm.reshape(1, D_MODEL)

    cond8 = jnp.zeros((8, D_MODEL), F32).at[0].set(c_ctx).at[1:1 + DB].set(c)
    mod = _ada(cond8, w_ada, b_ada).reshape(DEPTH, 8, 6, D_MODEL)

    kx, vx = _ctx_mla(cache_mla_ckv, _pad_cols(cache_mla_kpe, LANES), lw['wkv'])
    kcx = cache_gqa_k.reshape(DB, DEPTH, past, 2 * HEAD_DIM).astype(BF16)
    vcx = cache_gqa_v.reshape(DB, DEPTH, past, 2 * HEAD_DIM).astype(BF16)
    kdx = cache_swa_k.reshape(DB, DEPTH, past, 2 * HEAD_DIM).astype(BF16)
    vdx = cache_swa_v.reshape(DB, DEPTH, past, 2 * HEAD_DIM).astype(BF16)

    xp = x_prompt.reshape(B * T, D_MODEL)
    xs = x_sample.reshape(DB * DT, D_MODEL)
    caches = []
    for l in range(DEPTH):
        final = l == DEPTH - 1
        (z, gb, qm, km, vm, qc, kc, vc, qd, kd, vd, *cache_l) = _inproj(xp, mod, lw, l, T, 0, None)
        caches.append(cache_l)
        yb, yc, yd = _attn_ctx(l, T, swa_sink, qm, km, vm, qc, kc, vc, qd, kd, vd)
        x1, h2 = _outproj(xp, mod, lw, l, T, 0, z, gb, yb, yc, yd)
        xp = _mlp(x1, h2, mod, lw, l, T, 0, fnorm, final)
        (z, gb, qmr, qm, km, vm, qcr, qc, kc, vc, qdr, qd, kd, vd) = _inproj(xs, mod, lw, l, DT, 1, rope_tabs)
        yb, yc, yd = _attn_lat(l, DB, DT, swa_sink, qmr, qm, km, vm, kx, vx, qcr, qc, kc, vc, kcx, vcx,
                               qdr, qd, kd, vd, kdx, vdx)
        x1, h2 = _outproj(xs, mod, lw, l, DT, 1, z, gb, yb, yc, yd)
        xs = _mlp(x1, h2, mod, lw, l, DT, 1, fnorm, final)

    def stack(idx, shape):
        return jnp.stack([caches[l][idx].reshape((B, T) + shape) for l in range(DEPTH)], axis=1)

    return (xp.reshape(B, T, D_MODEL), xs.reshape(DB, DT, D_MODEL),
            stack(0, (MLA_KV_LORA,)), stack(1, (MLA_ROPE,)),
            stack(2, (2, HEAD_DIM)), stack(3, (2, HEAD_DIM)), stack(4, (2, HEAD_DIM)), stack(5, (2, HEAD_DIM)))
```

```python
import functools

import jax
import jax.numpy as jnp
from jax import lax
from jax.experimental import pallas as pl
from jax.experimental.pallas import tpu as pltpu

F32 = jnp.float32
BF16 = jnp.bfloat16

D_MODEL = 1024
DEPTH = 2
GRID_W = 64
HEAD_DIM = 64
GROUP_WIDTH = D_MODEL // 4
MLA_HEADS = 4
MLA_NOPE = 64
MLA_ROPE = 32
MLA_V = 64
MLA_Q_LORA = 192
MLA_KV_LORA = 128
WINDOW = 128
D_FF = 2816
ROPE_THETA = 10000.0
EPS = 1e-6
NEG_INF = -1e30
ATTN_SCALE = HEAD_DIM ** -0.5
MLA_SCALE = (MLA_NOPE + MLA_ROPE) ** -0.5

LANES = 128
BF16_ROWS = 16
ROW_TILE = 512
FF_TILE = 256
N_FF_TILES = D_FF // FF_TILE
DOWN_CHUNK = 4
IN_COLS_PACKED = 2304
VMEM_LIMIT = 56 * 1024 * 1024

_NT = (((1,), (1,)), ((), ()))


def _dot(a, b):
    return jnp.dot(a, b, preferred_element_type=F32)


def _dot_nt(a, b):
    return lax.dot_general(a, b, _NT, preferred_element_type=F32)


def _rms(x, g, n):
    ms = jnp.sum(x * x, axis=-1, keepdims=True) * (1.0 / n)
    return x * lax.rsqrt(ms + EPS) * g


def _lane(shape):
    return lax.broadcasted_iota(jnp.int32, shape, len(shape) - 1)


def _rope(x, c, s, half):
    w = x.shape[-1]
    lo = (_lane(x.shape) % (2 * half)) < half
    sw = jnp.where(lo, pltpu.roll(x, w - half, 1), pltpu.roll(x, half, 1))
    return x * c + sw * s


def _shift_rows(zz, halo, rows):
    n = zz.shape[0]
    prev = pltpu.roll(zz, 1, 0)[halo:halo + rows]
    nxt = pltpu.roll(zz, n - 1, 0)[halo:halo + rows]
    return prev, nxt


def _seq_pos(tile, rows, seq_len):
    r = lax.broadcasted_iota(jnp.int32, (rows, 1), 0)
    return (tile * rows + r) % seq_len


def _ada_body(c_ref, w_ref, b_ref, o_ref):
    c = c_ref[...]
    s = c * jax.nn.sigmoid(c)
    o_ref[...] = jnp.dot(s, w_ref[...], precision=lax.Precision.HIGHEST,
                         preferred_element_type=F32) + b_ref[...]


def _ada(cond8, w_ada, b_ada):
    tn = 1536
    n = 6 * D_MODEL
    return pl.pallas_call(
        _ada_body,
        grid=(DEPTH, n // tn),
        in_specs=[pl.BlockSpec((8, D_MODEL), lambda l, j: (0, 0)),
                  pl.BlockSpec((None, D_MODEL, tn), lambda l, j: (l, 0, j)),
                  pl.BlockSpec((None, 1, tn), lambda l, j: (l, 0, j))],
        out_specs=pl.BlockSpec((None, 8, tn), lambda l, j: (l, 0, j)),
        out_shape=jax.ShapeDtypeStruct((DEPTH, 8, n), F32),
        compiler_params=pltpu.CompilerParams(
            dimension_semantics=("arbitrary", "arbitrary"), vmem_limit_bytes=VMEM_LIMIT),
        name="ada",
    )(cond8, w_ada, b_ada.reshape(DEPTH, 1, n))


def _ctx_body(ckv_ref, kpe_ref, w_ref, kx_ref, vx_ref):
    kv = _dot(ckv_ref[...].astype(BF16), w_ref[...])
    kpe = kpe_ref[...]
    kx_ref[...] = (kv[:, :4 * LANES] + jnp.concatenate([kpe] * MLA_HEADS, axis=1)).astype(BF16)
    vx_ref[...] = kv[:, 4 * LANES:].astype(BF16)


def _ctx_mla(cache_ckv, cache_kpe_pad, wkv_p):
    b, _, s, _ = cache_ckv.shape
    return pl.pallas_call(
        _ctx_body,
        grid=(DEPTH, b),
        in_specs=[pl.BlockSpec((None, None, s, MLA_KV_LORA), lambda l, i: (i, l, 0, 0)),
                  pl.BlockSpec((None, None, s, LANES), lambda l, i: (i, l, 0, 0)),
                  pl.BlockSpec((None, MLA_KV_LORA, 6 * LANES), lambda l, i: (l, 0, 0))],
        out_specs=[pl.BlockSpec((None, None, s, 4 * LANES), lambda l, i: (l, i, 0, 0)),
                   pl.BlockSpec((None, None, s, 2 * LANES), lambda l, i: (l, i, 0, 0))],
        out_shape=[jax.ShapeDtypeStruct((DEPTH, b, s, 4 * LANES), BF16),
                   jax.ShapeDtypeStruct((DEPTH, b, s, 2 * LANES), BF16)],
        compiler_params=pltpu.CompilerParams(
            dimension_semantics=("arbitrary", "arbitrary"), vmem_limit_bytes=VMEM_LIMIT),
        name="ctx_mla",
    )(cache_ckv, cache_kpe_pad, wkv_p)


def _inproj_body(rope, x_ref, mod_ref, n1_ref, w_ref, wq_ref, wkv_ref, gq_ref, gkv_ref, gqc_ref, gkc_ref,
                 hm_ref, *rest):
    if rope:
        c64_ref, s64_ref, c32_ref, s32_ref = rest[:4]
        outs = rest[4:]
    else:
        outs = rest
    x = x_ref[...]
    sh1 = mod_ref[0:1, :]
    sc1 = mod_ref[1:2, :]
    h = _rms(x, n1_ref[...], D_MODEL) * (1.0 + sc1) + sh1
    acc = _dot(h.astype(BF16), w_ref[...])

    xa, gb, gc = acc[:, 0:256], acc[:, 256:512], acc[:, 512:768]
    z = gc * xa
    cq = acc[:, 768:1024]
    cqn = _rms(cq, gq_ref[...], MLA_Q_LORA)
    qm = _dot(cqn.astype(BF16), wq_ref[...]) * MLA_SCALE
    ckvn = _rms(acc[:, 1024:1152], gkv_ref[...], MLA_KV_LORA)
    kv = _dot(ckvn.astype(BF16), wkv_ref[...])
    kpe = acc[:, 1152:1280]
    vm = kv[:, 4 * LANES:]

    hm = hm_ref[...]

    def head_rms(v, g):
        v2 = v * v
        hi = v2.astype(BF16)
        lo = (v2 - hi.astype(F32)).astype(BF16)
        w = v.shape[-1]
        ss = _dot(hi, hm[:w, :w]) + _dot(lo, hm[:w, :w])
        return v * lax.rsqrt(ss * (1.0 / HEAD_DIM) + EPS) * g

    qc = head_rms(acc[:, 1280:1536], gqc_ref[...]) * ATTN_SCALE
    kc = head_rms(acc[:, 1536:1664], gkc_ref[...])
    vc = acc[:, 1664:1792]
    qd = acc[:, 1792:2048] * ATTN_SCALE
    kd = acc[:, 2048:2176]
    vd = acc[:, 2176:2304]

    if not rope:
        (z_o, gb_o, qm_o, km_o, vm_o, qc_o, kc_o, vc_o, qd_o, kd_o, vd_o,
         ckv_c, kpe_c, kc_c, vc_c, kd_c, vd_c) = outs
        km = kv[:, :4 * LANES] + jnp.concatenate([kpe] * MLA_HEADS, axis=1)
        z_o[...] = z.astype(BF16)
        gb_o[...] = gb.astype(BF16)
        qm_o[...] = qm.astype(BF16)
        km_o[...] = km.astype(BF16)
        vm_o[...] = vm.astype(BF16)
        qc_o[...] = qc.astype(BF16)
        kc_o[...] = kc.astype(BF16)
        vc_o[...] = vc.astype(BF16)
        qd_o[...] = qd.astype(BF16)
        kd_o[...] = kd.astype(BF16)
        vd_o[...] = vd.astype(BF16)
        ckv_c[...] = ckvn
        kpe_c[...] = kpe[:, :MLA_ROPE]
        kc_c[...] = kc
        vc_c[...] = vc
        kd_c[...] = kd
        vd_c[...] = vd
    else:
        (z_o, gb_o, qmr_o, qm_o, km_o, vm_o, qcr_o, qc_o, kc_o, vc_o, qdr_o, qd_o, kd_o, vd_o) = outs
        c64, s64, c32, s32 = c64_ref[...], s64_ref[...], c32_ref[...], s32_ref[...]
        rope64 = lambda v: jnp.concatenate(
            [_rope(v[:, i:i + LANES], c64, s64, 16) for i in range(0, v.shape[-1], LANES)], axis=1)
        rope32 = lambda v: jnp.concatenate(
            [_rope(v[:, i:i + LANES], c32, s32, 8) for i in range(0, v.shape[-1], LANES)], axis=1)
        km = kv[:, :4 * LANES] + jnp.concatenate([_rope(kpe, c32, s32, 8)] * MLA_HEADS, axis=1)
        z_o[...] = z.astype(BF16)
        gb_o[...] = gb.astype(BF16)
        qmr_o[...] = rope32(qm).astype(BF16)
        qm_o[...] = qm.astype(BF16)
        km_o[...] = km.astype(BF16)
        vm_o[...] = vm.astype(BF16)
        qcr_o[...] = rope64(qc).astype(BF16)
        qc_o[...] = qc.astype(BF16)
        kc_o[...] = rope64(kc).astype(BF16)
        vc_o[...] = vc.astype(BF16)
        qdr_o[...] = rope64(qd).astype(BF16)
        qd_o[...] = qd.astype(BF16)
        kd_o[...] = rope64(kd).astype(BF16)
        vd_o[...] = vd.astype(BF16)


def _inproj(x, mod, lw, l, seq_len, cond_base, rope_tabs):
    rows = x.shape[0]
    r = ROW_TILE
    rope = rope_tabs is not None
    tiles_per_seq = max(seq_len // r, 1)
    step = 1 if seq_len >= r else 0

    def cond_map(i):
        return (l, cond_base + step * (i // tiles_per_seq), 0, 0)

    row = lambda w: pl.BlockSpec((r, w), lambda i: (i, 0))
    full = lambda a: pl.BlockSpec((None,) + a.shape[1:], lambda i: (l,) + (0,) * (a.ndim - 1))
    in_specs = [row(D_MODEL),
                pl.BlockSpec((None, None, 6, D_MODEL), cond_map),
                full(lw['norm1']), full(lw['w_in']), full(lw['wq']), full(lw['wkv']),
                full(lw['gq']), full(lw['gkv']), full(lw['gqc']), full(lw['gkc']),
                pl.BlockSpec((2 * LANES, 2 * LANES), lambda i: (0, 0))]
    args = [x, mod, lw['norm1'], lw['w_in'], lw['wq'], lw['wkv'], lw['gq'], lw['gkv'], lw['gqc'], lw['gkc'],
            lw['hm']]
    bf = lambda w: jax.ShapeDtypeStruct((rows, w), BF16)
    f32 = lambda w: jax.ShapeDtypeStruct((rows, w), F32)
    if rope:
        tab = pl.BlockSpec((r, LANES), lambda i: (i % tiles_per_seq, 0))
        in_specs += [tab] * 4
        args += list(rope_tabs)
        widths = [256, 256, 512, 512, 512, 256, 256, 256, 128, 128, 256, 256, 128, 128]
        out_shape = [bf(w) for w in widths]
        out_specs = [row(w) for w in widths]
    else:
        widths = [256, 256, 512, 512, 256, 256, 128, 128, 256, 128, 128]
        cwidths = [128, MLA_ROPE, 128, 128, 128, 128]
        out_shape = [bf(w) for w in widths] + [f32(w) for w in cwidths]
        out_specs = [row(w) for w in widths + cwidths]
    return pl.pallas_call(
        functools.partial(_inproj_body, rope),
        grid=(rows // r,),
        in_specs=in_specs, out_specs=out_specs, out_shape=out_shape,
        compiler_params=pltpu.CompilerParams(
            dimension_semantics=("arbitrary",), vmem_limit_bytes=VMEM_LIMIT),
        name="inproj_rope" if rope else "inproj",
    )(*args)


def _softmax_pv(s_list, v_list, extra=None):
    m = s_list[0].max(axis=-1, keepdims=True)
    for s in s_list[1:]:
        m = jnp.maximum(m, s.max(axis=-1, keepdims=True))
    if extra is not None:
        m = jnp.maximum(m, extra)
    den = jnp.exp(extra - m) if extra is not None else 0.0
    out = None
    for s, v in zip(s_list, v_list):
        p = jnp.exp(s - m)
        den = den + p.sum(axis=-1, keepdims=True)
        pv = _dot(p.astype(BF16), v)
        out = pv if out is None else out + pv
    return out / den


def _half_masks(q):
    lo = _lane(q.shape) < HEAD_DIM
    zero = jnp.zeros_like(q)
    return jnp.where(lo, q, zero), jnp.where(lo, zero, q)


def _merge_halves(a, b):
    return jnp.where(_lane(a.shape) < HEAD_DIM, a, b)


def _sink_col(sink_ref, l, h0, h1, tq):
    r = lax.broadcasted_iota(jnp.int32, (2 * tq, 1), 0)
    return jnp.where(r < tq, sink_ref[l, h0], sink_ref[l, h1])


def _attn_ctx_body(l, nb, t, sink_ref, qm_ref, km_ref, vm_ref, qc_ref, kc_ref, vc_ref, qd_ref, kd_ref, vd_ref,
                   yb_ref, yc_ref, yd_ref):
    for b in range(nb):
        rs = slice(b * t, (b + 1) * t)
        o = []
        for h in range(MLA_HEADS):
            cs = slice(h * LANES, (h + 1) * LANES)
            vs = slice((h // 2) * LANES, (h // 2 + 1) * LANES)
            s = _dot_nt(qm_ref[rs, cs], km_ref[rs, cs])
            o.append(_softmax_pv([s], [vm_ref[rs, vs]]))
        yb_ref[rs, 0:LANES] = _merge_halves(o[0], o[1]).astype(BF16)
        yb_ref[rs, LANES:] = _merge_halves(o[2], o[3]).astype(BF16)

        for q_ref, k_ref, v_ref, y_ref, use_sink in ((qc_ref, kc_ref, vc_ref, yc_ref, False),
                                                      (qd_ref, kd_ref, vd_ref, yd_ref, True)):
            qa_lo, qa_hi = _half_masks(q_ref[rs, 0:LANES])
            qb_lo, qb_hi = _half_masks(q_ref[rs, LANES:])
            k = k_ref[rs, :]
            v = v_ref[rs, :]
            res = []
            for j, (qa, qb) in enumerate(((qa_lo, qb_lo), (qa_hi, qb_hi))):
                s = _dot_nt(jnp.concatenate([qa, qb], axis=0), k)
                extra = _sink_col(sink_ref, l, 2 * j, 2 * j + 1, t) if use_sink else None
                res.append(_softmax_pv([s], [v], extra))
            y_ref[rs, 0:LANES] = _merge_halves(res[0][:t], res[1][:t]).astype(BF16)
            y_ref[rs, LANES:] = _merge_halves(res[0][t:], res[1][t:]).astype(BF16)


def _attn_ctx(l, t, sink, qm, km, vm, qc, kc, vc, qd, kd, vd):
    rows = qm.shape[0]
    nb = 2
    r = nb * t
    row = lambda a: pl.BlockSpec((r, a.shape[1]), lambda i: (i, 0))
    ins = [qm, km, vm, qc, kc, vc, qd, kd, vd]
    return pl.pallas_call(
        functools.partial(_attn_ctx_body, l, nb, t),
        grid=(rows // r,),
        in_specs=[pl.BlockSpec(memory_space=pltpu.SMEM)] + [row(a) for a in ins],
        out_specs=[pl.BlockSpec((r, 2 * LANES), lambda i: (i, 0))] * 3,
        out_shape=[jax.ShapeDtypeStruct((rows, 2 * LANES), BF16)] * 3,
        compiler_params=pltpu.CompilerParams(
            dimension_semantics=("arbitrary",), vmem_limit_bytes=VMEM_LIMIT),
        name="attn_ctx",
    )(sink, *ins)


def _attn_lat_body(l, tq, t, sink_ref, qmr_ref, qm_ref, km_ref, vm_ref, kx_ref, vx_ref,
                   qcr_ref, qc_ref, kc_ref, vc_ref, kcx_ref, vcx_ref,
                   qdr_ref, qd_ref, kd_ref, vd_ref, kdx_ref, vdx_ref,
                   yb_ref, yc_ref, yd_ref):
    i = pl.program_id(1)
    o = []
    for h in range(MLA_HEADS):
        cs = slice(h * LANES, (h + 1) * LANES)
        vs = slice((h // 2) * LANES, (h // 2 + 1) * LANES)
        s1 = _dot_nt(qmr_ref[:, cs], km_ref[:, cs])
        s2 = _dot_nt(qm_ref[:, cs], kx_ref[:, cs])
        o.append(_softmax_pv([s1, s2], [vm_ref[:, vs], vx_ref[:, vs]]))
    yb_ref[:, 0:LANES] = _merge_halves(o[0], o[1]).astype(BF16)
    yb_ref[:, LANES:] = _merge_halves(o[2], o[3]).astype(BF16)

    ra_lo, ra_hi = _half_masks(qcr_ref[:, 0:LANES])
    rb_lo, rb_hi = _half_masks(qcr_ref[:, LANES:])
    qa_lo, qa_hi = _half_masks(qc_ref[:, 0:LANES])
    qb_lo, qb_hi = _half_masks(qc_ref[:, LANES:])
    res = []
    for (ra, rb, qa, qb) in ((ra_lo, rb_lo, qa_lo, qb_lo), (ra_hi, rb_hi, qa_hi, qb_hi)):
        s1 = _dot_nt(jnp.concatenate([ra, rb], axis=0), kc_ref[...])
        s2 = _dot_nt(jnp.concatenate([qa, qb], axis=0), kcx_ref[...])
        res.append(_softmax_pv([s1, s2], [vc_ref[...], vcx_ref[...]]))
    yc_ref[:, 0:LANES] = _merge_halves(res[0][:tq], res[1][:tq]).astype(BF16)
    yc_ref[:, LANES:] = _merge_halves(res[0][tq:], res[1][tq:]).astype(BF16)

    wk = tq + 2 * WINDOW
    start = pl.multiple_of(jnp.clip(i * tq - WINDOW, 0, t - wk), WINDOW)
    kw = kd_ref[pl.ds(start, wk), :]
    vw = vd_ref[pl.ds(start, wk), :]
    qpos = i * tq + lax.broadcasted_iota(jnp.int32, (2 * tq, wk), 0) % tq
    kpos = start + lax.broadcasted_iota(jnp.int32, (2 * tq, wk), 1)
    valid = jnp.abs(kpos - qpos) <= WINDOW
    ra_lo, ra_hi = _half_masks(qdr_ref[:, 0:LANES])
    rb_lo, rb_hi = _half_masks(qdr_ref[:, LANES:])
    qa_lo, qa_hi = _half_masks(qd_ref[:, 0:LANES])
    qb_lo, qb_hi = _half_masks(qd_ref[:, LANES:])
    res = []
    for j, (ra, rb, qa, qb) in enumerate(((ra_lo, rb_lo, qa_lo, qb_lo), (ra_hi, rb_hi, qa_hi, qb_hi))):
        s1 = jnp.where(valid, _dot_nt(jnp.concatenate([ra, rb], axis=0), kw), NEG_INF)
        s2 = _dot_nt(jnp.concatenate([qa, qb], axis=0), kdx_ref[...])
        extra = _sink_col(sink_ref, l, 2 * j, 2 * j + 1, tq)
        res.append(_softmax_pv([s1, s2], [vw, vdx_ref[...]], extra))
    yd_ref[:, 0:LANES] = _merge_halves(res[0][:tq], res[1][:tq]).astype(BF16)
    yd_ref[:, LANES:] = _merge_halves(res[0][tq:], res[1][tq:]).astype(BF16)


def _attn_lat(l, nbatch, t, sink, qmr, qm, km, vm, kx, vx, qcr, qc, kc, vc, kcx, vcx, qdr, qd, kd, vd, kdx, vdx):
    tq = 256
    nq = t // tq
    rows = qm.shape[0]
    qs = lambda a: pl.BlockSpec((tq, a.shape[1]), lambda b, i: (b * nq + i, 0))
    ks = lambda a: pl.BlockSpec((t, a.shape[1]), lambda b, i: (b, 0))
    xs = lambda a: pl.BlockSpec((None, None) + a.shape[2:], lambda b, i: (l, b, 0, 0))
    cs = lambda a: pl.BlockSpec((None, None) + a.shape[2:], lambda b, i: (b, l, 0, 0))
    in_specs = [pl.BlockSpec(memory_space=pltpu.SMEM),
                qs(qmr), qs(qm), ks(km), ks(vm), xs(kx), xs(vx),
                qs(qcr), qs(qc), ks(kc), ks(vc), cs(kcx), cs(vcx),
                qs(qdr), qs(qd), ks(kd), ks(vd), cs(kdx), cs(vdx)]
    return pl.pallas_call(
        functools.partial(_attn_lat_body, l, tq, t),
        grid=(nbatch, nq),
        in_specs=in_specs,
        out_specs=[pl.BlockSpec((tq, 2 * LANES), lambda b, i: (b * nq + i, 0))] * 3,
        out_shape=[jax.ShapeDtypeStruct((rows, 2 * LANES), BF16)] * 3,
        compiler_params=pltpu.CompilerParams(
            dimension_semantics=("arbitrary", "arbitrary"), vmem_limit_bytes=VMEM_LIMIT),
        name="attn_lat",
    )(sink, qmr, qm, km, vm, kx, vx, qcr, qc, kc, vc, kcx, vcx, qdr, qd, kd, vd, kdx, vdx)


def _outproj_body(seq_len, x_ref, mod_ref, n2_ref, cw_ref, wo_ref, zp_ref, z_ref, zn_ref, gb_ref,
                  yb_ref, yc_ref, yd_ref, x1_ref, h2_ref):
    r = x_ref.shape[0]
    i = pl.program_id(0)
    zz = jnp.concatenate([zp_ref[...], z_ref[...], zn_ref[...]], axis=0).astype(F32)
    prev, nxt = _shift_rows(zz, BF16_ROWS, r)
    pos = _seq_pos(i, r, seq_len)
    prev = jnp.where(pos == 0, 0.0, prev)
    nxt = jnp.where(pos == seq_len - 1, 0.0, nxt)
    cw = cw_ref[...]
    ya = gb_ref[...].astype(F32) * (prev * cw[0:1] + z_ref[...].astype(F32) * cw[1:2] + nxt * cw[2:3])
    ycat = jnp.concatenate([ya.astype(BF16), yb_ref[...], yc_ref[...], yd_ref[...]], axis=1)
    y = _dot(ycat, wo_ref[...])
    g1 = mod_ref[2:3, :]
    sh2 = mod_ref[3:4, :]
    sc2 = mod_ref[4:5, :]
    x1 = x_ref[...] + g1 * y
    x1_ref[...] = x1
    h2_ref[...] = (_rms(x1, n2_ref[...], D_MODEL) * (1.0 + sc2) + sh2).astype(BF16)


def _halo_specs(r, w, rows):
    nblk = rows // BF16_ROWS
    per = r // BF16_ROWS
    prev = pl.BlockSpec((BF16_ROWS, w), lambda i: (jnp.maximum(i * per - 1, 0), 0))
    cur = pl.BlockSpec((r, w), lambda i: (i, 0))
    nxt = pl.BlockSpec((BF16_ROWS, w), lambda i: (jnp.minimum((i + 1) * per, nblk - 1), 0))
    return prev, cur, nxt


def _cond_spec(l, seq_len, cond_base):
    r = ROW_TILE
    tiles_per_seq = max(seq_len // r, 1)
    step = 1 if seq_len >= r else 0
    return pl.BlockSpec((None, None, 6, D_MODEL),
                        lambda i: (l, cond_base + step * (i // tiles_per_seq), 0, 0))


def _outproj(x, mod, lw, l, seq_len, cond_base, z, gb, yb, yc, yd):
    rows = x.shape[0]
    r = ROW_TILE
    row = lambda w: pl.BlockSpec((r, w), lambda i: (i, 0))
    full = lambda a: pl.BlockSpec((None,) + a.shape[1:], lambda i: (l,) + (0,) * (a.ndim - 1))
    zp, zc, zn = _halo_specs(r, 2 * LANES, rows)
    return pl.pallas_call(
        functools.partial(_outproj_body, seq_len),
        grid=(rows // r,),
        in_specs=[row(D_MODEL), _cond_spec(l, seq_len, cond_base), full(lw['norm2']), full(lw['conv_a']),
                  full(lw['w_out']), zp, zc, zn, row(256), row(256), row(256), row(256)],
        out_specs=[row(D_MODEL), row(D_MODEL)],
        out_shape=[jax.ShapeDtypeStruct((rows, D_MODEL), F32), jax.ShapeDtypeStruct((rows, D_MODEL), BF16)],
        compiler_params=pltpu.CompilerParams(
            dimension_semantics=("arbitrary",), vmem_limit_bytes=VMEM_LIMIT),
        name="outproj",
    )(x, mod, lw['norm2'], lw['conv_a'], lw['w_out'], z, z, z, gb, yb, yc, yd)


def _mlp_body(seq_len, final, x1_ref, mod_ref, cw_ref, wup_ref, wdn_ref, fn_ref, hp_ref, h_ref, hn_ref,
              o_ref, hbuf, gbuf):
    r = x1_ref.shape[0]
    halo = BF16_ROWS
    i = pl.program_id(0)
    hbuf[0:halo, :] = hp_ref[...]
    hbuf[halo:halo + r, :] = h_ref[...]
    hbuf[halo + r:, :] = hn_ref[...]
    pos = _seq_pos(i, r, seq_len)
    first = pos == 0
    last = pos == seq_len - 1

    def up(j):
        ga = slice(j * FF_TILE, (j + 1) * FF_TILE)
        va = slice(D_FF + j * FF_TILE, D_FF + (j + 1) * FF_TILE)
        return _dot(hbuf[...], wup_ref[:, ga]), _dot(hbuf[...], wup_ref[:, va])

    def conv(u, cw):
        prev, nxt = _shift_rows(u, halo, r)
        return (jnp.where(first, 0.0, prev) * cw[0:1] + u[halo:halo + r] * cw[1:2]
                + jnp.where(last, 0.0, nxt) * cw[2:3])

    acc = None
    nxt_u = up(0)
    for j in range(N_FF_TILES):
        ua, ub = nxt_u
        if j + 1 < N_FF_TILES:
            nxt_u = up(j + 1)
        ga = slice(j * FF_TILE, (j + 1) * FF_TILE)
        va = slice(D_FF + j * FF_TILE, D_FF + (j + 1) * FF_TILE)
        ua = conv(ua, cw_ref[:, ga])
        ub = conv(ub, cw_ref[:, va])
        gbuf[:, ga] = (ua * jax.nn.sigmoid(ua) * ub).astype(BF16)
        if j % DOWN_CHUNK == DOWN_CHUNK - 1 or j == N_FF_TILES - 1:
            lo = (j // DOWN_CHUNK) * DOWN_CHUNK * FF_TILE
            hi = (j + 1) * FF_TILE
            d = _dot(gbuf[:, lo:hi], wdn_ref[lo:hi, :])
            acc = d if acc is None else acc + d
    g2 = mod_ref[5:6, :]
    x2 = x1_ref[...] + g2 * acc
    if final:
        x2 = _rms(x2, fn_ref[...], D_MODEL)
    o_ref[...] = x2


def _mlp(x1, h2, mod, lw, l, seq_len, cond_base, final_norm, final):
    rows = x1.shape[0]
    r = ROW_TILE
    row = lambda w: pl.BlockSpec((r, w), lambda i: (i, 0))
    full = lambda a: pl.BlockSpec((None,) + a.shape[1:], lambda i: (l,) + (0,) * (a.ndim - 1))
    hp, hc, hn = _halo_specs(r, D_MODEL, rows)
    return pl.pallas_call(
        functools.partial(_mlp_body, seq_len, final),
        grid=(rows // r,),
        in_specs=[row(D_MODEL), _cond_spec(l, seq_len, cond_base), full(lw['conv_ff']),
                  full(lw['w_up']), full(lw['w_down']),
                  pl.BlockSpec((1, D_MODEL), lambda i: (0, 0)), hp, hc, hn],
        out_specs=row(D_MODEL),
        out_shape=jax.ShapeDtypeStruct((rows, D_MODEL), F32),
        scratch_shapes=[pltpu.VMEM((r + 2 * BF16_ROWS, D_MODEL), BF16), pltpu.VMEM((r, D_FF), BF16)],
        compiler_params=pltpu.CompilerParams(
            dimension_semantics=("arbitrary",), vmem_limit_bytes=VMEM_LIMIT),
        name="mlp",
    )(x1, mod, lw['conv_ff'], lw['w_up'], lw['w_down'], final_norm, h2, h2, h2)


def _pad_cols(a, n):
    return jnp.pad(a, [(0, 0)] * (a.ndim - 1) + [(0, n - a.shape[-1])])


def _perm_heads(a, axis):
    h = [lax.slice_in_dim(a, k * HEAD_DIM, (k + 1) * HEAD_DIM, axis=axis) for k in range(4)]
    return jnp.concatenate([h[0], h[2], h[1], h[3]], axis=axis)


def _prep_weights(w_in, mla_wq_b, mla_wkv_b, w_out, w_up, conv_ff, w_down, norm1, norm2, conv_a,
                  mla_q_norm, mla_kv_norm, gqa_q_norm, gqa_k_norm):
    L = DEPTH
    a = w_in[..., 0:768]
    cq = _pad_cols(w_in[..., 768:960], 256)
    ckv = w_in[..., 960:1088]
    kpe = _pad_cols(w_in[..., 1088:1120], LANES)
    qc = _perm_heads(w_in[..., 1120:1376], 2)
    kvc = w_in[..., 1376:1632]
    qd = _perm_heads(w_in[..., 1632:1888], 2)
    kvd = w_in[..., 1888:2144]
    w_in_p = jnp.concatenate([a, cq, ckv, kpe, qc, kvc, qd, kvd], axis=-1).astype(BF16)

    wq = mla_wq_b.reshape(L, MLA_Q_LORA, MLA_HEADS, MLA_NOPE + MLA_ROPE)
    wq = jnp.concatenate([wq[..., MLA_NOPE:], wq[..., :MLA_NOPE],
                          jnp.zeros((L, MLA_Q_LORA, MLA_HEADS, LANES - MLA_NOPE - MLA_ROPE), F32)], axis=-1)
    wq = jnp.pad(wq.reshape(L, MLA_Q_LORA, MLA_HEADS * LANES), ((0, 0), (0, 256 - MLA_Q_LORA), (0, 0)))

    wkv = mla_wkv_b.reshape(L, MLA_KV_LORA, MLA_HEADS, MLA_NOPE + MLA_V)
    zk = jnp.zeros((L, MLA_KV_LORA, MLA_HEADS, MLA_ROPE), F32)
    wk = jnp.concatenate([zk, wkv[..., :MLA_NOPE], zk], axis=-1).reshape(L, MLA_KV_LORA, MLA_HEADS * LANES)
    wv = wkv[..., MLA_NOPE:].reshape(L, MLA_KV_LORA, MLA_HEADS * MLA_V)
    wkv_p = jnp.concatenate([wk, wv], axis=-1)

    wo = jnp.concatenate([w_out[:, 0:512], _perm_heads(w_out[:, 512:768], 1),
                          _perm_heads(w_out[:, 768:1024], 1)], axis=1)

    hm = jnp.kron(jnp.eye(2 * LANES // HEAD_DIM, dtype=F32), jnp.ones((HEAD_DIM, HEAD_DIM), F32)).astype(BF16)
    return {
        'w_in': w_in_p, 'wq': wq.astype(BF16), 'wkv': wkv_p.astype(BF16), 'w_out': wo.astype(BF16),
        'w_up': w_up.astype(BF16), 'conv_ff': conv_ff, 'w_down': w_down.astype(BF16),
        'norm1': norm1.reshape(L, 1, D_MODEL), 'norm2': norm2.reshape(L, 1, D_MODEL), 'conv_a': conv_a,
        'gq': _pad_cols(mla_q_norm, 256).reshape(L, 1, 256), 'gkv': mla_kv_norm.reshape(L, 1, MLA_KV_LORA),
        'gqc': jnp.tile(gqa_q_norm, (1, 4)).reshape(L, 1, 256), 'gkc': jnp.tile(gqa_k_norm, (1, 2)).reshape(L, 1, 128),
        'hm': hm,
    }


def _rope_tables(t):
    rows = t // GRID_W
    row = jnp.repeat(jnp.arange(rows, dtype=F32), GRID_W)
    col = jnp.tile(jnp.arange(GRID_W, dtype=F32), rows)

    def tabs(dim):
        half = dim // 2
        inv = jnp.power(ROPE_THETA, -jnp.arange(0, half, 2, dtype=F32) / half)
        ar = row[:, None] * inv
        ac = col[:, None] * inv
        c = jnp.concatenate([jnp.cos(ar), jnp.cos(ar), jnp.cos(ac), jnp.cos(ac)], axis=1)
        s = jnp.concatenate([-jnp.sin(ar), jnp.sin(ar), -jnp.sin(ac), jnp.sin(ac)], axis=1)
        return c, s

    c64, s64 = tabs(HEAD_DIM)
    c32, s32 = tabs(MLA_ROPE)
    c64 = jnp.tile(c64, (1, 2))
    s64 = jnp.tile(s64, (1, 2))
    c32 = jnp.concatenate([c32, jnp.ones((t, LANES - MLA_ROPE), F32)], axis=1)
    s32 = _pad_cols(s32, LANES)
    return c64, s64, c32, s32


def kernel(x_prompt, x_sample, cache_mla_ckv, cache_mla_kpe, cache_gqa_k, cache_gqa_v, cache_swa_k, cache_swa_v,
           c, c_ctx, w_ada, b_ada, norm1, w_in, conv_a, mla_q_norm, mla_wq_b, mla_kv_norm, mla_wkv_b,
           gqa_q_norm, gqa_k_norm, swa_sink, w_out, norm2, w_up, conv_ff, w_down, final_norm):
    B, T, _ = x_prompt.shape
    DB, DT, _ = x_sample.shape
    past = cache_mla_ckv.shape[2]

    lw = _prep_weights(w_in, mla_wq_b, mla_wkv_b, w_out, w_up, conv_ff, w_down, norm1, norm2, conv_a,
                       mla_q_norm, mla_kv_norm, gqa_q_norm, gqa_k_norm)
    rope_tabs = _rope_tables(DT)
    fnorm = final_norm.reshape(1, D_MODEL)

    cond8 = jnp.zeros((8, D_MODEL), F32).at[0].set(c_ctx).at[1:1 + DB].set(c)
    mod = _ada(cond8, w_ada, b_ada).reshape(DEPTH, 8, 6, D_MODEL)

    kx, vx = _ctx_mla(cache_mla_ckv, _pad_cols(cache_mla_kpe, LANES), lw['wkv'])
    kcx = cache_gqa_k.reshape(DB, DEPTH, past, 2 * HEAD_DIM).astype(BF16)
    vcx = cache_gqa_v.reshape(DB, DEPTH, past, 2 * HEAD_DIM).astype(BF16)
    kdx = cache_swa_k.reshape(DB, DEPTH, past, 2 * HEAD_DIM).astype(BF16)
    vdx = cache_swa_v.reshape(DB, DEPTH, past, 2 * HEAD_DIM).astype(BF16)

    xp = x_prompt.reshape(B * T, D_MODEL)
    xs = x_sample.reshape(DB * DT, D_MODEL)
    caches = []
    for l in range(DEPTH):
        final = l == DEPTH - 1
        (z, gb, qm, km, vm, qc, kc, vc, qd, kd, vd, *cache_l) = _inproj(xp, mod, lw, l, T, 0, None)
        caches.append(cache_l)
        yb, yc, yd = _attn_ctx(l, T, swa_sink, qm, km, vm, qc, kc, vc, qd, kd, vd)
        x1, h2 = _outproj(xp, mod, lw, l, T, 0, z, gb, yb, yc, yd)
        xp = _mlp(x1, h2, mod, lw, l, T, 0, fnorm, final)
        (z, gb, qmr, qm, km, vm, qcr, qc, kc, vc, qdr, qd, kd, vd) = _inproj(xs, mod, lw, l, DT, 1, rope_tabs)
        yb, yc, yd = _attn_lat(l, DB, DT, swa_sink, qmr, qm, km, vm, kx, vx, qcr, qc, kc, vc, kcx, vcx,
                               qdr, qd, kd, vd, kdx, vdx)
        x1, h2 = _outproj(xs, mod, lw, l, DT, 1, z, gb, yb, yc, yd)
        xs = _mlp(x1, h2, mod, lw, l, DT, 1, fnorm, final)

    def stack(idx, shape):
        return jnp.stack([caches[l][idx].reshape((B, T) + shape) for l in range(DEPTH)], axis=1)

    return (xp.reshape(B, T, D_MODEL), xs.reshape(DB, DT, D_MODEL),
            stack(0, (MLA_KV_LORA,)), stack(1, (MLA_ROPE,)),
            stack(2, (2, HEAD_DIM)), stack(3, (2, HEAD_DIM)), stack(4, (2, HEAD_DIM)), stack(5, (2, HEAD_DIM)))
```

```python
import functools

import jax
import jax.numpy as jnp
import numpy as np
from jax import lax
from jax.experimental import pallas as pl
from jax.experimental.pallas import tpu as pltpu

F32 = jnp.float32
BF16 = jnp.bfloat16

D_MODEL = 1024
DEPTH = 2
GRID_W = 64
HEAD_DIM = 64
GROUP_WIDTH = D_MODEL // 4
MLA_HEADS = 4
MLA_NOPE = 64
MLA_ROPE = 32
MLA_V = 64
MLA_Q_LORA = 192
MLA_KV_LORA = 128
WINDOW = 128
D_FF = 2816
ROPE_THETA = 10000.0
EPS = 1e-6
NEG_INF = -1e30
ATTN_SCALE = HEAD_DIM ** -0.5
MLA_SCALE = (MLA_NOPE + MLA_ROPE) ** -0.5

LANES = 128
BF16_ROWS = 16
ROW_TILE = 512
FF_TILE = 256
N_FF_TILES = D_FF // FF_TILE
DOWN_CHUNK = 4
IN_COLS_PACKED = 2304
VMEM_LIMIT = 56 * 1024 * 1024

_NT = (((1,), (1,)), ((), ()))


def _dot(a, b):
    return jnp.dot(a, b, preferred_element_type=F32)


def _dot_nt(a, b):
    return lax.dot_general(a, b, _NT, preferred_element_type=F32)


def _rms(x, g, n):
    ms = jnp.sum(x * x, axis=-1, keepdims=True) * (1.0 / n)
    return x * lax.rsqrt(ms + EPS) * g


def _lane(shape):
    return lax.broadcasted_iota(jnp.int32, shape, len(shape) - 1)


def _rope(x, c, s, half):
    w = x.shape[-1]
    lo = (_lane(x.shape) % (2 * half)) < half
    sw = jnp.where(lo, pltpu.roll(x, w - half, 1), pltpu.roll(x, half, 1))
    return x * c + sw * s


def _shift_rows(zz, halo, rows):
    n = zz.shape[0]
    prev = pltpu.roll(zz, 1, 0)[halo:halo + rows]
    nxt = pltpu.roll(zz, n - 1, 0)[halo:halo + rows]
    return prev, nxt


def _seq_pos(tile, rows, seq_len):
    r = lax.broadcasted_iota(jnp.int32, (rows, 1), 0)
    return (tile * rows + r) % seq_len


def _ada_body(c_ref, w_ref, b_ref, o_ref):
    c = c_ref[...]
    s = c * jax.nn.sigmoid(c)
    o_ref[...] = jnp.dot(s, w_ref[...], precision=lax.Precision.HIGHEST,
                         preferred_element_type=F32) + b_ref[...]


def _ada(cond8, w_ada, b_ada):
    tn = 1536
    n = 6 * D_MODEL
    return pl.pallas_call(
        _ada_body,
        grid=(DEPTH, n // tn),
        in_specs=[pl.BlockSpec((8, D_MODEL), lambda l, j: (0, 0)),
                  pl.BlockSpec((None, D_MODEL, tn), lambda l, j: (l, 0, j)),
                  pl.BlockSpec((None, 1, tn), lambda l, j: (l, 0, j))],
        out_specs=pl.BlockSpec((None, 8, tn), lambda l, j: (l, 0, j)),
        out_shape=jax.ShapeDtypeStruct((DEPTH, 8, n), F32),
        compiler_params=pltpu.CompilerParams(
            dimension_semantics=("arbitrary", "arbitrary"), vmem_limit_bytes=VMEM_LIMIT),
        name="ada",
    )(cond8, w_ada, b_ada.reshape(DEPTH, 1, n))


def _ctx_body(ckv_ref, kpe_ref, w_ref, kx_ref, vx_ref):
    kv = _dot(ckv_ref[...].astype(BF16), w_ref[...])
    kpe = kpe_ref[...]
    kx_ref[...] = (kv[:, :4 * LANES] + jnp.concatenate([kpe] * MLA_HEADS, axis=1)).astype(BF16)
    vx_ref[...] = kv[:, 4 * LANES:].astype(BF16)


def _ctx_mla(cache_ckv, cache_kpe_pad, wkv_p):
    b, _, s, _ = cache_ckv.shape
    return pl.pallas_call(
        _ctx_body,
        grid=(DEPTH, b),
        in_specs=[pl.BlockSpec((None, None, s, MLA_KV_LORA), lambda l, i: (i, l, 0, 0)),
                  pl.BlockSpec((None, None, s, LANES), lambda l, i: (i, l, 0, 0)),
                  pl.BlockSpec((None, MLA_KV_LORA, 6 * LANES), lambda l, i: (l, 0, 0))],
        out_specs=[pl.BlockSpec((None, None, s, 4 * LANES), lambda l, i: (l, i, 0, 0)),
                   pl.BlockSpec((None, None, s, 2 * LANES), lambda l, i: (l, i, 0, 0))],
        out_shape=[jax.ShapeDtypeStruct((DEPTH, b, s, 4 * LANES), BF16),
                   jax.ShapeDtypeStruct((DEPTH, b, s, 2 * LANES), BF16)],
        compiler_params=pltpu.CompilerParams(
            dimension_semantics=("arbitrary", "arbitrary"), vmem_limit_bytes=VMEM_LIMIT),
        name="ctx_mla",
    )(cache_ckv, cache_kpe_pad, wkv_p)


def _inproj_body(rope, n_alias, x_ref, mod_ref, n1_ref, w_ref, wq_ref, wkv_ref, gq_ref, gkv_ref, gqc_ref, gkc_ref,
                 hm_ref, *rest):
    if rope:
        c64_ref, s64_ref, c32_ref, s32_ref = rest[:4]
        outs = rest[4:]
    else:
        outs = rest[n_alias:]
    x = x_ref[...]
    sh1 = mod_ref[0:1, :]
    sc1 = mod_ref[1:2, :]
    h = _rms(x, n1_ref[...], D_MODEL) * (1.0 + sc1) + sh1
    acc = _dot(h.astype(BF16), w_ref[...])

    xa, gb, gc = acc[:, 0:256], acc[:, 256:512], acc[:, 512:768]
    z = gc * xa
    cq = acc[:, 768:1024]
    cqn = _rms(cq, gq_ref[...], MLA_Q_LORA)
    qm = _dot(cqn.astype(BF16), wq_ref[...]) * MLA_SCALE
    ckvn = _rms(acc[:, 1024:1152], gkv_ref[...], MLA_KV_LORA)
    kv = _dot(ckvn.astype(BF16), wkv_ref[...])
    kpe = acc[:, 1152:1280]
    vm = kv[:, 4 * LANES:]

    hm = hm_ref[...]

    def head_rms(v, g):
        v2 = v * v
        hi = v2.astype(BF16)
        lo = (v2 - hi.astype(F32)).astype(BF16)
        w = v.shape[-1]
        ss = _dot(hi, hm[:w, :w]) + _dot(lo, hm[:w, :w])
        return v * lax.rsqrt(ss * (1.0 / HEAD_DIM) + EPS) * g

    qc = head_rms(acc[:, 1280:1536], gqc_ref[...]) * ATTN_SCALE
    kc = head_rms(acc[:, 1536:1664], gkc_ref[...])
    vc = acc[:, 1664:1792]
    qd = acc[:, 1792:2048] * ATTN_SCALE
    kd = acc[:, 2048:2176]
    vd = acc[:, 2176:2304]

    if not rope:
        (z_o, gb_o, qm_o, km_o, vm_o, qc_o, kc_o, vc_o, qd_o, kd_o, vd_o,
         ckv_c, kpe_c, kc_c, vc_c, kd_c, vd_c) = outs
        km = kv[:, :4 * LANES] + jnp.concatenate([kpe] * MLA_HEADS, axis=1)
        z_o[...] = z.astype(BF16)
        gb_o[...] = gb.astype(BF16)
        qm_o[...] = qm.astype(BF16)
        km_o[...] = km.astype(BF16)
        vm_o[...] = vm.astype(BF16)
        qc_o[...] = qc.astype(BF16)
        kc_o[...] = kc.astype(BF16)
        vc_o[...] = vc.astype(BF16)
        qd_o[...] = qd.astype(BF16)
        kd_o[...] = kd.astype(BF16)
        vd_o[...] = vd.astype(BF16)
        for o, v in ((ckv_c, ckvn), (kpe_c, kpe[:, :MLA_ROPE]), (kc_c, kc), (vc_c, vc), (kd_c, kd), (vd_c, vd)):
            o[...] = v.reshape(o.shape)
    else:
        (z_o, gb_o, qmr_o, qm_o, km_o, vm_o, qcr_o, qc_o, kc_o, vc_o, qdr_o, qd_o, kd_o, vd_o) = outs
        c64, s64, c32, s32 = c64_ref[...], s64_ref[...], c32_ref[...], s32_ref[...]
        rope64 = lambda v: jnp.concatenate(
            [_rope(v[:, i:i + LANES], c64, s64, 16) for i in range(0, v.shape[-1], LANES)], axis=1)
        rope32 = lambda v: jnp.concatenate(
            [_rope(v[:, i:i + LANES], c32, s32, 8) for i in range(0, v.shape[-1], LANES)], axis=1)
        km = kv[:, :4 * LANES] + jnp.concatenate([_rope(kpe, c32, s32, 8)] * MLA_HEADS, axis=1)
        z_o[...] = z.astype(BF16)
        gb_o[...] = gb.astype(BF16)
        qmr_o[...] = rope32(qm).astype(BF16)
        qm_o[...] = qm.astype(BF16)
        km_o[...] = km.astype(BF16)
        vm_o[...] = vm.astype(BF16)
        qcr_o[...] = rope64(qc).astype(BF16)
        qc_o[...] = qc.astype(BF16)
        kc_o[...] = rope64(kc).astype(BF16)
        vc_o[...] = vc.astype(BF16)
        qdr_o[...] = rope64(qd).astype(BF16)
        qd_o[...] = qd.astype(BF16)
        kd_o[...] = rope64(kd).astype(BF16)
        vd_o[...] = vd.astype(BF16)


def _inproj(x, mod, lw, l, seq_len, cond_base, rope_tabs, prev_caches=None):
    rows = x.shape[0]
    r = ROW_TILE
    rope = rope_tabs is not None
    aliases = {}
    tiles_per_seq = max(seq_len // r, 1)
    step = 1 if seq_len >= r else 0

    def cond_map(i):
        return (l, cond_base + step * (i // tiles_per_seq), 0, 0)

    row = lambda w: pl.BlockSpec((r, w), lambda i: (i, 0))
    full = lambda a: pl.BlockSpec((None,) + a.shape[1:], lambda i: (l,) + (0,) * (a.ndim - 1))
    in_specs = [row(D_MODEL),
                pl.BlockSpec((None, None, 6, D_MODEL), cond_map),
                full(lw['norm1']), full(lw['w_in']), full(lw['wq']), full(lw['wkv']),
                full(lw['gq']), full(lw['gkv']), full(lw['gqc']), full(lw['gkc']),
                pl.BlockSpec((2 * LANES, 2 * LANES), lambda i: (0, 0))]
    args = [x, mod, lw['norm1'], lw['w_in'], lw['wq'], lw['wkv'], lw['gq'], lw['gkv'], lw['gqc'], lw['gkc'],
            lw['hm']]
    bf = lambda w: jax.ShapeDtypeStruct((rows, w), BF16)
    f32 = lambda w: jax.ShapeDtypeStruct((rows, w), F32)
    if rope:
        tab = pl.BlockSpec((r, LANES), lambda i: (i % tiles_per_seq, 0))
        in_specs += [tab] * 4
        args += list(rope_tabs)
        widths = [256, 256, 512, 512, 512, 256, 256, 256, 128, 128, 256, 256, 128, 128]
        out_shape = [bf(w) for w in widths]
        out_specs = [row(w) for w in widths]
    else:
        widths = [256, 256, 512, 512, 256, 256, 128, 128, 256, 128, 128]
        cwidths = [128, MLA_ROPE, 128, 128, 128, 128]
        nb = r // seq_len
        out_shape = [bf(w) for w in widths] + [
            jax.ShapeDtypeStruct((rows // seq_len, DEPTH, seq_len, w), F32) for w in cwidths]
        out_specs = [row(w) for w in widths] + [
            pl.BlockSpec((nb, None, seq_len, w), lambda i: (i, l, 0, 0)) for w in cwidths]
        if prev_caches is not None:
            aliases = {len(args) + k: len(widths) + k for k in range(len(cwidths))}
            in_specs += [pl.BlockSpec(memory_space=pl.ANY)] * len(cwidths)
            args += list(prev_caches)
    return pl.pallas_call(
        functools.partial(_inproj_body, rope, len(aliases)),
        grid=(rows // r,),
        in_specs=in_specs, out_specs=out_specs, out_shape=out_shape,
        input_output_aliases=aliases,
        compiler_params=pltpu.CompilerParams(
            dimension_semantics=("arbitrary",), vmem_limit_bytes=VMEM_LIMIT),
        name="inproj_rope" if rope else "inproj",
    )(*args)


def _softmax_pv(s_list, v_list, extra=None):
    m = s_list[0].max(axis=-1, keepdims=True)
    for s in s_list[1:]:
        m = jnp.maximum(m, s.max(axis=-1, keepdims=True))
    if extra is not None:
        m = jnp.maximum(m, extra)
    den = jnp.exp(extra - m) if extra is not None else 0.0
    out = None
    for s, v in zip(s_list, v_list):
        p = jnp.exp(s - m)
        den = den + p.sum(axis=-1, keepdims=True)
        pv = _dot(p.astype(BF16), v)
        out = pv if out is None else out + pv
    return out / den


def _half_masks(q):
    lo = _lane(q.shape) < HEAD_DIM
    zero = jnp.zeros_like(q)
    return jnp.where(lo, q, zero), jnp.where(lo, zero, q)


def _merge_halves(a, b):
    return jnp.where(_lane(a.shape) < HEAD_DIM, a, b)


def _sink_col(sink_ref, l, h0, h1, tq):
    r = lax.broadcasted_iota(jnp.int32, (2 * tq, 1), 0)
    return jnp.where(r < tq, sink_ref[l, h0], sink_ref[l, h1])


def _attn_ctx_body(l, nb, t, sink_ref, qm_ref, km_ref, vm_ref, qc_ref, kc_ref, vc_ref, qd_ref, kd_ref, vd_ref,
                   yb_ref, yc_ref, yd_ref):
    for b in range(nb):
        rs = slice(b * t, (b + 1) * t)
        o = []
        for h in range(MLA_HEADS):
            cs = slice(h * LANES, (h + 1) * LANES)
            vs = slice((h // 2) * LANES, (h // 2 + 1) * LANES)
            s = _dot_nt(qm_ref[rs, cs], km_ref[rs, cs])
            o.append(_softmax_pv([s], [vm_ref[rs, vs]]))
        yb_ref[rs, 0:LANES] = _merge_halves(o[0], o[1]).astype(BF16)
        yb_ref[rs, LANES:] = _merge_halves(o[2], o[3]).astype(BF16)

        for q_ref, k_ref, v_ref, y_ref, use_sink in ((qc_ref, kc_ref, vc_ref, yc_ref, False),
                                                      (qd_ref, kd_ref, vd_ref, yd_ref, True)):
            qa_lo, qa_hi = _half_masks(q_ref[rs, 0:LANES])
            qb_lo, qb_hi = _half_masks(q_ref[rs, LANES:])
            k = k_ref[rs, :]
            v = v_ref[rs, :]
            res = []
            for j, (qa, qb) in enumerate(((qa_lo, qb_lo), (qa_hi, qb_hi))):
                s = _dot_nt(jnp.concatenate([qa, qb], axis=0), k)
                extra = _sink_col(sink_ref, l, 2 * j, 2 * j + 1, t) if use_sink else None
                res.append(_softmax_pv([s], [v], extra))
            y_ref[rs, 0:LANES] = _merge_halves(res[0][:t], res[1][:t]).astype(BF16)
            y_ref[rs, LANES:] = _merge_halves(res[0][t:], res[1][t:]).astype(BF16)


def _attn_ctx(l, t, sink, qm, km, vm, qc, kc, vc, qd, kd, vd):
    rows = qm.shape[0]
    nb = 2
    r = nb * t
    row = lambda a: pl.BlockSpec((r, a.shape[1]), lambda i: (i, 0))
    ins = [qm, km, vm, qc, kc, vc, qd, kd, vd]
    return pl.pallas_call(
        functools.partial(_attn_ctx_body, l, nb, t),
        grid=(rows // r,),
        in_specs=[pl.BlockSpec(memory_space=pltpu.SMEM)] + [row(a) for a in ins],
        out_specs=[pl.BlockSpec((r, 2 * LANES), lambda i: (i, 0))] * 3,
        out_shape=[jax.ShapeDtypeStruct((rows, 2 * LANES), BF16)] * 3,
        compiler_params=pltpu.CompilerParams(
            dimension_semantics=("arbitrary",), vmem_limit_bytes=VMEM_LIMIT),
        name="attn_ctx",
    )(sink, *ins)


def _attn_lat_body(l, tq, t, sink_ref, qmr_ref, qm_ref, km_ref, vm_ref, kx_ref, vx_ref,
                   qcr_ref, qc_ref, kc_ref, vc_ref, kcx_ref, vcx_ref,
                   qdr_ref, qd_ref, kd_ref, vd_ref, kdx_ref, vdx_ref,
                   yb_ref, yc_ref, yd_ref):
    i = pl.program_id(1)
    o = []
    for h in range(MLA_HEADS):
        cs = slice(h * LANES, (h + 1) * LANES)
        vs = slice((h // 2) * LANES, (h // 2 + 1) * LANES)
        s1 = _dot_nt(qmr_ref[:, cs], km_ref[:, cs])
        s2 = _dot_nt(qm_ref[:, cs], kx_ref[:, cs])
        o.append(_softmax_pv([s1, s2], [vm_ref[:, vs], vx_ref[:, vs]]))
    yb_ref[:, 0:LANES] = _merge_halves(o[0], o[1]).astype(BF16)
    yb_ref[:, LANES:] = _merge_halves(o[2], o[3]).astype(BF16)

    ra_lo, ra_hi = _half_masks(qcr_ref[:, 0:LANES])
    rb_lo, rb_hi = _half_masks(qcr_ref[:, LANES:])
    qa_lo, qa_hi = _half_masks(qc_ref[:, 0:LANES])
    qb_lo, qb_hi = _half_masks(qc_ref[:, LANES:])
    res = []
    for (ra, rb, qa, qb) in ((ra_lo, rb_lo, qa_lo, qb_lo), (ra_hi, rb_hi, qa_hi, qb_hi)):
        s1 = _dot_nt(jnp.concatenate([ra, rb], axis=0), kc_ref[...])
        s2 = _dot_nt(jnp.concatenate([qa, qb], axis=0), kcx_ref[...])
        res.append(_softmax_pv([s1, s2], [vc_ref[...], vcx_ref[...]]))
    yc_ref[:, 0:LANES] = _merge_halves(res[0][:tq], res[1][:tq]).astype(BF16)
    yc_ref[:, LANES:] = _merge_halves(res[0][tq:], res[1][tq:]).astype(BF16)

    wk = tq + 2 * WINDOW
    start = pl.multiple_of(jnp.clip(i * tq - WINDOW, 0, t - wk), WINDOW)
    kw = kd_ref[pl.ds(start, wk), :]
    vw = vd_ref[pl.ds(start, wk), :]
    qpos = i * tq + lax.broadcasted_iota(jnp.int32, (2 * tq, wk), 0) % tq
    kpos = start + lax.broadcasted_iota(jnp.int32, (2 * tq, wk), 1)
    valid = jnp.abs(kpos - qpos) <= WINDOW
    ra_lo, ra_hi = _half_masks(qdr_ref[:, 0:LANES])
    rb_lo, rb_hi = _half_masks(qdr_ref[:, LANES:])
    qa_lo, qa_hi = _half_masks(qd_ref[:, 0:LANES])
    qb_lo, qb_hi = _half_masks(qd_ref[:, LANES:])
    res = []
    for j, (ra, rb, qa, qb) in enumerate(((ra_lo, rb_lo, qa_lo, qb_lo), (ra_hi, rb_hi, qa_hi, qb_hi))):
        s1 = jnp.where(valid, _dot_nt(jnp.concatenate([ra, rb], axis=0), kw), NEG_INF)
        s2 = _dot_nt(jnp.concatenate([qa, qb], axis=0), kdx_ref[...])
        extra = _sink_col(sink_ref, l, 2 * j, 2 * j + 1, tq)
        res.append(_softmax_pv([s1, s2], [vw, vdx_ref[...]], extra))
    yd_ref[:, 0:LANES] = _merge_halves(res[0][:tq], res[1][:tq]).astype(BF16)
    yd_ref[:, LANES:] = _merge_halves(res[0][tq:], res[1][tq:]).astype(BF16)


def _attn_lat(l, nbatch, t, sink, qmr, qm, km, vm, kx, vx, qcr, qc, kc, vc, kcx, vcx, qdr, qd, kd, vd, kdx, vdx):
    tq = 256
    nq = t // tq
    rows = qm.shape[0]
    qs = lambda a: pl.BlockSpec((tq, a.shape[1]), lambda b, i: (b * nq + i, 0))
    ks = lambda a: pl.BlockSpec((t, a.shape[1]), lambda b, i: (b, 0))
    xs = lambda a: pl.BlockSpec((None, None) + a.shape[2:], lambda b, i: (l, b, 0, 0))
    cs = lambda a: pl.BlockSpec((None, None) + a.shape[2:], lambda b, i: (b, l, 0, 0))
    in_specs = [pl.BlockSpec(memory_space=pltpu.SMEM),
                qs(qmr), qs(qm), ks(km), ks(vm), xs(kx), xs(vx),
                qs(qcr), qs(qc), ks(kc), ks(vc), cs(kcx), cs(vcx),
                qs(qdr), qs(qd), ks(kd), ks(vd), cs(kdx), cs(vdx)]
    return pl.pallas_call(
        functools.partial(_attn_lat_body, l, tq, t),
        grid=(nbatch, nq),
        in_specs=in_specs,
        out_specs=[pl.BlockSpec((tq, 2 * LANES), lambda b, i: (b * nq + i, 0))] * 3,
        out_shape=[jax.ShapeDtypeStruct((rows, 2 * LANES), BF16)] * 3,
        compiler_params=pltpu.CompilerParams(
            dimension_semantics=("arbitrary", "arbitrary"), vmem_limit_bytes=VMEM_LIMIT),
        name="attn_lat",
    )(sink, qmr, qm, km, vm, kx, vx, qcr, qc, kc, vc, kcx, vcx, qdr, qd, kd, vd, kdx, vdx)


def _outproj_body(seq_len, x_ref, mod_ref, n2_ref, cw_ref, wo_ref, zp_ref, z_ref, zn_ref, gb_ref,
                  yb_ref, yc_ref, yd_ref, x1_ref, h2_ref):
    r = x_ref.shape[0]
    i = pl.program_id(0)
    zz = jnp.concatenate([zp_ref[...], z_ref[...], zn_ref[...]], axis=0).astype(F32)
    prev, nxt = _shift_rows(zz, BF16_ROWS, r)
    pos = _seq_pos(i, r, seq_len)
    prev = jnp.where(pos == 0, 0.0, prev)
    nxt = jnp.where(pos == seq_len - 1, 0.0, nxt)
    cw = cw_ref[...]
    ya = gb_ref[...].astype(F32) * (prev * cw[0:1] + z_ref[...].astype(F32) * cw[1:2] + nxt * cw[2:3])
    ycat = jnp.concatenate([ya.astype(BF16), yb_ref[...], yc_ref[...], yd_ref[...]], axis=1)
    y = _dot(ycat, wo_ref[...])
    g1 = mod_ref[2:3, :]
    sh2 = mod_ref[3:4, :]
    sc2 = mod_ref[4:5, :]
    x1 = x_ref[...] + g1 * y
    x1_ref[...] = x1
    h2_ref[...] = (_rms(x1, n2_ref[...], D_MODEL) * (1.0 + sc2) + sh2).astype(BF16)


def _halo_specs(r, w, rows):
    nblk = rows // BF16_ROWS
    per = r // BF16_ROWS
    prev = pl.BlockSpec((BF16_ROWS, w), lambda i: (jnp.maximum(i * per - 1, 0), 0))
    cur = pl.BlockSpec((r, w), lambda i: (i, 0))
    nxt = pl.BlockSpec((BF16_ROWS, w), lambda i: (jnp.minimum((i + 1) * per, nblk - 1), 0))
    return prev, cur, nxt


def _cond_spec(l, seq_len, cond_base):
    r = ROW_TILE
    tiles_per_seq = max(seq_len // r, 1)
    step = 1 if seq_len >= r else 0
    return pl.BlockSpec((None, None, 6, D_MODEL),
                        lambda i: (l, cond_base + step * (i // tiles_per_seq), 0, 0))


def _outproj(x, mod, lw, l, seq_len, cond_base, z, gb, yb, yc, yd):
    rows = x.shape[0]
    r = ROW_TILE
    row = lambda w: pl.BlockSpec((r, w), lambda i: (i, 0))
    full = lambda a: pl.BlockSpec((None,) + a.shape[1:], lambda i: (l,) + (0,) * (a.ndim - 1))
    zp, zc, zn = _halo_specs(r, 2 * LANES, rows)
    return pl.pallas_call(
        functools.partial(_outproj_body, seq_len),
        grid=(rows // r,),
        in_specs=[row(D_MODEL), _cond_spec(l, seq_len, cond_base), full(lw['norm2']), full(lw['conv_a']),
                  full(lw['w_out']), zp, zc, zn, row(256), row(256), row(256), row(256)],
        out_specs=[row(D_MODEL), row(D_MODEL)],
        out_shape=[jax.ShapeDtypeStruct((rows, D_MODEL), F32), jax.ShapeDtypeStruct((rows, D_MODEL), BF16)],
        compiler_params=pltpu.CompilerParams(
            dimension_semantics=("arbitrary",), vmem_limit_bytes=VMEM_LIMIT),
        name="outproj",
    )(x, mod, lw['norm2'], lw['conv_a'], lw['w_out'], z, z, z, gb, yb, yc, yd)


def _mlp_body(seq_len, final, x1_ref, mod_ref, cw_ref, wup_ref, wdn_ref, fn_ref, hp_ref, h_ref, hn_ref,
              o_ref, hbuf, gbuf):
    r = x1_ref.shape[0]
    halo = BF16_ROWS
    i = pl.program_id(0)
    hbuf[0:halo, :] = hp_ref[...]
    hbuf[halo:halo + r, :] = h_ref[...]
    hbuf[halo + r:, :] = hn_ref[...]
    pos = _seq_pos(i, r, seq_len)
    first = pos == 0
    last = pos == seq_len - 1

    def up(j):
        ga = slice(j * FF_TILE, (j + 1) * FF_TILE)
        va = slice(D_FF + j * FF_TILE, D_FF + (j + 1) * FF_TILE)
        return _dot(hbuf[...], wup_ref[:, ga]), _dot(hbuf[...], wup_ref[:, va])

    def conv(u, cw):
        prev, nxt = _shift_rows(u, halo, r)
        return (jnp.where(first, 0.0, prev) * cw[0:1] + u[halo:halo + r] * cw[1:2]
                + jnp.where(last, 0.0, nxt) * cw[2:3])

    acc = None
    nxt_u = up(0)
    for j in range(N_FF_TILES):
        ua, ub = nxt_u
        if j + 1 < N_FF_TILES:
            nxt_u = up(j + 1)
        ga = slice(j * FF_TILE, (j + 1) * FF_TILE)
        va = slice(D_FF + j * FF_TILE, D_FF + (j + 1) * FF_TILE)
        ua = conv(ua, cw_ref[:, ga])
        ub = conv(ub, cw_ref[:, va])
        gbuf[:, ga] = (ua * jax.nn.sigmoid(ua) * ub).astype(BF16)
        if j % DOWN_CHUNK == DOWN_CHUNK - 1 or j == N_FF_TILES - 1:
            lo = (j // DOWN_CHUNK) * DOWN_CHUNK * FF_TILE
            hi = (j + 1) * FF_TILE
            d = _dot(gbuf[:, lo:hi], wdn_ref[lo:hi, :])
            acc = d if acc is None else acc + d
    g2 = mod_ref[5:6, :]
    x2 = x1_ref[...] + g2 * acc
    if final:
        x2 = _rms(x2, fn_ref[...], D_MODEL)
    o_ref[...] = x2


def _mlp(x1, h2, mod, lw, l, seq_len, cond_base, final_norm, final):
    rows = x1.shape[0]
    r = ROW_TILE
    row = lambda w: pl.BlockSpec((r, w), lambda i: (i, 0))
    full = lambda a: pl.BlockSpec((None,) + a.shape[1:], lambda i: (l,) + (0,) * (a.ndim - 1))
    hp, hc, hn = _halo_specs(r, D_MODEL, rows)
    return pl.pallas_call(
        functools.partial(_mlp_body, seq_len, final),
        grid=(rows // r,),
        in_specs=[row(D_MODEL), _cond_spec(l, seq_len, cond_base), full(lw['conv_ff']),
                  full(lw['w_up']), full(lw['w_down']),
                  pl.BlockSpec((1, D_MODEL), lambda i: (0, 0)), hp, hc, hn],
        out_specs=row(D_MODEL),
        out_shape=jax.ShapeDtypeStruct((rows, D_MODEL), F32),
        scratch_shapes=[pltpu.VMEM((r + 2 * BF16_ROWS, D_MODEL), BF16), pltpu.VMEM((r, D_FF), BF16)],
        compiler_params=pltpu.CompilerParams(
            dimension_semantics=("arbitrary",), vmem_limit_bytes=VMEM_LIMIT),
        name="mlp",
    )(x1, mod, lw['conv_ff'], lw['w_up'], lw['w_down'], final_norm, h2, h2, h2)


def _pad_cols(a, n):
    return jnp.pad(a, [(0, 0)] * (a.ndim - 1) + [(0, n - a.shape[-1])])


def _perm_heads(a, axis):
    h = [lax.slice_in_dim(a, k * HEAD_DIM, (k + 1) * HEAD_DIM, axis=axis) for k in range(4)]
    return jnp.concatenate([h[0], h[2], h[1], h[3]], axis=axis)


def _prep_weights(w_in, mla_wq_b, mla_wkv_b, w_out, w_up, conv_ff, w_down, norm1, norm2, conv_a,
                  mla_q_norm, mla_kv_norm, gqa_q_norm, gqa_k_norm):
    L = DEPTH
    a = w_in[..., 0:768]
    cq = _pad_cols(w_in[..., 768:960], 256)
    ckv = w_in[..., 960:1088]
    kpe = _pad_cols(w_in[..., 1088:1120], LANES)
    qc = _perm_heads(w_in[..., 1120:1376], 2)
    kvc = w_in[..., 1376:1632]
    qd = _perm_heads(w_in[..., 1632:1888], 2)
    kvd = w_in[..., 1888:2144]
    w_in_p = jnp.concatenate([a, cq, ckv, kpe, qc, kvc, qd, kvd], axis=-1).astype(BF16)

    wq = mla_wq_b.reshape(L, MLA_Q_LORA, MLA_HEADS, MLA_NOPE + MLA_ROPE)
    wq = jnp.concatenate([wq[..., MLA_NOPE:], wq[..., :MLA_NOPE],
                          jnp.zeros((L, MLA_Q_LORA, MLA_HEADS, LANES - MLA_NOPE - MLA_ROPE), F32)], axis=-1)
    wq = jnp.pad(wq.reshape(L, MLA_Q_LORA, MLA_HEADS * LANES), ((0, 0), (0, 256 - MLA_Q_LORA), (0, 0)))

    wkv = mla_wkv_b.reshape(L, MLA_KV_LORA, MLA_HEADS, MLA_NOPE + MLA_V)
    zk = jnp.zeros((L, MLA_KV_LORA, MLA_HEADS, MLA_ROPE), F32)
    wk = jnp.concatenate([zk, wkv[..., :MLA_NOPE], zk], axis=-1).reshape(L, MLA_KV_LORA, MLA_HEADS * LANES)
    wv = wkv[..., MLA_NOPE:].reshape(L, MLA_KV_LORA, MLA_HEADS * MLA_V)
    wkv_p = jnp.concatenate([wk, wv], axis=-1)

    wo = jnp.concatenate([w_out[:, 0:512], _perm_heads(w_out[:, 512:768], 1),
                          _perm_heads(w_out[:, 768:1024], 1)], axis=1)

    hm = jnp.asarray(np.kron(np.eye(2 * LANES // HEAD_DIM), np.ones((HEAD_DIM, HEAD_DIM))), BF16)
    return {
        'w_in': w_in_p, 'wq': wq.astype(BF16), 'wkv': wkv_p.astype(BF16), 'w_out': wo.astype(BF16),
        'w_up': w_up.astype(BF16), 'conv_ff': conv_ff, 'w_down': w_down.astype(BF16),
        'norm1': norm1.reshape(L, 1, D_MODEL), 'norm2': norm2.reshape(L, 1, D_MODEL), 'conv_a': conv_a,
        'gq': _pad_cols(mla_q_norm, 256).reshape(L, 1, 256), 'gkv': mla_kv_norm.reshape(L, 1, MLA_KV_LORA),
        'gqc': jnp.tile(gqa_q_norm, (1, 4)).reshape(L, 1, 256), 'gkc': jnp.tile(gqa_k_norm, (1, 2)).reshape(L, 1, 128),
        'hm': hm,
    }


def _rope_tables(t):
    rows = t // GRID_W
    row = np.repeat(np.arange(rows, dtype=np.float64), GRID_W)
    col = np.tile(np.arange(GRID_W, dtype=np.float64), rows)

    def tabs(dim):
        half = dim // 2
        inv = np.power(ROPE_THETA, -np.arange(0, half, 2, dtype=np.float64) / half)
        ar = row[:, None] * inv
        ac = col[:, None] * inv
        c = np.concatenate([np.cos(ar), np.cos(ar), np.cos(ac), np.cos(ac)], axis=1)
        s = np.concatenate([-np.sin(ar), np.sin(ar), -np.sin(ac), np.sin(ac)], axis=1)
        return c, s

    c64, s64 = tabs(HEAD_DIM)
    c32, s32 = tabs(MLA_ROPE)
    c64 = np.tile(c64, (1, 2))
    s64 = np.tile(s64, (1, 2))
    c32 = np.concatenate([c32, np.ones((t, LANES - MLA_ROPE))], axis=1)
    s32 = np.concatenate([s32, np.zeros((t, LANES - MLA_ROPE))], axis=1)
    return tuple(jnp.asarray(a.astype(np.float32)) for a in (c64, s64, c32, s32))


def kernel(x_prompt, x_sample, cache_mla_ckv, cache_mla_kpe, cache_gqa_k, cache_gqa_v, cache_swa_k, cache_swa_v,
           c, c_ctx, w_ada, b_ada, norm1, w_in, conv_a, mla_q_norm, mla_wq_b, mla_kv_norm, mla_wkv_b,
           gqa_q_norm, gqa_k_norm, swa_sink, w_out, norm2, w_up, conv_ff, w_down, final_norm):
    B, T, _ = x_prompt.shape
    DB, DT, _ = x_sample.shape
    past = cache_mla_ckv.shape[2]

    lw = _prep_weights(w_in, mla_wq_b, mla_wkv_b, w_out, w_up, conv_ff, w_down, norm1, norm2, conv_a,
                       mla_q_norm, mla_kv_norm, gqa_q_norm, gqa_k_norm)
    rope_tabs = _rope_tables(DT)
    fnorm = final_norm.reshape(1, D_MODEL)

    cond8 = jnp.concatenate([c_ctx[None, :], c, jnp.zeros((8 - 1 - DB, D_MODEL), F32)], axis=0)
    mod = _ada(cond8, w_ada, b_ada).reshape(DEPTH, 8, 6, D_MODEL)

    kx, vx = _ctx_mla(cache_mla_ckv, _pad_cols(cache_mla_kpe, LANES), lw['wkv'])
    kcx = cache_gqa_k.reshape(DB, DEPTH, past, 2 * HEAD_DIM).astype(BF16)
    vcx = cache_gqa_v.reshape(DB, DEPTH, past, 2 * HEAD_DIM).astype(BF16)
    kdx = cache_swa_k.reshape(DB, DEPTH, past, 2 * HEAD_DIM).astype(BF16)
    vdx = cache_swa_v.reshape(DB, DEPTH, past, 2 * HEAD_DIM).astype(BF16)

    xp = x_prompt.reshape(B * T, D_MODEL)
    xs = x_sample.reshape(DB * DT, D_MODEL)
    caches = None
    for l in range(DEPTH):
        final = l == DEPTH - 1
        (z, gb, qm, km, vm, qc, kc, vc, qd, kd, vd, *caches) = _inproj(xp, mod, lw, l, T, 0, None, caches)
        yb, yc, yd = _attn_ctx(l, T, swa_sink, qm, km, vm, qc, kc, vc, qd, kd, vd)
        x1, h2 = _outproj(xp, mod, lw, l, T, 0, z, gb, yb, yc, yd)
        xp = _mlp(x1, h2, mod, lw, l, T, 0, fnorm, final)
        (z, gb, qmr, qm, km, vm, qcr, qc, kc, vc, qdr, qd, kd, vd) = _inproj(xs, mod, lw, l, DT, 1, rope_tabs)
        yb, yc, yd = _attn_lat(l, DB, DT, swa_sink, qmr, qm, km, vm, kx, vx, qcr, qc, kc, vc, kcx, vcx,
                               qdr, qd, kd, vd, kdx, vdx)
        x1, h2 = _outproj(xs, mod, lw, l, DT, 1, z, gb, yb, yc, yd)
        xs = _mlp(x1, h2, mod, lw, l, DT, 1, fnorm, final)

    heads = lambda a: a.reshape(B, DEPTH, T, 2, HEAD_DIM)
    return (xp.reshape(B, T, D_MODEL), xs.reshape(DB, DT, D_MODEL), caches[0], caches[1],
            heads(caches[2]), heads(caches[3]), heads(caches[4]), heads(caches[5]))
```

```python
import functools
import math

import jax
import jax.numpy as jnp
import numpy as np
from jax import lax
from jax.experimental import pallas as pl
from jax.experimental.pallas import tpu as pltpu

F32 = jnp.float32
BF16 = jnp.bfloat16

D_MODEL = 1024
DEPTH = 2
GRID_W = 64
HEAD_DIM = 64
GROUP_WIDTH = D_MODEL // 4
MLA_HEADS = 4
MLA_NOPE = 64
MLA_ROPE = 32
MLA_V = 64
MLA_Q_LORA = 192
MLA_KV_LORA = 128
WINDOW = 128
D_FF = 2816
ROPE_THETA = 10000.0
EPS = 1e-6
NEG_INF = -1e30
LOG2E = math.log2(math.e)
ATTN_SCALE = HEAD_DIM ** -0.5 * LOG2E
MLA_SCALE = (MLA_NOPE + MLA_ROPE) ** -0.5 * LOG2E

LANES = 128
BF16_ROWS = 16
ROW_TILE = 512
FF_TILE = 256
N_FF_TILES = D_FF // FF_TILE
DOWN_CHUNK = 4
IN_COLS_PACKED = 2304
VMEM_LIMIT = 56 * 1024 * 1024

_NT = (((1,), (1,)), ((), ()))


def _dot(a, b):
    return jnp.dot(a, b, preferred_element_type=F32)


def _dot_nt(a, b):
    return lax.dot_general(a, b, _NT, preferred_element_type=F32)


def _rms(x, g, n):
    ms = jnp.sum(x * x, axis=-1, keepdims=True) * (1.0 / n)
    return x * lax.rsqrt(ms + EPS) * g


def _lane(shape):
    return lax.broadcasted_iota(jnp.int32, shape, len(shape) - 1)


def _rope(x, c, s, half):
    w = x.shape[-1]
    lo = (_lane(x.shape) % (2 * half)) < half
    sw = jnp.where(lo, pltpu.roll(x, w - half, 1), pltpu.roll(x, half, 1))
    return x * c + sw * s


def _shift_rows(zz, halo, rows):
    n = zz.shape[0]
    prev = pltpu.roll(zz, 1, 0)[halo:halo + rows]
    nxt = pltpu.roll(zz, n - 1, 0)[halo:halo + rows]
    return prev, nxt


def _seq_pos(tile, rows, seq_len):
    r = lax.broadcasted_iota(jnp.int32, (rows, 1), 0)
    return (tile * rows + r) % seq_len


def _ada_body(c_ref, w_ref, b_ref, o_ref):
    c = c_ref[...]
    s = c * jax.nn.sigmoid(c)
    o_ref[...] = jnp.dot(s, w_ref[...], precision=lax.Precision.HIGHEST,
                         preferred_element_type=F32) + b_ref[...]


def _ada(cond8, w_ada, b_ada):
    tn = 1536
    n = 6 * D_MODEL
    return pl.pallas_call(
        _ada_body,
        grid=(DEPTH, n // tn),
        in_specs=[pl.BlockSpec((8, D_MODEL), lambda l, j: (0, 0)),
                  pl.BlockSpec((None, D_MODEL, tn), lambda l, j: (l, 0, j)),
                  pl.BlockSpec((None, 1, tn), lambda l, j: (l, 0, j))],
        out_specs=pl.BlockSpec((None, 8, tn), lambda l, j: (l, 0, j)),
        out_shape=jax.ShapeDtypeStruct((DEPTH, 8, n), F32),
        compiler_params=pltpu.CompilerParams(
            dimension_semantics=("arbitrary", "arbitrary"), vmem_limit_bytes=VMEM_LIMIT),
        name="ada",
    )(cond8, w_ada, b_ada.reshape(DEPTH, 1, n))


def _ctx_body(ckv_ref, kpe_ref, w_ref, kx_ref, vx_ref):
    kv = _dot(ckv_ref[...].astype(BF16), w_ref[...])
    kpe = kpe_ref[...]
    kx_ref[...] = (kv[:, :4 * LANES] + jnp.concatenate([kpe] * MLA_HEADS, axis=1)).astype(BF16)
    vx_ref[...] = kv[:, 4 * LANES:].T.astype(BF16)


def _ctx_mla(cache_ckv, cache_kpe_pad, wkv_p):
    b, _, s, _ = cache_ckv.shape
    return pl.pallas_call(
        _ctx_body,
        grid=(DEPTH, b),
        in_specs=[pl.BlockSpec((None, None, s, MLA_KV_LORA), lambda l, i: (i, l, 0, 0)),
                  pl.BlockSpec((None, None, s, LANES), lambda l, i: (i, l, 0, 0)),
                  pl.BlockSpec((None, MLA_KV_LORA, 6 * LANES), lambda l, i: (l, 0, 0))],
        out_specs=[pl.BlockSpec((None, None, s, 4 * LANES), lambda l, i: (l, i, 0, 0)),
                   pl.BlockSpec((None, None, 2 * LANES, s), lambda l, i: (l, i, 0, 0))],
        out_shape=[jax.ShapeDtypeStruct((DEPTH, b, s, 4 * LANES), BF16),
                   jax.ShapeDtypeStruct((DEPTH, b, 2 * LANES, s), BF16)],
        compiler_params=pltpu.CompilerParams(
            dimension_semantics=("arbitrary", "arbitrary"), vmem_limit_bytes=VMEM_LIMIT),
        name="ctx_mla",
    )(cache_ckv, cache_kpe_pad, wkv_p)


def _inproj_body(rope, n_alias, x_ref, mod_ref, n1_ref, w_ref, wq_ref, wkv_ref, gq_ref, gkv_ref, gqc_ref, gkc_ref,
                 hm_ref, *rest):
    if rope:
        c64_ref, s64_ref, c32_ref, s32_ref = rest[:4]
        outs = rest[4:]
    else:
        outs = rest[n_alias:]
    x = x_ref[...]
    sh1 = mod_ref[0:1, :]
    sc1 = mod_ref[1:2, :]
    h = _rms(x, n1_ref[...], D_MODEL) * (1.0 + sc1) + sh1
    acc = _dot(h.astype(BF16), w_ref[...])

    xa, gb, gc = acc[:, 0:256], acc[:, 256:512], acc[:, 512:768]
    z = gc * xa
    cq = acc[:, 768:1024]
    cqn = _rms(cq, gq_ref[...], MLA_Q_LORA)
    qm = _dot(cqn.astype(BF16), wq_ref[...]) * MLA_SCALE
    ckvn = _rms(acc[:, 1024:1152], gkv_ref[...], MLA_KV_LORA)
    kv = _dot(ckvn.astype(BF16), wkv_ref[...])
    kpe = acc[:, 1152:1280]
    vm = kv[:, 4 * LANES:]

    hm = hm_ref[...]

    def head_rms(v, g):
        v2 = v * v
        hi = v2.astype(BF16)
        lo = (v2 - hi.astype(F32)).astype(BF16)
        w = v.shape[-1]
        ss = _dot(hi, hm[:w, :w]) + _dot(lo, hm[:w, :w])
        return v * lax.rsqrt(ss * (1.0 / HEAD_DIM) + EPS) * g

    qc = head_rms(acc[:, 1280:1536], gqc_ref[...]) * ATTN_SCALE
    kc = head_rms(acc[:, 1536:1664], gkc_ref[...])
    vc = acc[:, 1664:1792]
    qd = acc[:, 1792:2048] * ATTN_SCALE
    kd = acc[:, 2048:2176]
    vd = acc[:, 2176:2304]

    if not rope:
        (z_o, gb_o, qm_o, km_o, vm_o, qc_o, kc_o, vc_o, qd_o, kd_o, vd_o,
         ckv_c, kpe_c, kc_c, vc_c, kd_c, vd_c) = outs
        km = kv[:, :4 * LANES] + jnp.concatenate([kpe] * MLA_HEADS, axis=1)
        z_o[...] = z.astype(BF16)
        gb_o[...] = gb.astype(BF16)
        qm_o[...] = qm.astype(BF16)
        km_o[...] = km.astype(BF16)
        vm_o[...] = vm.T.astype(BF16)
        qc_o[...] = qc.astype(BF16)
        kc_o[...] = kc.astype(BF16)
        vc_o[...] = vc.T.astype(BF16)
        qd_o[...] = qd.astype(BF16)
        kd_o[...] = kd.astype(BF16)
        vd_o[...] = vd.T.astype(BF16)
        for o, v in ((ckv_c, ckvn), (kpe_c, kpe[:, :MLA_ROPE]), (kc_c, kc), (vc_c, vc), (kd_c, kd), (vd_c, vd)):
            o[...] = v.reshape(o.shape)
    else:
        (z_o, gb_o, qmr_o, qm_o, km_o, vm_o, qcr_o, qc_o, kc_o, vc_o, qdr_o, qd_o, kd_o, vd_o) = outs
        c64, s64, c32, s32 = c64_ref[...], s64_ref[...], c32_ref[...], s32_ref[...]
        rope64 = lambda v: jnp.concatenate(
            [_rope(v[:, i:i + LANES], c64, s64, 16) for i in range(0, v.shape[-1], LANES)], axis=1)
        rope32 = lambda v: jnp.concatenate(
            [_rope(v[:, i:i + LANES], c32, s32, 8) for i in range(0, v.shape[-1], LANES)], axis=1)
        km = kv[:, :4 * LANES] + jnp.concatenate([_rope(kpe, c32, s32, 8)] * MLA_HEADS, axis=1)
        z_o[...] = z.astype(BF16)
        gb_o[...] = gb.astype(BF16)
        qmr_o[...] = rope32(qm).astype(BF16)
        qm_o[...] = qm.astype(BF16)
        km_o[...] = km.astype(BF16)
        vm_o[...] = vm.T.astype(BF16)
        qcr_o[...] = rope64(qc).astype(BF16)
        qc_o[...] = qc.astype(BF16)
        kc_o[...] = rope64(kc).astype(BF16)
        vc_o[...] = vc.T.astype(BF16)
        qdr_o[...] = rope64(qd).astype(BF16)
        qd_o[...] = qd.astype(BF16)
        kd_o[...] = rope64(kd).astype(BF16)
        vd_o[...] = vd.T.astype(BF16)


def _inproj(x, mod, lw, l, seq_len, cond_base, rope_tabs, prev_caches=None):
    rows = x.shape[0]
    r = ROW_TILE
    rope = rope_tabs is not None
    aliases = {}
    tiles_per_seq = max(seq_len // r, 1)
    step = 1 if seq_len >= r else 0

    def cond_map(i):
        return (l, cond_base + step * (i // tiles_per_seq), 0, 0)

    row = lambda w: pl.BlockSpec((r, w), lambda i: (i, 0))
    full = lambda a: pl.BlockSpec((None,) + a.shape[1:], lambda i: (l,) + (0,) * (a.ndim - 1))
    in_specs = [row(D_MODEL),
                pl.BlockSpec((None, None, 6, D_MODEL), cond_map),
                full(lw['norm1']), full(lw['w_in']), full(lw['wq']), full(lw['wkv']),
                full(lw['gq']), full(lw['gkv']), full(lw['gqc']), full(lw['gkc']),
                pl.BlockSpec((2 * LANES, 2 * LANES), lambda i: (0, 0))]
    args = [x, mod, lw['norm1'], lw['w_in'], lw['wq'], lw['wkv'], lw['gq'], lw['gkv'], lw['gqc'], lw['gkc'],
            lw['hm']]
    def act(w, transposed):
        if transposed:
            return jax.ShapeDtypeStruct((w, rows), BF16), pl.BlockSpec((w, r), lambda i: (0, i))
        return jax.ShapeDtypeStruct((rows, w), BF16), row(w)

    if rope:
        tab = pl.BlockSpec((r, LANES), lambda i: (i % tiles_per_seq, 0))
        in_specs += [tab] * 4
        args += list(rope_tabs)
        widths = [(256, 0), (256, 0), (512, 0), (512, 0), (512, 0), (256, 1), (256, 0), (256, 0), (128, 0),
                  (128, 1), (256, 0), (256, 0), (128, 0), (128, 1)]
        out_shape, out_specs = map(list, zip(*[act(w, t) for w, t in widths]))
    else:
        widths = [(256, 0), (256, 0), (512, 0), (512, 0), (256, 1), (256, 0), (128, 0), (128, 1), (256, 0),
                  (128, 0), (128, 1)]
        cwidths = [128, MLA_ROPE, 128, 128, 128, 128]
        nb = r // seq_len
        out_shape, out_specs = map(list, zip(*[act(w, t) for w, t in widths]))
        out_shape += [jax.ShapeDtypeStruct((rows // seq_len, DEPTH, seq_len, w), F32) for w in cwidths]
        out_specs += [pl.BlockSpec((nb, None, seq_len, w), lambda i: (i, l, 0, 0)) for w in cwidths]
        if prev_caches is not None:
            aliases = {len(args) + k: len(widths) + k for k in range(len(cwidths))}
            in_specs += [pl.BlockSpec(memory_space=pl.ANY)] * len(cwidths)
            args += list(prev_caches)
    return pl.pallas_call(
        functools.partial(_inproj_body, rope, len(aliases)),
        grid=(rows // r,),
        in_specs=in_specs, out_specs=out_specs, out_shape=out_shape,
        input_output_aliases=aliases,
        compiler_params=pltpu.CompilerParams(
            dimension_semantics=("arbitrary",), vmem_limit_bytes=VMEM_LIMIT),
        name="inproj_rope" if rope else "inproj",
    )(*args)


def _softmax_pv_t(s_list, vt_list, extra=None):
    m = s_list[0].max(axis=0, keepdims=True)
    for s in s_list[1:]:
        m = jnp.maximum(m, s.max(axis=0, keepdims=True))
    if extra is not None:
        m = jnp.maximum(m, extra)
    den = jnp.exp2(extra - m) if extra is not None else 0.0
    out = None
    for s, vt in zip(s_list, vt_list):
        p = jnp.exp2(s - m)
        den = den + p.sum(axis=0, keepdims=True)
        pv = _dot(vt, p.astype(BF16))
        out = pv if out is None else out + pv
    return out * (1.0 / den)


def _half_masks(q):
    lo = _lane(q.shape) < HEAD_DIM
    zero = jnp.zeros_like(q)
    return jnp.where(lo, q, zero), jnp.where(lo, zero, q)


def _merge_rows(a, b):
    r = lax.broadcasted_iota(jnp.int32, a.shape, 0)
    return jnp.where(r < HEAD_DIM, a, b)


def _sink_row(sink_ref, l, h0, h1, tq):
    c = lax.broadcasted_iota(jnp.int32, (1, 2 * tq), 1)
    return jnp.where(c < tq, sink_ref[l, h0], sink_ref[l, h1]) * LOG2E


def _run_groups(stage1, stage2, n):
    out = []
    nxt = stage1(0)
    for g in range(n):
        cur = nxt
        if g + 1 < n:
            nxt = stage1(g + 1)
        out.append(stage2(g, cur))
    return out


def _mla_scores(h, q_sets, k_sets):
    res = []
    for q, k in zip(q_sets, k_sets):
        cs = [slice(hh * LANES, (hh + 1) * LANES) for hh in (2 * h, 2 * h + 1)]
        res.append(jnp.concatenate([_dot_nt(k[:, c], q[:, c]) for c in cs], axis=1))
    return res


def _gqa_scores(j, q_sets, k_sets):
    res = []
    for q, k in zip(q_sets, k_sets):
        qa = _half_masks(q[:, 0:LANES])[j]
        qb = _half_masks(q[:, LANES:])[j]
        res.append(_dot_nt(k[...], jnp.concatenate([qa, qb], axis=0)))
    return res


def _store_pair(y_ref, rows, blk, ot):
    y_ref[rows, blk * LANES:(blk + 1) * LANES] = ot.T.astype(BF16)


def _attn_ctx_body(l, nb, t, sink_ref, qm_ref, km_ref, vm_ref, qc_ref, kc_ref, vc_ref, qd_ref, kd_ref, vd_ref,
                   yb_ref, yc_ref, yd_ref):
    def stage1(g):
        b, k = divmod(g, 6)
        rs = pl.ds(b * t, t)
        if k < 2:
            return _mla_scores(k, [qm_ref.at[rs]], [km_ref.at[rs]])
        q_ref, k_ref = (qc_ref, kc_ref) if k < 4 else (qd_ref, kd_ref)
        return _gqa_scores(k % 2, [q_ref.at[rs]], [k_ref.at[rs]])

    def stage2(g, s):
        b, k = divmod(g, 6)
        rs = slice(b * t, (b + 1) * t)
        if k < 2:
            return _softmax_pv_t(s, [vm_ref[k * LANES:(k + 1) * LANES, rs]])
        if k < 4:
            return _softmax_pv_t(s, [vc_ref[:, rs]])
        j = k % 2
        return _softmax_pv_t(s, [vd_ref[:, rs]], _sink_row(sink_ref, l, 2 * j, 2 * j + 1, t))

    o = _run_groups(stage1, stage2, 6 * nb)
    for b in range(nb):
        rs = slice(b * t, (b + 1) * t)
        ob = o[6 * b:6 * b + 6]
        for p in range(2):
            _store_pair(yb_ref, rs, p, _merge_rows(ob[p][:, :t], ob[p][:, t:]))
        for y_ref, (o0, o1) in ((yc_ref, ob[2:4]), (yd_ref, ob[4:6])):
            _store_pair(y_ref, rs, 0, _merge_rows(o0[:, :t], o1[:, :t]))
            _store_pair(y_ref, rs, 1, _merge_rows(o0[:, t:], o1[:, t:]))


def _attn_ctx(l, t, sink, qm, km, vm, qc, kc, vc, qd, kd, vd):
    rows = qm.shape[0]
    nb = 2
    r = nb * t
    row = lambda a: (pl.BlockSpec((a.shape[0], r), lambda i: (0, i)) if a.shape[1] == rows
                     else pl.BlockSpec((r, a.shape[1]), lambda i: (i, 0)))
    ins = [qm, km, vm, qc, kc, vc, qd, kd, vd]
    return pl.pallas_call(
        functools.partial(_attn_ctx_body, l, nb, t),
        grid=(rows // r,),
        in_specs=[pl.BlockSpec(memory_space=pltpu.SMEM)] + [row(a) for a in ins],
        out_specs=[pl.BlockSpec((r, 2 * LANES), lambda i: (i, 0))] * 3,
        out_shape=[jax.ShapeDtypeStruct((rows, 2 * LANES), BF16)] * 3,
        compiler_params=pltpu.CompilerParams(
            dimension_semantics=("arbitrary",), vmem_limit_bytes=VMEM_LIMIT),
        name="attn_ctx",
    )(sink, *ins)


def _window_block(i, tq, t, nwb):
    return jnp.clip(i * (tq // LANES) - WINDOW // LANES, 0, t // LANES - nwb)


def _attn_lat_body(l, tq, t, nwb, sink_ref, qmr_ref, qm_ref, km_ref, vm_ref, kx_ref, vx_ref,
                   qcr_ref, qc_ref, kc_ref, vc_ref, kcx_ref, vcx_ref,
                   qdr_ref, qd_ref, kdx_ref, vdx_ref, *rest):
    kd_refs, vd_refs = rest[:nwb], rest[nwb:2 * nwb]
    yb_ref, yc_ref, yd_ref = rest[2 * nwb:]
    i = pl.program_id(1)
    wk = nwb * LANES
    kpos = _window_block(i, tq, t, nwb) * LANES + lax.broadcasted_iota(jnp.int32, (wk, 2 * tq), 0)
    qpos = i * tq + lax.broadcasted_iota(jnp.int32, (wk, 2 * tq), 1) % tq
    valid = jnp.abs(kpos - qpos) <= WINDOW

    def stage1(g):
        if g < 2:
            return _mla_scores(g, [qmr_ref, qm_ref], [km_ref, kx_ref])
        if g < 4:
            return _gqa_scores(g % 2, [qcr_ref, qc_ref], [kc_ref, kcx_ref])
        kw = jnp.concatenate([r[...] for r in kd_refs], axis=0)
        s1, s2 = _gqa_scores(g % 2, [qdr_ref, qd_ref], [kw, kdx_ref])
        return [jnp.where(valid, s1, NEG_INF), s2]

    def stage2(g, s):
        if g < 2:
            blk = slice(g * LANES, (g + 1) * LANES)
            return _softmax_pv_t(s, [vm_ref[blk, :], vx_ref[blk, :]])
        if g < 4:
            return _softmax_pv_t(s, [vc_ref[...], vcx_ref[...]])
        j = g % 2
        vw = jnp.concatenate([r[...] for r in vd_refs], axis=1)
        return _softmax_pv_t(s, [vw, vdx_ref[...]], _sink_row(sink_ref, l, 2 * j, 2 * j + 1, tq))

    o = _run_groups(stage1, stage2, 6)
    rs = slice(None)
    for p in range(2):
        _store_pair(yb_ref, rs, p, _merge_rows(o[p][:, :tq], o[p][:, tq:]))
    for y_ref, (o0, o1) in ((yc_ref, o[2:4]), (yd_ref, o[4:6])):
        _store_pair(y_ref, rs, 0, _merge_rows(o0[:, :tq], o1[:, :tq]))
        _store_pair(y_ref, rs, 1, _merge_rows(o0[:, tq:], o1[:, tq:]))


def _attn_lat(l, nbatch, t, sink, qmr, qm, km, vmt, kx, vxt, qcr, qc, kc, vct, kcx, vcxt, qdr, qd, kd, vdt,
              kdx, vdxt):
    tq = 256
    nq = t // tq
    nwb = (tq + 2 * WINDOW) // LANES
    nkb = t // LANES
    rows = qm.shape[0]
    qs = lambda a: pl.BlockSpec((tq, a.shape[1]), lambda b, i: (b * nq + i, 0))
    ks = lambda a: pl.BlockSpec((t, a.shape[1]), lambda b, i: (b, 0))
    kts = lambda a: pl.BlockSpec((a.shape[0], t), lambda b, i: (0, b))
    xs = lambda a: pl.BlockSpec((None, None) + a.shape[2:], lambda b, i: (l, b, 0, 0))
    cs = lambda a: pl.BlockSpec((None, None) + a.shape[2:], lambda b, i: (b, l, 0, 0))
    wblk = lambda b, i, k: b * nkb + _window_block(i, tq, t, nwb) + k
    kd_specs = [pl.BlockSpec((LANES, LANES), lambda b, i, k=k: (wblk(b, i, k), 0)) for k in range(nwb)]
    vd_specs = [pl.BlockSpec((LANES, LANES), lambda b, i, k=k: (0, wblk(b, i, k))) for k in range(nwb)]
    in_specs = [pl.BlockSpec(memory_space=pltpu.SMEM),
                qs(qmr), qs(qm), ks(km), kts(vmt), xs(kx), xs(vxt),
                qs(qcr), qs(qc), ks(kc), kts(vct), cs(kcx), cs(vcxt),
                qs(qdr), qs(qd), cs(kdx), cs(vdxt)] + kd_specs + vd_specs
    return pl.pallas_call(
        functools.partial(_attn_lat_body, l, tq, t, nwb),
        grid=(nbatch, nq),
        in_specs=in_specs,
        out_specs=[pl.BlockSpec((tq, 2 * LANES), lambda b, i: (b * nq + i, 0))] * 3,
        out_shape=[jax.ShapeDtypeStruct((rows, 2 * LANES), BF16)] * 3,
        compiler_params=pltpu.CompilerParams(
            dimension_semantics=("arbitrary", "arbitrary"), vmem_limit_bytes=VMEM_LIMIT),
        name="attn_lat",
    )(sink, qmr, qm, km, vmt, kx, vxt, qcr, qc, kc, vct, kcx, vcxt, qdr, qd, kdx, vdxt,
      *([kd] * nwb), *([vdt] * nwb))


def _outproj_body(seq_len, x_ref, mod_ref, n2_ref, cw_ref, wo_ref, zp_ref, z_ref, zn_ref, gb_ref,
                  yb_ref, yc_ref, yd_ref, x1_ref, h2_ref):
    r = x_ref.shape[0]
    i = pl.program_id(0)
    zz = jnp.concatenate([zp_ref[...], z_ref[...], zn_ref[...]], axis=0).astype(F32)
    prev, nxt = _shift_rows(zz, BF16_ROWS, r)
    pos = _seq_pos(i, r, seq_len)
    prev = jnp.where(pos == 0, 0.0, prev)
    nxt = jnp.where(pos == seq_len - 1, 0.0, nxt)
    cw = cw_ref[...]
    ya = gb_ref[...].astype(F32) * (prev * cw[0:1] + z_ref[...].astype(F32) * cw[1:2] + nxt * cw[2:3])
    ycat = jnp.concatenate([ya.astype(BF16), yb_ref[...], yc_ref[...], yd_ref[...]], axis=1)
    y = _dot(ycat, wo_ref[...])
    g1 = mod_ref[2:3, :]
    sh2 = mod_ref[3:4, :]
    sc2 = mod_ref[4:5, :]
    x1 = x_ref[...] + g1 * y
    x1_ref[...] = x1
    h2_ref[...] = (_rms(x1, n2_ref[...], D_MODEL) * (1.0 + sc2) + sh2).astype(BF16)


def _halo_specs(r, w, rows):
    nblk = rows // BF16_ROWS
    per = r // BF16_ROWS
    prev = pl.BlockSpec((BF16_ROWS, w), lambda i: (jnp.maximum(i * per - 1, 0), 0))
    cur = pl.BlockSpec((r, w), lambda i: (i, 0))
    nxt = pl.BlockSpec((BF16_ROWS, w), lambda i: (jnp.minimum((i + 1) * per, nblk - 1), 0))
    return prev, cur, nxt


def _cond_spec(l, seq_len, cond_base):
    r = ROW_TILE
    tiles_per_seq = max(seq_len // r, 1)
    step = 1 if seq_len >= r else 0
    return pl.BlockSpec((None, None, 6, D_MODEL),
                        lambda i: (l, cond_base + step * (i // tiles_per_seq), 0, 0))


def _outproj(x, mod, lw, l, seq_len, cond_base, z, gb, yb, yc, yd):
    rows = x.shape[0]
    r = ROW_TILE
    row = lambda w: pl.BlockSpec((r, w), lambda i: (i, 0))
    full = lambda a: pl.BlockSpec((None,) + a.shape[1:], lambda i: (l,) + (0,) * (a.ndim - 1))
    zp, zc, zn = _halo_specs(r, 2 * LANES, rows)
    return pl.pallas_call(
        functools.partial(_outproj_body, seq_len),
        grid=(rows // r,),
        in_specs=[row(D_MODEL), _cond_spec(l, seq_len, cond_base), full(lw['norm2']), full(lw['conv_a']),
                  full(lw['w_out']), zp, zc, zn, row(256), row(256), row(256), row(256)],
        out_specs=[row(D_MODEL), row(D_MODEL)],
        out_shape=[jax.ShapeDtypeStruct((rows, D_MODEL), F32), jax.ShapeDtypeStruct((rows, D_MODEL), BF16)],
        compiler_params=pltpu.CompilerParams(
            dimension_semantics=("arbitrary",), vmem_limit_bytes=VMEM_LIMIT),
        name="outproj",
    )(x, mod, lw['norm2'], lw['conv_a'], lw['w_out'], z, z, z, gb, yb, yc, yd)


def _mlp_body(seq_len, final, x1_ref, mod_ref, cw_ref, wup_ref, wdn_ref, fn_ref, hp_ref, h_ref, hn_ref,
              o_ref, hbuf, gbuf):
    r = x1_ref.shape[0]
    halo = BF16_ROWS
    i = pl.program_id(0)
    hbuf[0:halo, :] = hp_ref[...]
    hbuf[halo:halo + r, :] = h_ref[...]
    hbuf[halo + r:, :] = hn_ref[...]
    pos = _seq_pos(i, r, seq_len)
    first = pos == 0
    last = pos == seq_len - 1

    def up(j):
        ga = slice(j * FF_TILE, (j + 1) * FF_TILE)
        va = slice(D_FF + j * FF_TILE, D_FF + (j + 1) * FF_TILE)
        return _dot(hbuf[...], wup_ref[:, ga]), _dot(hbuf[...], wup_ref[:, va])

    def conv(u, cw):
        prev, nxt = _shift_rows(u, halo, r)
        return (jnp.where(first, 0.0, prev) * cw[0:1] + u[halo:halo + r] * cw[1:2]
                + jnp.where(last, 0.0, nxt) * cw[2:3])

    acc = None
    nxt_u = up(0)
    for j in range(N_FF_TILES):
        ua, ub = nxt_u
        if j + 1 < N_FF_TILES:
            nxt_u = up(j + 1)
        ga = slice(j * FF_TILE, (j + 1) * FF_TILE)
        va = slice(D_FF + j * FF_TILE, D_FF + (j + 1) * FF_TILE)
        ua = conv(ua, cw_ref[:, ga])
        ub = conv(ub, cw_ref[:, va])
        gbuf[:, ga] = (ua * jax.nn.sigmoid(ua) * ub).astype(BF16)
        if j % DOWN_CHUNK == DOWN_CHUNK - 1 or j == N_FF_TILES - 1:
            lo = (j // DOWN_CHUNK) * DOWN_CHUNK * FF_TILE
            hi = (j + 1) * FF_TILE
            d = _dot(gbuf[:, lo:hi], wdn_ref[lo:hi, :])
            acc = d if acc is None else acc + d
    g2 = mod_ref[5:6, :]
    x2 = x1_ref[...] + g2 * acc
    if final:
        x2 = _rms(x2, fn_ref[...], D_MODEL)
    o_ref[...] = x2


def _mlp(x1, h2, mod, lw, l, seq_len, cond_base, final_norm, final):
    rows = x1.shape[0]
    r = ROW_TILE
    row = lambda w: pl.BlockSpec((r, w), lambda i: (i, 0))
    full = lambda a: pl.BlockSpec((None,) + a.shape[1:], lambda i: (l,) + (0,) * (a.ndim - 1))
    hp, hc, hn = _halo_specs(r, D_MODEL, rows)
    return pl.pallas_call(
        functools.partial(_mlp_body, seq_len, final),
        grid=(rows // r,),
        in_specs=[row(D_MODEL), _cond_spec(l, seq_len, cond_base), full(lw['conv_ff']),
                  full(lw['w_up']), full(lw['w_down']),
                  pl.BlockSpec((1, D_MODEL), lambda i: (0, 0)), hp, hc, hn],
        out_specs=row(D_MODEL),
        out_shape=jax.ShapeDtypeStruct((rows, D_MODEL), F32),
        scratch_shapes=[pltpu.VMEM((r + 2 * BF16_ROWS, D_MODEL), BF16), pltpu.VMEM((r, D_FF), BF16)],
        compiler_params=pltpu.CompilerParams(
            dimension_semantics=("arbitrary",), vmem_limit_bytes=VMEM_LIMIT),
        name="mlp",
    )(x1, mod, lw['conv_ff'], lw['w_up'], lw['w_down'], final_norm, h2, h2, h2)


def _pad_cols(a, n):
    return jnp.pad(a, [(0, 0)] * (a.ndim - 1) + [(0, n - a.shape[-1])])


def _perm_heads(a, axis):
    h = [lax.slice_in_dim(a, k * HEAD_DIM, (k + 1) * HEAD_DIM, axis=axis) for k in range(4)]
    return jnp.concatenate([h[0], h[2], h[1], h[3]], axis=axis)


def _prep_weights(w_in, mla_wq_b, mla_wkv_b, w_out, w_up, conv_ff, w_down, norm1, norm2, conv_a,
                  mla_q_norm, mla_kv_norm, gqa_q_norm, gqa_k_norm):
    L = DEPTH
    a = w_in[..., 0:768]
    cq = _pad_cols(w_in[..., 768:960], 256)
    ckv = w_in[..., 960:1088]
    kpe = _pad_cols(w_in[..., 1088:1120], LANES)
    qc = _perm_heads(w_in[..., 1120:1376], 2)
    kvc = w_in[..., 1376:1632]
    qd = _perm_heads(w_in[..., 1632:1888], 2)
    kvd = w_in[..., 1888:2144]
    w_in_p = jnp.concatenate([a, cq, ckv, kpe, qc, kvc, qd, kvd], axis=-1).astype(BF16)

    wq = mla_wq_b.reshape(L, MLA_Q_LORA, MLA_HEADS, MLA_NOPE + MLA_ROPE)
    wq = jnp.concatenate([wq[..., MLA_NOPE:], wq[..., :MLA_NOPE],
                          jnp.zeros((L, MLA_Q_LORA, MLA_HEADS, LANES - MLA_NOPE - MLA_ROPE), F32)], axis=-1)
    wq = jnp.pad(wq.reshape(L, MLA_Q_LORA, MLA_HEADS * LANES), ((0, 0), (0, 256 - MLA_Q_LORA), (0, 0)))

    wkv = mla_wkv_b.reshape(L, MLA_KV_LORA, MLA_HEADS, MLA_NOPE + MLA_V)
    zk = jnp.zeros((L, MLA_KV_LORA, MLA_HEADS, MLA_ROPE), F32)
    wk = jnp.concatenate([zk, wkv[..., :MLA_NOPE], zk], axis=-1).reshape(L, MLA_KV_LORA, MLA_HEADS * LANES)
    wv = wkv[..., MLA_NOPE:].reshape(L, MLA_KV_LORA, MLA_HEADS * MLA_V)
    wkv_p = jnp.concatenate([wk, wv], axis=-1)

    wo = jnp.concatenate([w_out[:, 0:512], _perm_heads(w_out[:, 512:768], 1),
                          _perm_heads(w_out[:, 768:1024], 1)], axis=1)

    hm = jnp.asarray(np.kron(np.eye(2 * LANES // HEAD_DIM), np.ones((HEAD_DIM, HEAD_DIM))), BF16)
    return {
        'w_in': w_in_p, 'wq': wq.astype(BF16), 'wkv': wkv_p.astype(BF16), 'w_out': wo.astype(BF16),
        'w_up': w_up.astype(BF16), 'conv_ff': conv_ff, 'w_down': w_down.astype(BF16),
        'norm1': norm1.reshape(L, 1, D_MODEL), 'norm2': norm2.reshape(L, 1, D_MODEL), 'conv_a': conv_a,
        'gq': _pad_cols(mla_q_norm, 256).reshape(L, 1, 256), 'gkv': mla_kv_norm.reshape(L, 1, MLA_KV_LORA),
        'gqc': jnp.tile(gqa_q_norm, (1, 4)).reshape(L, 1, 256), 'gkc': jnp.tile(gqa_k_norm, (1, 2)).reshape(L, 1, 128),
        'hm': hm,
    }


def _rope_tables(t):
    rows = t // GRID_W
    row = np.repeat(np.arange(rows, dtype=np.float64), GRID_W)
    col = np.tile(np.arange(GRID_W, dtype=np.float64), rows)

    def tabs(dim):
        half = dim // 2
        inv = np.power(ROPE_THETA, -np.arange(0, half, 2, dtype=np.float64) / half)
        ar = row[:, None] * inv
        ac = col[:, None] * inv
        c = np.concatenate([np.cos(ar), np.cos(ar), np.cos(ac), np.cos(ac)], axis=1)
        s = np.concatenate([-np.sin(ar), np.sin(ar), -np.sin(ac), np.sin(ac)], axis=1)
        return c, s

    c64, s64 = tabs(HEAD_DIM)
    c32, s32 = tabs(MLA_ROPE)
    c64 = np.tile(c64, (1, 2))
    s64 = np.tile(s64, (1, 2))
    c32 = np.concatenate([c32, np.ones((t, LANES - MLA_ROPE))], axis=1)
    s32 = np.concatenate([s32, np.zeros((t, LANES - MLA_ROPE))], axis=1)
    return tuple(jnp.asarray(a.astype(np.float32)) for a in (c64, s64, c32, s32))


def kernel(x_prompt, x_sample, cache_mla_ckv, cache_mla_kpe, cache_gqa_k, cache_gqa_v, cache_swa_k, cache_swa_v,
           c, c_ctx, w_ada, b_ada, norm1, w_in, conv_a, mla_q_norm, mla_wq_b, mla_kv_norm, mla_wkv_b,
           gqa_q_norm, gqa_k_norm, swa_sink, w_out, norm2, w_up, conv_ff, w_down, final_norm):
    B, T, _ = x_prompt.shape
    DB, DT, _ = x_sample.shape
    past = cache_mla_ckv.shape[2]

    lw = _prep_weights(w_in, mla_wq_b, mla_wkv_b, w_out, w_up, conv_ff, w_down, norm1, norm2, conv_a,
                       mla_q_norm, mla_kv_norm, gqa_q_norm, gqa_k_norm)
    rope_tabs = _rope_tables(DT)
    fnorm = final_norm.reshape(1, D_MODEL)

    cond8 = jnp.concatenate([c_ctx[None, :], c, jnp.zeros((8 - 1 - DB, D_MODEL), F32)], axis=0)
    mod = _ada(cond8, w_ada, b_ada).reshape(DEPTH, 8, 6, D_MODEL)

    kx, vx = _ctx_mla(cache_mla_ckv, _pad_cols(cache_mla_kpe, LANES), lw['wkv'])
    flat = lambda a: a.reshape(DB, DEPTH, past, 2 * HEAD_DIM).astype(BF16)
    kcx, kdx = flat(cache_gqa_k), flat(cache_swa_k)
    vcx, vdx = jnp.swapaxes(flat(cache_gqa_v), 2, 3), jnp.swapaxes(flat(cache_swa_v), 2, 3)

    xp = x_prompt.reshape(B * T, D_MODEL)
    xs = x_sample.reshape(DB * DT, D_MODEL)
    caches = None
    for l in range(DEPTH):
        final = l == DEPTH - 1
        (z, gb, qm, km, vm, qc, kc, vc, qd, kd, vd, *caches) = _inproj(xp, mod, lw, l, T, 0, None, caches)
        yb, yc, yd = _attn_ctx(l, T, swa_sink, qm, km, vm, qc, kc, vc, qd, kd, vd)
        x1, h2 = _outproj(xp, mod, lw, l, T, 0, z, gb, yb, yc, yd)
        xp = _mlp(x1, h2, mod, lw, l, T, 0, fnorm, final)
        (z, gb, qmr, qm, km, vm, qcr, qc, kc, vc, qdr, qd, kd, vd) = _inproj(xs, mod, lw, l, DT, 1, rope_tabs)
        yb, yc, yd = _attn_lat(l, DB, DT, swa_sink, qmr, qm, km, vm, kx, vx, qcr, qc, kc, vc, kcx, vcx,
                               qdr, qd, kd, vd, kdx, vdx)
        x1, h2 = _outproj(xs, mod, lw, l, DT, 1, z, gb, yb, yc, yd)
        xs = _mlp(x1, h2, mod, lw, l, DT, 1, fnorm, final)

    heads = lambda a: a.reshape(B, DEPTH, T, 2, HEAD_DIM)
    return (xp.reshape(B, T, D_MODEL), xs.reshape(DB, DT, D_MODEL), caches[0], caches[1],
            heads(caches[2]), heads(caches[3]), heads(caches[4]), heads(caches[5]))
```

```python
import functools
import math

import jax
import jax.numpy as jnp
import numpy as np
from jax import lax
from jax.experimental import pallas as pl
from jax.experimental.pallas import tpu as pltpu

F32 = jnp.float32
BF16 = jnp.bfloat16

D_MODEL = 1024
DEPTH = 2
GRID_W = 64
HEAD_DIM = 64
GROUP_WIDTH = D_MODEL // 4
MLA_HEADS = 4
MLA_NOPE = 64
MLA_ROPE = 32
MLA_V = 64
MLA_Q_LORA = 192
MLA_KV_LORA = 128
WINDOW = 128
D_FF = 2816
ROPE_THETA = 10000.0
EPS = 1e-6
NEG_INF = -1e30
LOG2E = math.log2(math.e)
ATTN_SCALE = HEAD_DIM ** -0.5 * LOG2E
MLA_SCALE = (MLA_NOPE + MLA_ROPE) ** -0.5 * LOG2E

LANES = 128
BF16_ROWS = 16
ROW_TILE = 512
FF_TILE = 256
N_FF_TILES = D_FF // FF_TILE
DOWN_CHUNK = 4
IN_COLS_PACKED = 2304
VMEM_LIMIT = 56 * 1024 * 1024

_NT = (((1,), (1,)), ((), ()))


def _dot(a, b):
    return jnp.dot(a, b, preferred_element_type=F32)


def _dot_nt(a, b):
    return lax.dot_general(a, b, _NT, preferred_element_type=F32)


def _rms(x, g, n):
    ms = jnp.sum(x * x, axis=-1, keepdims=True) * (1.0 / n)
    return x * lax.rsqrt(ms + EPS) * g


def _lane(shape):
    return lax.broadcasted_iota(jnp.int32, shape, len(shape) - 1)


def _rope(x, c, s, half):
    w = x.shape[-1]
    lo = (_lane(x.shape) % (2 * half)) < half
    sw = jnp.where(lo, pltpu.roll(x, w - half, 1), pltpu.roll(x, half, 1))
    return x * c + sw * s


def _shift_rows(zz, halo, rows):
    n = zz.shape[0]
    prev = pltpu.roll(zz, 1, 0)[halo:halo + rows]
    nxt = pltpu.roll(zz, n - 1, 0)[halo:halo + rows]
    return prev, nxt


def _seq_pos(tile, rows, seq_len):
    r = lax.broadcasted_iota(jnp.int32, (rows, 1), 0)
    return (tile * rows + r) % seq_len


def _ada_body(c_ref, w_ref, b_ref, o_ref):
    c = c_ref[...]
    s = c * jax.nn.sigmoid(c)
    o_ref[...] = jnp.dot(s, w_ref[...], precision=lax.Precision.HIGHEST,
                         preferred_element_type=F32) + b_ref[...]


def _ada(cond8, w_ada, b_ada):
    tn = 1536
    n = 6 * D_MODEL
    return pl.pallas_call(
        _ada_body,
        grid=(DEPTH, n // tn),
        in_specs=[pl.BlockSpec((8, D_MODEL), lambda l, j: (0, 0)),
                  pl.BlockSpec((None, D_MODEL, tn), lambda l, j: (l, 0, j)),
                  pl.BlockSpec((None, 1, tn), lambda l, j: (l, 0, j))],
        out_specs=pl.BlockSpec((None, 8, tn), lambda l, j: (l, 0, j)),
        out_shape=jax.ShapeDtypeStruct((DEPTH, 8, n), F32),
        compiler_params=pltpu.CompilerParams(
            dimension_semantics=("arbitrary", "arbitrary"), vmem_limit_bytes=VMEM_LIMIT),
        name="ada",
    )(cond8, w_ada, b_ada.reshape(DEPTH, 1, n))


def _ctx_body(ckv_ref, kpe_ref, w_ref, kx_ref, vx_ref):
    kv = _dot(ckv_ref[...].astype(BF16), w_ref[...])
    kpe = kpe_ref[...]
    kx_ref[...] = (kv[:, :4 * LANES] + jnp.concatenate([kpe] * MLA_HEADS, axis=1)).astype(BF16)
    vx_ref[...] = kv[:, 4 * LANES:].T.astype(BF16)


def _ctx_mla(cache_ckv, cache_kpe_pad, wkv_p):
    b, _, s, _ = cache_ckv.shape
    return pl.pallas_call(
        _ctx_body,
        grid=(DEPTH, b),
        in_specs=[pl.BlockSpec((None, None, s, MLA_KV_LORA), lambda l, i: (i, l, 0, 0)),
                  pl.BlockSpec((None, None, s, LANES), lambda l, i: (i, l, 0, 0)),
                  pl.BlockSpec((None, MLA_KV_LORA, 6 * LANES), lambda l, i: (l, 0, 0))],
        out_specs=[pl.BlockSpec((None, None, s, 4 * LANES), lambda l, i: (l, i, 0, 0)),
                   pl.BlockSpec((None, None, 2 * LANES, s), lambda l, i: (l, i, 0, 0))],
        out_shape=[jax.ShapeDtypeStruct((DEPTH, b, s, 4 * LANES), BF16),
                   jax.ShapeDtypeStruct((DEPTH, b, 2 * LANES, s), BF16)],
        compiler_params=pltpu.CompilerParams(
            dimension_semantics=("arbitrary", "arbitrary"), vmem_limit_bytes=VMEM_LIMIT),
        name="ctx_mla",
    )(cache_ckv, cache_kpe_pad, wkv_p)


def _inproj_body(rope, n_alias, x_ref, mod_ref, n1_ref, w_ref, wq_ref, wkv_ref, gq_ref, gkv_ref, gqc_ref, gkc_ref,
                 hm_ref, *rest):
    if rope:
        c64_ref, s64_ref, c32_ref, s32_ref = rest[:4]
        outs = rest[4:]
    else:
        outs = rest[n_alias:]
    x = x_ref[...]
    sh1 = mod_ref[0:1, :]
    sc1 = mod_ref[1:2, :]
    h = _rms(x, n1_ref[...], D_MODEL) * (1.0 + sc1) + sh1
    acc = _dot(h.astype(BF16), w_ref[...])

    xa, gb, gc = acc[:, 0:256], acc[:, 256:512], acc[:, 512:768]
    z = gc * xa
    cq = acc[:, 768:1024]
    cqn = _rms(cq, gq_ref[...], MLA_Q_LORA)
    qm = _dot(cqn.astype(BF16), wq_ref[...]) * MLA_SCALE
    ckvn = _rms(acc[:, 1024:1152], gkv_ref[...], MLA_KV_LORA)
    kv = _dot(ckvn.astype(BF16), wkv_ref[...])
    kpe = acc[:, 1152:1280]
    vm = kv[:, 4 * LANES:]

    hm = hm_ref[...]

    def head_rms(v, g):
        v2 = v * v
        hi = v2.astype(BF16)
        lo = (v2 - hi.astype(F32)).astype(BF16)
        w = v.shape[-1]
        ss = _dot(hi, hm[:w, :w]) + _dot(lo, hm[:w, :w])
        return v * lax.rsqrt(ss * (1.0 / HEAD_DIM) + EPS) * g

    qc = head_rms(acc[:, 1280:1536], gqc_ref[...]) * ATTN_SCALE
    kc = head_rms(acc[:, 1536:1664], gkc_ref[...])
    vc = acc[:, 1664:1792]
    qd = acc[:, 1792:2048] * ATTN_SCALE
    kd = acc[:, 2048:2176]
    vd = acc[:, 2176:2304]

    if not rope:
        (z_o, gb_o, qm_o, km_o, vm_o, qc_o, kc_o, vc_o, qd_o, kd_o, vd_o,
         ckv_c, kpe_c, kc_c, vc_c, kd_c, vd_c) = outs
        km = kv[:, :4 * LANES] + jnp.concatenate([kpe] * MLA_HEADS, axis=1)
        z_o[...] = z.astype(BF16)
        gb_o[...] = gb.astype(BF16)
        qm_o[...] = qm.astype(BF16)
        km_o[...] = km.astype(BF16)
        vm_o[...] = vm.T.astype(BF16)
        qc_o[...] = qc.astype(BF16)
        kc_o[...] = kc.astype(BF16)
        vc_o[...] = vc.T.astype(BF16)
        qd_o[...] = qd.astype(BF16)
        kd_o[...] = kd.astype(BF16)
        vd_o[...] = vd.T.astype(BF16)
        for o, v in ((ckv_c, ckvn), (kpe_c, kpe[:, :MLA_ROPE]), (kc_c, kc), (vc_c, vc), (kd_c, kd), (vd_c, vd)):
            if n_alias:
                o[...] = v.reshape(o.shape)
            else:
                o[...] = jnp.zeros(o.shape, F32)
                o[:, 0] = v.reshape((o.shape[0],) + o.shape[2:])
    else:
        (z_o, gb_o, qmr_o, qm_o, km_o, vm_o, qcr_o, qc_o, kc_o, vc_o, qdr_o, qd_o, kd_o, vd_o) = outs
        c64, s64, c32, s32 = c64_ref[...], s64_ref[...], c32_ref[...], s32_ref[...]
        rope64 = lambda v: jnp.concatenate(
            [_rope(v[:, i:i + LANES], c64, s64, 16) for i in range(0, v.shape[-1], LANES)], axis=1)
        rope32 = lambda v: jnp.concatenate(
            [_rope(v[:, i:i + LANES], c32, s32, 8) for i in range(0, v.shape[-1], LANES)], axis=1)
        km = kv[:, :4 * LANES] + jnp.concatenate([_rope(kpe, c32, s32, 8)] * MLA_HEADS, axis=1)
        z_o[...] = z.astype(BF16)
        gb_o[...] = gb.astype(BF16)
        qmr_o[...] = rope32(qm).astype(BF16)
        qm_o[...] = qm.astype(BF16)
        km_o[...] = km.astype(BF16)
        vm_o[...] = vm.T.astype(BF16)
        qcr_o[...] = rope64(qc).astype(BF16)
        qc_o[...] = qc.astype(BF16)
        kc_o[...] = rope64(kc).astype(BF16)
        vc_o[...] = vc.T.astype(BF16)
        qdr_o[...] = rope64(qd).astype(BF16)
        qd_o[...] = qd.astype(BF16)
        kd_o[...] = rope64(kd).astype(BF16)
        vd_o[...] = vd.T.astype(BF16)


def _inproj(x, mod, lw, l, seq_len, cond_base, rope_tabs, prev_caches=None):
    rows = x.shape[0]
    r = ROW_TILE
    rope = rope_tabs is not None
    aliases = {}
    tiles_per_seq = max(seq_len // r, 1)
    step = 1 if seq_len >= r else 0

    def cond_map(i):
        return (l, cond_base + step * (i // tiles_per_seq), 0, 0)

    row = lambda w: pl.BlockSpec((r, w), lambda i: (i, 0))
    full = lambda a: pl.BlockSpec((None,) + a.shape[1:], lambda i: (l,) + (0,) * (a.ndim - 1))
    in_specs = [row(D_MODEL),
                pl.BlockSpec((None, None, 6, D_MODEL), cond_map),
                full(lw['norm1']), full(lw['w_in']), full(lw['wq']), full(lw['wkv']),
                full(lw['gq']), full(lw['gkv']), full(lw['gqc']), full(lw['gkc']),
                pl.BlockSpec((2 * LANES, 2 * LANES), lambda i: (0, 0))]
    args = [x, mod, lw['norm1'], lw['w_in'], lw['wq'], lw['wkv'], lw['gq'], lw['gkv'], lw['gqc'], lw['gkc'],
            lw['hm']]
    def act(w, transposed):
        if transposed:
            return jax.ShapeDtypeStruct((w, rows), BF16), pl.BlockSpec((w, r), lambda i: (0, i))
        return jax.ShapeDtypeStruct((rows, w), BF16), row(w)

    if rope:
        tab = pl.BlockSpec((r, LANES), lambda i: (i % tiles_per_seq, 0))
        in_specs += [tab] * 4
        args += list(rope_tabs)
        widths = [(256, 0), (256, 0), (512, 0), (512, 0), (512, 0), (256, 1), (256, 0), (256, 0), (128, 0),
                  (128, 1), (256, 0), (256, 0), (128, 0), (128, 1)]
        out_shape, out_specs = map(list, zip(*[act(w, t) for w, t in widths]))
    else:
        widths = [(256, 0), (256, 0), (512, 0), (512, 0), (256, 1), (256, 0), (128, 0), (128, 1), (256, 0),
                  (128, 0), (128, 1)]
        cwidths = [128, MLA_ROPE, 128, 128, 128, 128]
        nb = r // seq_len
        out_shape, out_specs = map(list, zip(*[act(w, t) for w, t in widths]))
        out_shape += [jax.ShapeDtypeStruct((rows // seq_len, DEPTH, seq_len, w), F32) for w in cwidths]
        assert (prev_caches is None) == (l == 0)
        if l == 0:
            out_specs += [pl.BlockSpec((nb, DEPTH, seq_len, w), lambda i: (i, 0, 0, 0)) for w in cwidths]
        else:
            out_specs += [pl.BlockSpec((nb, None, seq_len, w), lambda i: (i, l, 0, 0)) for w in cwidths]
        if prev_caches is not None:
            aliases = {len(args) + k: len(widths) + k for k in range(len(cwidths))}
            in_specs += [pl.BlockSpec(memory_space=pl.ANY)] * len(cwidths)
            args += list(prev_caches)
    return pl.pallas_call(
        functools.partial(_inproj_body, rope, len(aliases)),
        grid=(rows // r,),
        in_specs=in_specs, out_specs=out_specs, out_shape=out_shape,
        input_output_aliases=aliases,
        compiler_params=pltpu.CompilerParams(
            dimension_semantics=("arbitrary",), vmem_limit_bytes=VMEM_LIMIT),
        name="inproj_rope" if rope else "inproj",
    )(*args)


def _softmax_pv_t(s_list, vt_list, extra=None):
    m = s_list[0].max(axis=0, keepdims=True)
    for s in s_list[1:]:
        m = jnp.maximum(m, s.max(axis=0, keepdims=True))
    if extra is not None:
        m = jnp.maximum(m, extra)
    den = jnp.exp2(extra - m) if extra is not None else 0.0
    out = None
    for s, vt in zip(s_list, vt_list):
        p = jnp.exp2(s - m)
        den = den + p.sum(axis=0, keepdims=True)
        pv = _dot(vt, p.astype(BF16))
        out = pv if out is None else out + pv
    return out * (1.0 / den)


def _half_masks(q):
    lo = _lane(q.shape) < HEAD_DIM
    zero = jnp.zeros_like(q)
    return jnp.where(lo, q, zero), jnp.where(lo, zero, q)


def _merge_rows(a, b):
    r = lax.broadcasted_iota(jnp.int32, a.shape, 0)
    return jnp.where(r < HEAD_DIM, a, b)


def _sink_row(sink_ref, l, h0, h1, tq):
    c = lax.broadcasted_iota(jnp.int32, (1, 2 * tq), 1)
    return jnp.where(c < tq, sink_ref[l, h0], sink_ref[l, h1]) * LOG2E


def _run_groups(stage1, stage2, n):
    out = []
    nxt = stage1(0)
    for g in range(n):
        cur = nxt
        if g + 1 < n:
            nxt = stage1(g + 1)
        out.append(stage2(g, cur))
    return out


def _mla_scores(h, q_sets, k_sets):
    res = []
    for q, k in zip(q_sets, k_sets):
        cs = [slice(hh * LANES, (hh + 1) * LANES) for hh in (2 * h, 2 * h + 1)]
        res.append(jnp.concatenate([_dot_nt(k[:, c], q[:, c]) for c in cs], axis=1))
    return res


def _gqa_scores(j, q_sets, k_sets):
    res = []
    for q, k in zip(q_sets, k_sets):
        qa = _half_masks(q[:, 0:LANES])[j]
        qb = _half_masks(q[:, LANES:])[j]
        res.append(_dot_nt(k[...], jnp.concatenate([qa, qb], axis=0)))
    return res


def _store_pair(y_ref, rows, blk, ot):
    y_ref[rows, blk * LANES:(blk + 1) * LANES] = ot.T.astype(BF16)


def _attn_ctx_body(l, nb, t, sink_ref, qm_ref, km_ref, vm_ref, qc_ref, kc_ref, vc_ref, qd_ref, kd_ref, vd_ref,
                   yb_ref, yc_ref, yd_ref):
    def stage1(g):
        b, k = divmod(g, 6)
        rs = pl.ds(b * t, t)
        if k < 2:
            return _mla_scores(k, [qm_ref.at[rs]], [km_ref.at[rs]])
        q_ref, k_ref = (qc_ref, kc_ref) if k < 4 else (qd_ref, kd_ref)
        return _gqa_scores(k % 2, [q_ref.at[rs]], [k_ref.at[rs]])

    def stage2(g, s):
        b, k = divmod(g, 6)
        rs = slice(b * t, (b + 1) * t)
        if k < 2:
            return _softmax_pv_t(s, [vm_ref[k * LANES:(k + 1) * LANES, rs]])
        if k < 4:
            return _softmax_pv_t(s, [vc_ref[:, rs]])
        j = k % 2
        return _softmax_pv_t(s, [vd_ref[:, rs]], _sink_row(sink_ref, l, 2 * j, 2 * j + 1, t))

    o = _run_groups(stage1, stage2, 6 * nb)
    for b in range(nb):
        rs = slice(b * t, (b + 1) * t)
        ob = o[6 * b:6 * b + 6]
        for p in range(2):
            _store_pair(yb_ref, rs, p, _merge_rows(ob[p][:, :t], ob[p][:, t:]))
        for y_ref, (o0, o1) in ((yc_ref, ob[2:4]), (yd_ref, ob[4:6])):
            _store_pair(y_ref, rs, 0, _merge_rows(o0[:, :t], o1[:, :t]))
            _store_pair(y_ref, rs, 1, _merge_rows(o0[:, t:], o1[:, t:]))


def _attn_ctx(l, t, sink, qm, km, vm, qc, kc, vc, qd, kd, vd):
    rows = qm.shape[0]
    nb = 2
    r = nb * t
    row = lambda a: (pl.BlockSpec((a.shape[0], r), lambda i: (0, i)) if a.shape[1] == rows
                     else pl.BlockSpec((r, a.shape[1]), lambda i: (i, 0)))
    ins = [qm, km, vm, qc, kc, vc, qd, kd, vd]
    return pl.pallas_call(
        functools.partial(_attn_ctx_body, l, nb, t),
        grid=(rows // r,),
        in_specs=[pl.BlockSpec(memory_space=pltpu.SMEM)] + [row(a) for a in ins],
        out_specs=[pl.BlockSpec((r, 2 * LANES), lambda i: (i, 0))] * 3,
        out_shape=[jax.ShapeDtypeStruct((rows, 2 * LANES), BF16)] * 3,
        compiler_params=pltpu.CompilerParams(
            dimension_semantics=("arbitrary",), vmem_limit_bytes=VMEM_LIMIT),
        name="attn_ctx",
    )(sink, *ins)


def _window_block(i, tq, t, nwb):
    return jnp.clip(i * (tq // LANES) - WINDOW // LANES, 0, t // LANES - nwb)


def _attn_lat_body(l, tq, t, nwb, sink_ref, qmr_ref, qm_ref, km_ref, vm_ref, kx_ref, vx_ref,
                   qcr_ref, qc_ref, kc_ref, vc_ref, kcx_ref, vcx_ref,
                   qdr_ref, qd_ref, kdx_ref, vdx_ref, *rest):
    kd_refs, vd_refs = rest[:nwb], rest[nwb:2 * nwb]
    yb_ref, yc_ref, yd_ref = rest[2 * nwb:]
    i = pl.program_id(1)
    wk = nwb * LANES
    kpos = _window_block(i, tq, t, nwb) * LANES + lax.broadcasted_iota(jnp.int32, (wk, 2 * tq), 0)
    qpos = i * tq + lax.broadcasted_iota(jnp.int32, (wk, 2 * tq), 1) % tq
    valid = jnp.abs(kpos - qpos) <= WINDOW

    def stage1(g):
        if g < 2:
            return _mla_scores(g, [qmr_ref, qm_ref], [km_ref, kx_ref])
        if g < 4:
            return _gqa_scores(g % 2, [qcr_ref, qc_ref], [kc_ref, kcx_ref])
        kw = jnp.concatenate([r[...] for r in kd_refs], axis=0)
        s1, s2 = _gqa_scores(g % 2, [qdr_ref, qd_ref], [kw, kdx_ref])
        return [jnp.where(valid, s1, NEG_INF), s2]

    def stage2(g, s):
        if g < 2:
            blk = slice(g * LANES, (g + 1) * LANES)
            return _softmax_pv_t(s, [vm_ref[blk, :], vx_ref[blk, :]])
        if g < 4:
            return _softmax_pv_t(s, [vc_ref[...], vcx_ref[...]])
        j = g % 2
        vw = jnp.concatenate([r[...] for r in vd_refs], axis=1)
        return _softmax_pv_t(s, [vw, vdx_ref[...]], _sink_row(sink_ref, l, 2 * j, 2 * j + 1, tq))

    o = _run_groups(stage1, stage2, 6)
    rs = slice(None)
    for p in range(2):
        _store_pair(yb_ref, rs, p, _merge_rows(o[p][:, :tq], o[p][:, tq:]))
    for y_ref, (o0, o1) in ((yc_ref, o[2:4]), (yd_ref, o[4:6])):
        _store_pair(y_ref, rs, 0, _merge_rows(o0[:, :tq], o1[:, :tq]))
        _store_pair(y_ref, rs, 1, _merge_rows(o0[:, tq:], o1[:, tq:]))


def _attn_lat(l, nbatch, t, sink, qmr, qm, km, vmt, kx, vxt, qcr, qc, kc, vct, kcx, vcxt, qdr, qd, kd, vdt,
              kdx, vdxt):
    tq = 256
    nq = t // tq
    nwb = (tq + 2 * WINDOW) // LANES
    nkb = t // LANES
    rows = qm.shape[0]
    qs = lambda a: pl.BlockSpec((tq, a.shape[1]), lambda b, i: (b * nq + i, 0))
    ks = lambda a: pl.BlockSpec((t, a.shape[1]), lambda b, i: (b, 0))
    kts = lambda a: pl.BlockSpec((a.shape[0], t), lambda b, i: (0, b))
    xs = lambda a: pl.BlockSpec((None, None) + a.shape[2:], lambda b, i: (l, b, 0, 0))
    cs = lambda a: pl.BlockSpec((None, None) + a.shape[2:], lambda b, i: (b, l, 0, 0))
    wblk = lambda b, i, k: b * nkb + _window_block(i, tq, t, nwb) + k
    kd_specs = [pl.BlockSpec((LANES, LANES), lambda b, i, k=k: (wblk(b, i, k), 0)) for k in range(nwb)]
    vd_specs = [pl.BlockSpec((LANES, LANES), lambda b, i, k=k: (0, wblk(b, i, k))) for k in range(nwb)]
    in_specs = [pl.BlockSpec(memory_space=pltpu.SMEM),
                qs(qmr), qs(qm), ks(km), kts(vmt), xs(kx), xs(vxt),
                qs(qcr), qs(qc), ks(kc), kts(vct), cs(kcx), cs(vcxt),
                qs(qdr), qs(qd), cs(kdx), cs(vdxt)] + kd_specs + vd_specs
    return pl.pallas_call(
        functools.partial(_attn_lat_body, l, tq, t, nwb),
        grid=(nbatch, nq),
        in_specs=in_specs,
        out_specs=[pl.BlockSpec((tq, 2 * LANES), lambda b, i: (b * nq + i, 0))] * 3,
        out_shape=[jax.ShapeDtypeStruct((rows, 2 * LANES), BF16)] * 3,
        compiler_params=pltpu.CompilerParams(
            dimension_semantics=("arbitrary", "arbitrary"), vmem_limit_bytes=VMEM_LIMIT),
        name="attn_lat",
    )(sink, qmr, qm, km, vmt, kx, vxt, qcr, qc, kc, vct, kcx, vcxt, qdr, qd, kdx, vdxt,
      *([kd] * nwb), *([vdt] * nwb))


def _post_body(seq_len, final, mod_ref, n2_ref, cwa_ref, wo_ref, cwf_ref, wup_ref, wdn_ref, fn_ref, *rest):
    ins, (o_ref, hbuf, gbuf, pbuf, abuf) = rest[:18], rest[18:]
    cat = lambda k: jnp.concatenate([ins[3 * k][...], ins[3 * k + 1][...], ins[3 * k + 2][...]], axis=0)
    r = o_ref.shape[0]
    halo = BF16_ROWS
    n = r + 2 * halo
    grp = r // 8
    nchunk = D_MODEL // LANES
    i = pl.program_id(0)

    zz = cat(1).astype(F32)
    pos = (i * r - halo + lax.broadcasted_iota(jnp.int32, (n, 1), 0)) % seq_len
    prev = jnp.where(pos == 0, 0.0, pltpu.roll(zz, 1, 0))
    nxt = jnp.where(pos == seq_len - 1, 0.0, pltpu.roll(zz, n - 1, 0))
    cwa = cwa_ref[...]
    ya = cat(2).astype(F32) * (prev * cwa[0:1] + zz * cwa[1:2] + nxt * cwa[2:3])
    ycat = jnp.concatenate([ya.astype(BF16), cat(3), cat(4), cat(5)], axis=1)
    x1 = cat(0) + mod_ref[2:3, :] * _dot(ycat, wo_ref[...])
    h2 = _rms(x1, n2_ref[...], D_MODEL) * (1.0 + mod_ref[4:5, :]) + mod_ref[3:4, :]
    x1c = x1[halo:halo + r]

    for c in range(nchunk):
        for s in range(8):
            pbuf[c, pl.ds(s, grp, stride=8), :] = h2[halo + s * grp:halo + (s + 1) * grp, c * LANES:(c + 1) * LANES]
    hbuf[0:halo, :] = h2[0:halo].astype(BF16)
    hbuf[halo:halo + r, :] = jnp.concatenate([pbuf[c] for c in range(nchunk)], axis=1).astype(BF16)
    hbuf[halo + r:, :] = h2[halo + r:].astype(BF16)

    sub = lax.broadcasted_iota(jnp.int32, (8, 1), 0)
    seq_first = (i * r + sub * grp) % seq_len == 0
    seq_last = (i * r + sub * grp + grp - 1) % seq_len == seq_len - 1

    def up(j):
        ga = slice(j * FF_TILE, (j + 1) * FF_TILE)
        va = slice(D_FF + j * FF_TILE, D_FF + (j + 1) * FF_TILE)
        return _dot(hbuf[...], wup_ref[:, ga]), _dot(hbuf[...], wup_ref[:, va])

    def conv(u, cw):
        head = jnp.where(sub == 0, pltpu.roll(u[halo - 8:halo], 1, 0), pltpu.roll(u[halo + r - 8:halo + r], 1, 0))
        tail = jnp.where(sub == 7, pltpu.roll(u[halo + r:halo + r + 8], 7, 0), pltpu.roll(u[halo:halo + 8], 7, 0))
        prev = jnp.concatenate([jnp.where(seq_first, 0.0, head), u[halo:halo + r - 8]], axis=0)
        nxt = jnp.concatenate([u[halo + 8:halo + r], jnp.where(seq_last, 0.0, tail)], axis=0)
        return prev * cw[0:1] + u[halo:halo + r] * cw[1:2] + nxt * cw[2:3]

    acc = None
    nxt_u = up(0)
    for j in range(N_FF_TILES):
        ua, ub = nxt_u
        if j + 1 < N_FF_TILES:
            nxt_u = up(j + 1)
        ga = slice(j * FF_TILE, (j + 1) * FF_TILE)
        va = slice(D_FF + j * FF_TILE, D_FF + (j + 1) * FF_TILE)
        ua = conv(ua, cwf_ref[:, ga])
        ub = conv(ub, cwf_ref[:, va])
        gbuf[:, ga] = (ua * jax.nn.sigmoid(ua) * ub).astype(BF16)
        if j % DOWN_CHUNK == DOWN_CHUNK - 1 or j == N_FF_TILES - 1:
            lo = (j // DOWN_CHUNK) * DOWN_CHUNK * FF_TILE
            hi = (j + 1) * FF_TILE
            d = _dot(gbuf[:, lo:hi], wdn_ref[lo:hi, :])
            acc = d if acc is None else acc + d

    for c in range(nchunk):
        abuf[c] = acc[:, c * LANES:(c + 1) * LANES]
    g2 = mod_ref[5:6, :]
    for c in range(nchunk):
        cs = slice(c * LANES, (c + 1) * LANES)
        for s in range(8):
            rs = slice(s * grp, (s + 1) * grp)
            o_ref[rs, cs] = x1c[rs, cs] + g2[:, cs] * abuf[c, pl.ds(s, grp, stride=8), :]
    if final:
        o_ref[...] = _rms(o_ref[...], fn_ref[...], D_MODEL)


def _halo_specs(r, w, rows):
    nblk = rows // BF16_ROWS
    per = r // BF16_ROWS
    prev = pl.BlockSpec((BF16_ROWS, w), lambda i: (jnp.maximum(i * per - 1, 0), 0))
    cur = pl.BlockSpec((r, w), lambda i: (i, 0))
    nxt = pl.BlockSpec((BF16_ROWS, w), lambda i: (jnp.minimum((i + 1) * per, nblk - 1), 0))
    return prev, cur, nxt


def _cond_spec(l, seq_len, cond_base):
    r = ROW_TILE
    tiles_per_seq = max(seq_len // r, 1)
    step = 1 if seq_len >= r else 0
    return pl.BlockSpec((None, None, 6, D_MODEL),
                        lambda i: (l, cond_base + step * (i // tiles_per_seq), 0, 0))


def _post(x, mod, lw, l, seq_len, cond_base, z, gb, yb, yc, yd, final_norm, final):
    rows = x.shape[0]
    r = ROW_TILE
    full = lambda a: pl.BlockSpec((None,) + a.shape[1:], lambda i: (l,) + (0,) * (a.ndim - 1),
                                  pipeline_mode=pl.Buffered(1))
    acts = [x, z, gb, yb, yc, yd]
    act_specs = [s for a in acts for s in _halo_specs(r, a.shape[1], rows)]
    act_args = [a for a in acts for _ in range(3)]
    return pl.pallas_call(
        functools.partial(_post_body, seq_len, final),
        grid=(rows // r,),
        in_specs=[_cond_spec(l, seq_len, cond_base), full(lw['norm2']), full(lw['conv_a']), full(lw['w_out']),
                  full(lw['conv_ff']), full(lw['w_up']), full(lw['w_down']),
                  pl.BlockSpec((1, D_MODEL), lambda i: (0, 0))] + act_specs,
        out_specs=pl.BlockSpec((r, D_MODEL), lambda i: (i, 0)),
        out_shape=jax.ShapeDtypeStruct((rows, D_MODEL), F32),
        scratch_shapes=[pltpu.VMEM((r + 2 * BF16_ROWS, D_MODEL), BF16), pltpu.VMEM((r, D_FF), BF16),
                        pltpu.VMEM((D_MODEL // LANES, r, LANES), F32),
                        pltpu.VMEM((D_MODEL // LANES, r, LANES), F32)],
        compiler_params=pltpu.CompilerParams(
            dimension_semantics=("arbitrary",), vmem_limit_bytes=VMEM_LIMIT),
        name="post",
    )(mod, lw['norm2'], lw['conv_a'], lw['w_out'], lw['conv_ff'], lw['w_up'], lw['w_down'], final_norm,
      *act_args)


def _pad_cols(a, n):
    return jnp.pad(a, [(0, 0)] * (a.ndim - 1) + [(0, n - a.shape[-1])])


def _perm_heads(a, axis):
    h = [lax.slice_in_dim(a, k * HEAD_DIM, (k + 1) * HEAD_DIM, axis=axis) for k in range(4)]
    return jnp.concatenate([h[0], h[2], h[1], h[3]], axis=axis)


def _prep_weights(w_in, mla_wq_b, mla_wkv_b, w_out, w_up, conv_ff, w_down, norm1, norm2, conv_a,
                  mla_q_norm, mla_kv_norm, gqa_q_norm, gqa_k_norm):
    L = DEPTH
    a = w_in[..., 0:768]
    cq = _pad_cols(w_in[..., 768:960], 256)
    ckv = w_in[..., 960:1088]
    kpe = _pad_cols(w_in[..., 1088:1120], LANES)
    qc = _perm_heads(w_in[..., 1120:1376], 2)
    kvc = w_in[..., 1376:1632]
    qd = _perm_heads(w_in[..., 1632:1888], 2)
    kvd = w_in[..., 1888:2144]
    w_in_p = jnp.concatenate([a, cq, ckv, kpe, qc, kvc, qd, kvd], axis=-1).astype(BF16)

    wq = mla_wq_b.reshape(L, MLA_Q_LORA, MLA_HEADS, MLA_NOPE + MLA_ROPE)
    wq = jnp.concatenate([wq[..., MLA_NOPE:], wq[..., :MLA_NOPE],
                          jnp.zeros((L, MLA_Q_LORA, MLA_HEADS, LANES - MLA_NOPE - MLA_ROPE), F32)], axis=-1)
    wq = jnp.pad(wq.reshape(L, MLA_Q_LORA, MLA_HEADS * LANES), ((0, 0), (0, 256 - MLA_Q_LORA), (0, 0)))

    wkv = mla_wkv_b.reshape(L, MLA_KV_LORA, MLA_HEADS, MLA_NOPE + MLA_V)
    zk = jnp.zeros((L, MLA_KV_LORA, MLA_HEADS, MLA_ROPE), F32)
    wk = jnp.concatenate([zk, wkv[..., :MLA_NOPE], zk], axis=-1).reshape(L, MLA_KV_LORA, MLA_HEADS * LANES)
    wv = wkv[..., MLA_NOPE:].reshape(L, MLA_KV_LORA, MLA_HEADS * MLA_V)
    wkv_p = jnp.concatenate([wk, wv], axis=-1)

    wo = jnp.concatenate([w_out[:, 0:512], _perm_heads(w_out[:, 512:768], 1),
                          _perm_heads(w_out[:, 768:1024], 1)], axis=1)

    hm = jnp.asarray(np.kron(np.eye(2 * LANES // HEAD_DIM), np.ones((HEAD_DIM, HEAD_DIM))), BF16)
    return {
        'w_in': w_in_p, 'wq': wq.astype(BF16), 'wkv': wkv_p.astype(BF16), 'w_out': wo.astype(BF16),
        'w_up': w_up.astype(BF16), 'conv_ff': conv_ff, 'w_down': w_down.astype(BF16),
        'norm1': norm1.reshape(L, 1, D_MODEL), 'norm2': norm2.reshape(L, 1, D_MODEL), 'conv_a': conv_a,
        'gq': _pad_cols(mla_q_norm, 256).reshape(L, 1, 256), 'gkv': mla_kv_norm.reshape(L, 1, MLA_KV_LORA),
        'gqc': jnp.tile(gqa_q_norm, (1, 4)).reshape(L, 1, 256), 'gkc': jnp.tile(gqa_k_norm, (1, 2)).reshape(L, 1, 128),
        'hm': hm,
    }


def _rope_tables(t):
    rows = t // GRID_W
    row = np.repeat(np.arange(rows, dtype=np.float64), GRID_W)
    col = np.tile(np.arange(GRID_W, dtype=np.float64), rows)

    def tabs(dim):
        half = dim // 2
        inv = np.power(ROPE_THETA, -np.arange(0, half, 2, dtype=np.float64) / half)
        ar = row[:, None] * inv
        ac = col[:, None] * inv
        c = np.concatenate([np.cos(ar), np.cos(ar), np.cos(ac), np.cos(ac)], axis=1)
        s = np.concatenate([-np.sin(ar), np.sin(ar), -np.sin(ac), np.sin(ac)], axis=1)
        return c, s

    c64, s64 = tabs(HEAD_DIM)
    c32, s32 = tabs(MLA_ROPE)
    c64 = np.tile(c64, (1, 2))
    s64 = np.tile(s64, (1, 2))
    c32 = np.concatenate([c32, np.ones((t, LANES - MLA_ROPE))], axis=1)
    s32 = np.concatenate([s32, np.zeros((t, LANES - MLA_ROPE))], axis=1)
    return tuple(jnp.asarray(a.astype(np.float32)) for a in (c64, s64, c32, s32))


def kernel(x_prompt, x_sample, cache_mla_ckv, cache_mla_kpe, cache_gqa_k, cache_gqa_v, cache_swa_k, cache_swa_v,
           c, c_ctx, w_ada, b_ada, norm1, w_in, conv_a, mla_q_norm, mla_wq_b, mla_kv_norm, mla_wkv_b,
           gqa_q_norm, gqa_k_norm, swa_sink, w_out, norm2, w_up, conv_ff, w_down, final_norm):
    B, T, _ = x_prompt.shape
    DB, DT, _ = x_sample.shape
    past = cache_mla_ckv.shape[2]

    lw = _prep_weights(w_in, mla_wq_b, mla_wkv_b, w_out, w_up, conv_ff, w_down, norm1, norm2, conv_a,
                       mla_q_norm, mla_kv_norm, gqa_q_norm, gqa_k_norm)
    rope_tabs = _rope_tables(DT)
    fnorm = final_norm.reshape(1, D_MODEL)

    cond8 = jnp.concatenate([c_ctx[None, :], c, jnp.zeros((8 - 1 - DB, D_MODEL), F32)], axis=0)
    mod = _ada(cond8, w_ada, b_ada).reshape(DEPTH, 8, 6, D_MODEL)

    kx, vx = _ctx_mla(cache_mla_ckv, _pad_cols(cache_mla_kpe, LANES), lw['wkv'])
    flat = lambda a: a.reshape(DB, DEPTH, past, 2 * HEAD_DIM).astype(BF16)
    kcx, kdx = flat(cache_gqa_k), flat(cache_swa_k)
    vcx, vdx = jnp.swapaxes(flat(cache_gqa_v), 2, 3), jnp.swapaxes(flat(cache_swa_v), 2, 3)

    xp = x_prompt.reshape(B * T, D_MODEL)
    xs = x_sample.reshape(DB * DT, D_MODEL)
    caches = None
    for l in range(DEPTH):
        final = l == DEPTH - 1
        (z, gb, qm, km, vm, qc, kc, vc, qd, kd, vd, *caches) = _inproj(xp, mod, lw, l, T, 0, None, caches)
        yb, yc, yd = _attn_ctx(l, T, swa_sink, qm, km, vm, qc, kc, vc, qd, kd, vd)
        xp = _post(xp, mod, lw, l, T, 0, z, gb, yb, yc, yd, fnorm, final)
        (z, gb, qmr, qm, km, vm, qcr, qc, kc, vc, qdr, qd, kd, vd) = _inproj(xs, mod, lw, l, DT, 1, rope_tabs)
        yb, yc, yd = _attn_lat(l, DB, DT, swa_sink, qmr, qm, km, vm, kx, vx, qcr, qc, kc, vc, kcx, vcx,
                               qdr, qd, kd, vd, kdx, vdx)
        xs = _post(xs, mod, lw, l, DT, 1, z, gb, yb, yc, yd, fnorm, final)

    heads = lambda a: a.reshape(B, DEPTH, T, 2, HEAD_DIM)
    return (xp.reshape(B, T, D_MODEL), xs.reshape(DB, DT, D_MODEL), caches[0], caches[1],
            heads(caches[2]), heads(caches[3]), heads(caches[4]), heads(caches[5]))
```

```python
import functools
import math

import jax
import jax.numpy as jnp
import numpy as np
from jax import lax
from jax.experimental import pallas as pl
from jax.experimental.pallas import tpu as pltpu

F32 = jnp.float32
BF16 = jnp.bfloat16

D_MODEL = 1024
DEPTH = 2
GRID_W = 64
HEAD_DIM = 64
GROUP_WIDTH = D_MODEL // 4
MLA_HEADS = 4
MLA_NOPE = 64
MLA_ROPE = 32
MLA_V = 64
MLA_Q_LORA = 192
MLA_KV_LORA = 128
WINDOW = 128
D_FF = 2816
ROPE_THETA = 10000.0
EPS = 1e-6
NEG_INF = -1e30
LOG2E = math.log2(math.e)
ATTN_SCALE = HEAD_DIM ** -0.5 * LOG2E
MLA_SCALE = (MLA_NOPE + MLA_ROPE) ** -0.5 * LOG2E

LANES = 128
BF16_ROWS = 16
ROW_TILE = 512
FF_TILE = 256
N_FF_TILES = D_FF // FF_TILE
DOWN_CHUNK = 4
IN_COLS_PACKED = 2304
VMEM_LIMIT = 56 * 1024 * 1024

_NT = (((1,), (1,)), ((), ()))


def _dot(a, b):
    return jnp.dot(a, b, preferred_element_type=F32)


def _dot_nt(a, b):
    return lax.dot_general(a, b, _NT, preferred_element_type=F32)


def _rms(x, g, n):
    ms = jnp.sum(x * x, axis=-1, keepdims=True) * (1.0 / n)
    return x * lax.rsqrt(ms + EPS) * g


def _lane(shape):
    return lax.broadcasted_iota(jnp.int32, shape, len(shape) - 1)


def _rope(x, c, s, half):
    w = x.shape[-1]
    lo = (_lane(x.shape) % (2 * half)) < half
    sw = jnp.where(lo, pltpu.roll(x, w - half, 1), pltpu.roll(x, half, 1))
    return x * c + sw * s


def _shift_rows(zz, halo, rows):
    n = zz.shape[0]
    prev = pltpu.roll(zz, 1, 0)[halo:halo + rows]
    nxt = pltpu.roll(zz, n - 1, 0)[halo:halo + rows]
    return prev, nxt


def _seq_pos(tile, rows, seq_len):
    r = lax.broadcasted_iota(jnp.int32, (rows, 1), 0)
    return (tile * rows + r) % seq_len


def _ada_body(ncond, c_ref, w_ref, b_ref, o_ref):
    c = c_ref[...]
    s = c * jax.nn.sigmoid(c)
    tn = w_ref.shape[1]
    w = w_ref[...].reshape(D_MODEL // 8, 8, tn)
    rows = []
    for r in range(ncond):
        part = jnp.sum(w * s[:, r:r + 1].reshape(D_MODEL // 8, 8, 1), axis=0)
        rows.append(jnp.sum(part, axis=0, keepdims=True))
    rows.append(jnp.zeros((8 - ncond, tn), F32))
    o_ref[...] = jnp.concatenate(rows, axis=0) + b_ref[...]


def _ada(cond_t, ncond, w_ada, b_ada):
    tn = 1536
    n = 6 * D_MODEL
    return pl.pallas_call(
        functools.partial(_ada_body, ncond),
        grid=(DEPTH, n // tn),
        in_specs=[pl.BlockSpec((D_MODEL, 8), lambda l, j: (0, 0)),
                  pl.BlockSpec((None, D_MODEL, tn), lambda l, j: (l, 0, j)),
                  pl.BlockSpec((None, 1, tn), lambda l, j: (l, 0, j))],
        out_specs=pl.BlockSpec((None, 8, tn), lambda l, j: (l, 0, j)),
        out_shape=jax.ShapeDtypeStruct((DEPTH, 8, n), F32),
        compiler_params=pltpu.CompilerParams(
            dimension_semantics=("arbitrary", "arbitrary"), vmem_limit_bytes=VMEM_LIMIT),
        name="ada",
    )(cond_t, w_ada, b_ada.reshape(DEPTH, 1, n))


def _ctx_body(ckv_ref, kpe_ref, w_ref, kx_ref, vx_ref):
    kv = _dot(ckv_ref[...].astype(BF16), w_ref[...])
    kpe = kpe_ref[...]
    kx_ref[...] = (kv[:, :4 * LANES] + jnp.concatenate([kpe] * MLA_HEADS, axis=1)).astype(BF16)
    vx_ref[...] = kv[:, 4 * LANES:].T.astype(BF16)


def _ctx_mla(cache_ckv, cache_kpe_pad, wkv_p):
    b, _, s, _ = cache_ckv.shape
    return pl.pallas_call(
        _ctx_body,
        grid=(DEPTH, b),
        in_specs=[pl.BlockSpec((None, None, s, MLA_KV_LORA), lambda l, i: (i, l, 0, 0)),
                  pl.BlockSpec((None, None, s, LANES), lambda l, i: (i, l, 0, 0)),
                  pl.BlockSpec((None, MLA_KV_LORA, 6 * LANES), lambda l, i: (l, 0, 0))],
        out_specs=[pl.BlockSpec((None, None, s, 4 * LANES), lambda l, i: (l, i, 0, 0)),
                   pl.BlockSpec((None, None, 2 * LANES, s), lambda l, i: (l, i, 0, 0))],
        out_shape=[jax.ShapeDtypeStruct((DEPTH, b, s, 4 * LANES), BF16),
                   jax.ShapeDtypeStruct((DEPTH, b, 2 * LANES, s), BF16)],
        compiler_params=pltpu.CompilerParams(
            dimension_semantics=("arbitrary", "arbitrary"), vmem_limit_bytes=VMEM_LIMIT),
        name="ctx_mla",
    )(cache_ckv, cache_kpe_pad, wkv_p)


def _inproj_body(rope, n_alias, x_ref, mod_ref, n1_ref, w_ref, wq_ref, wkv_ref, gq_ref, gkv_ref, gqc_ref, gkc_ref,
                 hm_ref, *rest):
    if rope:
        c64_ref, s64_ref, c32_ref, s32_ref = rest[:4]
        outs = rest[4:]
    else:
        outs = rest[n_alias:]
    x = x_ref[...]
    sh1 = mod_ref[0:1, :]
    sc1 = mod_ref[1:2, :]
    h = _rms(x, n1_ref[...], D_MODEL) * (1.0 + sc1) + sh1
    acc = _dot(h.astype(BF16), w_ref[...])

    xa, gb, gc = acc[:, 0:256], acc[:, 256:512], acc[:, 512:768]
    z = gc * xa
    cq = acc[:, 768:1024]
    cqn = _rms(cq, gq_ref[...], MLA_Q_LORA)
    qm = _dot(cqn.astype(BF16), wq_ref[...]) * MLA_SCALE
    ckvn = _rms(acc[:, 1024:1152], gkv_ref[...], MLA_KV_LORA)
    kv = _dot(ckvn.astype(BF16), wkv_ref[...])
    kpe = acc[:, 1152:1280]
    vm = kv[:, 4 * LANES:]

    hm = hm_ref[...]

    def head_rms(v, g):
        v2 = v * v
        hi = v2.astype(BF16)
        lo = (v2 - hi.astype(F32)).astype(BF16)
        w = v.shape[-1]
        ss = _dot(hi, hm[:w, :w]) + _dot(lo, hm[:w, :w])
        return v * lax.rsqrt(ss * (1.0 / HEAD_DIM) + EPS) * g

    qc = head_rms(acc[:, 1280:1536], gqc_ref[...]) * ATTN_SCALE
    kc = head_rms(acc[:, 1536:1664], gkc_ref[...])
    vc = acc[:, 1664:1792]
    qd = acc[:, 1792:2048] * ATTN_SCALE
    kd = acc[:, 2048:2176]
    vd = acc[:, 2176:2304]

    if not rope:
        (z_o, gb_o, qm_o, km_o, vm_o, qc_o, kc_o, vc_o, qd_o, kd_o, vd_o,
         ckv_c, kpe_c, kc_c, vc_c, kd_c, vd_c) = outs
        km = kv[:, :4 * LANES] + jnp.concatenate([kpe] * MLA_HEADS, axis=1)
        z_o[...] = z.astype(BF16)
        gb_o[...] = gb.astype(BF16)
        qm_o[...] = qm.astype(BF16)
        km_o[...] = km.astype(BF16)
        vm_o[...] = vm.T.astype(BF16)
        qc_o[...] = qc.astype(BF16)
        kc_o[...] = kc.astype(BF16)
        vc_o[...] = vc.T.astype(BF16)
        qd_o[...] = qd.astype(BF16)
        kd_o[...] = kd.astype(BF16)
        vd_o[...] = vd.T.astype(BF16)
        for o, v in ((ckv_c, ckvn), (kpe_c, kpe[:, :MLA_ROPE]), (kc_c, kc), (vc_c, vc), (kd_c, kd), (vd_c, vd)):
            if n_alias:
                o[...] = v.reshape(o.shape)
            else:
                o[...] = jnp.zeros(o.shape, F32)
                o[:, 0] = v.reshape((o.shape[0],) + o.shape[2:])
    else:
        (z_o, gb_o, qmr_o, qm_o, km_o, vm_o, qcr_o, qc_o, kc_o, vc_o, qdr_o, qd_o, kd_o, vd_o) = outs
        c64, s64, c32, s32 = c64_ref[...], s64_ref[...], c32_ref[...], s32_ref[...]
        rope64 = lambda v: jnp.concatenate(
            [_rope(v[:, i:i + LANES], c64, s64, 16) for i in range(0, v.shape[-1], LANES)], axis=1)
        rope32 = lambda v: jnp.concatenate(
            [_rope(v[:, i:i + LANES], c32, s32, 8) for i in range(0, v.shape[-1], LANES)], axis=1)
        km = kv[:, :4 * LANES] + jnp.concatenate([_rope(kpe, c32, s32, 8)] * MLA_HEADS, axis=1)
        z_o[...] = z.astype(BF16)
        gb_o[...] = gb.astype(BF16)
        qmr_o[...] = rope32(qm).astype(BF16)
        qm_o[...] = qm.astype(BF16)
        km_o[...] = km.astype(BF16)
        vm_o[...] = vm.T.astype(BF16)
        qcr_o[...] = rope64(qc).astype(BF16)
        qc_o[...] = qc.astype(BF16)
        kc_o[...] = rope64(kc).astype(BF16)
        vc_o[...] = vc.T.astype(BF16)
        qdr_o[...] = rope64(qd).astype(BF16)
        qd_o[...] = qd.astype(BF16)
        kd_o[...] = rope64(kd).astype(BF16)
        vd_o[...] = vd.T.astype(BF16)


def _inproj(x, mod, lw, l, seq_len, cond_base, rope_tabs, prev_caches=None):
    rows = x.shape[0]
    r = ROW_TILE
    rope = rope_tabs is not None
    aliases = {}
    tiles_per_seq = max(seq_len // r, 1)
    step = 1 if seq_len >= r else 0

    def cond_map(i):
        return (l, cond_base + step * (i // tiles_per_seq), 0, 0)

    row = lambda w: pl.BlockSpec((r, w), lambda i: (i, 0))
    full = lambda a: pl.BlockSpec((None,) + a.shape[1:], lambda i: (l,) + (0,) * (a.ndim - 1))
    in_specs = [row(D_MODEL),
                pl.BlockSpec((None, None, 6, D_MODEL), cond_map),
                full(lw['norm1']), full(lw['w_in']), full(lw['wq']), full(lw['wkv']),
                full(lw['gq']), full(lw['gkv']), full(lw['gqc']), full(lw['gkc']),
                pl.BlockSpec((2 * LANES, 2 * LANES), lambda i: (0, 0))]
    args = [x, mod, lw['norm1'], lw['w_in'], lw['wq'], lw['wkv'], lw['gq'], lw['gkv'], lw['gqc'], lw['gkc'],
            lw['hm']]
    def act(w, transposed):
        if transposed:
            return jax.ShapeDtypeStruct((w, rows), BF16), pl.BlockSpec((w, r), lambda i: (0, i))
        return jax.ShapeDtypeStruct((rows, w), BF16), row(w)

    if rope:
        tab = pl.BlockSpec((r, LANES), lambda i: (i % tiles_per_seq, 0))
        in_specs += [tab] * 4
        args += list(rope_tabs)
        widths = [(256, 0), (256, 0), (512, 0), (512, 0), (512, 0), (256, 1), (256, 0), (256, 0), (128, 0),
                  (128, 1), (256, 0), (256, 0), (128, 0), (128, 1)]
        out_shape, out_specs = map(list, zip(*[act(w, t) for w, t in widths]))
    else:
        widths = [(256, 0), (256, 0), (512, 0), (512, 0), (256, 1), (256, 0), (128, 0), (128, 1), (256, 0),
                  (128, 0), (128, 1)]
        cwidths = [128, MLA_ROPE, 128, 128, 128, 128]
        nb = r // seq_len
        out_shape, out_specs = map(list, zip(*[act(w, t) for w, t in widths]))
        out_shape += [jax.ShapeDtypeStruct((rows // seq_len, DEPTH, seq_len, w), F32) for w in cwidths]
        assert (prev_caches is None) == (l == 0)
        if l == 0:
            out_specs += [pl.BlockSpec((nb, DEPTH, seq_len, w), lambda i: (i, 0, 0, 0)) for w in cwidths]
        else:
            out_specs += [pl.BlockSpec((nb, None, seq_len, w), lambda i: (i, l, 0, 0)) for w in cwidths]
        if prev_caches is not None:
            aliases = {len(args) + k: len(widths) + k for k in range(len(cwidths))}
            in_specs += [pl.BlockSpec(memory_space=pl.ANY)] * len(cwidths)
            args += list(prev_caches)
    return pl.pallas_call(
        functools.partial(_inproj_body, rope, len(aliases)),
        grid=(rows // r,),
        in_specs=in_specs, out_specs=out_specs, out_shape=out_shape,
        input_output_aliases=aliases,
        compiler_params=pltpu.CompilerParams(
            dimension_semantics=("arbitrary",), vmem_limit_bytes=VMEM_LIMIT),
        name="inproj_rope" if rope else "inproj",
    )(*args)


def _softmax_pv_t(s_list, vt_list, extra=None):
    m = s_list[0].max(axis=0, keepdims=True)
    for s in s_list[1:]:
        m = jnp.maximum(m, s.max(axis=0, keepdims=True))
    if extra is not None:
        m = jnp.maximum(m, extra)
    den = jnp.exp2(extra - m) if extra is not None else 0.0
    out = None
    for s, vt in zip(s_list, vt_list):
        p = jnp.exp2(s - m)
        den = den + p.sum(axis=0, keepdims=True)
        pv = _dot(vt, p.astype(BF16))
        out = pv if out is None else out + pv
    return out * (1.0 / den)


def _half_masks(q):
    lo = _lane(q.shape) < HEAD_DIM
    zero = jnp.zeros_like(q)
    return jnp.where(lo, q, zero), jnp.where(lo, zero, q)


def _merge_rows(a, b):
    r = lax.broadcasted_iota(jnp.int32, a.shape, 0)
    return jnp.where(r < HEAD_DIM, a, b)


def _sink_row(sink_ref, l, h0, h1, tq):
    c = lax.broadcasted_iota(jnp.int32, (1, 2 * tq), 1)
    return jnp.where(c < tq, sink_ref[l, h0], sink_ref[l, h1]) * LOG2E


def _run_groups(stage1, stage2, n, ahead=2):
    out = []
    pending = [stage1(g) for g in range(min(ahead, n))]
    for g in range(n):
        if g + ahead < n:
            pending.append(stage1(g + ahead))
        out.append(stage2(g, pending.pop(0)))
    return out


def _mla_scores(h, q_sets, k_sets):
    res = []
    for q, k in zip(q_sets, k_sets):
        cs = [slice(hh * LANES, (hh + 1) * LANES) for hh in (2 * h, 2 * h + 1)]
        res.append(jnp.concatenate([_dot_nt(k[:, c], q[:, c]) for c in cs], axis=1))
    return res


def _gqa_scores(j, q_sets, k_sets):
    res = []
    for q, k in zip(q_sets, k_sets):
        qa = _half_masks(q[:, 0:LANES])[j]
        qb = _half_masks(q[:, LANES:])[j]
        res.append(_dot_nt(k[...], jnp.concatenate([qa, qb], axis=0)))
    return res


def _store_pair(y_ref, rows, blk, ot):
    y_ref[rows, blk * LANES:(blk + 1) * LANES] = ot.T.astype(BF16)


def _attn_ctx_body(l, nb, t, sink_ref, qm_ref, km_ref, vm_ref, qc_ref, kc_ref, vc_ref, qd_ref, kd_ref, vd_ref,
                   yb_ref, yc_ref, yd_ref):
    def stage1(g):
        b, k = divmod(g, 6)
        rs = pl.ds(b * t, t)
        if k < 2:
            return _mla_scores(k, [qm_ref.at[rs]], [km_ref.at[rs]])
        q_ref, k_ref = (qc_ref, kc_ref) if k < 4 else (qd_ref, kd_ref)
        return _gqa_scores(k % 2, [q_ref.at[rs]], [k_ref.at[rs]])

    def stage2(g, s):
        b, k = divmod(g, 6)
        rs = slice(b * t, (b + 1) * t)
        if k < 2:
            return _softmax_pv_t(s, [vm_ref[k * LANES:(k + 1) * LANES, rs]])
        if k < 4:
            return _softmax_pv_t(s, [vc_ref[:, rs]])
        j = k % 2
        return _softmax_pv_t(s, [vd_ref[:, rs]], _sink_row(sink_ref, l, 2 * j, 2 * j + 1, t))

    o = _run_groups(stage1, stage2, 6 * nb)
    for b in range(nb):
        rs = slice(b * t, (b + 1) * t)
        ob = o[6 * b:6 * b + 6]
        for p in range(2):
            _store_pair(yb_ref, rs, p, _merge_rows(ob[p][:, :t], ob[p][:, t:]))
        for y_ref, (o0, o1) in ((yc_ref, ob[2:4]), (yd_ref, ob[4:6])):
            _store_pair(y_ref, rs, 0, _merge_rows(o0[:, :t], o1[:, :t]))
            _store_pair(y_ref, rs, 1, _merge_rows(o0[:, t:], o1[:, t:]))


def _attn_ctx(l, t, sink, qm, km, vm, qc, kc, vc, qd, kd, vd):
    rows = qm.shape[0]
    nb = 2
    r = nb * t
    row = lambda a: (pl.BlockSpec((a.shape[0], r), lambda i: (0, i)) if a.shape[1] == rows
                     else pl.BlockSpec((r, a.shape[1]), lambda i: (i, 0)))
    ins = [qm, km, vm, qc, kc, vc, qd, kd, vd]
    return pl.pallas_call(
        functools.partial(_attn_ctx_body, l, nb, t),
        grid=(rows // r,),
        in_specs=[pl.BlockSpec(memory_space=pltpu.SMEM)] + [row(a) for a in ins],
        out_specs=[pl.BlockSpec((r, 2 * LANES), lambda i: (i, 0))] * 3,
        out_shape=[jax.ShapeDtypeStruct((rows, 2 * LANES), BF16)] * 3,
        compiler_params=pltpu.CompilerParams(
            dimension_semantics=("arbitrary",), vmem_limit_bytes=VMEM_LIMIT),
        name="attn_ctx",
    )(sink, *ins)


def _window_block(i, tq, t, nwb):
    return jnp.clip(i * (tq // LANES) - WINDOW // LANES, 0, t // LANES - nwb)


def _attn_lat_body(l, tq, t, nwb, sink_ref, qmr_ref, qm_ref, km_ref, vm_ref, kx_ref, vx_ref,
                   qcr_ref, qc_ref, kc_ref, vc_ref, kcx_ref, vcx_ref,
                   qdr_ref, qd_ref, kdx_ref, vdx_ref, *rest):
    kd_refs, vd_refs = rest[:nwb], rest[nwb:2 * nwb]
    yb_ref, yc_ref, yd_ref = rest[2 * nwb:]
    i = pl.program_id(1)
    wk = nwb * LANES
    kpos = _window_block(i, tq, t, nwb) * LANES + lax.broadcasted_iota(jnp.int32, (wk, 2 * tq), 0)
    qpos = i * tq + lax.broadcasted_iota(jnp.int32, (wk, 2 * tq), 1) % tq
    valid = jnp.abs(kpos - qpos) <= WINDOW

    def stage1(g):
        if g < 2:
            return _mla_scores(g, [qmr_ref, qm_ref], [km_ref, kx_ref])
        if g < 4:
            return _gqa_scores(g % 2, [qcr_ref, qc_ref], [kc_ref, kcx_ref])
        kw = jnp.concatenate([r[...] for r in kd_refs], axis=0)
        s1, s2 = _gqa_scores(g % 2, [qdr_ref, qd_ref], [kw, kdx_ref])
        return [jnp.where(valid, s1, NEG_INF), s2]

    def stage2(g, s):
        if g < 2:
            blk = slice(g * LANES, (g + 1) * LANES)
            return _softmax_pv_t(s, [vm_ref[blk, :], vx_ref[blk, :]])
        if g < 4:
            return _softmax_pv_t(s, [vc_ref[...], vcx_ref[...]])
        j = g % 2
        vw = jnp.concatenate([r[...] for r in vd_refs], axis=1)
        return _softmax_pv_t(s, [vw, vdx_ref[...]], _sink_row(sink_ref, l, 2 * j, 2 * j + 1, tq))

    o = _run_groups(stage1, stage2, 6)
    rs = slice(None)
    for p in range(2):
        _store_pair(yb_ref, rs, p, _merge_rows(o[p][:, :tq], o[p][:, tq:]))
    for y_ref, (o0, o1) in ((yc_ref, o[2:4]), (yd_ref, o[4:6])):
        _store_pair(y_ref, rs, 0, _merge_rows(o0[:, :tq], o1[:, :tq]))
        _store_pair(y_ref, rs, 1, _merge_rows(o0[:, tq:], o1[:, tq:]))


def _attn_lat(l, nbatch, t, sink, qmr, qm, km, vmt, kx, vxt, qcr, qc, kc, vct, kcx, vcxt, qdr, qd, kd, vdt,
              kdx, vdxt):
    tq = 256
    nq = t // tq
    nwb = (tq + 2 * WINDOW) // LANES
    nkb = t // LANES
    rows = qm.shape[0]
    qs = lambda a: pl.BlockSpec((tq, a.shape[1]), lambda b, i: (b * nq + i, 0))
    ks = lambda a: pl.BlockSpec((t, a.shape[1]), lambda b, i: (b, 0))
    kts = lambda a: pl.BlockSpec((a.shape[0], t), lambda b, i: (0, b))
    xs = lambda a: pl.BlockSpec((None, None) + a.shape[2:], lambda b, i: (l, b, 0, 0))
    cs = lambda a: pl.BlockSpec((None, None) + a.shape[2:], lambda b, i: (b, l, 0, 0))
    wblk = lambda b, i, k: b * nkb + _window_block(i, tq, t, nwb) + k
    kd_specs = [pl.BlockSpec((LANES, LANES), lambda b, i, k=k: (wblk(b, i, k), 0)) for k in range(nwb)]
    vd_specs = [pl.BlockSpec((LANES, LANES), lambda b, i, k=k: (0, wblk(b, i, k))) for k in range(nwb)]
    in_specs = [pl.BlockSpec(memory_space=pltpu.SMEM),
                qs(qmr), qs(qm), ks(km), kts(vmt), xs(kx), xs(vxt),
                qs(qcr), qs(qc), ks(kc), kts(vct), cs(kcx), cs(vcxt),
                qs(qdr), qs(qd), cs(kdx), cs(vdxt)] + kd_specs + vd_specs
    return pl.pallas_call(
        functools.partial(_attn_lat_body, l, tq, t, nwb),
        grid=(nbatch, nq),
        in_specs=in_specs,
        out_specs=[pl.BlockSpec((tq, 2 * LANES), lambda b, i: (b * nq + i, 0))] * 3,
        out_shape=[jax.ShapeDtypeStruct((rows, 2 * LANES), BF16)] * 3,
        compiler_params=pltpu.CompilerParams(
            dimension_semantics=("arbitrary", "arbitrary"), vmem_limit_bytes=VMEM_LIMIT),
        name="attn_lat",
    )(sink, qmr, qm, km, vmt, kx, vxt, qcr, qc, kc, vct, kcx, vcxt, qdr, qd, kdx, vdxt,
      *([kd] * nwb), *([vdt] * nwb))


def _post_body(seq_len, final, mod_ref, n2_ref, cwa_ref, wo_ref, cwf_ref, wup_ref, wdn_ref, fn_ref, *rest):
    ins, (o_ref, hbuf, gbuf, pbuf, abuf) = rest[:18], rest[18:]
    cat = lambda k: jnp.concatenate([ins[3 * k][...], ins[3 * k + 1][...], ins[3 * k + 2][...]], axis=0)
    r = o_ref.shape[0]
    halo = BF16_ROWS
    n = r + 2 * halo
    grp = r // 8
    nchunk = D_MODEL // LANES
    i = pl.program_id(0)

    zz = cat(1).astype(F32)
    pos = (i * r - halo + lax.broadcasted_iota(jnp.int32, (n, 1), 0)) % seq_len
    prev = jnp.where(pos == 0, 0.0, pltpu.roll(zz, 1, 0))
    nxt = jnp.where(pos == seq_len - 1, 0.0, pltpu.roll(zz, n - 1, 0))
    cwa = cwa_ref[...]
    ya = cat(2).astype(F32) * (prev * cwa[0:1] + zz * cwa[1:2] + nxt * cwa[2:3])
    ycat = jnp.concatenate([ya.astype(BF16), cat(3), cat(4), cat(5)], axis=1)
    x1 = cat(0) + mod_ref[2:3, :] * _dot(ycat, wo_ref[...])
    h2 = _rms(x1, n2_ref[...], D_MODEL) * (1.0 + mod_ref[4:5, :]) + mod_ref[3:4, :]
    x1c = x1[halo:halo + r]

    for c in range(nchunk):
        for s in range(8):
            pbuf[c, pl.ds(s, grp, stride=8), :] = h2[halo + s * grp:halo + (s + 1) * grp, c * LANES:(c + 1) * LANES]
    hrow = lax.broadcasted_iota(jnp.int32, (halo, 1), 0)
    edge = jnp.where(hrow == 0, h2[halo - 1:halo], jnp.where(hrow == 1, h2[halo + r:halo + r + 1], 0.0))
    hbuf[0:halo, :] = edge.astype(BF16)
    hbuf[halo:, :] = jnp.concatenate([pbuf[c] for c in range(nchunk)], axis=1).astype(BF16)

    sub = lax.broadcasted_iota(jnp.int32, (8, 1), 0)
    seq_first = (i * r + sub * grp) % seq_len == 0
    seq_last = (i * r + sub * grp + grp - 1) % seq_len == seq_len - 1

    def up(j):
        ga = slice(j * FF_TILE, (j + 1) * FF_TILE)
        va = slice(D_FF + j * FF_TILE, D_FF + (j + 1) * FF_TILE)
        return _dot(hbuf[...], wup_ref[:, ga]), _dot(hbuf[...], wup_ref[:, va])

    def conv(u, cw):
        head = jnp.where(sub == 0, u[0:8], pltpu.roll(u[halo + r - 8:halo + r], 1, 0))
        tail = jnp.where(sub == 7, pltpu.roll(u[0:8], 6, 0), pltpu.roll(u[halo:halo + 8], 7, 0))
        prev = jnp.concatenate([jnp.where(seq_first, 0.0, head), u[halo:halo + r - 8]], axis=0)
        nxt = jnp.concatenate([u[halo + 8:halo + r], jnp.where(seq_last, 0.0, tail)], axis=0)
        return prev * cw[0:1] + u[halo:halo + r] * cw[1:2] + nxt * cw[2:3]

    acc = None
    nxt_u = up(0)
    for j in range(N_FF_TILES):
        ua, ub = nxt_u
        if j + 1 < N_FF_TILES:
            nxt_u = up(j + 1)
        ga = slice(j * FF_TILE, (j + 1) * FF_TILE)
        va = slice(D_FF + j * FF_TILE, D_FF + (j + 1) * FF_TILE)
        ua = conv(ua, cwf_ref[:, ga])
        ub = conv(ub, cwf_ref[:, va])
        gbuf[:, ga] = (ua * jax.nn.sigmoid(ua) * ub).astype(BF16)
        if j % DOWN_CHUNK == DOWN_CHUNK - 1 or j == N_FF_TILES - 1:
            lo = (j // DOWN_CHUNK) * DOWN_CHUNK * FF_TILE
            hi = (j + 1) * FF_TILE
            d = _dot(gbuf[:, lo:hi], wdn_ref[lo:hi, :])
            acc = d if acc is None else acc + d

    for c in range(nchunk):
        abuf[c] = acc[:, c * LANES:(c + 1) * LANES]
    g2 = mod_ref[5:6, :]
    for c in range(nchunk):
        cs = slice(c * LANES, (c + 1) * LANES)
        for s in range(8):
            rs = slice(s * grp, (s + 1) * grp)
            o_ref[rs, cs] = x1c[rs, cs] + g2[:, cs] * abuf[c, pl.ds(s, grp, stride=8), :]
    if final:
        o_ref[...] = _rms(o_ref[...], fn_ref[...], D_MODEL)


def _halo_specs(r, w, rows):
    nblk = rows // BF16_ROWS
    per = r // BF16_ROWS
    prev = pl.BlockSpec((BF16_ROWS, w), lambda i: (jnp.maximum(i * per - 1, 0), 0))
    cur = pl.BlockSpec((r, w), lambda i: (i, 0))
    nxt = pl.BlockSpec((BF16_ROWS, w), lambda i: (jnp.minimum((i + 1) * per, nblk - 1), 0))
    return prev, cur, nxt


def _cond_spec(l, seq_len, cond_base):
    r = ROW_TILE
    tiles_per_seq = max(seq_len // r, 1)
    step = 1 if seq_len >= r else 0
    return pl.BlockSpec((None, None, 6, D_MODEL),
                        lambda i: (l, cond_base + step * (i // tiles_per_seq), 0, 0))


def _post(x, mod, lw, l, seq_len, cond_base, z, gb, yb, yc, yd, final_norm, final):
    rows = x.shape[0]
    r = ROW_TILE
    full = lambda a: pl.BlockSpec((None,) + a.shape[1:], lambda i: (l,) + (0,) * (a.ndim - 1),
                                  pipeline_mode=pl.Buffered(1))
    acts = [x, z, gb, yb, yc, yd]
    act_specs = [s for a in acts for s in _halo_specs(r, a.shape[1], rows)]
    act_args = [a for a in acts for _ in range(3)]
    return pl.pallas_call(
        functools.partial(_post_body, seq_len, final),
        grid=(rows // r,),
        in_specs=[_cond_spec(l, seq_len, cond_base), full(lw['norm2']), full(lw['conv_a']), full(lw['w_out']),
                  full(lw['conv_ff']), full(lw['w_up']), full(lw['w_down']),
                  pl.BlockSpec((1, D_MODEL), lambda i: (0, 0))] + act_specs,
        out_specs=pl.BlockSpec((r, D_MODEL), lambda i: (i, 0)),
        out_shape=jax.ShapeDtypeStruct((rows, D_MODEL), F32),
        scratch_shapes=[pltpu.VMEM((r + BF16_ROWS, D_MODEL), BF16), pltpu.VMEM((r, D_FF), BF16),
                        pltpu.VMEM((D_MODEL // LANES, r, LANES), F32),
                        pltpu.VMEM((D_MODEL // LANES, r, LANES), F32)],
        compiler_params=pltpu.CompilerParams(
            dimension_semantics=("arbitrary",), vmem_limit_bytes=VMEM_LIMIT),
        name="post",
    )(mod, lw['norm2'], lw['conv_a'], lw['w_out'], lw['conv_ff'], lw['w_up'], lw['w_down'], final_norm,
      *act_args)


def _pad_cols(a, n):
    return jnp.pad(a, [(0, 0)] * (a.ndim - 1) + [(0, n - a.shape[-1])])


def _perm_heads(a, axis):
    h = [lax.slice_in_dim(a, k * HEAD_DIM, (k + 1) * HEAD_DIM, axis=axis) for k in range(4)]
    return jnp.concatenate([h[0], h[2], h[1], h[3]], axis=axis)


def _prep_weights(w_in, mla_wq_b, mla_wkv_b, w_out, w_up, conv_ff, w_down, norm1, norm2, conv_a,
                  mla_q_norm, mla_kv_norm, gqa_q_norm, gqa_k_norm):
    L = DEPTH
    a = w_in[..., 0:768]
    cq = _pad_cols(w_in[..., 768:960], 256)
    ckv = w_in[..., 960:1088]
    kpe = _pad_cols(w_in[..., 1088:1120], LANES)
    qc = _perm_heads(w_in[..., 1120:1376], 2)
    kvc = w_in[..., 1376:1632]
    qd = _perm_heads(w_in[..., 1632:1888], 2)
    kvd = w_in[..., 1888:2144]
    w_in_p = jnp.concatenate([a, cq, ckv, kpe, qc, kvc, qd, kvd], axis=-1).astype(BF16)

    wq = mla_wq_b.reshape(L, MLA_Q_LORA, MLA_HEADS, MLA_NOPE + MLA_ROPE)
    wq = jnp.concatenate([wq[..., MLA_NOPE:], wq[..., :MLA_NOPE],
                          jnp.zeros((L, MLA_Q_LORA, MLA_HEADS, LANES - MLA_NOPE - MLA_ROPE), F32)], axis=-1)
    wq = jnp.pad(wq.reshape(L, MLA_Q_LORA, MLA_HEADS * LANES), ((0, 0), (0, 256 - MLA_Q_LORA), (0, 0)))

    wkv = mla_wkv_b.reshape(L, MLA_KV_LORA, MLA_HEADS, MLA_NOPE + MLA_V)
    zk = jnp.zeros((L, MLA_KV_LORA, MLA_HEADS, MLA_ROPE), F32)
    wk = jnp.concatenate([zk, wkv[..., :MLA_NOPE], zk], axis=-1).reshape(L, MLA_KV_LORA, MLA_HEADS * LANES)
    wv = wkv[..., MLA_NOPE:].reshape(L, MLA_KV_LORA, MLA_HEADS * MLA_V)
    wkv_p = jnp.concatenate([wk, wv], axis=-1)

    wo = jnp.concatenate([w_out[:, 0:512], _perm_heads(w_out[:, 512:768], 1),
                          _perm_heads(w_out[:, 768:1024], 1)], axis=1)

    hm = jnp.asarray(np.kron(np.eye(2 * LANES // HEAD_DIM), np.ones((HEAD_DIM, HEAD_DIM))), BF16)
    return {
        'w_in': w_in_p, 'wq': wq.astype(BF16), 'wkv': wkv_p.astype(BF16), 'w_out': wo.astype(BF16),
        'w_up': w_up.astype(BF16), 'conv_ff': conv_ff, 'w_down': w_down.astype(BF16),
        'norm1': norm1.reshape(L, 1, D_MODEL), 'norm2': norm2.reshape(L, 1, D_MODEL), 'conv_a': conv_a,
        'gq': _pad_cols(mla_q_norm, 256).reshape(L, 1, 256), 'gkv': mla_kv_norm.reshape(L, 1, MLA_KV_LORA),
        'gqc': jnp.tile(gqa_q_norm, (1, 4)).reshape(L, 1, 256), 'gkc': jnp.tile(gqa_k_norm, (1, 2)).reshape(L, 1, 128),
        'hm': hm,
    }


def _rope_tables(t):
    rows = t // GRID_W
    row = np.repeat(np.arange(rows, dtype=np.float64), GRID_W)
    col = np.tile(np.arange(GRID_W, dtype=np.float64), rows)

    def tabs(dim):
        half = dim // 2
        inv = np.power(ROPE_THETA, -np.arange(0, half, 2, dtype=np.float64) / half)
        ar = row[:, None] * inv
        ac = col[:, None] * inv
        c = np.concatenate([np.cos(ar), np.cos(ar), np.cos(ac), np.cos(ac)], axis=1)
        s = np.concatenate([-np.sin(ar), np.sin(ar), -np.sin(ac), np.sin(ac)], axis=1)
        return c, s

    c64, s64 = tabs(HEAD_DIM)
    c32, s32 = tabs(MLA_ROPE)
    c64 = np.tile(c64, (1, 2))
    s64 = np.tile(s64, (1, 2))
    c32 = np.concatenate([c32, np.ones((t, LANES - MLA_ROPE))], axis=1)
    s32 = np.concatenate([s32, np.zeros((t, LANES - MLA_ROPE))], axis=1)
    return tuple(jnp.asarray(a.astype(np.float32)) for a in (c64, s64, c32, s32))


def kernel(x_prompt, x_sample, cache_mla_ckv, cache_mla_kpe, cache_gqa_k, cache_gqa_v, cache_swa_k, cache_swa_v,
           c, c_ctx, w_ada, b_ada, norm1, w_in, conv_a, mla_q_norm, mla_wq_b, mla_kv_norm, mla_wkv_b,
           gqa_q_norm, gqa_k_norm, swa_sink, w_out, norm2, w_up, conv_ff, w_down, final_norm):
    B, T, _ = x_prompt.shape
    DB, DT, _ = x_sample.shape
    past = cache_mla_ckv.shape[2]

    lw = _prep_weights(w_in, mla_wq_b, mla_wkv_b, w_out, w_up, conv_ff, w_down, norm1, norm2, conv_a,
                       mla_q_norm, mla_kv_norm, gqa_q_norm, gqa_k_norm)
    rope_tabs = _rope_tables(DT)
    fnorm = final_norm.reshape(1, D_MODEL)

    cond_t = jnp.concatenate([c_ctx[:, None], c.T, jnp.zeros((D_MODEL, 8 - 1 - DB), F32)], axis=1)
    mod = _ada(cond_t, 1 + DB, w_ada, b_ada).reshape(DEPTH, 8, 6, D_MODEL)

    kx, vx = _ctx_mla(cache_mla_ckv, _pad_cols(cache_mla_kpe, LANES), lw['wkv'])
    flat = lambda a: a.reshape(DB, DEPTH, past, 2 * HEAD_DIM).astype(BF16)
    kcx, kdx = flat(cache_gqa_k), flat(cache_swa_k)
    vcx, vdx = jnp.swapaxes(flat(cache_gqa_v), 2, 3), jnp.swapaxes(flat(cache_swa_v), 2, 3)

    xp = x_prompt.reshape(B * T, D_MODEL)
    xs = x_sample.reshape(DB * DT, D_MODEL)
    caches = None
    for l in range(DEPTH):
        final = l == DEPTH - 1
        (z, gb, qm, km, vm, qc, kc, vc, qd, kd, vd, *caches) = _inproj(xp, mod, lw, l, T, 0, None, caches)
        yb, yc, yd = _attn_ctx(l, T, swa_sink, qm, km, vm, qc, kc, vc, qd, kd, vd)
        xp = _post(xp, mod, lw, l, T, 0, z, gb, yb, yc, yd, fnorm, final)
        (z, gb, qmr, qm, km, vm, qcr, qc, kc, vc, qdr, qd, kd, vd) = _inproj(xs, mod, lw, l, DT, 1, rope_tabs)
        yb, yc, yd = _attn_lat(l, DB, DT, swa_sink, qmr, qm, km, vm, kx, vx, qcr, qc, kc, vc, kcx, vcx,
                               qdr, qd, kd, vd, kdx, vdx)
        xs = _post(xs, mod, lw, l, DT, 1, z, gb, yb, yc, yd, fnorm, final)

    heads = lambda a: a.reshape(B, DEPTH, T, 2, HEAD_DIM)
    return (xp.reshape(B, T, D_MODEL), xs.reshape(DB, DT, D_MODEL), caches[0], caches[1],
            heads(caches[2]), heads(caches[3]), heads(caches[4]), heads(caches[5]))
```

```python
import functools
import math

import jax
import jax.numpy as jnp
import numpy as np
from jax import lax
from jax.experimental import pallas as pl
from jax.experimental.pallas import tpu as pltpu

F32 = jnp.float32
BF16 = jnp.bfloat16

D_MODEL = 1024
DEPTH = 2
GRID_W = 64
HEAD_DIM = 64
GROUP_WIDTH = D_MODEL // 4
MLA_HEADS = 4
MLA_NOPE = 64
MLA_ROPE = 32
MLA_V = 64
MLA_Q_LORA = 192
MLA_KV_LORA = 128
WINDOW = 128
D_FF = 2816
ROPE_THETA = 10000.0
EPS = 1e-6
NEG_INF = -1e30
LOG2E = math.log2(math.e)
ATTN_SCALE = HEAD_DIM ** -0.5 * LOG2E
MLA_SCALE = (MLA_NOPE + MLA_ROPE) ** -0.5 * LOG2E

LANES = 128
BF16_ROWS = 16
ROW_TILE = 512
IN_ROW_TILE = 1024
IN_SUB_TILE = 256
FF_TILE = 256
N_FF_TILES = D_FF // FF_TILE
DOWN_CHUNK = 4
IN_COLS_PACKED = 2304
VMEM_LIMIT = 56 * 1024 * 1024

_NT = (((1,), (1,)), ((), ()))


def _dot(a, b):
    return jnp.dot(a, b, preferred_element_type=F32)


def _dot_nt(a, b):
    return lax.dot_general(a, b, _NT, preferred_element_type=F32)


def _rms(x, g, n):
    ms = jnp.sum(x * x, axis=-1, keepdims=True) * (1.0 / n)
    return x * lax.rsqrt(ms + EPS) * g


def _lane(shape):
    return lax.broadcasted_iota(jnp.int32, shape, len(shape) - 1)


def _rope(x, c, s, half):
    w = x.shape[-1]
    lo = (_lane(x.shape) % (2 * half)) < half
    sw = jnp.where(lo, pltpu.roll(x, w - half, 1), pltpu.roll(x, half, 1))
    return x * c + sw * s


def _shift_rows(zz, halo, rows):
    n = zz.shape[0]
    prev = pltpu.roll(zz, 1, 0)[halo:halo + rows]
    nxt = pltpu.roll(zz, n - 1, 0)[halo:halo + rows]
    return prev, nxt


def _seq_pos(tile, rows, seq_len):
    r = lax.broadcasted_iota(jnp.int32, (rows, 1), 0)
    return (tile * rows + r) % seq_len


def _ada_body(ncond, c_ref, w_ref, b_ref, o_ref):
    c = c_ref[...]
    s = c * jax.nn.sigmoid(c)
    tn = w_ref.shape[1]
    w = w_ref[...].reshape(D_MODEL // 8, 8, tn)
    rows = []
    for r in range(ncond):
        part = jnp.sum(w * s[:, r:r + 1].reshape(D_MODEL // 8, 8, 1), axis=0)
        rows.append(jnp.sum(part, axis=0, keepdims=True))
    rows.append(jnp.zeros((8 - ncond, tn), F32))
    o_ref[...] = jnp.concatenate(rows, axis=0) + b_ref[...]


def _ada(cond_t, ncond, w_ada, b_ada):
    tn = 1536
    n = 6 * D_MODEL
    return pl.pallas_call(
        functools.partial(_ada_body, ncond),
        grid=(DEPTH, n // tn),
        in_specs=[pl.BlockSpec((D_MODEL, 8), lambda l, j: (0, 0)),
                  pl.BlockSpec((None, D_MODEL, tn), lambda l, j: (l, 0, j)),
                  pl.BlockSpec((None, 1, tn), lambda l, j: (l, 0, j))],
        out_specs=pl.BlockSpec((None, 8, tn), lambda l, j: (l, 0, j)),
        out_shape=jax.ShapeDtypeStruct((DEPTH, 8, n), F32),
        compiler_params=pltpu.CompilerParams(
            dimension_semantics=("arbitrary", "arbitrary"), vmem_limit_bytes=VMEM_LIMIT),
        name="ada",
    )(cond_t, w_ada, b_ada.reshape(DEPTH, 1, n))


def _ctx_body(ckv_ref, kpe_ref, w_ref, kx_ref, vx_ref):
    kv = _dot(ckv_ref[...].astype(BF16), w_ref[...])
    kpe = kpe_ref[...]
    kx_ref[...] = (kv[:, :4 * LANES] + jnp.concatenate([kpe] * MLA_HEADS, axis=1)).astype(BF16)
    vx_ref[...] = kv[:, 4 * LANES:].T.astype(BF16)


def _ctx_mla(cache_ckv, cache_kpe_pad, wkv_p):
    b, _, s, _ = cache_ckv.shape
    return pl.pallas_call(
        _ctx_body,
        grid=(DEPTH, b),
        in_specs=[pl.BlockSpec((None, None, s, MLA_KV_LORA), lambda l, i: (i, l, 0, 0)),
                  pl.BlockSpec((None, None, s, LANES), lambda l, i: (i, l, 0, 0)),
                  pl.BlockSpec((None, MLA_KV_LORA, 6 * LANES), lambda l, i: (l, 0, 0))],
        out_specs=[pl.BlockSpec((None, None, s, 4 * LANES), lambda l, i: (l, i, 0, 0)),
                   pl.BlockSpec((None, None, 2 * LANES, s), lambda l, i: (l, i, 0, 0))],
        out_shape=[jax.ShapeDtypeStruct((DEPTH, b, s, 4 * LANES), BF16),
                   jax.ShapeDtypeStruct((DEPTH, b, 2 * LANES, s), BF16)],
        compiler_params=pltpu.CompilerParams(
            dimension_semantics=("arbitrary", "arbitrary"), vmem_limit_bytes=VMEM_LIMIT),
        name="ctx_mla",
    )(cache_ckv, cache_kpe_pad, wkv_p)


def _inproj_body(rope, n_alias, x_ref, mod_ref, n1_ref, w_ref, wq_ref, wkv_ref, gq_ref, gkv_ref, gqc_ref, gkc_ref,
                 hm_ref, *rest):
    if rope:
        c64_ref, s64_ref, c32_ref, s32_ref = rest[:4]
        outs = rest[4:]
    else:
        outs = rest[n_alias:]
    sh1 = mod_ref[0:1, :]
    sc1 = mod_ref[1:2, :]
    hm = hm_ref[...]
    sub = IN_SUB_TILE
    nsub = x_ref.shape[0] // sub

    def project(s):
        x = x_ref[s * sub:(s + 1) * sub, :]
        h = _rms(x, n1_ref[...], D_MODEL) * (1.0 + sc1) + sh1
        return _dot(h.astype(BF16), w_ref[...])

    def head_rms(v, g):
        v2 = v * v
        hi = v2.astype(BF16)
        lo = (v2 - hi.astype(F32)).astype(BF16)
        w = v.shape[-1]
        ss = _dot(hi, hm[:w, :w]) + _dot(lo, hm[:w, :w])
        return v * lax.rsqrt(ss * (1.0 / HEAD_DIM) + EPS) * g

    def finish(s, acc):
        rs = slice(s * sub, (s + 1) * sub)
        xa, gb, gc = acc[:, 0:256], acc[:, 256:512], acc[:, 512:768]
        z = gc * xa
        cq = acc[:, 768:1024]
        cqn = _rms(cq, gq_ref[...], MLA_Q_LORA)
        qm = _dot(cqn.astype(BF16), wq_ref[...]) * MLA_SCALE
        ckvn = _rms(acc[:, 1024:1152], gkv_ref[...], MLA_KV_LORA)
        kv = _dot(ckvn.astype(BF16), wkv_ref[...])
        kpe = acc[:, 1152:1280]
        vm = kv[:, 4 * LANES:]
        qc = head_rms(acc[:, 1280:1536], gqc_ref[...]) * ATTN_SCALE
        kc = head_rms(acc[:, 1536:1664], gkc_ref[...])
        vc = acc[:, 1664:1792]
        qd = acc[:, 1792:2048] * ATTN_SCALE
        kd = acc[:, 2048:2176]
        vd = acc[:, 2176:2304]

        if not rope:
            (z_o, gb_o, qm_o, km_o, vm_o, qc_o, kc_o, vc_o, qd_o, kd_o, vd_o,
             ckv_c, kpe_c, kc_c, vc_c, kd_c, vd_c) = outs
            km = kv[:, :4 * LANES] + jnp.concatenate([kpe] * MLA_HEADS, axis=1)
            z_o[rs, :] = z.astype(BF16)
            gb_o[rs, :] = gb.astype(BF16)
            qm_o[rs, :] = qm.astype(BF16)
            km_o[rs, :] = km.astype(BF16)
            vm_o[:, rs] = vm.T.astype(BF16)
            qc_o[rs, :] = qc.astype(BF16)
            kc_o[rs, :] = kc.astype(BF16)
            vc_o[:, rs] = vc.T.astype(BF16)
            qd_o[rs, :] = qd.astype(BF16)
            kd_o[rs, :] = kd.astype(BF16)
            vd_o[:, rs] = vd.T.astype(BF16)
            for o, v in ((ckv_c, ckvn), (kpe_c, kpe[:, :MLA_ROPE]), (kc_c, kc), (vc_c, vc), (kd_c, kd),
                         (vd_c, vd)):
                t = o.shape[-2]
                bs = slice(s * sub // t, (s + 1) * sub // t)
                v = v.reshape(sub // t, t, v.shape[-1])
                if n_alias:
                    o[bs] = v
                else:
                    o[bs, 0] = v
                    o[bs, 1:] = jnp.zeros((sub // t, o.shape[1] - 1) + v.shape[1:], F32)
        else:
            (z_o, gb_o, qmr_o, qm_o, km_o, vm_o, qcr_o, qc_o, kc_o, vc_o, qdr_o, qd_o, kd_o, vd_o) = outs
            c64, s64, c32, s32 = c64_ref[rs, :], s64_ref[rs, :], c32_ref[rs, :], s32_ref[rs, :]
            rope64 = lambda v: jnp.concatenate(
                [_rope(v[:, i:i + LANES], c64, s64, 16) for i in range(0, v.shape[-1], LANES)], axis=1)
            rope32 = lambda v: jnp.concatenate(
                [_rope(v[:, i:i + LANES], c32, s32, 8) for i in range(0, v.shape[-1], LANES)], axis=1)
            km = kv[:, :4 * LANES] + jnp.concatenate([_rope(kpe, c32, s32, 8)] * MLA_HEADS, axis=1)
            z_o[rs, :] = z.astype(BF16)
            gb_o[rs, :] = gb.astype(BF16)
            qmr_o[rs, :] = rope32(qm).astype(BF16)
            qm_o[rs, :] = qm.astype(BF16)
            km_o[rs, :] = km.astype(BF16)
            vm_o[:, rs] = vm.T.astype(BF16)
            qcr_o[rs, :] = rope64(qc).astype(BF16)
            qc_o[rs, :] = qc.astype(BF16)
            kc_o[rs, :] = rope64(kc).astype(BF16)
            vc_o[:, rs] = vc.T.astype(BF16)
            qdr_o[rs, :] = rope64(qd).astype(BF16)
            qd_o[rs, :] = qd.astype(BF16)
            kd_o[rs, :] = rope64(kd).astype(BF16)
            vd_o[:, rs] = vd.T.astype(BF16)

    nxt = project(0)
    for s in range(nsub):
        acc = nxt
        if s + 1 < nsub:
            nxt = project(s + 1)
        finish(s, acc)


def _inproj(x, mod, lw, l, seq_len, cond_base, rope_tabs, prev_caches=None):
    rows = x.shape[0]
    r = IN_ROW_TILE
    rope = rope_tabs is not None
    aliases = {}
    tiles_per_seq = max(seq_len // r, 1)
    step = 1 if seq_len >= r else 0

    def cond_map(i):
        return (l, cond_base + step * (i // tiles_per_seq), 0, 0)

    row = lambda w: pl.BlockSpec((r, w), lambda i: (i, 0))
    full = lambda a: pl.BlockSpec((None,) + a.shape[1:], lambda i: (l,) + (0,) * (a.ndim - 1))
    in_specs = [row(D_MODEL),
                pl.BlockSpec((None, None, 6, D_MODEL), cond_map),
                full(lw['norm1']), full(lw['w_in']), full(lw['wq']), full(lw['wkv']),
                full(lw['gq']), full(lw['gkv']), full(lw['gqc']), full(lw['gkc']),
                pl.BlockSpec((2 * LANES, 2 * LANES), lambda i: (0, 0))]
    args = [x, mod, lw['norm1'], lw['w_in'], lw['wq'], lw['wkv'], lw['gq'], lw['gkv'], lw['gqc'], lw['gkc'],
            lw['hm']]
    def act(w, transposed):
        if transposed:
            return jax.ShapeDtypeStruct((w, rows), BF16), pl.BlockSpec((w, r), lambda i: (0, i))
        return jax.ShapeDtypeStruct((rows, w), BF16), row(w)

    if rope:
        tab = pl.BlockSpec((r, LANES), lambda i: (i % tiles_per_seq, 0))
        in_specs += [tab] * 4
        args += list(rope_tabs)
        widths = [(256, 0), (256, 0), (512, 0), (512, 0), (512, 0), (256, 1), (256, 0), (256, 0), (128, 0),
                  (128, 1), (256, 0), (256, 0), (128, 0), (128, 1)]
        out_shape, out_specs = map(list, zip(*[act(w, t) for w, t in widths]))
    else:
        widths = [(256, 0), (256, 0), (512, 0), (512, 0), (256, 1), (256, 0), (128, 0), (128, 1), (256, 0),
                  (128, 0), (128, 1)]
        cwidths = [128, MLA_ROPE, 128, 128, 128, 128]
        nb = r // seq_len
        out_shape, out_specs = map(list, zip(*[act(w, t) for w, t in widths]))
        out_shape += [jax.ShapeDtypeStruct((rows // seq_len, DEPTH, seq_len, w), F32) for w in cwidths]
        assert (prev_caches is None) == (l == 0)
        if l == 0:
            out_specs += [pl.BlockSpec((nb, DEPTH, seq_len, w), lambda i: (i, 0, 0, 0)) for w in cwidths]
        else:
            out_specs += [pl.BlockSpec((nb, None, seq_len, w), lambda i: (i, l, 0, 0)) for w in cwidths]
        if prev_caches is not None:
            aliases = {len(args) + k: len(widths) + k for k in range(len(cwidths))}
            in_specs += [pl.BlockSpec(memory_space=pl.ANY)] * len(cwidths)
            args += list(prev_caches)
    return pl.pallas_call(
        functools.partial(_inproj_body, rope, len(aliases)),
        grid=(rows // r,),
        in_specs=in_specs, out_specs=out_specs, out_shape=out_shape,
        input_output_aliases=aliases,
        compiler_params=pltpu.CompilerParams(
            dimension_semantics=("arbitrary",), vmem_limit_bytes=VMEM_LIMIT),
        name="inproj_rope" if rope else "inproj",
    )(*args)


def _softmax_pv_t(s_list, vt_list, extra=None):
    m = s_list[0].max(axis=0, keepdims=True)
    for s in s_list[1:]:
        m = jnp.maximum(m, s.max(axis=0, keepdims=True))
    if extra is not None:
        m = jnp.maximum(m, extra)
    den = jnp.exp2(extra - m) if extra is not None else 0.0
    out = None
    for s, vt in zip(s_list, vt_list):
        p = jnp.exp2(s - m)
        den = den + p.sum(axis=0, keepdims=True)
        pv = _dot(vt, p.astype(BF16))
        out = pv if out is None else out + pv
    return out * (1.0 / den)


def _half_masks(q):
    lo = _lane(q.shape) < HEAD_DIM
    zero = jnp.zeros_like(q)
    return jnp.where(lo, q, zero), jnp.where(lo, zero, q)


def _merge_rows(a, b):
    r = lax.broadcasted_iota(jnp.int32, a.shape, 0)
    return jnp.where(r < HEAD_DIM, a, b)


def _sink_row(sink_ref, l, h0, h1, tq):
    c = lax.broadcasted_iota(jnp.int32, (1, 2 * tq), 1)
    return jnp.where(c < tq, sink_ref[l, h0], sink_ref[l, h1]) * LOG2E


def _run_groups(stage1, stage2, n, ahead=2):
    out = []
    pending = [stage1(g) for g in range(min(ahead, n))]
    for g in range(n):
        if g + ahead < n:
            pending.append(stage1(g + ahead))
        out.append(stage2(g, pending.pop(0)))
    return out


def _mla_scores(h, q_sets, k_sets):
    res = []
    for q, k in zip(q_sets, k_sets):
        cs = [slice(hh * LANES, (hh + 1) * LANES) for hh in (2 * h, 2 * h + 1)]
        res.append(jnp.concatenate([_dot_nt(k[:, c], q[:, c]) for c in cs], axis=1))
    return res


def _gqa_scores(j, q_sets, k_sets):
    res = []
    for q, k in zip(q_sets, k_sets):
        qa = _half_masks(q[:, 0:LANES])[j]
        qb = _half_masks(q[:, LANES:])[j]
        res.append(_dot_nt(k[...], jnp.concatenate([qa, qb], axis=0)))
    return res


def _store_pair(y_ref, rows, blk, ot):
    y_ref[rows, blk * LANES:(blk + 1) * LANES] = ot.T.astype(BF16)


def _attn_ctx_body(l, nb, t, sink_ref, qm_ref, km_ref, vm_ref, qc_ref, kc_ref, vc_ref, qd_ref, kd_ref, vd_ref,
                   yb_ref, yc_ref, yd_ref):
    def stage1(g):
        b, k = divmod(g, 6)
        rs = pl.ds(b * t, t)
        if k < 2:
            return _mla_scores(k, [qm_ref.at[rs]], [km_ref.at[rs]])
        q_ref, k_ref = (qc_ref, kc_ref) if k < 4 else (qd_ref, kd_ref)
        return _gqa_scores(k % 2, [q_ref.at[rs]], [k_ref.at[rs]])

    def stage2(g, s):
        b, k = divmod(g, 6)
        rs = slice(b * t, (b + 1) * t)
        if k < 2:
            return _softmax_pv_t(s, [vm_ref[k * LANES:(k + 1) * LANES, rs]])
        if k < 4:
            return _softmax_pv_t(s, [vc_ref[:, rs]])
        j = k % 2
        return _softmax_pv_t(s, [vd_ref[:, rs]], _sink_row(sink_ref, l, 2 * j, 2 * j + 1, t))

    o = _run_groups(stage1, stage2, 6 * nb)
    for b in range(nb):
        rs = slice(b * t, (b + 1) * t)
        ob = o[6 * b:6 * b + 6]
        for p in range(2):
            _store_pair(yb_ref, rs, p, _merge_rows(ob[p][:, :t], ob[p][:, t:]))
        for y_ref, (o0, o1) in ((yc_ref, ob[2:4]), (yd_ref, ob[4:6])):
            _store_pair(y_ref, rs, 0, _merge_rows(o0[:, :t], o1[:, :t]))
            _store_pair(y_ref, rs, 1, _merge_rows(o0[:, t:], o1[:, t:]))


def _attn_ctx(l, t, sink, qm, km, vm, qc, kc, vc, qd, kd, vd):
    rows = qm.shape[0]
    nb = 4
    r = nb * t
    row = lambda a: (pl.BlockSpec((a.shape[0], r), lambda i: (0, i)) if a.shape[1] == rows
                     else pl.BlockSpec((r, a.shape[1]), lambda i: (i, 0)))
    ins = [qm, km, vm, qc, kc, vc, qd, kd, vd]
    return pl.pallas_call(
        functools.partial(_attn_ctx_body, l, nb, t),
        grid=(rows // r,),
        in_specs=[pl.BlockSpec(memory_space=pltpu.SMEM)] + [row(a) for a in ins],
        out_specs=[pl.BlockSpec((r, 2 * LANES), lambda i: (i, 0))] * 3,
        out_shape=[jax.ShapeDtypeStruct((rows, 2 * LANES), BF16)] * 3,
        compiler_params=pltpu.CompilerParams(
            dimension_semantics=("arbitrary",), vmem_limit_bytes=VMEM_LIMIT),
        name="attn_ctx",
    )(sink, *ins)


def _window_block(i, tq, t, nwb):
    return jnp.clip(i * (tq // LANES) - WINDOW // LANES, 0, t // LANES - nwb)


def _attn_lat_body(l, tq, t, nwb, sink_ref, qmr_ref, qm_ref, km_ref, vm_ref, kx_ref, vx_ref,
                   qcr_ref, qc_ref, kc_ref, vc_ref, kcx_ref, vcx_ref,
                   qdr_ref, qd_ref, kdx_ref, vdx_ref, *rest):
    kd_refs, vd_refs = rest[:nwb], rest[nwb:2 * nwb]
    yb_ref, yc_ref, yd_ref = rest[2 * nwb:]
    i = pl.program_id(1)
    wk = nwb * LANES
    kpos = _window_block(i, tq, t, nwb) * LANES + lax.broadcasted_iota(jnp.int32, (wk, 2 * tq), 0)
    qpos = i * tq + lax.broadcasted_iota(jnp.int32, (wk, 2 * tq), 1) % tq
    valid = jnp.abs(kpos - qpos) <= WINDOW

    def stage1(g):
        if g < 2:
            return _mla_scores(g, [qmr_ref, qm_ref], [km_ref, kx_ref])
        if g < 4:
            return _gqa_scores(g % 2, [qcr_ref, qc_ref], [kc_ref, kcx_ref])
        kw = jnp.concatenate([r[...] for r in kd_refs], axis=0)
        s1, s2 = _gqa_scores(g % 2, [qdr_ref, qd_ref], [kw, kdx_ref])
        return [jnp.where(valid, s1, NEG_INF), s2]

    def stage2(g, s):
        if g < 2:
            blk = slice(g * LANES, (g + 1) * LANES)
            return _softmax_pv_t(s, [vm_ref[blk, :], vx_ref[blk, :]])
        if g < 4:
            return _softmax_pv_t(s, [vc_ref[...], vcx_ref[...]])
        j = g % 2
        vw = jnp.concatenate([r[...] for r in vd_refs], axis=1)
        return _softmax_pv_t(s, [vw, vdx_ref[...]], _sink_row(sink_ref, l, 2 * j, 2 * j + 1, tq))

    o = _run_groups(stage1, stage2, 6)
    rs = slice(None)
    for p in range(2):
        _store_pair(yb_ref, rs, p, _merge_rows(o[p][:, :tq], o[p][:, tq:]))
    for y_ref, (o0, o1) in ((yc_ref, o[2:4]), (yd_ref, o[4:6])):
        _store_pair(y_ref, rs, 0, _merge_rows(o0[:, :tq], o1[:, :tq]))
        _store_pair(y_ref, rs, 1, _merge_rows(o0[:, tq:], o1[:, tq:]))


def _attn_lat(l, nbatch, t, sink, qmr, qm, km, vmt, kx, vxt, qcr, qc, kc, vct, kcx, vcxt, qdr, qd, kd, vdt,
              kdx, vdxt):
    tq = 256
    nq = t // tq
    nwb = (tq + 2 * WINDOW) // LANES
    nkb = t // LANES
    rows = qm.shape[0]
    qs = lambda a: pl.BlockSpec((tq, a.shape[1]), lambda b, i: (b * nq + i, 0))
    ks = lambda a: pl.BlockSpec((t, a.shape[1]), lambda b, i: (b, 0))
    kts = lambda a: pl.BlockSpec((a.shape[0], t), lambda b, i: (0, b))
    xs = lambda a: pl.BlockSpec((None, None) + a.shape[2:], lambda b, i: (l, b, 0, 0))
    cs = lambda a: pl.BlockSpec((None, None) + a.shape[2:], lambda b, i: (b, l, 0, 0))
    wblk = lambda b, i, k: b * nkb + _window_block(i, tq, t, nwb) + k
    kd_specs = [pl.BlockSpec((LANES, LANES), lambda b, i, k=k: (wblk(b, i, k), 0)) for k in range(nwb)]
    vd_specs = [pl.BlockSpec((LANES, LANES), lambda b, i, k=k: (0, wblk(b, i, k))) for k in range(nwb)]
    in_specs = [pl.BlockSpec(memory_space=pltpu.SMEM),
                qs(qmr), qs(qm), ks(km), kts(vmt), xs(kx), xs(vxt),
                qs(qcr), qs(qc), ks(kc), kts(vct), cs(kcx), cs(vcxt),
                qs(qdr), qs(qd), cs(kdx), cs(vdxt)] + kd_specs + vd_specs
    return pl.pallas_call(
        functools.partial(_attn_lat_body, l, tq, t, nwb),
        grid=(nbatch, nq),
        in_specs=in_specs,
        out_specs=[pl.BlockSpec((tq, 2 * LANES), lambda b, i: (b * nq + i, 0))] * 3,
        out_shape=[jax.ShapeDtypeStruct((rows, 2 * LANES), BF16)] * 3,
        compiler_params=pltpu.CompilerParams(
            dimension_semantics=("arbitrary", "arbitrary"), vmem_limit_bytes=VMEM_LIMIT),
        name="attn_lat",
    )(sink, qmr, qm, km, vmt, kx, vxt, qcr, qc, kc, vct, kcx, vcxt, qdr, qd, kdx, vdxt,
      *([kd] * nwb), *([vdt] * nwb))


def _post_body(seq_len, final, mod_ref, n2_ref, cwa_ref, wo_ref, cwf_ref, wup_ref, wdn_ref, fn_ref, *rest):
    ins, (o_ref, hbuf, gbuf, pbuf, abuf) = rest[:18], rest[18:]
    cat = lambda k: jnp.concatenate([ins[3 * k][...], ins[3 * k + 1][...], ins[3 * k + 2][...]], axis=0)
    r = o_ref.shape[0]
    halo = BF16_ROWS
    n = r + 2 * halo
    grp = r // 8
    nchunk = D_MODEL // LANES
    i = pl.program_id(0)

    zz = cat(1).astype(F32)
    pos = (i * r - halo + lax.broadcasted_iota(jnp.int32, (n, 1), 0)) % seq_len
    prev = jnp.where(pos == 0, 0.0, pltpu.roll(zz, 1, 0))
    nxt = jnp.where(pos == seq_len - 1, 0.0, pltpu.roll(zz, n - 1, 0))
    cwa = cwa_ref[...]
    ya = cat(2).astype(F32) * (prev * cwa[0:1] + zz * cwa[1:2] + nxt * cwa[2:3])
    ycat = jnp.concatenate([ya.astype(BF16), cat(3), cat(4), cat(5)], axis=1)
    x1 = cat(0) + mod_ref[2:3, :] * _dot(ycat, wo_ref[...])
    h2 = _rms(x1, n2_ref[...], D_MODEL) * (1.0 + mod_ref[4:5, :]) + mod_ref[3:4, :]
    x1c = x1[halo:halo + r]

    for c in range(nchunk):
        for s in range(8):
            pbuf[c, pl.ds(s, grp, stride=8), :] = h2[halo + s * grp:halo + (s + 1) * grp, c * LANES:(c + 1) * LANES]
    hrow = lax.broadcasted_iota(jnp.int32, (halo, 1), 0)
    edge = jnp.where(hrow == 0, h2[halo - 1:halo], jnp.where(hrow == 1, h2[halo + r:halo + r + 1], 0.0))
    hbuf[0:halo, :] = edge.astype(BF16)
    hbuf[halo:, :] = jnp.concatenate([pbuf[c] for c in range(nchunk)], axis=1).astype(BF16)

    sub = lax.broadcasted_iota(jnp.int32, (8, 1), 0)
    seq_first = (i * r + sub * grp) % seq_len == 0
    seq_last = (i * r + sub * grp + grp - 1) % seq_len == seq_len - 1

    def up(j):
        ga = slice(j * FF_TILE, (j + 1) * FF_TILE)
        va = slice(D_FF + j * FF_TILE, D_FF + (j + 1) * FF_TILE)
        return _dot(hbuf[...], wup_ref[:, ga]), _dot(hbuf[...], wup_ref[:, va])

    def conv(u, cw):
        head = jnp.where(sub == 0, u[0:8], pltpu.roll(u[halo + r - 8:halo + r], 1, 0))
        tail = jnp.where(sub == 7, pltpu.roll(u[0:8], 6, 0), pltpu.roll(u[halo:halo + 8], 7, 0))
        prev = jnp.concatenate([jnp.where(seq_first, 0.0, head), u[halo:halo + r - 8]], axis=0)
        nxt = jnp.concatenate([u[halo + 8:halo + r], jnp.where(seq_last, 0.0, tail)], axis=0)
        return prev * cw[0:1] + u[halo:halo + r] * cw[1:2] + nxt * cw[2:3]

    acc = None
    nxt_u = up(0)
    for j in range(N_FF_TILES):
        ua, ub = nxt_u
        if j + 1 < N_FF_TILES:
            nxt_u = up(j + 1)
        ga = slice(j * FF_TILE, (j + 1) * FF_TILE)
        va = slice(D_FF + j * FF_TILE, D_FF + (j + 1) * FF_TILE)
        ua = conv(ua, cwf_ref[:, ga])
        ub = conv(ub, cwf_ref[:, va])
        gbuf[:, ga] = (ua * jax.nn.sigmoid(ua) * ub).astype(BF16)
        if j % DOWN_CHUNK == DOWN_CHUNK - 1 or j == N_FF_TILES - 1:
            lo = (j // DOWN_CHUNK) * DOWN_CHUNK * FF_TILE
            hi = (j + 1) * FF_TILE
            d = _dot(gbuf[:, lo:hi], wdn_ref[lo:hi, :])
            acc = d if acc is None else acc + d

    for c in range(nchunk):
        abuf[c] = acc[:, c * LANES:(c + 1) * LANES]
    g2 = mod_ref[5:6, :]
    for c in range(nchunk):
        cs = slice(c * LANES, (c + 1) * LANES)
        for s in range(8):
            rs = slice(s * grp, (s + 1) * grp)
            o_ref[rs, cs] = x1c[rs, cs] + g2[:, cs] * abuf[c, pl.ds(s, grp, stride=8), :]
    if final:
        o_ref[...] = _rms(o_ref[...], fn_ref[...], D_MODEL)


def _halo_specs(r, w, rows):
    nblk = rows // BF16_ROWS
    per = r // BF16_ROWS
    prev = pl.BlockSpec((BF16_ROWS, w), lambda i: (jnp.maximum(i * per - 1, 0), 0))
    cur = pl.BlockSpec((r, w), lambda i: (i, 0))
    nxt = pl.BlockSpec((BF16_ROWS, w), lambda i: (jnp.minimum((i + 1) * per, nblk - 1), 0))
    return prev, cur, nxt


def _cond_spec(l, seq_len, cond_base):
    r = ROW_TILE
    tiles_per_seq = max(seq_len // r, 1)
    step = 1 if seq_len >= r else 0
    return pl.BlockSpec((None, None, 6, D_MODEL),
                        lambda i: (l, cond_base + step * (i // tiles_per_seq), 0, 0))


def _post(x, mod, lw, l, seq_len, cond_base, z, gb, yb, yc, yd, final_norm, final):
    rows = x.shape[0]
    r = ROW_TILE
    full = lambda a: pl.BlockSpec((None,) + a.shape[1:], lambda i: (l,) + (0,) * (a.ndim - 1),
                                  pipeline_mode=pl.Buffered(1))
    acts = [x, z, gb, yb, yc, yd]
    act_specs = [s for a in acts for s in _halo_specs(r, a.shape[1], rows)]
    act_args = [a for a in acts for _ in range(3)]
    return pl.pallas_call(
        functools.partial(_post_body, seq_len, final),
        grid=(rows // r,),
        in_specs=[_cond_spec(l, seq_len, cond_base), full(lw['norm2']), full(lw['conv_a']), full(lw['w_out']),
                  full(lw['conv_ff']), full(lw['w_up']), full(lw['w_down']),
                  pl.BlockSpec((1, D_MODEL), lambda i: (0, 0))] + act_specs,
        out_specs=pl.BlockSpec((r, D_MODEL), lambda i: (i, 0)),
        out_shape=jax.ShapeDtypeStruct((rows, D_MODEL), F32),
        scratch_shapes=[pltpu.VMEM((r + BF16_ROWS, D_MODEL), BF16), pltpu.VMEM((r, D_FF), BF16),
                        pltpu.VMEM((D_MODEL // LANES, r, LANES), F32),
                        pltpu.VMEM((D_MODEL // LANES, r, LANES), F32)],
        compiler_params=pltpu.CompilerParams(
            dimension_semantics=("arbitrary",), vmem_limit_bytes=VMEM_LIMIT),
        name="post",
    )(mod, lw['norm2'], lw['conv_a'], lw['w_out'], lw['conv_ff'], lw['w_up'], lw['w_down'], final_norm,
      *act_args)


def _pad_cols(a, n):
    return jnp.pad(a, [(0, 0)] * (a.ndim - 1) + [(0, n - a.shape[-1])])


def _perm_heads(a, axis):
    h = [lax.slice_in_dim(a, k * HEAD_DIM, (k + 1) * HEAD_DIM, axis=axis) for k in range(4)]
    return jnp.concatenate([h[0], h[2], h[1], h[3]], axis=axis)


def _prep_weights(w_in, mla_wq_b, mla_wkv_b, w_out, w_up, conv_ff, w_down, norm1, norm2, conv_a,
                  mla_q_norm, mla_kv_norm, gqa_q_norm, gqa_k_norm):
    L = DEPTH
    a = w_in[..., 0:768]
    cq = _pad_cols(w_in[..., 768:960], 256)
    ckv = w_in[..., 960:1088]
    kpe = _pad_cols(w_in[..., 1088:1120], LANES)
    qc = _perm_heads(w_in[..., 1120:1376], 2)
    kvc = w_in[..., 1376:1632]
    qd = _perm_heads(w_in[..., 1632:1888], 2)
    kvd = w_in[..., 1888:2144]
    w_in_p = jnp.concatenate([a, cq, ckv, kpe, qc, kvc, qd, kvd], axis=-1).astype(BF16)

    wq = mla_wq_b.reshape(L, MLA_Q_LORA, MLA_HEADS, MLA_NOPE + MLA_ROPE)
    wq = jnp.concatenate([wq[..., MLA_NOPE:], wq[..., :MLA_NOPE],
                          jnp.zeros((L, MLA_Q_LORA, MLA_HEADS, LANES - MLA_NOPE - MLA_ROPE), F32)], axis=-1)
    wq = jnp.pad(wq.reshape(L, MLA_Q_LORA, MLA_HEADS * LANES), ((0, 0), (0, 256 - MLA_Q_LORA), (0, 0)))

    wkv = mla_wkv_b.reshape(L, MLA_KV_LORA, MLA_HEADS, MLA_NOPE + MLA_V)
    zk = jnp.zeros((L, MLA_KV_LORA, MLA_HEADS, MLA_ROPE), F32)
    wk = jnp.concatenate([zk, wkv[..., :MLA_NOPE], zk], axis=-1).reshape(L, MLA_KV_LORA, MLA_HEADS * LANES)
    wv = wkv[..., MLA_NOPE:].reshape(L, MLA_KV_LORA, MLA_HEADS * MLA_V)
    wkv_p = jnp.concatenate([wk, wv], axis=-1)

    wo = jnp.concatenate([w_out[:, 0:512], _perm_heads(w_out[:, 512:768], 1),
                          _perm_heads(w_out[:, 768:1024], 1)], axis=1)

    hm = jnp.asarray(np.kron(np.eye(2 * LANES // HEAD_DIM), np.ones((HEAD_DIM, HEAD_DIM))), BF16)
    return {
        'w_in': w_in_p, 'wq': wq.astype(BF16), 'wkv': wkv_p.astype(BF16), 'w_out': wo.astype(BF16),
        'w_up': w_up.astype(BF16), 'conv_ff': conv_ff, 'w_down': w_down.astype(BF16),
        'norm1': norm1.reshape(L, 1, D_MODEL), 'norm2': norm2.reshape(L, 1, D_MODEL), 'conv_a': conv_a,
        'gq': _pad_cols(mla_q_norm, 256).reshape(L, 1, 256), 'gkv': mla_kv_norm.reshape(L, 1, MLA_KV_LORA),
        'gqc': jnp.tile(gqa_q_norm, (1, 4)).reshape(L, 1, 256), 'gkc': jnp.tile(gqa_k_norm, (1, 2)).reshape(L, 1, 128),
        'hm': hm,
    }


def _rope_tables(t):
    rows = t // GRID_W
    row = np.repeat(np.arange(rows, dtype=np.float64), GRID_W)
    col = np.tile(np.arange(GRID_W, dtype=np.float64), rows)

    def tabs(dim):
        half = dim // 2
        inv = np.power(ROPE_THETA, -np.arange(0, half, 2, dtype=np.float64) / half)
        ar = row[:, None] * inv
        ac = col[:, None] * inv
        c = np.concatenate([np.cos(ar), np.cos(ar), np.cos(ac), np.cos(ac)], axis=1)
        s = np.concatenate([-np.sin(ar), np.sin(ar), -np.sin(ac), np.sin(ac)], axis=1)
        return c, s

    c64, s64 = tabs(HEAD_DIM)
    c32, s32 = tabs(MLA_ROPE)
    c64 = np.tile(c64, (1, 2))
    s64 = np.tile(s64, (1, 2))
    c32 = np.concatenate([c32, np.ones((t, LANES - MLA_ROPE))], axis=1)
    s32 = np.concatenate([s32, np.zeros((t, LANES - MLA_ROPE))], axis=1)
    return tuple(jnp.asarray(a.astype(np.float32)) for a in (c64, s64, c32, s32))


def kernel(x_prompt, x_sample, cache_mla_ckv, cache_mla_kpe, cache_gqa_k, cache_gqa_v, cache_swa_k, cache_swa_v,
           c, c_ctx, w_ada, b_ada, norm1, w_in, conv_a, mla_q_norm, mla_wq_b, mla_kv_norm, mla_wkv_b,
           gqa_q_norm, gqa_k_norm, swa_sink, w_out, norm2, w_up, conv_ff, w_down, final_norm):
    B, T, _ = x_prompt.shape
    DB, DT, _ = x_sample.shape
    past = cache_mla_ckv.shape[2]

    lw = _prep_weights(w_in, mla_wq_b, mla_wkv_b, w_out, w_up, conv_ff, w_down, norm1, norm2, conv_a,
                       mla_q_norm, mla_kv_norm, gqa_q_norm, gqa_k_norm)
    rope_tabs = _rope_tables(DT)
    fnorm = final_norm.reshape(1, D_MODEL)

    cond_t = jnp.concatenate([c_ctx[:, None], c.T, jnp.zeros((D_MODEL, 8 - 1 - DB), F32)], axis=1)
    mod = _ada(cond_t, 1 + DB, w_ada, b_ada).reshape(DEPTH, 8, 6, D_MODEL)

    kx, vx = _ctx_mla(cache_mla_ckv, _pad_cols(cache_mla_kpe, LANES), lw['wkv'])
    flat = lambda a: a.reshape(DB, DEPTH, past, 2 * HEAD_DIM).astype(BF16)
    kcx, kdx = flat(cache_gqa_k), flat(cache_swa_k)
    vcx, vdx = jnp.swapaxes(flat(cache_gqa_v), 2, 3), jnp.swapaxes(flat(cache_swa_v), 2, 3)

    xp = x_prompt.reshape(B * T, D_MODEL)
    xs = x_sample.reshape(DB * DT, D_MODEL)
    caches = None
    for l in range(DEPTH):
        final = l == DEPTH - 1
        (z, gb, qm, km, vm, qc, kc, vc, qd, kd, vd, *caches) = _inproj(xp, mod, lw, l, T, 0, None, caches)
        yb, yc, yd = _attn_ctx(l, T, swa_sink, qm, km, vm, qc, kc, vc, qd, kd, vd)
        xp = _post(xp, mod, lw, l, T, 0, z, gb, yb, yc, yd, fnorm, final)
        (z, gb, qmr, qm, km, vm, qcr, qc, kc, vc, qdr, qd, kd, vd) = _inproj(xs, mod, lw, l, DT, 1, rope_tabs)
        yb, yc, yd = _attn_lat(l, DB, DT, swa_sink, qmr, qm, km, vm, kx, vx, qcr, qc, kc, vc, kcx, vcx,
                               qdr, qd, kd, vd, kdx, vdx)
        xs = _post(xs, mod, lw, l, DT, 1, z, gb, yb, yc, yd, fnorm, final)

    heads = lambda a: a.reshape(B, DEPTH, T, 2, HEAD_DIM)
    return (xp.reshape(B, T, D_MODEL), xs.reshape(DB, DT, D_MODEL), caches[0], caches[1],
            heads(caches[2]), heads(caches[3]), heads(caches[4]), heads(caches[5]))
```

```python
import functools
import math

import jax
import jax.numpy as jnp
import numpy as np
from jax import lax
from jax.experimental import pallas as pl
from jax.experimental.pallas import tpu as pltpu

F32 = jnp.float32
BF16 = jnp.bfloat16

D_MODEL = 1024
DEPTH = 2
GRID_W = 64
HEAD_DIM = 64
GROUP_WIDTH = D_MODEL // 4
MLA_HEADS = 4
MLA_NOPE = 64
MLA_ROPE = 32
MLA_V = 64
MLA_Q_LORA = 192
MLA_KV_LORA = 128
WINDOW = 128
D_FF = 2816
ROPE_THETA = 10000.0
EPS = 1e-6
NEG_INF = -1e30
LOG2E = math.log2(math.e)
ATTN_SCALE = HEAD_DIM ** -0.5 * LOG2E
MLA_SCALE = (MLA_NOPE + MLA_ROPE) ** -0.5 * LOG2E

LANES = 128
BF16_ROWS = 16
ROW_TILE = 512
IN_ROW_TILE = 1024
IN_SUB_TILE = 256
FF_TILE = 256
N_FF_TILES = D_FF // FF_TILE
DOWN_CHUNK = 4
IN_COLS_PACKED = 2304
VMEM_LIMIT = 56 * 1024 * 1024

_NT = (((1,), (1,)), ((), ()))


def _dot(a, b):
    return jnp.dot(a, b, preferred_element_type=F32)


def _dot_nt(a, b):
    return lax.dot_general(a, b, _NT, preferred_element_type=F32)


def _rms(x, g, n):
    ms = jnp.sum(x * x, axis=-1, keepdims=True) * (1.0 / n)
    return x * lax.rsqrt(ms + EPS) * g


def _lane(shape):
    return lax.broadcasted_iota(jnp.int32, shape, len(shape) - 1)


def _rope(x, c, s, half):
    w = x.shape[-1]
    lo = (_lane(x.shape) % (2 * half)) < half
    sw = jnp.where(lo, pltpu.roll(x, w - half, 1), pltpu.roll(x, half, 1))
    return x * c + sw * s


def _shift_rows(zz, halo, rows):
    n = zz.shape[0]
    prev = pltpu.roll(zz, 1, 0)[halo:halo + rows]
    nxt = pltpu.roll(zz, n - 1, 0)[halo:halo + rows]
    return prev, nxt


def _seq_pos(tile, rows, seq_len):
    r = lax.broadcasted_iota(jnp.int32, (rows, 1), 0)
    return (tile * rows + r) % seq_len


def _ada_body(ncond, c_ref, w_ref, b_ref, o_ref):
    c = c_ref[...]
    s = c * jax.nn.sigmoid(c)
    tn = w_ref.shape[1]
    w = w_ref[...].reshape(D_MODEL // 8, 8, tn)
    rows = []
    for r in range(ncond):
        part = jnp.sum(w * s[:, r:r + 1].reshape(D_MODEL // 8, 8, 1), axis=0)
        rows.append(jnp.sum(part, axis=0, keepdims=True))
    rows.append(jnp.zeros((8 - ncond, tn), F32))
    o_ref[...] = jnp.concatenate(rows, axis=0) + b_ref[...]


def _ada(cond_t, ncond, w_ada, b_ada):
    tn = 1536
    n = 6 * D_MODEL
    return pl.pallas_call(
        functools.partial(_ada_body, ncond),
        grid=(DEPTH, n // tn),
        in_specs=[pl.BlockSpec((D_MODEL, 8), lambda l, j: (0, 0)),
                  pl.BlockSpec((None, D_MODEL, tn), lambda l, j: (l, 0, j)),
                  pl.BlockSpec((None, 1, tn), lambda l, j: (l, 0, j))],
        out_specs=pl.BlockSpec((None, 8, tn), lambda l, j: (l, 0, j)),
        out_shape=jax.ShapeDtypeStruct((DEPTH, 8, n), F32),
        compiler_params=pltpu.CompilerParams(
            dimension_semantics=("arbitrary", "arbitrary"), vmem_limit_bytes=VMEM_LIMIT),
        name="ada",
    )(cond_t, w_ada, b_ada.reshape(DEPTH, 1, n))


def _ctx_body(ckv_ref, kpe_ref, w_ref, kx_ref, vx_ref):
    kv = _dot(ckv_ref[...].astype(BF16), w_ref[...])
    kpe = kpe_ref[...]
    kx_ref[...] = (kv[:, :4 * LANES] + jnp.concatenate([kpe] * MLA_HEADS, axis=1)).astype(BF16)
    vx_ref[...] = kv[:, 4 * LANES:].T.astype(BF16)


def _ctx_mla(cache_ckv, cache_kpe_pad, wkv_p):
    b, _, s, _ = cache_ckv.shape
    return pl.pallas_call(
        _ctx_body,
        grid=(DEPTH, b),
        in_specs=[pl.BlockSpec((None, None, s, MLA_KV_LORA), lambda l, i: (i, l, 0, 0)),
                  pl.BlockSpec((None, None, s, LANES), lambda l, i: (i, l, 0, 0)),
                  pl.BlockSpec((None, MLA_KV_LORA, 6 * LANES), lambda l, i: (l, 0, 0))],
        out_specs=[pl.BlockSpec((None, None, s, 4 * LANES), lambda l, i: (l, i, 0, 0)),
                   pl.BlockSpec((None, None, 2 * LANES, s), lambda l, i: (l, i, 0, 0))],
        out_shape=[jax.ShapeDtypeStruct((DEPTH, b, s, 4 * LANES), BF16),
                   jax.ShapeDtypeStruct((DEPTH, b, 2 * LANES, s), BF16)],
        compiler_params=pltpu.CompilerParams(
            dimension_semantics=("arbitrary", "arbitrary"), vmem_limit_bytes=VMEM_LIMIT),
        name="ctx_mla",
    )(cache_ckv, cache_kpe_pad, wkv_p)


def _inproj_body(rope, n_alias, x_ref, mod_ref, n1_ref, w_ref, wq_ref, wkv_ref, gq_ref, gkv_ref, gqc_ref, gkc_ref,
                 hm_ref, *rest):
    if rope:
        c64_ref, s64_ref, c32_ref, s32_ref = rest[:4]
        outs = rest[4:]
    else:
        outs = rest[n_alias:]
    sh1 = mod_ref[0:1, :]
    sc1 = mod_ref[1:2, :]
    hm = hm_ref[...]
    sub = IN_SUB_TILE
    nsub = x_ref.shape[0] // sub

    def project(s):
        x = x_ref[s * sub:(s + 1) * sub, :]
        h = _rms(x, n1_ref[...], D_MODEL) * (1.0 + sc1) + sh1
        return _dot(h.astype(BF16), w_ref[...])

    def head_rms(v, g):
        v2 = v * v
        hi = v2.astype(BF16)
        lo = (v2 - hi.astype(F32)).astype(BF16)
        w = v.shape[-1]
        ss = _dot(hi, hm[:w, :w]) + _dot(lo, hm[:w, :w])
        return v * lax.rsqrt(ss * (1.0 / HEAD_DIM) + EPS) * g

    def finish(s, acc):
        rs = slice(s * sub, (s + 1) * sub)
        xa, gb, gc = acc[:, 0:256], acc[:, 256:512], acc[:, 512:768]
        z = gc * xa
        cq = acc[:, 768:1024]
        cqn = _rms(cq, gq_ref[...], MLA_Q_LORA)
        qm = _dot(cqn.astype(BF16), wq_ref[...]) * MLA_SCALE
        ckvn = _rms(acc[:, 1024:1152], gkv_ref[...], MLA_KV_LORA)
        kv = _dot(ckvn.astype(BF16), wkv_ref[...])
        kpe = acc[:, 1152:1280]
        vm = kv[:, 4 * LANES:]
        qc = head_rms(acc[:, 1280:1536], gqc_ref[...]) * ATTN_SCALE
        kc = head_rms(acc[:, 1536:1664], gkc_ref[...])
        vc = acc[:, 1664:1792]
        qd = acc[:, 1792:2048] * ATTN_SCALE
        kd = acc[:, 2048:2176]
        vd = acc[:, 2176:2304]

        if not rope:
            (zg_o, qm_o, km_o, vm_o, qc_o, kc_o, vc_o, qd_o, kd_o, vd_o,
             ckv_c, kpe_c, kc_c, vc_c, kd_c, vd_c) = outs
            km = kv[:, :4 * LANES] + jnp.concatenate([kpe] * MLA_HEADS, axis=1)
            zg_o[rs, 0:256] = z.astype(BF16)
            zg_o[rs, 256:512] = gb.astype(BF16)
            qm_o[rs, :] = qm.astype(BF16)
            km_o[rs, :] = km.astype(BF16)
            vm_o[:, rs] = vm.T.astype(BF16)
            qc_o[rs, :] = qc.astype(BF16)
            kc_o[rs, :] = kc.astype(BF16)
            vc_o[:, rs] = vc.T.astype(BF16)
            qd_o[rs, :] = qd.astype(BF16)
            kd_o[rs, :] = kd.astype(BF16)
            vd_o[:, rs] = vd.T.astype(BF16)
            for o, v in ((ckv_c, ckvn), (kpe_c, kpe[:, :MLA_ROPE]), (kc_c, kc), (vc_c, vc), (kd_c, kd),
                         (vd_c, vd)):
                t = o.shape[-2]
                bs = slice(s * sub // t, (s + 1) * sub // t)
                v = v.reshape(sub // t, t, v.shape[-1])
                if n_alias:
                    o[bs] = v
                else:
                    o[bs, 0] = v
                    o[bs, 1:] = jnp.zeros((sub // t, o.shape[1] - 1) + v.shape[1:], F32)
        else:
            (zg_o, qmr_o, qm_o, km_o, vm_o, qcr_o, qc_o, kc_o, vc_o, qdr_o, qd_o, kd_o, vd_o) = outs
            c64, s64, c32, s32 = c64_ref[rs, :], s64_ref[rs, :], c32_ref[rs, :], s32_ref[rs, :]
            rope64 = lambda v: jnp.concatenate(
                [_rope(v[:, i:i + LANES], c64, s64, 16) for i in range(0, v.shape[-1], LANES)], axis=1)
            rope32 = lambda v: jnp.concatenate(
                [_rope(v[:, i:i + LANES], c32, s32, 8) for i in range(0, v.shape[-1], LANES)], axis=1)
            km = kv[:, :4 * LANES] + jnp.concatenate([_rope(kpe, c32, s32, 8)] * MLA_HEADS, axis=1)
            zg_o[rs, 0:256] = z.astype(BF16)
            zg_o[rs, 256:512] = gb.astype(BF16)
            qmr_o[rs, :] = rope32(qm).astype(BF16)
            qm_o[rs, :] = qm.astype(BF16)
            km_o[rs, :] = km.astype(BF16)
            vm_o[:, rs] = vm.T.astype(BF16)
            qcr_o[rs, :] = rope64(qc).astype(BF16)
            qc_o[rs, :] = qc.astype(BF16)
            kc_o[rs, :] = rope64(kc).astype(BF16)
            vc_o[:, rs] = vc.T.astype(BF16)
            qdr_o[rs, :] = rope64(qd).astype(BF16)
            qd_o[rs, :] = qd.astype(BF16)
            kd_o[rs, :] = rope64(kd).astype(BF16)
            vd_o[:, rs] = vd.T.astype(BF16)

    nxt = project(0)
    for s in range(nsub):
        acc = nxt
        if s + 1 < nsub:
            nxt = project(s + 1)
        finish(s, acc)


def _inproj(x, mod, lw, l, seq_len, cond_base, rope_tabs, prev_caches=None):
    rows = x.shape[0]
    r = IN_ROW_TILE
    rope = rope_tabs is not None
    aliases = {}
    tiles_per_seq = max(seq_len // r, 1)
    step = 1 if seq_len >= r else 0

    def cond_map(i):
        return (l, cond_base + step * (i // tiles_per_seq), 0, 0)

    row = lambda w: pl.BlockSpec((r, w), lambda i: (i, 0))
    full = lambda a: pl.BlockSpec((None,) + a.shape[1:], lambda i: (l,) + (0,) * (a.ndim - 1))
    in_specs = [row(D_MODEL),
                pl.BlockSpec((None, None, 6, D_MODEL), cond_map),
                full(lw['norm1']), full(lw['w_in']), full(lw['wq']), full(lw['wkv']),
                full(lw['gq']), full(lw['gkv']), full(lw['gqc']), full(lw['gkc']),
                pl.BlockSpec((2 * LANES, 2 * LANES), lambda i: (0, 0))]
    args = [x, mod, lw['norm1'], lw['w_in'], lw['wq'], lw['wkv'], lw['gq'], lw['gkv'], lw['gqc'], lw['gkc'],
            lw['hm']]
    def act(w, transposed):
        if transposed:
            return jax.ShapeDtypeStruct((w, rows), BF16), pl.BlockSpec((w, r), lambda i: (0, i))
        return jax.ShapeDtypeStruct((rows, w), BF16), row(w)

    if rope:
        tab = pl.BlockSpec((r, LANES), lambda i: (i % tiles_per_seq, 0))
        in_specs += [tab] * 4
        args += list(rope_tabs)
        widths = [(512, 0), (512, 0), (512, 0), (512, 0), (256, 1), (256, 0), (256, 0), (128, 0),
                  (128, 1), (256, 0), (256, 0), (128, 0), (128, 1)]
        out_shape, out_specs = map(list, zip(*[act(w, t) for w, t in widths]))
    else:
        widths = [(512, 0), (512, 0), (512, 0), (256, 1), (256, 0), (128, 0), (128, 1), (256, 0),
                  (128, 0), (128, 1)]
        cwidths = [128, MLA_ROPE, 128, 128, 128, 128]
        nb = r // seq_len
        out_shape, out_specs = map(list, zip(*[act(w, t) for w, t in widths]))
        out_shape += [jax.ShapeDtypeStruct((rows // seq_len, DEPTH, seq_len, w), F32) for w in cwidths]
        assert (prev_caches is None) == (l == 0)
        if l == 0:
            out_specs += [pl.BlockSpec((nb, DEPTH, seq_len, w), lambda i: (i, 0, 0, 0)) for w in cwidths]
        else:
            out_specs += [pl.BlockSpec((nb, None, seq_len, w), lambda i: (i, l, 0, 0)) for w in cwidths]
        if prev_caches is not None:
            aliases = {len(args) + k: len(widths) + k for k in range(len(cwidths))}
            in_specs += [pl.BlockSpec(memory_space=pl.ANY)] * len(cwidths)
            args += list(prev_caches)
    return pl.pallas_call(
        functools.partial(_inproj_body, rope, len(aliases)),
        grid=(rows // r,),
        in_specs=in_specs, out_specs=out_specs, out_shape=out_shape,
        input_output_aliases=aliases,
        compiler_params=pltpu.CompilerParams(
            dimension_semantics=("arbitrary",), vmem_limit_bytes=VMEM_LIMIT),
        name="inproj_rope" if rope else "inproj",
    )(*args)


def _softmax_pv_t(s_list, vt_list, extra=None):
    m = s_list[0].max(axis=0, keepdims=True)
    for s in s_list[1:]:
        m = jnp.maximum(m, s.max(axis=0, keepdims=True))
    if extra is not None:
        m = jnp.maximum(m, extra)
    den = jnp.exp2(extra - m) if extra is not None else 0.0
    out = None
    for s, vt in zip(s_list, vt_list):
        p = jnp.exp2(s - m)
        den = den + p.sum(axis=0, keepdims=True)
        pv = _dot(vt, p.astype(BF16))
        out = pv if out is None else out + pv
    return out * (1.0 / den)


def _half_masks(q):
    lo = _lane(q.shape) < HEAD_DIM
    zero = jnp.zeros_like(q)
    return jnp.where(lo, q, zero), jnp.where(lo, zero, q)


def _merge_rows(a, b):
    r = lax.broadcasted_iota(jnp.int32, a.shape, 0)
    return jnp.where(r < HEAD_DIM, a, b)


def _sink_row(sink_ref, l, h0, h1, tq):
    c = lax.broadcasted_iota(jnp.int32, (1, 2 * tq), 1)
    return jnp.where(c < tq, sink_ref[l, h0], sink_ref[l, h1]) * LOG2E


def _run_groups(stage1, stage2, n, ahead=2):
    out = []
    pending = [stage1(g) for g in range(min(ahead, n))]
    for g in range(n):
        if g + ahead < n:
            pending.append(stage1(g + ahead))
        out.append(stage2(g, pending.pop(0)))
    return out


def _mla_scores(h, q_sets, k_sets):
    res = []
    for q, k in zip(q_sets, k_sets):
        cs = [slice(hh * LANES, (hh + 1) * LANES) for hh in (2 * h, 2 * h + 1)]
        res.append(jnp.concatenate([_dot_nt(k[:, c], q[:, c]) for c in cs], axis=1))
    return res


def _gqa_scores(j, q_sets, k_sets):
    res = []
    for q, k in zip(q_sets, k_sets):
        qa = _half_masks(q[:, 0:LANES])[j]
        qb = _half_masks(q[:, LANES:])[j]
        res.append(_dot_nt(k[...], jnp.concatenate([qa, qb], axis=0)))
    return res


def _store_pair(y_ref, rows, blk, ot):
    y_ref[rows, blk * LANES:(blk + 1) * LANES] = ot.T.astype(BF16)


def _store_groups(y_ref, rows, o, tq):
    for p in range(2):
        _store_pair(y_ref, rows, p, _merge_rows(o[p][:, :tq], o[p][:, tq:]))
    for base, (o0, o1) in ((2, o[2:4]), (4, o[4:6])):
        _store_pair(y_ref, rows, base, _merge_rows(o0[:, :tq], o1[:, :tq]))
        _store_pair(y_ref, rows, base + 1, _merge_rows(o0[:, tq:], o1[:, tq:]))


def _attn_ctx_body(l, nb, t, sink_ref, qm_ref, km_ref, vm_ref, qc_ref, kc_ref, vc_ref, qd_ref, kd_ref, vd_ref,
                   y_ref):
    def stage1(g):
        b, k = divmod(g, 6)
        rs = pl.ds(b * t, t)
        if k < 2:
            return _mla_scores(k, [qm_ref.at[rs]], [km_ref.at[rs]])
        q_ref, k_ref = (qc_ref, kc_ref) if k < 4 else (qd_ref, kd_ref)
        return _gqa_scores(k % 2, [q_ref.at[rs]], [k_ref.at[rs]])

    def stage2(g, s):
        b, k = divmod(g, 6)
        rs = slice(b * t, (b + 1) * t)
        if k < 2:
            return _softmax_pv_t(s, [vm_ref[k * LANES:(k + 1) * LANES, rs]])
        if k < 4:
            return _softmax_pv_t(s, [vc_ref[:, rs]])
        j = k % 2
        return _softmax_pv_t(s, [vd_ref[:, rs]], _sink_row(sink_ref, l, 2 * j, 2 * j + 1, t))

    o = _run_groups(stage1, stage2, 6 * nb)
    for b in range(nb):
        _store_groups(y_ref, slice(b * t, (b + 1) * t), o[6 * b:6 * b + 6], t)


def _attn_ctx(l, t, sink, qm, km, vm, qc, kc, vc, qd, kd, vd):
    rows = qm.shape[0]
    nb = 4
    r = nb * t
    row = lambda a: (pl.BlockSpec((a.shape[0], r), lambda i: (0, i)) if a.shape[1] == rows
                     else pl.BlockSpec((r, a.shape[1]), lambda i: (i, 0)))
    ins = [qm, km, vm, qc, kc, vc, qd, kd, vd]
    return pl.pallas_call(
        functools.partial(_attn_ctx_body, l, nb, t),
        grid=(rows // r,),
        in_specs=[pl.BlockSpec(memory_space=pltpu.SMEM)] + [row(a) for a in ins],
        out_specs=pl.BlockSpec((r, 6 * LANES), lambda i: (i, 0)),
        out_shape=jax.ShapeDtypeStruct((rows, 6 * LANES), BF16),
        compiler_params=pltpu.CompilerParams(
            dimension_semantics=("arbitrary",), vmem_limit_bytes=VMEM_LIMIT),
        name="attn_ctx",
    )(sink, *ins)


def _window_block(i, tq, t, nwb):
    return jnp.clip(i * (tq // LANES) - WINDOW // LANES, 0, t // LANES - nwb)


def _attn_lat_body(l, tq, t, nwb, sink_ref, qmr_ref, qm_ref, km_ref, vm_ref, kx_ref, vx_ref,
                   qcr_ref, qc_ref, kc_ref, vc_ref, kcx_ref, vcx_ref,
                   qdr_ref, qd_ref, kdx_ref, vdx_ref, *rest):
    kd_refs, vd_refs = rest[:nwb], rest[nwb:2 * nwb]
    y_ref, = rest[2 * nwb:]
    i = pl.program_id(1)
    wk = nwb * LANES
    kpos = _window_block(i, tq, t, nwb) * LANES + lax.broadcasted_iota(jnp.int32, (wk, 2 * tq), 0)
    qpos = i * tq + lax.broadcasted_iota(jnp.int32, (wk, 2 * tq), 1) % tq
    valid = jnp.abs(kpos - qpos) <= WINDOW

    def stage1(g):
        if g < 2:
            return _mla_scores(g, [qmr_ref, qm_ref], [km_ref, kx_ref])
        if g < 4:
            return _gqa_scores(g % 2, [qcr_ref, qc_ref], [kc_ref, kcx_ref])
        kw = jnp.concatenate([r[...] for r in kd_refs], axis=0)
        s1, s2 = _gqa_scores(g % 2, [qdr_ref, qd_ref], [kw, kdx_ref])
        return [jnp.where(valid, s1, NEG_INF), s2]

    def stage2(g, s):
        if g < 2:
            blk = slice(g * LANES, (g + 1) * LANES)
            return _softmax_pv_t(s, [vm_ref[blk, :], vx_ref[blk, :]])
        if g < 4:
            return _softmax_pv_t(s, [vc_ref[...], vcx_ref[...]])
        j = g % 2
        vw = jnp.concatenate([r[...] for r in vd_refs], axis=1)
        return _softmax_pv_t(s, [vw, vdx_ref[...]], _sink_row(sink_ref, l, 2 * j, 2 * j + 1, tq))

    _store_groups(y_ref, slice(None), _run_groups(stage1, stage2, 6), tq)


def _attn_lat(l, nbatch, t, sink, qmr, qm, km, vmt, kx, vxt, qcr, qc, kc, vct, kcx, vcxt, qdr, qd, kd, vdt,
              kdx, vdxt):
    tq = 256
    nq = t // tq
    nwb = (tq + 2 * WINDOW) // LANES
    nkb = t // LANES
    rows = qm.shape[0]
    qs = lambda a: pl.BlockSpec((tq, a.shape[1]), lambda b, i: (b * nq + i, 0))
    ks = lambda a: pl.BlockSpec((t, a.shape[1]), lambda b, i: (b, 0))
    kts = lambda a: pl.BlockSpec((a.shape[0], t), lambda b, i: (0, b))
    xs = lambda a: pl.BlockSpec((None, None) + a.shape[2:], lambda b, i: (l, b, 0, 0))
    cs = lambda a: pl.BlockSpec((None, None) + a.shape[2:], lambda b, i: (b, l, 0, 0))
    wblk = lambda b, i, k: b * nkb + _window_block(i, tq, t, nwb) + k
    kd_specs = [pl.BlockSpec((LANES, LANES), lambda b, i, k=k: (wblk(b, i, k), 0)) for k in range(nwb)]
    vd_specs = [pl.BlockSpec((LANES, LANES), lambda b, i, k=k: (0, wblk(b, i, k))) for k in range(nwb)]
    in_specs = [pl.BlockSpec(memory_space=pltpu.SMEM),
                qs(qmr), qs(qm), ks(km), kts(vmt), xs(kx), xs(vxt),
                qs(qcr), qs(qc), ks(kc), kts(vct), cs(kcx), cs(vcxt),
                qs(qdr), qs(qd), cs(kdx), cs(vdxt)] + kd_specs + vd_specs
    return pl.pallas_call(
        functools.partial(_attn_lat_body, l, tq, t, nwb),
        grid=(nbatch, nq),
        in_specs=in_specs,
        out_specs=pl.BlockSpec((tq, 6 * LANES), lambda b, i: (b * nq + i, 0)),
        out_shape=jax.ShapeDtypeStruct((rows, 6 * LANES), BF16),
        compiler_params=pltpu.CompilerParams(
            dimension_semantics=("arbitrary", "arbitrary"), vmem_limit_bytes=VMEM_LIMIT),
        name="attn_lat",
    )(sink, qmr, qm, km, vmt, kx, vxt, qcr, qc, kc, vct, kcx, vcxt, qdr, qd, kdx, vdxt,
      *([kd] * nwb), *([vdt] * nwb))


def _post_body(seq_len, final, mod_ref, n2_ref, cwa_ref, wo_ref, cwf_ref, wup_ref, wdn_ref, fn_ref, *rest):
    ins, (o_ref, hbuf, gbuf, pbuf, abuf) = rest[:9], rest[9:]
    cat = lambda k: jnp.concatenate([ins[3 * k][...], ins[3 * k + 1][...], ins[3 * k + 2][...]], axis=0)
    r = o_ref.shape[0]
    halo = BF16_ROWS
    n = r + 2 * halo
    grp = r // 8
    nchunk = D_MODEL // LANES
    i = pl.program_id(0)

    zg = cat(1).astype(F32)
    zz, gb = zg[:, :GROUP_WIDTH], zg[:, GROUP_WIDTH:]
    pos = (i * r - halo + lax.broadcasted_iota(jnp.int32, (n, 1), 0)) % seq_len
    prev = jnp.where(pos == 0, 0.0, pltpu.roll(zz, 1, 0))
    nxt = jnp.where(pos == seq_len - 1, 0.0, pltpu.roll(zz, n - 1, 0))
    cwa = cwa_ref[...]
    ya = gb * (prev * cwa[0:1] + zz * cwa[1:2] + nxt * cwa[2:3])
    ycat = jnp.concatenate([ya.astype(BF16), cat(2)], axis=1)
    x1 = cat(0) + mod_ref[2:3, :] * _dot(ycat, wo_ref[...])
    h2 = _rms(x1, n2_ref[...], D_MODEL) * (1.0 + mod_ref[4:5, :]) + mod_ref[3:4, :]
    x1c = x1[halo:halo + r]

    for c in range(nchunk):
        for s in range(8):
            pbuf[c, pl.ds(s, grp, stride=8), :] = h2[halo + s * grp:halo + (s + 1) * grp, c * LANES:(c + 1) * LANES]
    hrow = lax.broadcasted_iota(jnp.int32, (halo, 1), 0)
    edge = jnp.where(hrow == 0, h2[halo - 1:halo], jnp.where(hrow == 1, h2[halo + r:halo + r + 1], 0.0))
    hbuf[0:halo, :] = edge.astype(BF16)
    hbuf[halo:, :] = jnp.concatenate([pbuf[c] for c in range(nchunk)], axis=1).astype(BF16)

    sub = lax.broadcasted_iota(jnp.int32, (8, 1), 0)
    seq_first = (i * r + sub * grp) % seq_len == 0
    seq_last = (i * r + sub * grp + grp - 1) % seq_len == seq_len - 1

    def up(j):
        ga = slice(j * FF_TILE, (j + 1) * FF_TILE)
        va = slice(D_FF + j * FF_TILE, D_FF + (j + 1) * FF_TILE)
        return _dot(hbuf[...], wup_ref[:, ga]), _dot(hbuf[...], wup_ref[:, va])

    def conv(u, cw):
        head = jnp.where(sub == 0, u[0:8], pltpu.roll(u[halo + r - 8:halo + r], 1, 0))
        tail = jnp.where(sub == 7, pltpu.roll(u[0:8], 6, 0), pltpu.roll(u[halo:halo + 8], 7, 0))
        prev = jnp.concatenate([jnp.where(seq_first, 0.0, head), u[halo:halo + r - 8]], axis=0)
        nxt = jnp.concatenate([u[halo + 8:halo + r], jnp.where(seq_last, 0.0, tail)], axis=0)
        return prev * cw[0:1] + u[halo:halo + r] * cw[1:2] + nxt * cw[2:3]

    acc = None
    nxt_u = up(0)
    for j in range(N_FF_TILES):
        ua, ub = nxt_u
        if j + 1 < N_FF_TILES:
            nxt_u = up(j + 1)
        ga = slice(j * FF_TILE, (j + 1) * FF_TILE)
        va = slice(D_FF + j * FF_TILE, D_FF + (j + 1) * FF_TILE)
        ua = conv(ua, cwf_ref[:, ga])
        ub = conv(ub, cwf_ref[:, va])
        gbuf[:, ga] = (ua * jax.nn.sigmoid(ua) * ub).astype(BF16)
        if j % DOWN_CHUNK == DOWN_CHUNK - 1 or j == N_FF_TILES - 1:
            lo = (j // DOWN_CHUNK) * DOWN_CHUNK * FF_TILE
            hi = (j + 1) * FF_TILE
            d = _dot(gbuf[:, lo:hi], wdn_ref[lo:hi, :])
            acc = d if acc is None else acc + d

    for c in range(nchunk):
        abuf[c] = acc[:, c * LANES:(c + 1) * LANES]
    g2 = mod_ref[5:6, :]
    for c in range(nchunk):
        cs = slice(c * LANES, (c + 1) * LANES)
        for s in range(8):
            rs = slice(s * grp, (s + 1) * grp)
            o_ref[rs, cs] = x1c[rs, cs] + g2[:, cs] * abuf[c, pl.ds(s, grp, stride=8), :]
    if final:
        o_ref[...] = _rms(o_ref[...], fn_ref[...], D_MODEL)


def _halo_specs(r, w, rows):
    nblk = rows // BF16_ROWS
    per = r // BF16_ROWS
    prev = pl.BlockSpec((BF16_ROWS, w), lambda i: (jnp.maximum(i * per - 1, 0), 0))
    cur = pl.BlockSpec((r, w), lambda i: (i, 0))
    nxt = pl.BlockSpec((BF16_ROWS, w), lambda i: (jnp.minimum((i + 1) * per, nblk - 1), 0))
    return prev, cur, nxt


def _cond_spec(l, seq_len, cond_base):
    r = ROW_TILE
    tiles_per_seq = max(seq_len // r, 1)
    step = 1 if seq_len >= r else 0
    return pl.BlockSpec((None, None, 6, D_MODEL),
                        lambda i: (l, cond_base + step * (i // tiles_per_seq), 0, 0))


def _post(x, mod, lw, l, seq_len, cond_base, zg, y, final_norm, final):
    rows = x.shape[0]
    r = ROW_TILE
    full = lambda a: pl.BlockSpec((None,) + a.shape[1:], lambda i: (l,) + (0,) * (a.ndim - 1),
                                  pipeline_mode=pl.Buffered(1))
    acts = [x, zg, y]
    act_specs = [s for a in acts for s in _halo_specs(r, a.shape[1], rows)]
    act_args = [a for a in acts for _ in range(3)]
    return pl.pallas_call(
        functools.partial(_post_body, seq_len, final),
        grid=(rows // r,),
        in_specs=[_cond_spec(l, seq_len, cond_base), full(lw['norm2']), full(lw['conv_a']), full(lw['w_out']),
                  full(lw['conv_ff']), full(lw['w_up']), full(lw['w_down']),
                  pl.BlockSpec((1, D_MODEL), lambda i: (0, 0))] + act_specs,
        out_specs=pl.BlockSpec((r, D_MODEL), lambda i: (i, 0)),
        out_shape=jax.ShapeDtypeStruct((rows, D_MODEL), F32),
        scratch_shapes=[pltpu.VMEM((r + BF16_ROWS, D_MODEL), BF16), pltpu.VMEM((r, D_FF), BF16),
                        pltpu.VMEM((D_MODEL // LANES, r, LANES), F32),
                        pltpu.VMEM((D_MODEL // LANES, r, LANES), F32)],
        compiler_params=pltpu.CompilerParams(
            dimension_semantics=("arbitrary",), vmem_limit_bytes=VMEM_LIMIT),
        name="post",
    )(mod, lw['norm2'], lw['conv_a'], lw['w_out'], lw['conv_ff'], lw['w_up'], lw['w_down'], final_norm,
      *act_args)


def _pad_cols(a, n):
    return jnp.pad(a, [(0, 0)] * (a.ndim - 1) + [(0, n - a.shape[-1])])


def _perm_heads(a, axis):
    h = [lax.slice_in_dim(a, k * HEAD_DIM, (k + 1) * HEAD_DIM, axis=axis) for k in range(4)]
    return jnp.concatenate([h[0], h[2], h[1], h[3]], axis=axis)


def _prep_weights(w_in, mla_wq_b, mla_wkv_b, w_out, w_up, conv_ff, w_down, norm1, norm2, conv_a,
                  mla_q_norm, mla_kv_norm, gqa_q_norm, gqa_k_norm):
    L = DEPTH
    a = w_in[..., 0:768]
    cq = _pad_cols(w_in[..., 768:960], 256)
    ckv = w_in[..., 960:1088]
    kpe = _pad_cols(w_in[..., 1088:1120], LANES)
    qc = _perm_heads(w_in[..., 1120:1376], 2)
    kvc = w_in[..., 1376:1632]
    qd = _perm_heads(w_in[..., 1632:1888], 2)
    kvd = w_in[..., 1888:2144]
    w_in_p = jnp.concatenate([a, cq, ckv, kpe, qc, kvc, qd, kvd], axis=-1).astype(BF16)

    wq = mla_wq_b.reshape(L, MLA_Q_LORA, MLA_HEADS, MLA_NOPE + MLA_ROPE)
    wq = jnp.concatenate([wq[..., MLA_NOPE:], wq[..., :MLA_NOPE],
                          jnp.zeros((L, MLA_Q_LORA, MLA_HEADS, LANES - MLA_NOPE - MLA_ROPE), F32)], axis=-1)
    wq = jnp.pad(wq.reshape(L, MLA_Q_LORA, MLA_HEADS * LANES), ((0, 0), (0, 256 - MLA_Q_LORA), (0, 0)))

    wkv = mla_wkv_b.reshape(L, MLA_KV_LORA, MLA_HEADS, MLA_NOPE + MLA_V)
    zk = jnp.zeros((L, MLA_KV_LORA, MLA_HEADS, MLA_ROPE), F32)
    wk = jnp.concatenate([zk, wkv[..., :MLA_NOPE], zk], axis=-1).reshape(L, MLA_KV_LORA, MLA_HEADS * LANES)
    wv = wkv[..., MLA_NOPE:].reshape(L, MLA_KV_LORA, MLA_HEADS * MLA_V)
    wkv_p = jnp.concatenate([wk, wv], axis=-1)

    wo = jnp.concatenate([w_out[:, 0:512], _perm_heads(w_out[:, 512:768], 1),
                          _perm_heads(w_out[:, 768:1024], 1)], axis=1)

    hm = jnp.asarray(np.kron(np.eye(2 * LANES // HEAD_DIM), np.ones((HEAD_DIM, HEAD_DIM))), BF16)
    return {
        'w_in': w_in_p, 'wq': wq.astype(BF16), 'wkv': wkv_p.astype(BF16), 'w_out': wo.astype(BF16),
        'w_up': w_up.astype(BF16), 'conv_ff': conv_ff, 'w_down': w_down.astype(BF16),
        'norm1': norm1.reshape(L, 1, D_MODEL), 'norm2': norm2.reshape(L, 1, D_MODEL), 'conv_a': conv_a,
        'gq': _pad_cols(mla_q_norm, 256).reshape(L, 1, 256), 'gkv': mla_kv_norm.reshape(L, 1, MLA_KV_LORA),
        'gqc': jnp.tile(gqa_q_norm, (1, 4)).reshape(L, 1, 256), 'gkc': jnp.tile(gqa_k_norm, (1, 2)).reshape(L, 1, 128),
        'hm': hm,
    }


def _rope_tables(t):
    rows = t // GRID_W
    row = np.repeat(np.arange(rows, dtype=np.float64), GRID_W)
    col = np.tile(np.arange(GRID_W, dtype=np.float64), rows)

    def tabs(dim):
        half = dim // 2
        inv = np.power(ROPE_THETA, -np.arange(0, half, 2, dtype=np.float64) / half)
        ar = row[:, None] * inv
        ac = col[:, None] * inv
        c = np.concatenate([np.cos(ar), np.cos(ar), np.cos(ac), np.cos(ac)], axis=1)
        s = np.concatenate([-np.sin(ar), np.sin(ar), -np.sin(ac), np.sin(ac)], axis=1)
        return c, s

    c64, s64 = tabs(HEAD_DIM)
    c32, s32 = tabs(MLA_ROPE)
    c64 = np.tile(c64, (1, 2))
    s64 = np.tile(s64, (1, 2))
    c32 = np.concatenate([c32, np.ones((t, LANES - MLA_ROPE))], axis=1)
    s32 = np.concatenate([s32, np.zeros((t, LANES - MLA_ROPE))], axis=1)
    return tuple(jnp.asarray(a.astype(np.float32)) for a in (c64, s64, c32, s32))


def kernel(x_prompt, x_sample, cache_mla_ckv, cache_mla_kpe, cache_gqa_k, cache_gqa_v, cache_swa_k, cache_swa_v,
           c, c_ctx, w_ada, b_ada, norm1, w_in, conv_a, mla_q_norm, mla_wq_b, mla_kv_norm, mla_wkv_b,
           gqa_q_norm, gqa_k_norm, swa_sink, w_out, norm2, w_up, conv_ff, w_down, final_norm):
    B, T, _ = x_prompt.shape
    DB, DT, _ = x_sample.shape
    past = cache_mla_ckv.shape[2]

    lw = _prep_weights(w_in, mla_wq_b, mla_wkv_b, w_out, w_up, conv_ff, w_down, norm1, norm2, conv_a,
                       mla_q_norm, mla_kv_norm, gqa_q_norm, gqa_k_norm)
    rope_tabs = _rope_tables(DT)
    fnorm = final_norm.reshape(1, D_MODEL)

    cond_t = jnp.concatenate([c_ctx[:, None], c.T, jnp.zeros((D_MODEL, 8 - 1 - DB), F32)], axis=1)
    mod = _ada(cond_t, 1 + DB, w_ada, b_ada).reshape(DEPTH, 8, 6, D_MODEL)

    kx, vx = _ctx_mla(cache_mla_ckv, _pad_cols(cache_mla_kpe, LANES), lw['wkv'])
    flat = lambda a: a.reshape(DB, DEPTH, past, 2 * HEAD_DIM).astype(BF16)
    kcx, kdx = flat(cache_gqa_k), flat(cache_swa_k)
    vcx, vdx = jnp.swapaxes(flat(cache_gqa_v), 2, 3), jnp.swapaxes(flat(cache_swa_v), 2, 3)

    xp = x_prompt.reshape(B * T, D_MODEL)
    xs = x_sample.reshape(DB * DT, D_MODEL)
    caches = None
    for l in range(DEPTH):
        final = l == DEPTH - 1
        (zg, qm, km, vm, qc, kc, vc, qd, kd, vd, *caches) = _inproj(xp, mod, lw, l, T, 0, None, caches)
        y = _attn_ctx(l, T, swa_sink, qm, km, vm, qc, kc, vc, qd, kd, vd)
        xp = _post(xp, mod, lw, l, T, 0, zg, y, fnorm, final)
        (zg, qmr, qm, km, vm, qcr, qc, kc, vc, qdr, qd, kd, vd) = _inproj(xs, mod, lw, l, DT, 1, rope_tabs)
        y = _attn_lat(l, DB, DT, swa_sink, qmr, qm, km, vm, kx, vx, qcr, qc, kc, vc, kcx, vcx,
                      qdr, qd, kd, vd, kdx, vdx)
        xs = _post(xs, mod, lw, l, DT, 1, zg, y, fnorm, final)

    heads = lambda a: a.reshape(B, DEPTH, T, 2, HEAD_DIM)
    return (xp.reshape(B, T, D_MODEL), xs.reshape(DB, DT, D_MODEL), caches[0], caches[1],
            heads(caches[2]), heads(caches[3]), heads(caches[4]), heads(caches[5]))
```

```python
import functools
import math

import jax
import jax.numpy as jnp
import numpy as np
from jax import lax
from jax.experimental import pallas as pl
from jax.experimental.pallas import tpu as pltpu

F32 = jnp.float32
BF16 = jnp.bfloat16

D_MODEL = 1024
DEPTH = 2
GRID_W = 64
HEAD_DIM = 64
GROUP_WIDTH = D_MODEL // 4
MLA_HEADS = 4
MLA_NOPE = 64
MLA_ROPE = 32
MLA_V = 64
MLA_Q_LORA = 192
MLA_KV_LORA = 128
WINDOW = 128
D_FF = 2816
ROPE_THETA = 10000.0
EPS = 1e-6
NEG_INF = -1e30
LOG2E = math.log2(math.e)
ATTN_SCALE = HEAD_DIM ** -0.5 * LOG2E
MLA_SCALE = (MLA_NOPE + MLA_ROPE) ** -0.5 * LOG2E

LANES = 128
BF16_ROWS = 16
ROW_TILE = 512
IN_ROW_TILE = 1024
IN_SUB_TILE = 256
FF_TILE = 256
N_FF_TILES = D_FF // FF_TILE
DOWN_CHUNK = 4
IN_COLS_PACKED = 2304
VMEM_LIMIT = 56 * 1024 * 1024

_NT = (((1,), (1,)), ((), ()))


def _dot(a, b):
    return jnp.dot(a, b, preferred_element_type=F32)


def _dot_nt(a, b):
    return lax.dot_general(a, b, _NT, preferred_element_type=F32)


def _rms(x, g, n):
    ms = jnp.sum(x * x, axis=-1, keepdims=True) * (1.0 / n)
    return x * lax.rsqrt(ms + EPS) * g


def _lane(shape):
    return lax.broadcasted_iota(jnp.int32, shape, len(shape) - 1)


def _rope(x, c, s, half):
    w = x.shape[-1]
    lo = (_lane(x.shape) % (2 * half)) < half
    sw = jnp.where(lo, pltpu.roll(x, w - half, 1), pltpu.roll(x, half, 1))
    return x * c + sw * s


def _shift_rows(zz, halo, rows):
    n = zz.shape[0]
    prev = pltpu.roll(zz, 1, 0)[halo:halo + rows]
    nxt = pltpu.roll(zz, n - 1, 0)[halo:halo + rows]
    return prev, nxt


def _seq_pos(tile, rows, seq_len):
    r = lax.broadcasted_iota(jnp.int32, (rows, 1), 0)
    return (tile * rows + r) % seq_len


def _ada_body(ncond, c_ref, w_ref, b_ref, o_ref):
    c = c_ref[...]
    s = c * jax.nn.sigmoid(c)
    tn = w_ref.shape[1]
    w = w_ref[...].reshape(D_MODEL // 8, 8, tn)
    rows = []
    for r in range(ncond):
        part = jnp.sum(w * s[:, r:r + 1].reshape(D_MODEL // 8, 8, 1), axis=0)
        rows.append(jnp.sum(part, axis=0, keepdims=True))
    rows.append(jnp.zeros((8 - ncond, tn), F32))
    o_ref[...] = jnp.concatenate(rows, axis=0) + b_ref[...]


def _ada(cond_t, ncond, w_ada, b_ada):
    tn = 1536
    n = 6 * D_MODEL
    return pl.pallas_call(
        functools.partial(_ada_body, ncond),
        grid=(DEPTH, n // tn),
        in_specs=[pl.BlockSpec((D_MODEL, 8), lambda l, j: (0, 0)),
                  pl.BlockSpec((None, D_MODEL, tn), lambda l, j: (l, 0, j)),
                  pl.BlockSpec((None, 1, tn), lambda l, j: (l, 0, j))],
        out_specs=pl.BlockSpec((None, 8, tn), lambda l, j: (l, 0, j)),
        out_shape=jax.ShapeDtypeStruct((DEPTH, 8, n), F32),
        compiler_params=pltpu.CompilerParams(
            dimension_semantics=("arbitrary", "arbitrary"), vmem_limit_bytes=VMEM_LIMIT),
        name="ada",
    )(cond_t, w_ada, b_ada.reshape(DEPTH, 1, n))


def _ctx_body(ckv_ref, kpe_ref, w_ref, kx_ref, vx_ref):
    kv = _dot(ckv_ref[...].astype(BF16), w_ref[...])
    kpe = kpe_ref[...]
    kx_ref[...] = (kv[:, :4 * LANES] + jnp.concatenate([kpe] * MLA_HEADS, axis=1)).astype(BF16)
    vx_ref[...] = kv[:, 4 * LANES:].T.astype(BF16)


def _ctx_mla(cache_ckv, cache_kpe_pad, wkv_p):
    b, _, s, _ = cache_ckv.shape
    return pl.pallas_call(
        _ctx_body,
        grid=(DEPTH, b),
        in_specs=[pl.BlockSpec((None, None, s, MLA_KV_LORA), lambda l, i: (i, l, 0, 0)),
                  pl.BlockSpec((None, None, s, LANES), lambda l, i: (i, l, 0, 0)),
                  pl.BlockSpec((None, MLA_KV_LORA, 6 * LANES), lambda l, i: (l, 0, 0))],
        out_specs=[pl.BlockSpec((None, None, s, 4 * LANES), lambda l, i: (l, i, 0, 0)),
                   pl.BlockSpec((None, None, 2 * LANES, s), lambda l, i: (l, i, 0, 0))],
        out_shape=[jax.ShapeDtypeStruct((DEPTH, b, s, 4 * LANES), BF16),
                   jax.ShapeDtypeStruct((DEPTH, b, 2 * LANES, s), BF16)],
        compiler_params=pltpu.CompilerParams(
            dimension_semantics=("arbitrary", "arbitrary"), vmem_limit_bytes=VMEM_LIMIT),
        name="ctx_mla",
    )(cache_ckv, cache_kpe_pad, wkv_p)


def _inproj_body(rope, n_alias, x_ref, mod_ref, n1_ref, w_ref, wq_ref, wkv_ref, gq_ref, gkv_ref, gqc_ref, gkc_ref,
                 hm_ref, *rest):
    if rope:
        c64_ref, s64_ref, c32_ref, s32_ref = rest[:4]
        outs = rest[4:]
    else:
        outs = rest[n_alias:]
    sh1 = mod_ref[0:1, :]
    sc1 = mod_ref[1:2, :]
    hm = hm_ref[...]
    sub = IN_SUB_TILE
    nsub = x_ref.shape[0] // sub

    def project(s):
        x = x_ref[s * sub:(s + 1) * sub, :]
        h = _rms(x, n1_ref[...], D_MODEL) * (1.0 + sc1) + sh1
        return _dot(h.astype(BF16), w_ref[...])

    def head_rms(v, g):
        v2 = v * v
        hi = v2.astype(BF16)
        lo = (v2 - hi.astype(F32)).astype(BF16)
        w = v.shape[-1]
        ss = _dot(hi, hm[:w, :w]) + _dot(lo, hm[:w, :w])
        return v * lax.rsqrt(ss * (1.0 / HEAD_DIM) + EPS) * g

    def finish(s, acc):
        rs = slice(s * sub, (s + 1) * sub)
        xa, gb, gc = acc[:, 0:256], acc[:, 256:512], acc[:, 512:768]
        z = gc * xa
        cq = acc[:, 768:1024]
        cqn = _rms(cq, gq_ref[...], MLA_Q_LORA)
        qm = _dot(cqn.astype(BF16), wq_ref[...]) * MLA_SCALE
        ckvn = _rms(acc[:, 1024:1152], gkv_ref[...], MLA_KV_LORA)
        kv = _dot(ckvn.astype(BF16), wkv_ref[...])
        kpe = acc[:, 1152:1280]
        vm = kv[:, 4 * LANES:]
        qc = head_rms(acc[:, 1280:1536], gqc_ref[...]) * ATTN_SCALE
        kc = head_rms(acc[:, 1536:1664], gkc_ref[...])
        vc = acc[:, 1664:1792]
        qd = acc[:, 1792:2048] * ATTN_SCALE
        kd = acc[:, 2048:2176]
        vd = acc[:, 2176:2304]

        if not rope:
            (zg_o, qm_o, km_o, vm_o, qc_o, kc_o, vc_o, qd_o, kd_o, vd_o,
             ckv_c, kpe_c, kc_c, vc_c, kd_c, vd_c) = outs
            km = kv[:, :4 * LANES] + jnp.concatenate([kpe] * MLA_HEADS, axis=1)
            zg_o[rs, 0:256] = z.astype(BF16)
            zg_o[rs, 256:512] = gb.astype(BF16)
            qm_o[rs, :] = qm.astype(BF16)
            km_o[rs, :] = km.astype(BF16)
            vm_o[:, rs] = vm.T.astype(BF16)
            qc_o[rs, :] = qc.astype(BF16)
            kc_o[rs, :] = kc.astype(BF16)
            vc_o[:, rs] = vc.T.astype(BF16)
            qd_o[rs, :] = qd.astype(BF16)
            kd_o[rs, :] = kd.astype(BF16)
            vd_o[:, rs] = vd.T.astype(BF16)
            for o, v in ((ckv_c, ckvn), (kpe_c, kpe[:, :MLA_ROPE]), (kc_c, kc), (vc_c, vc), (kd_c, kd),
                         (vd_c, vd)):
                t = o.shape[-2]
                bs = slice(s * sub // t, (s + 1) * sub // t)
                v = v.reshape(sub // t, t, v.shape[-1])
                if n_alias:
                    o[bs] = v
                else:
                    o[bs, 0] = v
                    o[bs, 1:] = jnp.zeros((sub // t, o.shape[1] - 1) + v.shape[1:], F32)
        else:
            (zg_o, qmr_o, qm_o, km_o, vm_o, qcr_o, qc_o, kc_o, vc_o, qdr_o, qd_o, kd_o, vd_o) = outs
            c64, s64, c32, s32 = c64_ref[rs, :], s64_ref[rs, :], c32_ref[rs, :], s32_ref[rs, :]
            rope64 = lambda v: jnp.concatenate(
                [_rope(v[:, i:i + LANES], c64, s64, 16) for i in range(0, v.shape[-1], LANES)], axis=1)
            rope32 = lambda v: jnp.concatenate(
                [_rope(v[:, i:i + LANES], c32, s32, 8) for i in range(0, v.shape[-1], LANES)], axis=1)
            km = kv[:, :4 * LANES] + jnp.concatenate([_rope(kpe, c32, s32, 8)] * MLA_HEADS, axis=1)
            zg_o[rs, 0:256] = z.astype(BF16)
            zg_o[rs, 256:512] = gb.astype(BF16)
            qmr_o[rs, :] = rope32(qm).astype(BF16)
            qm_o[rs, :] = qm.astype(BF16)
            km_o[rs, :] = km.astype(BF16)
            vm_o[:, rs] = vm.T.astype(BF16)
            qcr_o[rs, :] = rope64(qc).astype(BF16)
            qc_o[rs, :] = qc.astype(BF16)
            kc_o[rs, :] = rope64(kc).astype(BF16)
            vc_o[:, rs] = vc.T.astype(BF16)
            qdr_o[rs, :] = rope64(qd).astype(BF16)
            qd_o[rs, :] = qd.astype(BF16)
            kd_o[rs, :] = rope64(kd).astype(BF16)
            vd_o[:, rs] = vd.T.astype(BF16)

    nxt = project(0)
    for s in range(nsub):
        acc = nxt
        if s + 1 < nsub:
            nxt = project(s + 1)
        finish(s, acc)


def _inproj(x, mod, lw, l, seq_len, cond_base, rope_tabs, prev_caches=None):
    rows = x.shape[0]
    r = IN_ROW_TILE
    rope = rope_tabs is not None
    aliases = {}
    tiles_per_seq = max(seq_len // r, 1)
    step = 1 if seq_len >= r else 0

    def cond_map(i):
        return (l, cond_base + step * (i // tiles_per_seq), 0, 0)

    row = lambda w: pl.BlockSpec((r, w), lambda i: (i, 0))
    full = lambda a: pl.BlockSpec((None,) + a.shape[1:], lambda i: (l,) + (0,) * (a.ndim - 1))
    in_specs = [row(D_MODEL),
                pl.BlockSpec((None, None, 6, D_MODEL), cond_map),
                full(lw['norm1']), full(lw['w_in']), full(lw['wq']), full(lw['wkv']),
                full(lw['gq']), full(lw['gkv']), full(lw['gqc']), full(lw['gkc']),
                pl.BlockSpec((2 * LANES, 2 * LANES), lambda i: (0, 0))]
    args = [x, mod, lw['norm1'], lw['w_in'], lw['wq'], lw['wkv'], lw['gq'], lw['gkv'], lw['gqc'], lw['gkc'],
            lw['hm']]
    def act(w, transposed):
        if transposed:
            return jax.ShapeDtypeStruct((w, rows), BF16), pl.BlockSpec((w, r), lambda i: (0, i))
        return jax.ShapeDtypeStruct((rows, w), BF16), row(w)

    if rope:
        tab = pl.BlockSpec((r, LANES), lambda i: (i % tiles_per_seq, 0))
        in_specs += [tab] * 4
        args += list(rope_tabs)
        widths = [(512, 0), (512, 0), (512, 0), (512, 0), (256, 1), (256, 0), (256, 0), (128, 0),
                  (128, 1), (256, 0), (256, 0), (128, 0), (128, 1)]
        out_shape, out_specs = map(list, zip(*[act(w, t) for w, t in widths]))
    else:
        widths = [(512, 0), (512, 0), (512, 0), (256, 1), (256, 0), (128, 0), (128, 1), (256, 0),
                  (128, 0), (128, 1)]
        cwidths = [128, MLA_ROPE, 128, 128, 128, 128]
        nb = r // seq_len
        out_shape, out_specs = map(list, zip(*[act(w, t) for w, t in widths]))
        out_shape += [jax.ShapeDtypeStruct((rows // seq_len, DEPTH, seq_len, w), F32) for w in cwidths]
        assert (prev_caches is None) == (l == 0)
        if l == 0:
            out_specs += [pl.BlockSpec((nb, DEPTH, seq_len, w), lambda i: (i, 0, 0, 0)) for w in cwidths]
        else:
            out_specs += [pl.BlockSpec((nb, None, seq_len, w), lambda i: (i, l, 0, 0)) for w in cwidths]
        if prev_caches is not None:
            aliases = {len(args) + k: len(widths) + k for k in range(len(cwidths))}
            in_specs += [pl.BlockSpec(memory_space=pl.ANY)] * len(cwidths)
            args += list(prev_caches)
    return pl.pallas_call(
        functools.partial(_inproj_body, rope, len(aliases)),
        grid=(rows // r,),
        in_specs=in_specs, out_specs=out_specs, out_shape=out_shape,
        input_output_aliases=aliases,
        compiler_params=pltpu.CompilerParams(
            dimension_semantics=("arbitrary",), vmem_limit_bytes=VMEM_LIMIT),
        name="inproj_rope" if rope else "inproj",
    )(*args)


def _softmax_pv_t(s_list, vt_list, first, extra=None):
    m = s_list[0].max(axis=0, keepdims=True)
    for s in s_list[1:]:
        m = jnp.maximum(m, s.max(axis=0, keepdims=True))
    if extra is not None:
        m = jnp.maximum(m, extra)
    out = None
    for s, vt in zip(s_list, vt_list):
        ones = jnp.ones(vt.shape, BF16)
        vt = jnp.concatenate([vt, ones] if first else [ones, vt], axis=0)
        pv = _dot(vt, jnp.exp2(s - m).astype(BF16))
        out = pv if out is None else out + pv
    den = out[HEAD_DIM:HEAD_DIM + 1] if first else out[0:1]
    if extra is not None:
        den = den + jnp.exp2(extra - m)
    return out * (1.0 / den)


def _half_masks(q):
    lo = _lane(q.shape) < HEAD_DIM
    zero = jnp.zeros_like(q)
    return jnp.where(lo, q, zero), jnp.where(lo, zero, q)


def _merge_rows(a, b):
    r = lax.broadcasted_iota(jnp.int32, a.shape, 0)
    return jnp.where(r < HEAD_DIM, a, b)


def _sink_row(sink_ref, l, h0, h1, tq):
    c = lax.broadcasted_iota(jnp.int32, (1, 2 * tq), 1)
    return jnp.where(c < tq, sink_ref[l, h0], sink_ref[l, h1]) * LOG2E


def _run_groups(stage1, stage2, n, ahead=3):
    out = []
    pending = [stage1(g) for g in range(min(ahead, n))]
    for g in range(n):
        if g + ahead < n:
            pending.append(stage1(g + ahead))
        out.append(stage2(g, pending.pop(0)))
    return out


def _mla_scores(h, q_sets, k_sets):
    c = slice(h * LANES, (h + 1) * LANES)
    return [_dot_nt(k[:, c], q[:, c]) for q, k in zip(q_sets, k_sets)]


def _gqa_scores(j, q_sets, k_sets):
    res = []
    for q, k in zip(q_sets, k_sets):
        qa = _half_masks(q[:, 0:LANES])[j]
        qb = _half_masks(q[:, LANES:])[j]
        res.append(_dot_nt(k[...], jnp.concatenate([qa, qb], axis=0)))
    return res


def _store_pair(y_ref, rows, blk, ot):
    y_ref[rows, blk * LANES:(blk + 1) * LANES] = ot.T.astype(BF16)


N_GROUPS = MLA_HEADS + 4


def _store_groups(y_ref, rows, o, tq):
    for p in range(MLA_HEADS // 2):
        _store_pair(y_ref, rows, p, jnp.concatenate([o[2 * p][:HEAD_DIM], o[2 * p + 1][:HEAD_DIM]], axis=0))
    for base, (o0, o1) in ((2, o[4:6]), (4, o[6:8])):
        _store_pair(y_ref, rows, base, _merge_rows(o0[:, :tq], o1[:, :tq]))
        _store_pair(y_ref, rows, base + 1, _merge_rows(o0[:, tq:], o1[:, tq:]))


def _attn_ctx_body(l, nb, t, sink_ref, qm_ref, km_ref, vm_ref, qc_ref, kc_ref, vc_ref, qd_ref, kd_ref, vd_ref,
                   y_ref):
    def stage1(g):
        b, k = divmod(g, N_GROUPS)
        rs = pl.ds(b * t, t)
        if k < MLA_HEADS:
            return _mla_scores(k, [qm_ref.at[rs]], [km_ref.at[rs]])
        q_ref, k_ref = (qc_ref, kc_ref) if k < MLA_HEADS + 2 else (qd_ref, kd_ref)
        return _gqa_scores(k % 2, [q_ref.at[rs]], [k_ref.at[rs]])

    def stage2(g, s):
        b, k = divmod(g, N_GROUPS)
        rs = slice(b * t, (b + 1) * t)
        if k < MLA_HEADS:
            return _softmax_pv_t(s, [vm_ref[k * HEAD_DIM:(k + 1) * HEAD_DIM, rs]], True)
        j = k % 2
        hs = slice(j * HEAD_DIM, (j + 1) * HEAD_DIM)
        if k < MLA_HEADS + 2:
            return _softmax_pv_t(s, [vc_ref[hs, rs]], j == 0)
        return _softmax_pv_t(s, [vd_ref[hs, rs]], j == 0, _sink_row(sink_ref, l, 2 * j, 2 * j + 1, t))

    o = _run_groups(stage1, stage2, N_GROUPS * nb)
    for b in range(nb):
        _store_groups(y_ref, slice(b * t, (b + 1) * t), o[N_GROUPS * b:N_GROUPS * (b + 1)], t)


def _attn_ctx(l, t, sink, qm, km, vm, qc, kc, vc, qd, kd, vd):
    rows = qm.shape[0]
    nb = 4
    r = nb * t
    row = lambda a: (pl.BlockSpec((a.shape[0], r), lambda i: (0, i)) if a.shape[1] == rows
                     else pl.BlockSpec((r, a.shape[1]), lambda i: (i, 0)))
    ins = [qm, km, vm, qc, kc, vc, qd, kd, vd]
    return pl.pallas_call(
        functools.partial(_attn_ctx_body, l, nb, t),
        grid=(rows // r,),
        in_specs=[pl.BlockSpec(memory_space=pltpu.SMEM)] + [row(a) for a in ins],
        out_specs=pl.BlockSpec((r, 6 * LANES), lambda i: (i, 0)),
        out_shape=jax.ShapeDtypeStruct((rows, 6 * LANES), BF16),
        compiler_params=pltpu.CompilerParams(
            dimension_semantics=("arbitrary",), vmem_limit_bytes=VMEM_LIMIT),
        name="attn_ctx",
    )(sink, *ins)


def _window_block(i, tq, t, nwb):
    return jnp.clip(i * (tq // LANES) - WINDOW // LANES, 0, t // LANES - nwb)


def _attn_lat_body(l, tq, t, nwb, sink_ref, qmr_ref, qm_ref, km_ref, vm_ref, kx_ref, vx_ref,
                   qcr_ref, qc_ref, kc_ref, vc_ref, kcx_ref, vcx_ref,
                   qdr_ref, qd_ref, kdx_ref, vdx_ref, *rest):
    kd_refs, vd_refs = rest[:nwb], rest[nwb:2 * nwb]
    y_ref, = rest[2 * nwb:]
    i = pl.program_id(1)
    wk = nwb * LANES
    kpos = _window_block(i, tq, t, nwb) * LANES + lax.broadcasted_iota(jnp.int32, (wk, 2 * tq), 0)
    qpos = i * tq + lax.broadcasted_iota(jnp.int32, (wk, 2 * tq), 1) % tq
    valid = jnp.abs(kpos - qpos) <= WINDOW

    def stage1(g):
        if g < MLA_HEADS:
            return _mla_scores(g, [qmr_ref, qm_ref], [km_ref, kx_ref])
        if g < MLA_HEADS + 2:
            return _gqa_scores(g % 2, [qcr_ref, qc_ref], [kc_ref, kcx_ref])
        kw = jnp.concatenate([r[...] for r in kd_refs], axis=0)
        s1, s2 = _gqa_scores(g % 2, [qdr_ref, qd_ref], [kw, kdx_ref])
        return [jnp.where(valid, s1, NEG_INF), s2]

    def stage2(g, s):
        if g < MLA_HEADS:
            hs = slice(g * HEAD_DIM, (g + 1) * HEAD_DIM)
            return _softmax_pv_t(s, [vm_ref[hs, :], vx_ref[hs, :]], True)
        j = g % 2
        hs = slice(j * HEAD_DIM, (j + 1) * HEAD_DIM)
        if g < MLA_HEADS + 2:
            return _softmax_pv_t(s, [vc_ref[hs, :], vcx_ref[hs, :]], j == 0)
        vw = jnp.concatenate([r[hs, :] for r in vd_refs], axis=1)
        return _softmax_pv_t(s, [vw, vdx_ref[hs, :]], j == 0, _sink_row(sink_ref, l, 2 * j, 2 * j + 1, tq))

    _store_groups(y_ref, slice(None), _run_groups(stage1, stage2, N_GROUPS), tq)


def _attn_lat(l, nbatch, t, sink, qmr, qm, km, vmt, kx, vxt, qcr, qc, kc, vct, kcx, vcxt, qdr, qd, kd, vdt,
              kdx, vdxt):
    tq = 256
    nq = t // tq
    nwb = (tq + 2 * WINDOW) // LANES
    nkb = t // LANES
    rows = qm.shape[0]
    qs = lambda a: pl.BlockSpec((tq, a.shape[1]), lambda b, i: (b * nq + i, 0))
    ks = lambda a: pl.BlockSpec((t, a.shape[1]), lambda b, i: (b, 0))
    kts = lambda a: pl.BlockSpec((a.shape[0], t), lambda b, i: (0, b))
    xs = lambda a: pl.BlockSpec((None, None) + a.shape[2:], lambda b, i: (l, b, 0, 0))
    cs = lambda a: pl.BlockSpec((None, None) + a.shape[2:], lambda b, i: (b, l, 0, 0))
    wblk = lambda b, i, k: b * nkb + _window_block(i, tq, t, nwb) + k
    kd_specs = [pl.BlockSpec((LANES, LANES), lambda b, i, k=k: (wblk(b, i, k), 0)) for k in range(nwb)]
    vd_specs = [pl.BlockSpec((LANES, LANES), lambda b, i, k=k: (0, wblk(b, i, k))) for k in range(nwb)]
    in_specs = [pl.BlockSpec(memory_space=pltpu.SMEM),
                qs(qmr), qs(qm), ks(km), kts(vmt), xs(kx), xs(vxt),
                qs(qcr), qs(qc), ks(kc), kts(vct), cs(kcx), cs(vcxt),
                qs(qdr), qs(qd), cs(kdx), cs(vdxt)] + kd_specs + vd_specs
    return pl.pallas_call(
        functools.partial(_attn_lat_body, l, tq, t, nwb),
        grid=(nbatch, nq),
        in_specs=in_specs,
        out_specs=pl.BlockSpec((tq, 6 * LANES), lambda b, i: (b * nq + i, 0)),
        out_shape=jax.ShapeDtypeStruct((rows, 6 * LANES), BF16),
        compiler_params=pltpu.CompilerParams(
            dimension_semantics=("arbitrary", "arbitrary"), vmem_limit_bytes=VMEM_LIMIT),
        name="attn_lat",
    )(sink, qmr, qm, km, vmt, kx, vxt, qcr, qc, kc, vct, kcx, vcxt, qdr, qd, kdx, vdxt,
      *([kd] * nwb), *([vdt] * nwb))


def _post_body(seq_len, final, mod_ref, n2_ref, cwa_ref, wo_ref, cwf_ref, wup_ref, wdn_ref, fn_ref, *rest):
    ins, (o_ref, hbuf, gbuf, pbuf, abuf) = rest[:9], rest[9:]
    cat = lambda k: jnp.concatenate([ins[3 * k][...], ins[3 * k + 1][...], ins[3 * k + 2][...]], axis=0)
    r = o_ref.shape[0]
    halo = BF16_ROWS
    n = r + 2 * halo
    grp = r // 8
    nchunk = D_MODEL // LANES
    i = pl.program_id(0)

    zg = cat(1).astype(F32)
    zz, gb = zg[:, :GROUP_WIDTH], zg[:, GROUP_WIDTH:]
    pos = (i * r - halo + lax.broadcasted_iota(jnp.int32, (n, 1), 0)) % seq_len
    prev = jnp.where(pos == 0, 0.0, pltpu.roll(zz, 1, 0))
    nxt = jnp.where(pos == seq_len - 1, 0.0, pltpu.roll(zz, n - 1, 0))
    cwa = cwa_ref[...]
    ya = gb * (prev * cwa[0:1] + zz * cwa[1:2] + nxt * cwa[2:3])
    ycat = jnp.concatenate([ya.astype(BF16), cat(2)], axis=1)
    x1 = cat(0) + mod_ref[2:3, :] * _dot(ycat, wo_ref[...])
    h2 = _rms(x1, n2_ref[...], D_MODEL) * (1.0 + mod_ref[4:5, :]) + mod_ref[3:4, :]
    x1c = x1[halo:halo + r]

    for c in range(nchunk):
        for s in range(8):
            pbuf[c, pl.ds(s, grp, stride=8), :] = h2[halo + s * grp:halo + (s + 1) * grp, c * LANES:(c + 1) * LANES]
    hrow = lax.broadcasted_iota(jnp.int32, (halo, 1), 0)
    edge = jnp.where(hrow == 0, h2[halo - 1:halo], jnp.where(hrow == 1, h2[halo + r:halo + r + 1], 0.0))
    hbuf[0:halo, :] = edge.astype(BF16)
    hbuf[halo:, :] = jnp.concatenate([pbuf[c] for c in range(nchunk)], axis=1).astype(BF16)

    sub = lax.broadcasted_iota(jnp.int32, (8, 1), 0)
    seq_first = (i * r + sub * grp) % seq_len == 0
    seq_last = (i * r + sub * grp + grp - 1) % seq_len == seq_len - 1

    def up(j):
        ga = slice(j * FF_TILE, (j + 1) * FF_TILE)
        va = slice(D_FF + j * FF_TILE, D_FF + (j + 1) * FF_TILE)
        return _dot(hbuf[...], wup_ref[:, ga]), _dot(hbuf[...], wup_ref[:, va])

    def conv(u, cw):
        head = jnp.where(sub == 0, u[0:8], pltpu.roll(u[halo + r - 8:halo + r], 1, 0))
        tail = jnp.where(sub == 7, pltpu.roll(u[0:8], 6, 0), pltpu.roll(u[halo:halo + 8], 7, 0))
        prev = jnp.concatenate([jnp.where(seq_first, 0.0, head), u[halo:halo + r - 8]], axis=0)
        nxt = jnp.concatenate([u[halo + 8:halo + r], jnp.where(seq_last, 0.0, tail)], axis=0)
        return prev * cw[0:1] + u[halo:halo + r] * cw[1:2] + nxt * cw[2:3]

    acc = None
    nxt_u = up(0)
    for j in range(N_FF_TILES):
        ua, ub = nxt_u
        if j + 1 < N_FF_TILES:
            nxt_u = up(j + 1)
        ga = slice(j * FF_TILE, (j + 1) * FF_TILE)
        va = slice(D_FF + j * FF_TILE, D_FF + (j + 1) * FF_TILE)
        ua = conv(ua, cwf_ref[:, ga])
        ub = conv(ub, cwf_ref[:, va])
        gbuf[:, ga] = (ua * jax.nn.sigmoid(ua) * ub).astype(BF16)
        if j % DOWN_CHUNK == DOWN_CHUNK - 1 or j == N_FF_TILES - 1:
            lo = (j // DOWN_CHUNK) * DOWN_CHUNK * FF_TILE
            hi = (j + 1) * FF_TILE
            d = _dot(gbuf[:, lo:hi], wdn_ref[lo:hi, :])
            acc = d if acc is None else acc + d

    for c in range(nchunk):
        abuf[c] = acc[:, c * LANES:(c + 1) * LANES]
    g2 = mod_ref[5:6, :]
    for c in range(nchunk):
        cs = slice(c * LANES, (c + 1) * LANES)
        for s in range(8):
            rs = slice(s * grp, (s + 1) * grp)
            o_ref[rs, cs] = x1c[rs, cs] + g2[:, cs] * abuf[c, pl.ds(s, grp, stride=8), :]
    if final:
        o_ref[...] = _rms(o_ref[...], fn_ref[...], D_MODEL)


def _halo_specs(r, w, rows):
    nblk = rows // BF16_ROWS
    per = r // BF16_ROWS
    prev = pl.BlockSpec((BF16_ROWS, w), lambda i: (jnp.maximum(i * per - 1, 0), 0))
    cur = pl.BlockSpec((r, w), lambda i: (i, 0))
    nxt = pl.BlockSpec((BF16_ROWS, w), lambda i: (jnp.minimum((i + 1) * per, nblk - 1), 0))
    return prev, cur, nxt


def _cond_spec(l, seq_len, cond_base):
    r = ROW_TILE
    tiles_per_seq = max(seq_len // r, 1)
    step = 1 if seq_len >= r else 0
    return pl.BlockSpec((None, None, 6, D_MODEL),
                        lambda i: (l, cond_base + step * (i // tiles_per_seq), 0, 0))


def _post(x, mod, lw, l, seq_len, cond_base, zg, y, final_norm, final):
    rows = x.shape[0]
    r = ROW_TILE
    full = lambda a: pl.BlockSpec((None,) + a.shape[1:], lambda i: (l,) + (0,) * (a.ndim - 1),
                                  pipeline_mode=pl.Buffered(1))
    acts = [x, zg, y]
    act_specs = [s for a in acts for s in _halo_specs(r, a.shape[1], rows)]
    act_args = [a for a in acts for _ in range(3)]
    return pl.pallas_call(
        functools.partial(_post_body, seq_len, final),
        grid=(rows // r,),
        in_specs=[_cond_spec(l, seq_len, cond_base), full(lw['norm2']), full(lw['conv_a']), full(lw['w_out']),
                  full(lw['conv_ff']), full(lw['w_up']), full(lw['w_down']),
                  pl.BlockSpec((1, D_MODEL), lambda i: (0, 0))] + act_specs,
        out_specs=pl.BlockSpec((r, D_MODEL), lambda i: (i, 0)),
        out_shape=jax.ShapeDtypeStruct((rows, D_MODEL), F32),
        scratch_shapes=[pltpu.VMEM((r + BF16_ROWS, D_MODEL), BF16), pltpu.VMEM((r, D_FF), BF16),
                        pltpu.VMEM((D_MODEL // LANES, r, LANES), F32),
                        pltpu.VMEM((D_MODEL // LANES, r, LANES), F32)],
        compiler_params=pltpu.CompilerParams(
            dimension_semantics=("arbitrary",), vmem_limit_bytes=VMEM_LIMIT),
        name="post",
    )(mod, lw['norm2'], lw['conv_a'], lw['w_out'], lw['conv_ff'], lw['w_up'], lw['w_down'], final_norm,
      *act_args)


def _pad_cols(a, n):
    return jnp.pad(a, [(0, 0)] * (a.ndim - 1) + [(0, n - a.shape[-1])])


def _perm_heads(a, axis):
    h = [lax.slice_in_dim(a, k * HEAD_DIM, (k + 1) * HEAD_DIM, axis=axis) for k in range(4)]
    return jnp.concatenate([h[0], h[2], h[1], h[3]], axis=axis)


def _prep_weights(w_in, mla_wq_b, mla_wkv_b, w_out, w_up, conv_ff, w_down, norm1, norm2, conv_a,
                  mla_q_norm, mla_kv_norm, gqa_q_norm, gqa_k_norm):
    L = DEPTH
    a = w_in[..., 0:768]
    cq = _pad_cols(w_in[..., 768:960], 256)
    ckv = w_in[..., 960:1088]
    kpe = _pad_cols(w_in[..., 1088:1120], LANES)
    qc = _perm_heads(w_in[..., 1120:1376], 2)
    kvc = w_in[..., 1376:1632]
    qd = _perm_heads(w_in[..., 1632:1888], 2)
    kvd = w_in[..., 1888:2144]
    w_in_p = jnp.concatenate([a, cq, ckv, kpe, qc, kvc, qd, kvd], axis=-1).astype(BF16)

    wq = mla_wq_b.reshape(L, MLA_Q_LORA, MLA_HEADS, MLA_NOPE + MLA_ROPE)
    wq = jnp.concatenate([wq[..., MLA_NOPE:], wq[..., :MLA_NOPE],
                          jnp.zeros((L, MLA_Q_LORA, MLA_HEADS, LANES - MLA_NOPE - MLA_ROPE), F32)], axis=-1)
    wq = jnp.pad(wq.reshape(L, MLA_Q_LORA, MLA_HEADS * LANES), ((0, 0), (0, 256 - MLA_Q_LORA), (0, 0)))

    wkv = mla_wkv_b.reshape(L, MLA_KV_LORA, MLA_HEADS, MLA_NOPE + MLA_V)
    zk = jnp.zeros((L, MLA_KV_LORA, MLA_HEADS, MLA_ROPE), F32)
    wk = jnp.concatenate([zk, wkv[..., :MLA_NOPE], zk], axis=-1).reshape(L, MLA_KV_LORA, MLA_HEADS * LANES)
    wv = wkv[..., MLA_NOPE:].reshape(L, MLA_KV_LORA, MLA_HEADS * MLA_V)
    wkv_p = jnp.concatenate([wk, wv], axis=-1)

    wo = jnp.concatenate([w_out[:, 0:512], _perm_heads(w_out[:, 512:768], 1),
                          _perm_heads(w_out[:, 768:1024], 1)], axis=1)

    hm = jnp.asarray(np.kron(np.eye(2 * LANES // HEAD_DIM), np.ones((HEAD_DIM, HEAD_DIM))), BF16)
    return {
        'w_in': w_in_p, 'wq': wq.astype(BF16), 'wkv': wkv_p.astype(BF16), 'w_out': wo.astype(BF16),
        'w_up': w_up.astype(BF16), 'conv_ff': conv_ff, 'w_down': w_down.astype(BF16),
        'norm1': norm1.reshape(L, 1, D_MODEL), 'norm2': norm2.reshape(L, 1, D_MODEL), 'conv_a': conv_a,
        'gq': _pad_cols(mla_q_norm, 256).reshape(L, 1, 256), 'gkv': mla_kv_norm.reshape(L, 1, MLA_KV_LORA),
        'gqc': jnp.tile(gqa_q_norm, (1, 4)).reshape(L, 1, 256), 'gkc': jnp.tile(gqa_k_norm, (1, 2)).reshape(L, 1, 128),
        'hm': hm,
    }


def _rope_tables(t):
    rows = t // GRID_W
    row = np.repeat(np.arange(rows, dtype=np.float64), GRID_W)
    col = np.tile(np.arange(GRID_W, dtype=np.float64), rows)

    def tabs(dim):
        half = dim // 2
        inv = np.power(ROPE_THETA, -np.arange(0, half, 2, dtype=np.float64) / half)
        ar = row[:, None] * inv
        ac = col[:, None] * inv
        c = np.concatenate([np.cos(ar), np.cos(ar), np.cos(ac), np.cos(ac)], axis=1)
        s = np.concatenate([-np.sin(ar), np.sin(ar), -np.sin(ac), np.sin(ac)], axis=1)
        return c, s

    c64, s64 = tabs(HEAD_DIM)
    c32, s32 = tabs(MLA_ROPE)
    c64 = np.tile(c64, (1, 2))
    s64 = np.tile(s64, (1, 2))
    c32 = np.concatenate([c32, np.ones((t, LANES - MLA_ROPE))], axis=1)
    s32 = np.concatenate([s32, np.zeros((t, LANES - MLA_ROPE))], axis=1)
    return tuple(jnp.asarray(a.astype(np.float32)) for a in (c64, s64, c32, s32))


def kernel(x_prompt, x_sample, cache_mla_ckv, cache_mla_kpe, cache_gqa_k, cache_gqa_v, cache_swa_k, cache_swa_v,
           c, c_ctx, w_ada, b_ada, norm1, w_in, conv_a, mla_q_norm, mla_wq_b, mla_kv_norm, mla_wkv_b,
           gqa_q_norm, gqa_k_norm, swa_sink, w_out, norm2, w_up, conv_ff, w_down, final_norm):
    B, T, _ = x_prompt.shape
    DB, DT, _ = x_sample.shape
    past = cache_mla_ckv.shape[2]

    lw = _prep_weights(w_in, mla_wq_b, mla_wkv_b, w_out, w_up, conv_ff, w_down, norm1, norm2, conv_a,
                       mla_q_norm, mla_kv_norm, gqa_q_norm, gqa_k_norm)
    rope_tabs = _rope_tables(DT)
    fnorm = final_norm.reshape(1, D_MODEL)

    cond_t = jnp.concatenate([c_ctx[:, None], c.T, jnp.zeros((D_MODEL, 8 - 1 - DB), F32)], axis=1)
    mod = _ada(cond_t, 1 + DB, w_ada, b_ada).reshape(DEPTH, 8, 6, D_MODEL)

    kx, vx = _ctx_mla(cache_mla_ckv, _pad_cols(cache_mla_kpe, LANES), lw['wkv'])
    flat = lambda a: a.reshape(DB, DEPTH, past, 2 * HEAD_DIM).astype(BF16)
    kcx, kdx = flat(cache_gqa_k), flat(cache_swa_k)
    vcx, vdx = jnp.swapaxes(flat(cache_gqa_v), 2, 3), jnp.swapaxes(flat(cache_swa_v), 2, 3)

    xp = x_prompt.reshape(B * T, D_MODEL)
    xs = x_sample.reshape(DB * DT, D_MODEL)
    caches = None
    for l in range(DEPTH):
        final = l == DEPTH - 1
        (zg, qm, km, vm, qc, kc, vc, qd, kd, vd, *caches) = _inproj(xp, mod, lw, l, T, 0, None, caches)
        y = _attn_ctx(l, T, swa_sink, qm, km, vm, qc, kc, vc, qd, kd, vd)
        xp = _post(xp, mod, lw, l, T, 0, zg, y, fnorm, final)
        (zg, qmr, qm, km, vm, qcr, qc, kc, vc, qdr, qd, kd, vd) = _inproj(xs, mod, lw, l, DT, 1, rope_tabs)
        y = _attn_lat(l, DB, DT, swa_sink, qmr, qm, km, vm, kx, vx, qcr, qc, kc, vc, kcx, vcx,
                      qdr, qd, kd, vd, kdx, vdx)
        xs = _post(xs, mod, lw, l, DT, 1, zg, y, fnorm, final)

    heads = lambda a: a.reshape(B, DEPTH, T, 2, HEAD_DIM)
    return (xp.reshape(B, T, D_MODEL), xs.reshape(DB, DT, D_MODEL), caches[0], caches[1],
            heads(caches[2]), heads(caches[3]), heads(caches[4]), heads(caches[5]))
```

```python
import functools
import math

import jax
import jax.numpy as jnp
import numpy as np
from jax import lax
from jax.experimental import pallas as pl
from jax.experimental.pallas import tpu as pltpu

F32 = jnp.float32
BF16 = jnp.bfloat16

D_MODEL = 1024
DEPTH = 2
GRID_W = 64
HEAD_DIM = 64
GROUP_WIDTH = D_MODEL // 4
MLA_HEADS = 4
MLA_NOPE = 64
MLA_ROPE = 32
MLA_V = 64
MLA_Q_LORA = 192
MLA_KV_LORA = 128
WINDOW = 128
D_FF = 2816
ROPE_THETA = 10000.0
EPS = 1e-6
NEG_INF = -1e30
LOG2E = math.log2(math.e)
ATTN_SCALE = HEAD_DIM ** -0.5 * LOG2E
MLA_SCALE = (MLA_NOPE + MLA_ROPE) ** -0.5 * LOG2E

LANES = 128
BF16_ROWS = 16
ROW_TILE = 512
IN_ROW_TILE = 1024
IN_SUB_TILE = 256
FF_TILE = 256
N_FF_TILES = D_FF // FF_TILE
DOWN_CHUNK = 4
IN_COLS_PACKED = 2304
VMEM_LIMIT = 56 * 1024 * 1024

_NT = (((1,), (1,)), ((), ()))


def _dot(a, b):
    return jnp.dot(a, b, preferred_element_type=F32)


def _dot_nt(a, b):
    return lax.dot_general(a, b, _NT, preferred_element_type=F32)


def _rms(x, g, n):
    ms = jnp.sum(x * x, axis=-1, keepdims=True) * (1.0 / n)
    return x * lax.rsqrt(ms + EPS) * g


def _lane(shape):
    return lax.broadcasted_iota(jnp.int32, shape, len(shape) - 1)


def _rope(x, c, s, half):
    w = x.shape[-1]
    lo = (_lane(x.shape) % (2 * half)) < half
    sw = jnp.where(lo, pltpu.roll(x, w - half, 1), pltpu.roll(x, half, 1))
    return x * c + sw * s


def _shift_rows(zz, halo, rows):
    n = zz.shape[0]
    prev = pltpu.roll(zz, 1, 0)[halo:halo + rows]
    nxt = pltpu.roll(zz, n - 1, 0)[halo:halo + rows]
    return prev, nxt


def _seq_pos(tile, rows, seq_len):
    r = lax.broadcasted_iota(jnp.int32, (rows, 1), 0)
    return (tile * rows + r) % seq_len


def _ada_body(ncond, c_ref, w_ref, b_ref, o_ref):
    c = c_ref[...]
    s = c * jax.nn.sigmoid(c)
    tn = w_ref.shape[1]
    w = w_ref[...].reshape(D_MODEL // 8, 8, tn)
    rows = []
    for r in range(ncond):
        part = jnp.sum(w * s[:, r:r + 1].reshape(D_MODEL // 8, 8, 1), axis=0)
        rows.append(jnp.sum(part, axis=0, keepdims=True))
    rows.append(jnp.zeros((8 - ncond, tn), F32))
    o_ref[...] = jnp.concatenate(rows, axis=0) + b_ref[...]


def _ada(cond_t, ncond, w_ada, b_ada):
    tn = 1536
    n = 6 * D_MODEL
    return pl.pallas_call(
        functools.partial(_ada_body, ncond),
        grid=(DEPTH, n // tn),
        in_specs=[pl.BlockSpec((D_MODEL, 8), lambda l, j: (0, 0)),
                  pl.BlockSpec((None, D_MODEL, tn), lambda l, j: (l, 0, j)),
                  pl.BlockSpec((None, 1, tn), lambda l, j: (l, 0, j))],
        out_specs=pl.BlockSpec((None, 8, tn), lambda l, j: (l, 0, j)),
        out_shape=jax.ShapeDtypeStruct((DEPTH, 8, n), F32),
        compiler_params=pltpu.CompilerParams(
            dimension_semantics=("arbitrary", "arbitrary"), vmem_limit_bytes=VMEM_LIMIT),
        name="ada",
    )(cond_t, w_ada, b_ada.reshape(DEPTH, 1, n))


def _ctx_body(ckv_ref, kpe_ref, w_ref, kx_ref, vx_ref):
    kv = _dot(ckv_ref[...].astype(BF16), w_ref[...])
    kpe = kpe_ref[...]
    kx_ref[...] = (kv[:, :4 * LANES] + jnp.concatenate([kpe] * MLA_HEADS, axis=1)).astype(BF16)
    vx_ref[...] = kv[:, 4 * LANES:].T.astype(BF16)


def _ctx_mla(cache_ckv, cache_kpe_pad, wkv_p):
    b, _, s, _ = cache_ckv.shape
    return pl.pallas_call(
        _ctx_body,
        grid=(DEPTH, b),
        in_specs=[pl.BlockSpec((None, None, s, MLA_KV_LORA), lambda l, i: (i, l, 0, 0)),
                  pl.BlockSpec((None, None, s, LANES), lambda l, i: (i, l, 0, 0)),
                  pl.BlockSpec((None, MLA_KV_LORA, 6 * LANES), lambda l, i: (l, 0, 0))],
        out_specs=[pl.BlockSpec((None, None, s, 4 * LANES), lambda l, i: (l, i, 0, 0)),
                   pl.BlockSpec((None, None, 2 * LANES, s), lambda l, i: (l, i, 0, 0))],
        out_shape=[jax.ShapeDtypeStruct((DEPTH, b, s, 4 * LANES), BF16),
                   jax.ShapeDtypeStruct((DEPTH, b, 2 * LANES, s), BF16)],
        compiler_params=pltpu.CompilerParams(
            dimension_semantics=("arbitrary", "arbitrary"), vmem_limit_bytes=VMEM_LIMIT),
        name="ctx_mla",
    )(cache_ckv, cache_kpe_pad, wkv_p)


def _inproj_body(rope, n_alias, x_ref, mod_ref, n1_ref, w_ref, wq_ref, wkv_ref, gq_ref, gkv_ref, gqc_ref, gkc_ref,
                 hm_ref, *rest):
    if rope:
        c64_ref, s64_ref, c32_ref, s32_ref = rest[:4]
        outs = rest[4:]
    else:
        outs = rest[n_alias:]
    sh1 = mod_ref[0:1, :]
    sc1 = mod_ref[1:2, :]
    hm = hm_ref[...]
    sub = IN_SUB_TILE
    nsub = x_ref.shape[0] // sub

    def project(s):
        x = x_ref[s * sub:(s + 1) * sub, :]
        h = _rms(x, n1_ref[...], D_MODEL) * (1.0 + sc1) + sh1
        return _dot(h.astype(BF16), w_ref[...])

    def head_rms(v, g):
        v2 = v * v
        hi = v2.astype(BF16)
        lo = (v2 - hi.astype(F32)).astype(BF16)
        w = v.shape[-1]
        ss = _dot(hi, hm[:w, :w]) + _dot(lo, hm[:w, :w])
        return v * lax.rsqrt(ss * (1.0 / HEAD_DIM) + EPS) * g

    def finish(s, acc):
        rs = slice(s * sub, (s + 1) * sub)
        xa, gb, gc = acc[:, 0:256], acc[:, 256:512], acc[:, 512:768]
        z = gc * xa
        cq = acc[:, 768:1024]
        cqn = _rms(cq, gq_ref[...], MLA_Q_LORA)
        qm = _dot(cqn.astype(BF16), wq_ref[...]) * MLA_SCALE
        ckvn = _rms(acc[:, 1024:1152], gkv_ref[...], MLA_KV_LORA)
        kv = _dot(ckvn.astype(BF16), wkv_ref[...])
        kpe = acc[:, 1152:1280]
        vm = kv[:, 4 * LANES:]
        qc = head_rms(acc[:, 1280:1536], gqc_ref[...]) * ATTN_SCALE
        kc = head_rms(acc[:, 1536:1664], gkc_ref[...])
        vc = acc[:, 1664:1792]
        qd = acc[:, 1792:2048] * ATTN_SCALE
        kd = acc[:, 2048:2176]
        vd = acc[:, 2176:2304]

        if not rope:
            (zg_o, qm_o, km_o, vm_o, qc_o, kc_o, vc_o, qd_o, kd_o, vd_o,
             ckv_c, kpe_c, kc_c, vc_c, kd_c, vd_c) = outs
            km = kv[:, :4 * LANES] + jnp.concatenate([kpe] * MLA_HEADS, axis=1)
            zg_o[rs, 0:256] = z.astype(BF16)
            zg_o[rs, 256:512] = gb.astype(BF16)
            qm_o[rs, :] = qm.astype(BF16)
            km_o[rs, :] = km.astype(BF16)
            vm_o[:, rs] = vm.T.astype(BF16)
            qc_o[rs, :] = qc.astype(BF16)
            kc_o[rs, :] = kc.astype(BF16)
            vc_o[:, rs] = vc.T.astype(BF16)
            qd_o[rs, :] = qd.astype(BF16)
            kd_o[rs, :] = kd.astype(BF16)
            vd_o[:, rs] = vd.T.astype(BF16)
            for o, v in ((ckv_c, ckvn), (kpe_c, kpe[:, :MLA_ROPE]), (kc_c, kc), (vc_c, vc), (kd_c, kd),
                         (vd_c, vd)):
                t = o.shape[-2]
                bs = slice(s * sub // t, (s + 1) * sub // t)
                v = v.reshape(sub // t, t, v.shape[-1])
                if n_alias:
                    o[bs] = v
                else:
                    o[bs, 0] = v
                    o[bs, 1:] = jnp.zeros((sub // t, o.shape[1] - 1) + v.shape[1:], F32)
        else:
            (zg_o, qmr_o, qm_o, km_o, vm_o, qcr_o, qc_o, kc_o, vc_o, qdr_o, qd_o, kd_o, vd_o) = outs
            c64, s64, c32, s32 = c64_ref[rs, :], s64_ref[rs, :], c32_ref[rs, :], s32_ref[rs, :]
            rope64 = lambda v: jnp.concatenate(
                [_rope(v[:, i:i + LANES], c64, s64, 16) for i in range(0, v.shape[-1], LANES)], axis=1)
            rope32 = lambda v: jnp.concatenate(
                [_rope(v[:, i:i + LANES], c32, s32, 8) for i in range(0, v.shape[-1], LANES)], axis=1)
            km = kv[:, :4 * LANES] + jnp.concatenate([_rope(kpe, c32, s32, 8)] * MLA_HEADS, axis=1)
            zg_o[rs, 0:256] = z.astype(BF16)
            zg_o[rs, 256:512] = gb.astype(BF16)
            qmr_o[rs, :] = rope32(qm).astype(BF16)
            qm_o[rs, :] = qm.astype(BF16)
            km_o[rs, :] = km.astype(BF16)
            vm_o[:, rs] = vm.T.astype(BF16)
            qcr_o[rs, :] = rope64(qc).astype(BF16)
            qc_o[rs, :] = qc.astype(BF16)
            kc_o[rs, :] = rope64(kc).astype(BF16)
            vc_o[:, rs] = vc.T.astype(BF16)
            qdr_o[rs, :] = rope64(qd).astype(BF16)
            qd_o[rs, :] = qd.astype(BF16)
            kd_o[rs, :] = rope64(kd).astype(BF16)
            vd_o[:, rs] = vd.T.astype(BF16)

    nxt = project(0)
    for s in range(nsub):
        acc = nxt
        if s + 1 < nsub:
            nxt = project(s + 1)
        finish(s, acc)


def _inproj(x, mod, lw, l, seq_len, cond_base, rope_tabs, prev_caches=None):
    rows = x.shape[0]
    r = IN_ROW_TILE
    rope = rope_tabs is not None
    aliases = {}
    tiles_per_seq = max(seq_len // r, 1)
    step = 1 if seq_len >= r else 0

    def cond_map(i):
        return (l, cond_base + step * (i // tiles_per_seq), 0, 0)

    row = lambda w: pl.BlockSpec((r, w), lambda i: (i, 0))
    full = lambda a: pl.BlockSpec((None,) + a.shape[1:], lambda i: (l,) + (0,) * (a.ndim - 1))
    in_specs = [row(D_MODEL),
                pl.BlockSpec((None, None, 6, D_MODEL), cond_map),
                full(lw['norm1']), full(lw['w_in']), full(lw['wq']), full(lw['wkv']),
                full(lw['gq']), full(lw['gkv']), full(lw['gqc']), full(lw['gkc']),
                pl.BlockSpec((2 * LANES, 2 * LANES), lambda i: (0, 0))]
    args = [x, mod, lw['norm1'], lw['w_in'], lw['wq'], lw['wkv'], lw['gq'], lw['gkv'], lw['gqc'], lw['gkc'],
            lw['hm']]
    def act(w, transposed):
        if transposed:
            return jax.ShapeDtypeStruct((w, rows), BF16), pl.BlockSpec((w, r), lambda i: (0, i))
        return jax.ShapeDtypeStruct((rows, w), BF16), row(w)

    if rope:
        tab = pl.BlockSpec((r, LANES), lambda i: (i % tiles_per_seq, 0))
        in_specs += [tab] * 4
        args += list(rope_tabs)
        widths = [(512, 0), (512, 0), (512, 0), (512, 0), (256, 1), (256, 0), (256, 0), (128, 0),
                  (128, 1), (256, 0), (256, 0), (128, 0), (128, 1)]
        out_shape, out_specs = map(list, zip(*[act(w, t) for w, t in widths]))
    else:
        widths = [(512, 0), (512, 0), (512, 0), (256, 1), (256, 0), (128, 0), (128, 1), (256, 0),
                  (128, 0), (128, 1)]
        cwidths = [128, MLA_ROPE, 128, 128, 128, 128]
        nb = r // seq_len
        out_shape, out_specs = map(list, zip(*[act(w, t) for w, t in widths]))
        out_shape += [jax.ShapeDtypeStruct((rows // seq_len, DEPTH, seq_len, w), F32) for w in cwidths]
        assert (prev_caches is None) == (l == 0)
        if l == 0:
            out_specs += [pl.BlockSpec((nb, DEPTH, seq_len, w), lambda i: (i, 0, 0, 0)) for w in cwidths]
        else:
            out_specs += [pl.BlockSpec((nb, None, seq_len, w), lambda i: (i, l, 0, 0)) for w in cwidths]
        if prev_caches is not None:
            aliases = {len(args) + k: len(widths) + k for k in range(len(cwidths))}
            in_specs += [pl.BlockSpec(memory_space=pl.ANY)] * len(cwidths)
            args += list(prev_caches)
    return pl.pallas_call(
        functools.partial(_inproj_body, rope, len(aliases)),
        grid=(rows // r,),
        in_specs=in_specs, out_specs=out_specs, out_shape=out_shape,
        input_output_aliases=aliases,
        compiler_params=pltpu.CompilerParams(
            dimension_semantics=("arbitrary",), vmem_limit_bytes=VMEM_LIMIT),
        name="inproj_rope" if rope else "inproj",
    )(*args)


DEN_FLOOR = 2.0 ** -80


def _key_mags(k):
    return jnp.max(jnp.abs(k), axis=0, keepdims=True).astype(F32)


def _score_bound(kmag, qmag):
    col = jnp.sum(qmag.astype(F32) * kmag, axis=1, keepdims=True)
    return jnp.broadcast_to(col, qmag.shape).T[0:1]


def _softmax_pv_t(s_list, vt_list, first, extra=None, bound=None):
    if bound is None:
        m = s_list[0].max(axis=0, keepdims=True)
        for s in s_list[1:]:
            m = jnp.maximum(m, s.max(axis=0, keepdims=True))
    else:
        m = bound
    if extra is not None:
        m = jnp.maximum(m, extra)
    out = None
    for s, vt in zip(s_list, vt_list):
        ones = jnp.ones(vt.shape, BF16)
        vt = jnp.concatenate([vt, ones] if first else [ones, vt], axis=0)
        pv = _dot(vt, jnp.exp2(s - m).astype(BF16))
        out = pv if out is None else out + pv
    den = out[HEAD_DIM:HEAD_DIM + 1] if first else out[0:1]
    if extra is not None:
        den = den + jnp.exp2(extra - m)
    return out * (1.0 / den), den


def _underflowed(dens):
    width = max(d.shape[1] for d in dens)
    low = None
    for d in dens:
        d = jnp.concatenate([d] * (width // d.shape[1]), axis=1)
        low = d if low is None else jnp.minimum(low, d)
    return jnp.logical_not(jnp.min(low) >= DEN_FLOOR)


def _half_masks(q):
    lo = _lane(q.shape) < HEAD_DIM
    zero = jnp.zeros_like(q)
    return jnp.where(lo, q, zero), jnp.where(lo, zero, q)


def _merge_rows(a, b):
    r = lax.broadcasted_iota(jnp.int32, a.shape, 0)
    return jnp.where(r < HEAD_DIM, a, b)


def _sink_row(sink_ref, l, h0, h1, tq):
    c = lax.broadcasted_iota(jnp.int32, (1, 2 * tq), 1)
    return jnp.where(c < tq, sink_ref[l, h0], sink_ref[l, h1]) * LOG2E


SCORE_AHEAD = 3


def _run_groups(stage1, stage2, n, ahead=SCORE_AHEAD):
    out = []
    pending = [stage1(g) for g in range(min(ahead, n))]
    for g in range(n):
        if g + ahead < n:
            pending.append(stage1(g + ahead))
        out.append(stage2(g, pending.pop(0)))
    return out


def _scores(pairs, kmags):
    s = [_dot_nt(k, q) for k, q in pairs]
    b = None
    if kmags is not None:
        kmag, qmag = None, None
        for (_, q), km in zip(pairs, kmags):
            kmag = km if kmag is None else jnp.maximum(kmag, km)
            qmag = jnp.abs(q) if qmag is None else jnp.maximum(qmag, jnp.abs(q))
        b = _score_bound(kmag, qmag)
    return s, b


def _mla_scores(h, q_sets, k_sets, kmags):
    c = slice(h * LANES, (h + 1) * LANES)
    return _scores([(k[:, c], q[:, c]) for q, k in zip(q_sets, k_sets)], kmags)


def _gqa_scores(j, q_sets, k_sets, kmags):
    pairs = []
    for q, k in zip(q_sets, k_sets):
        qa = _half_masks(q[:, 0:LANES])[j]
        qb = _half_masks(q[:, LANES:])[j]
        pairs.append((k[...], jnp.concatenate([qa, qb], axis=0)))
    return _scores(pairs, kmags)


def _store_pair(y_ref, rows, blk, ot):
    y_ref[rows, blk * LANES:(blk + 1) * LANES] = ot.T.astype(BF16)


N_GROUPS = MLA_HEADS + 4


def _store_groups(y_ref, rows, o, tq):
    for p in range(MLA_HEADS // 2):
        _store_pair(y_ref, rows, p, jnp.concatenate([o[2 * p][:HEAD_DIM], o[2 * p + 1][:HEAD_DIM]], axis=0))
    for base, (o0, o1) in ((2, o[4:6]), (4, o[6:8])):
        _store_pair(y_ref, rows, base, _merge_rows(o0[:, :tq], o1[:, :tq]))
        _store_pair(y_ref, rows, base + 1, _merge_rows(o0[:, tq:], o1[:, tq:]))


def _attn_ctx_body(l, nb, t, sink_ref, qm_ref, km_ref, vm_ref, qc_ref, kc_ref, vc_ref, qd_ref, kd_ref, vd_ref,
                   y_ref):
    def stage1(g):
        b, k = divmod(g, N_GROUPS)
        rs = pl.ds(b * t, t)
        if k < MLA_HEADS:
            return _mla_scores(k, [qm_ref.at[rs]], [km_ref.at[rs]], None)
        q_ref, k_ref = (qc_ref, kc_ref) if k < MLA_HEADS + 2 else (qd_ref, kd_ref)
        return _gqa_scores(k % 2, [q_ref.at[rs]], [k_ref.at[rs]], None)

    def stage2(g, scored):
        s, _ = scored
        b, k = divmod(g, N_GROUPS)
        rs = slice(b * t, (b + 1) * t)
        j = k % 2
        hs = slice(j * HEAD_DIM, (j + 1) * HEAD_DIM)
        if k < MLA_HEADS:
            return _softmax_pv_t(s, [vm_ref[k * HEAD_DIM:(k + 1) * HEAD_DIM, rs]], True)[0]
        if k < MLA_HEADS + 2:
            return _softmax_pv_t(s, [vc_ref[hs, rs]], j == 0)[0]
        return _softmax_pv_t(s, [vd_ref[hs, rs]], j == 0, _sink_row(sink_ref, l, 2 * j, 2 * j + 1, t))[0]

    o = _run_groups(stage1, stage2, N_GROUPS * nb)
    for b in range(nb):
        _store_groups(y_ref, slice(b * t, (b + 1) * t), o[N_GROUPS * b:N_GROUPS * (b + 1)], t)


def _attn_ctx(l, t, sink, qm, km, vm, qc, kc, vc, qd, kd, vd):
    rows = qm.shape[0]
    nb = 4
    r = nb * t
    row = lambda a: (pl.BlockSpec((a.shape[0], r), lambda i: (0, i)) if a.shape[1] == rows
                     else pl.BlockSpec((r, a.shape[1]), lambda i: (i, 0)))
    ins = [qm, km, vm, qc, kc, vc, qd, kd, vd]
    return pl.pallas_call(
        functools.partial(_attn_ctx_body, l, nb, t),
        grid=(rows // r,),
        in_specs=[pl.BlockSpec(memory_space=pltpu.SMEM)] + [row(a) for a in ins],
        out_specs=pl.BlockSpec((r, 6 * LANES), lambda i: (i, 0)),
        out_shape=jax.ShapeDtypeStruct((rows, 6 * LANES), BF16),
        compiler_params=pltpu.CompilerParams(
            dimension_semantics=("arbitrary",), vmem_limit_bytes=VMEM_LIMIT),
        name="attn_ctx",
    )(sink, *ins)


def _window_block(i, tq, t, nwb):
    return jnp.clip(i * (tq // LANES) - WINDOW // LANES, 0, t // LANES - nwb)


def _attn_lat_body(l, tq, t, nwb, sink_ref, qmr_ref, qm_ref, km_ref, vm_ref, kx_ref, vx_ref,
                   qcr_ref, qc_ref, kc_ref, vc_ref, kcx_ref, vcx_ref,
                   qdr_ref, qd_ref, kdx_ref, vdx_ref, *rest):
    kd_refs, vd_refs = rest[:nwb], rest[nwb:2 * nwb]
    y_ref, kmag_scr = rest[2 * nwb:]
    i = pl.program_id(1)
    wk = nwb * LANES
    kpos = _window_block(i, tq, t, nwb) * LANES + lax.broadcasted_iota(jnp.int32, (wk, 2 * tq), 0)
    qpos = i * tq + lax.broadcasted_iota(jnp.int32, (wk, 2 * tq), 1) % tq
    valid = jnp.abs(kpos - qpos) <= WINDOW


    @pl.when(i == 0)
    def _():
        for h in range(MLA_HEADS):
            c = slice(h * LANES, (h + 1) * LANES)
            kmag_scr[h:h + 1, :] = _key_mags(km_ref[:, c])
            kmag_scr[MLA_HEADS + h:MLA_HEADS + h + 1, :] = _key_mags(kx_ref[:, c])
        kmag_scr[8:9, :] = _key_mags(kc_ref[...])
        kmag_scr[9:10, :] = _key_mags(kcx_ref[...])
        kmag_scr[10:11, :] = _key_mags(kdx_ref[...])
        kmag_scr[11:, :] = jnp.zeros((kmag_scr.shape[0] - 11, LANES), F32)

    def run(bounded):
        dens = []
        row = lambda r: kmag_scr[r:r + 1, :]

        def stage1(g):
            if g < MLA_HEADS:
                return _mla_scores(g, [qmr_ref, qm_ref], [km_ref, kx_ref],
                                   [row(g), row(MLA_HEADS + g)] if bounded else None)
            if g < MLA_HEADS + 2:
                return _gqa_scores(g % 2, [qcr_ref, qc_ref], [kc_ref, kcx_ref],
                                   [row(8), row(9)] if bounded else None)
            kw = jnp.concatenate([r[...] for r in kd_refs], axis=0)
            (s1, s2), bound = _gqa_scores(g % 2, [qdr_ref, qd_ref], [kw, kdx_ref],
                                          [_key_mags(kw), row(10)] if bounded else None)
            return [jnp.where(valid, s1, NEG_INF), s2], bound

        def stage2(g, scored):
            s, bound = scored
            j = g % 2
            hs = slice(j * HEAD_DIM, (j + 1) * HEAD_DIM)
            if g < MLA_HEADS:
                hs = slice(g * HEAD_DIM, (g + 1) * HEAD_DIM)
                o, den = _softmax_pv_t(s, [vm_ref[hs, :], vx_ref[hs, :]], True, None, bound)
            elif g < MLA_HEADS + 2:
                o, den = _softmax_pv_t(s, [vc_ref[hs, :], vcx_ref[hs, :]], j == 0, None, bound)
            else:
                vw = jnp.concatenate([r[hs, :] for r in vd_refs], axis=1)
                o, den = _softmax_pv_t(s, [vw, vdx_ref[hs, :]], j == 0,
                                       _sink_row(sink_ref, l, 2 * j, 2 * j + 1, tq), bound)
            dens.append(den)
            return o

        _store_groups(y_ref, slice(None), _run_groups(stage1, stage2, N_GROUPS), tq)
        return dens

    @pl.when(_underflowed(run(True)))
    def _():
        run(False)


def _attn_lat(l, nbatch, t, sink, qmr, qm, km, vmt, kx, vxt, qcr, qc, kc, vct, kcx, vcxt, qdr, qd, kd, vdt,
              kdx, vdxt):
    tq = 256
    nq = t // tq
    nwb = (tq + 2 * WINDOW) // LANES
    nkb = t // LANES
    rows = qm.shape[0]
    qs = lambda a: pl.BlockSpec((tq, a.shape[1]), lambda b, i: (b * nq + i, 0))
    ks = lambda a: pl.BlockSpec((t, a.shape[1]), lambda b, i: (b, 0))
    kts = lambda a: pl.BlockSpec((a.shape[0], t), lambda b, i: (0, b))
    xs = lambda a: pl.BlockSpec((None, None) + a.shape[2:], lambda b, i: (l, b, 0, 0))
    cs = lambda a: pl.BlockSpec((None, None) + a.shape[2:], lambda b, i: (b, l, 0, 0))
    wblk = lambda b, i, k: b * nkb + _window_block(i, tq, t, nwb) + k
    kd_specs = [pl.BlockSpec((LANES, LANES), lambda b, i, k=k: (wblk(b, i, k), 0)) for k in range(nwb)]
    vd_specs = [pl.BlockSpec((LANES, LANES), lambda b, i, k=k: (0, wblk(b, i, k))) for k in range(nwb)]
    in_specs = [pl.BlockSpec(memory_space=pltpu.SMEM),
                qs(qmr), qs(qm), ks(km), kts(vmt), xs(kx), xs(vxt),
                qs(qcr), qs(qc), ks(kc), kts(vct), cs(kcx), cs(vcxt),
                qs(qdr), qs(qd), cs(kdx), cs(vdxt)] + kd_specs + vd_specs
    return pl.pallas_call(
        functools.partial(_attn_lat_body, l, tq, t, nwb),
        grid=(nbatch, nq),
        in_specs=in_specs,
        out_specs=pl.BlockSpec((tq, 6 * LANES), lambda b, i: (b * nq + i, 0)),
        out_shape=jax.ShapeDtypeStruct((rows, 6 * LANES), BF16),
        scratch_shapes=[pltpu.VMEM((16, LANES), F32)],
        compiler_params=pltpu.CompilerParams(
            dimension_semantics=("arbitrary", "arbitrary"), vmem_limit_bytes=VMEM_LIMIT),
        name="attn_lat",
    )(sink, qmr, qm, km, vmt, kx, vxt, qcr, qc, kc, vct, kcx, vcxt, qdr, qd, kdx, vdxt,
      *([kd] * nwb), *([vdt] * nwb))


def _post_body(seq_len, final, mod_ref, n2_ref, cwa_ref, wo_ref, cwf_ref, wup_ref, wdn_ref, fn_ref, *rest):
    ins, (o_ref, hbuf, gbuf, pbuf, abuf) = rest[:9], rest[9:]
    cat = lambda k: jnp.concatenate([ins[3 * k][...], ins[3 * k + 1][...], ins[3 * k + 2][...]], axis=0)
    r = o_ref.shape[0]
    halo = BF16_ROWS
    n = r + 2 * halo
    grp = r // 8
    nchunk = D_MODEL // LANES
    i = pl.program_id(0)

    zg = cat(1).astype(F32)
    zz, gb = zg[:, :GROUP_WIDTH], zg[:, GROUP_WIDTH:]
    pos = (i * r - halo + lax.broadcasted_iota(jnp.int32, (n, 1), 0)) % seq_len
    prev = jnp.where(pos == 0, 0.0, pltpu.roll(zz, 1, 0))
    nxt = jnp.where(pos == seq_len - 1, 0.0, pltpu.roll(zz, n - 1, 0))
    cwa = cwa_ref[...]
    ya = gb * (prev * cwa[0:1] + zz * cwa[1:2] + nxt * cwa[2:3])
    ycat = jnp.concatenate([ya.astype(BF16), cat(2)], axis=1)
    x1 = cat(0) + mod_ref[2:3, :] * _dot(ycat, wo_ref[...])
    h2 = _rms(x1, n2_ref[...], D_MODEL) * (1.0 + mod_ref[4:5, :]) + mod_ref[3:4, :]
    x1c = x1[halo:halo + r]

    for c in range(nchunk):
        for s in range(8):
            pbuf[c, pl.ds(s, grp, stride=8), :] = h2[halo + s * grp:halo + (s + 1) * grp, c * LANES:(c + 1) * LANES]
    hrow = lax.broadcasted_iota(jnp.int32, (halo, 1), 0)
    edge = jnp.where(hrow == 0, h2[halo - 1:halo], jnp.where(hrow == 1, h2[halo + r:halo + r + 1], 0.0))
    hbuf[0:halo, :] = edge.astype(BF16)
    hbuf[halo:, :] = jnp.concatenate([pbuf[c] for c in range(nchunk)], axis=1).astype(BF16)

    sub = lax.broadcasted_iota(jnp.int32, (8, 1), 0)
    seq_first = (i * r + sub * grp) % seq_len == 0
    seq_last = (i * r + sub * grp + grp - 1) % seq_len == seq_len - 1

    def up(j):
        ga = slice(j * FF_TILE, (j + 1) * FF_TILE)
        va = slice(D_FF + j * FF_TILE, D_FF + (j + 1) * FF_TILE)
        return _dot(hbuf[...], wup_ref[:, ga]), _dot(hbuf[...], wup_ref[:, va])

    def conv(u, cw):
        head = jnp.where(sub == 0, u[0:8], pltpu.roll(u[halo + r - 8:halo + r], 1, 0))
        tail = jnp.where(sub == 7, pltpu.roll(u[0:8], 6, 0), pltpu.roll(u[halo:halo + 8], 7, 0))
        prev = jnp.concatenate([jnp.where(seq_first, 0.0, head), u[halo:halo + r - 8]], axis=0)
        nxt = jnp.concatenate([u[halo + 8:halo + r], jnp.where(seq_last, 0.0, tail)], axis=0)
        return prev * cw[0:1] + u[halo:halo + r] * cw[1:2] + nxt * cw[2:3]

    acc = None
    nxt_u = up(0)
    for j in range(N_FF_TILES):
        ua, ub = nxt_u
        if j + 1 < N_FF_TILES:
            nxt_u = up(j + 1)
        ga = slice(j * FF_TILE, (j + 1) * FF_TILE)
        va = slice(D_FF + j * FF_TILE, D_FF + (j + 1) * FF_TILE)
        ua = conv(ua, cwf_ref[:, ga])
        ub = conv(ub, cwf_ref[:, va])
        gbuf[:, ga] = (ua * jax.nn.sigmoid(ua) * ub).astype(BF16)
        if j % DOWN_CHUNK == DOWN_CHUNK - 1 or j == N_FF_TILES - 1:
            lo = (j // DOWN_CHUNK) * DOWN_CHUNK * FF_TILE
            hi = (j + 1) * FF_TILE
            d = _dot(gbuf[:, lo:hi], wdn_ref[lo:hi, :])
            acc = d if acc is None else acc + d

    for c in range(nchunk):
        abuf[c] = acc[:, c * LANES:(c + 1) * LANES]
    g2 = mod_ref[5:6, :]
    for c in range(nchunk):
        cs = slice(c * LANES, (c + 1) * LANES)
        for s in range(8):
            rs = slice(s * grp, (s + 1) * grp)
            o_ref[rs, cs] = x1c[rs, cs] + g2[:, cs] * abuf[c, pl.ds(s, grp, stride=8), :]
    if final:
        o_ref[...] = _rms(o_ref[...], fn_ref[...], D_MODEL)


def _halo_specs(r, w, rows):
    nblk = rows // BF16_ROWS
    per = r // BF16_ROWS
    prev = pl.BlockSpec((BF16_ROWS, w), lambda i: (jnp.maximum(i * per - 1, 0), 0))
    cur = pl.BlockSpec((r, w), lambda i: (i, 0))
    nxt = pl.BlockSpec((BF16_ROWS, w), lambda i: (jnp.minimum((i + 1) * per, nblk - 1), 0))
    return prev, cur, nxt


def _cond_spec(l, seq_len, cond_base):
    r = ROW_TILE
    tiles_per_seq = max(seq_len // r, 1)
    step = 1 if seq_len >= r else 0
    return pl.BlockSpec((None, None, 6, D_MODEL),
                        lambda i: (l, cond_base + step * (i // tiles_per_seq), 0, 0))


def _post(x, mod, lw, l, seq_len, cond_base, zg, y, final_norm, final):
    rows = x.shape[0]
    r = ROW_TILE
    full = lambda a: pl.BlockSpec((None,) + a.shape[1:], lambda i: (l,) + (0,) * (a.ndim - 1),
                                  pipeline_mode=pl.Buffered(1))
    acts = [x, zg, y]
    act_specs = [s for a in acts for s in _halo_specs(r, a.shape[1], rows)]
    act_args = [a for a in acts for _ in range(3)]
    return pl.pallas_call(
        functools.partial(_post_body, seq_len, final),
        grid=(rows // r,),
        in_specs=[_cond_spec(l, seq_len, cond_base), full(lw['norm2']), full(lw['conv_a']), full(lw['w_out']),
                  full(lw['conv_ff']), full(lw['w_up']), full(lw['w_down']),
                  pl.BlockSpec((1, D_MODEL), lambda i: (0, 0))] + act_specs,
        out_specs=pl.BlockSpec((r, D_MODEL), lambda i: (i, 0)),
        out_shape=jax.ShapeDtypeStruct((rows, D_MODEL), F32),
        scratch_shapes=[pltpu.VMEM((r + BF16_ROWS, D_MODEL), BF16), pltpu.VMEM((r, D_FF), BF16),
                        pltpu.VMEM((D_MODEL // LANES, r, LANES), F32),
                        pltpu.VMEM((D_MODEL // LANES, r, LANES), F32)],
        compiler_params=pltpu.CompilerParams(
            dimension_semantics=("arbitrary",), vmem_limit_bytes=VMEM_LIMIT),
        name="post",
    )(mod, lw['norm2'], lw['conv_a'], lw['w_out'], lw['conv_ff'], lw['w_up'], lw['w_down'], final_norm,
      *act_args)


def _pad_cols(a, n):
    return jnp.pad(a, [(0, 0)] * (a.ndim - 1) + [(0, n - a.shape[-1])])


def _perm_heads(a, axis):
    h = [lax.slice_in_dim(a, k * HEAD_DIM, (k + 1) * HEAD_DIM, axis=axis) for k in range(4)]
    return jnp.concatenate([h[0], h[2], h[1], h[3]], axis=axis)


def _prep_weights(w_in, mla_wq_b, mla_wkv_b, w_out, w_up, conv_ff, w_down, norm1, norm2, conv_a,
                  mla_q_norm, mla_kv_norm, gqa_q_norm, gqa_k_norm):
    L = DEPTH
    a = w_in[..., 0:768]
    cq = _pad_cols(w_in[..., 768:960], 256)
    ckv = w_in[..., 960:1088]
    kpe = _pad_cols(w_in[..., 1088:1120], LANES)
    qc = _perm_heads(w_in[..., 1120:1376], 2)
    kvc = w_in[..., 1376:1632]
    qd = _perm_heads(w_in[..., 1632:1888], 2)
    kvd = w_in[..., 1888:2144]
    w_in_p = jnp.concatenate([a, cq, ckv, kpe, qc, kvc, qd, kvd], axis=-1).astype(BF16)

    wq = mla_wq_b.reshape(L, MLA_Q_LORA, MLA_HEADS, MLA_NOPE + MLA_ROPE)
    wq = jnp.concatenate([wq[..., MLA_NOPE:], wq[..., :MLA_NOPE],
                          jnp.zeros((L, MLA_Q_LORA, MLA_HEADS, LANES - MLA_NOPE - MLA_ROPE), F32)], axis=-1)
    wq = jnp.pad(wq.reshape(L, MLA_Q_LORA, MLA_HEADS * LANES), ((0, 0), (0, 256 - MLA_Q_LORA), (0, 0)))

    wkv = mla_wkv_b.reshape(L, MLA_KV_LORA, MLA_HEADS, MLA_NOPE + MLA_V)
    zk = jnp.zeros((L, MLA_KV_LORA, MLA_HEADS, MLA_ROPE), F32)
    wk = jnp.concatenate([zk, wkv[..., :MLA_NOPE], zk], axis=-1).reshape(L, MLA_KV_LORA, MLA_HEADS * LANES)
    wv = wkv[..., MLA_NOPE:].reshape(L, MLA_KV_LORA, MLA_HEADS * MLA_V)
    wkv_p = jnp.concatenate([wk, wv], axis=-1)

    wo = jnp.concatenate([w_out[:, 0:512], _perm_heads(w_out[:, 512:768], 1),
                          _perm_heads(w_out[:, 768:1024], 1)], axis=1)

    hm = jnp.asarray(np.kron(np.eye(2 * LANES // HEAD_DIM), np.ones((HEAD_DIM, HEAD_DIM))), BF16)
    return {
        'w_in': w_in_p, 'wq': wq.astype(BF16), 'wkv': wkv_p.astype(BF16), 'w_out': wo.astype(BF16),
        'w_up': w_up.astype(BF16), 'conv_ff': conv_ff, 'w_down': w_down.astype(BF16),
        'norm1': norm1.reshape(L, 1, D_MODEL), 'norm2': norm2.reshape(L, 1, D_MODEL), 'conv_a': conv_a,
        'gq': _pad_cols(mla_q_norm, 256).reshape(L, 1, 256), 'gkv': mla_kv_norm.reshape(L, 1, MLA_KV_LORA),
        'gqc': jnp.tile(gqa_q_norm, (1, 4)).reshape(L, 1, 256), 'gkc': jnp.tile(gqa_k_norm, (1, 2)).reshape(L, 1, 128),
        'hm': hm,
    }


def _rope_tables(t):
    rows = t // GRID_W
    row = np.repeat(np.arange(rows, dtype=np.float64), GRID_W)
    col = np.tile(np.arange(GRID_W, dtype=np.float64), rows)

    def tabs(dim):
        half = dim // 2
        inv = np.power(ROPE_THETA, -np.arange(0, half, 2, dtype=np.float64) / half)
        ar = row[:, None] * inv
        ac = col[:, None] * inv
        c = np.concatenate([np.cos(ar), np.cos(ar), np.cos(ac), np.cos(ac)], axis=1)
        s = np.concatenate([-np.sin(ar), np.sin(ar), -np.sin(ac), np.sin(ac)], axis=1)
        return c, s

    c64, s64 = tabs(HEAD_DIM)
    c32, s32 = tabs(MLA_ROPE)
    c64 = np.tile(c64, (1, 2))
    s64 = np.tile(s64, (1, 2))
    c32 = np.concatenate([c32, np.ones((t, LANES - MLA_ROPE))], axis=1)
    s32 = np.concatenate([s32, np.zeros((t, LANES - MLA_ROPE))], axis=1)
    return tuple(jnp.asarray(a.astype(np.float32)) for a in (c64, s64, c32, s32))


def kernel(x_prompt, x_sample, cache_mla_ckv, cache_mla_kpe, cache_gqa_k, cache_gqa_v, cache_swa_k, cache_swa_v,
           c, c_ctx, w_ada, b_ada, norm1, w_in, conv_a, mla_q_norm, mla_wq_b, mla_kv_norm, mla_wkv_b,
           gqa_q_norm, gqa_k_norm, swa_sink, w_out, norm2, w_up, conv_ff, w_down, final_norm):
    B, T, _ = x_prompt.shape
    DB, DT, _ = x_sample.shape
    past = cache_mla_ckv.shape[2]

    lw = _prep_weights(w_in, mla_wq_b, mla_wkv_b, w_out, w_up, conv_ff, w_down, norm1, norm2, conv_a,
                       mla_q_norm, mla_kv_norm, gqa_q_norm, gqa_k_norm)
    rope_tabs = _rope_tables(DT)
    fnorm = final_norm.reshape(1, D_MODEL)

    cond_t = jnp.concatenate([c_ctx[:, None], c.T, jnp.zeros((D_MODEL, 8 - 1 - DB), F32)], axis=1)
    mod = _ada(cond_t, 1 + DB, w_ada, b_ada).reshape(DEPTH, 8, 6, D_MODEL)

    kx, vx = _ctx_mla(cache_mla_ckv, _pad_cols(cache_mla_kpe, LANES), lw['wkv'])
    flat = lambda a: a.reshape(DB, DEPTH, past, 2 * HEAD_DIM).astype(BF16)
    kcx, kdx = flat(cache_gqa_k), flat(cache_swa_k)
    vcx, vdx = jnp.swapaxes(flat(cache_gqa_v), 2, 3), jnp.swapaxes(flat(cache_swa_v), 2, 3)

    xp = x_prompt.reshape(B * T, D_MODEL)
    xs = x_sample.reshape(DB * DT, D_MODEL)
    caches = None
    for l in range(DEPTH):
        final = l == DEPTH - 1
        (zg, qm, km, vm, qc, kc, vc, qd, kd, vd, *caches) = _inproj(xp, mod, lw, l, T, 0, None, caches)
        y = _attn_ctx(l, T, swa_sink, qm, km, vm, qc, kc, vc, qd, kd, vd)
        xp = _post(xp, mod, lw, l, T, 0, zg, y, fnorm, final)
        (zg, qmr, qm, km, vm, qcr, qc, kc, vc, qdr, qd, kd, vd) = _inproj(xs, mod, lw, l, DT, 1, rope_tabs)
        y = _attn_lat(l, DB, DT, swa_sink, qmr, qm, km, vm, kx, vx, qcr, qc, kc, vc, kcx, vcx,
                      qdr, qd, kd, vd, kdx, vdx)
        xs = _post(xs, mod, lw, l, DT, 1, zg, y, fnorm, final)

    heads = lambda a: a.reshape(B, DEPTH, T, 2, HEAD_DIM)
    return (xp.reshape(B, T, D_MODEL), xs.reshape(DB, DT, D_MODEL), caches[0], caches[1],
            heads(caches[2]), heads(caches[3]), heads(caches[4]), heads(caches[5]))
```

```python
import functools
import math

import jax
import jax.numpy as jnp
import numpy as np
from jax import lax
from jax.experimental import pallas as pl
from jax.experimental.pallas import tpu as pltpu

F32 = jnp.float32
BF16 = jnp.bfloat16

D_MODEL = 1024
DEPTH = 2
GRID_W = 64
HEAD_DIM = 64
GROUP_WIDTH = D_MODEL // 4
MLA_HEADS = 4
MLA_NOPE = 64
MLA_ROPE = 32
MLA_V = 64
MLA_Q_LORA = 192
MLA_KV_LORA = 128
WINDOW = 128
D_FF = 2816
ROPE_THETA = 10000.0
EPS = 1e-6
NEG_INF = -1e30
LOG2E = math.log2(math.e)
ATTN_SCALE = HEAD_DIM ** -0.5 * LOG2E
MLA_SCALE = (MLA_NOPE + MLA_ROPE) ** -0.5 * LOG2E

LANES = 128
BF16_ROWS = 16
ROW_TILE = 512
IN_ROW_TILE = 1024
IN_SUB_TILE = 256
FF_TILE = 256
N_FF_TILES = D_FF // FF_TILE
DOWN_CHUNK = 4
IN_COLS_PACKED = 2304
VMEM_LIMIT = 56 * 1024 * 1024

_NT = (((1,), (1,)), ((), ()))


def _dot(a, b):
    return jnp.dot(a, b, preferred_element_type=F32)


def _dot_nt(a, b):
    return lax.dot_general(a, b, _NT, preferred_element_type=F32)


def _rms(x, g, n):
    ms = jnp.sum(x * x, axis=-1, keepdims=True) * (1.0 / n)
    return x * lax.rsqrt(ms + EPS) * g


def _lane(shape):
    return lax.broadcasted_iota(jnp.int32, shape, len(shape) - 1)


def _rope(x, c, s, half):
    w = x.shape[-1]
    lo = (_lane(x.shape) % (2 * half)) < half
    sw = jnp.where(lo, pltpu.roll(x, w - half, 1), pltpu.roll(x, half, 1))
    return x * c + sw * s


def _shift_rows(zz, halo, rows):
    n = zz.shape[0]
    prev = pltpu.roll(zz, 1, 0)[halo:halo + rows]
    nxt = pltpu.roll(zz, n - 1, 0)[halo:halo + rows]
    return prev, nxt


def _seq_pos(tile, rows, seq_len):
    r = lax.broadcasted_iota(jnp.int32, (rows, 1), 0)
    return (tile * rows + r) % seq_len


def _ada_body(ncond, c_ref, w_ref, b_ref, o_ref):
    c = c_ref[...]
    s = c * jax.nn.sigmoid(c)
    tn = w_ref.shape[1]
    w = w_ref[...].reshape(D_MODEL // 8, 8, tn)
    rows = []
    for r in range(ncond):
        part = jnp.sum(w * s[:, r:r + 1].reshape(D_MODEL // 8, 8, 1), axis=0)
        rows.append(jnp.sum(part, axis=0, keepdims=True))
    rows.append(jnp.zeros((8 - ncond, tn), F32))
    o_ref[...] = jnp.concatenate(rows, axis=0) + b_ref[...]


def _ada(cond_t, ncond, w_ada, b_ada):
    tn = 1536
    n = 6 * D_MODEL
    return pl.pallas_call(
        functools.partial(_ada_body, ncond),
        grid=(DEPTH, n // tn),
        in_specs=[pl.BlockSpec((D_MODEL, 8), lambda l, j: (0, 0)),
                  pl.BlockSpec((None, D_MODEL, tn), lambda l, j: (l, 0, j)),
                  pl.BlockSpec((None, 1, tn), lambda l, j: (l, 0, j))],
        out_specs=pl.BlockSpec((None, 8, tn), lambda l, j: (l, 0, j)),
        out_shape=jax.ShapeDtypeStruct((DEPTH, 8, n), F32),
        compiler_params=pltpu.CompilerParams(
            dimension_semantics=("arbitrary", "arbitrary"), vmem_limit_bytes=VMEM_LIMIT),
        name="ada",
    )(cond_t, w_ada, b_ada.reshape(DEPTH, 1, n))


def _ctx_body(ckv_ref, kpe_ref, w_ref, kx_ref, vx_ref):
    kv = _dot(ckv_ref[...].astype(BF16), w_ref[...])
    kpe = kpe_ref[...]
    kx_ref[...] = (kv[:, :4 * LANES] + jnp.concatenate([kpe] * MLA_HEADS, axis=1)).astype(BF16)
    vx_ref[...] = kv[:, 4 * LANES:].T.astype(BF16)


def _ctx_mla(cache_ckv, cache_kpe_pad, wkv_p):
    b, _, s, _ = cache_ckv.shape
    return pl.pallas_call(
        _ctx_body,
        grid=(DEPTH, b),
        in_specs=[pl.BlockSpec((None, None, s, MLA_KV_LORA), lambda l, i: (i, l, 0, 0)),
                  pl.BlockSpec((None, None, s, LANES), lambda l, i: (i, l, 0, 0)),
                  pl.BlockSpec((None, MLA_KV_LORA, 6 * LANES), lambda l, i: (l, 0, 0))],
        out_specs=[pl.BlockSpec((None, None, s, 4 * LANES), lambda l, i: (l, i, 0, 0)),
                   pl.BlockSpec((None, None, 2 * LANES, s), lambda l, i: (l, i, 0, 0))],
        out_shape=[jax.ShapeDtypeStruct((DEPTH, b, s, 4 * LANES), BF16),
                   jax.ShapeDtypeStruct((DEPTH, b, 2 * LANES, s), BF16)],
        compiler_params=pltpu.CompilerParams(
            dimension_semantics=("arbitrary", "arbitrary"), vmem_limit_bytes=VMEM_LIMIT),
        name="ctx_mla",
    )(cache_ckv, cache_kpe_pad, wkv_p)


def _inproj_body(rope, n_alias, x_ref, mod_ref, n1_ref, w_ref, wq_ref, wkv_ref, gq_ref, gkv_ref, gqc_ref, gkc_ref,
                 hm_ref, *rest):
    if rope:
        c64_ref, s64_ref, c32_ref, s32_ref = rest[:4]
        outs = rest[4:]
    else:
        outs = rest[n_alias:]
    sh1 = mod_ref[0:1, :]
    sc1 = mod_ref[1:2, :]
    hm = hm_ref[...]
    sub = IN_SUB_TILE
    nsub = x_ref.shape[0] // sub

    def project(s):
        x = x_ref[s * sub:(s + 1) * sub, :]
        h = _rms(x, n1_ref[...], D_MODEL) * (1.0 + sc1) + sh1
        return _dot(h.astype(BF16), w_ref[...])

    def head_rms(v, g):
        v2 = v * v
        hi = v2.astype(BF16)
        lo = (v2 - hi.astype(F32)).astype(BF16)
        w = v.shape[-1]
        ss = _dot(hi, hm[:w, :w]) + _dot(lo, hm[:w, :w])
        return v * lax.rsqrt(ss * (1.0 / HEAD_DIM) + EPS) * g

    def finish(s, acc):
        rs = slice(s * sub, (s + 1) * sub)
        xa, gb, gc = acc[:, 0:256], acc[:, 256:512], acc[:, 512:768]
        z = gc * xa
        cq = acc[:, 768:1024]
        cqn = _rms(cq, gq_ref[...], MLA_Q_LORA)
        qm = _dot(cqn.astype(BF16), wq_ref[...]) * MLA_SCALE
        ckvn = _rms(acc[:, 1024:1152], gkv_ref[...], MLA_KV_LORA)
        kv = _dot(ckvn.astype(BF16), wkv_ref[...])
        kpe = acc[:, 1152:1280]
        vm = kv[:, 4 * LANES:]
        qc = head_rms(acc[:, 1280:1536], gqc_ref[...]) * ATTN_SCALE
        kc = head_rms(acc[:, 1536:1664], gkc_ref[...])
        vc = acc[:, 1664:1792]
        qd = acc[:, 1792:2048] * ATTN_SCALE
        kd = acc[:, 2048:2176]
        vd = acc[:, 2176:2304]

        if not rope:
            (zg_o, qm_o, km_o, vm_o, qc_o, kc_o, vc_o, qd_o, kd_o, vd_o,
             ckv_c, kpe_c, kc_c, vc_c, kd_c, vd_c) = outs
            km = kv[:, :4 * LANES] + jnp.concatenate([kpe] * MLA_HEADS, axis=1)
            zg_o[rs, 0:256] = z.astype(BF16)
            zg_o[rs, 256:512] = gb.astype(BF16)
            qm_o[rs, :] = qm.astype(BF16)
            km_o[rs, :] = km.astype(BF16)
            vm_o[:, rs] = vm.T.astype(BF16)
            qc_o[rs, :] = qc.astype(BF16)
            kc_o[rs, :] = kc.astype(BF16)
            vc_o[:, rs] = vc.T.astype(BF16)
            qd_o[rs, :] = qd.astype(BF16)
            kd_o[rs, :] = kd.astype(BF16)
            vd_o[:, rs] = vd.T.astype(BF16)
            for o, v in ((ckv_c, ckvn), (kpe_c, kpe[:, :MLA_ROPE]), (kc_c, kc), (vc_c, vc), (kd_c, kd),
                         (vd_c, vd)):
                t = o.shape[-2]
                bs = slice(s * sub // t, (s + 1) * sub // t)
                v = v.reshape(sub // t, t, v.shape[-1])
                if n_alias:
                    o[bs] = v
                else:
                    o[bs, 0] = v
                    o[bs, 1:] = jnp.zeros((sub // t, o.shape[1] - 1) + v.shape[1:], F32)
        else:
            (zg_o, qmr_o, qm_o, km_o, vm_o, qcr_o, qc_o, kc_o, vc_o, qdr_o, qd_o, kd_o, vd_o) = outs
            c64, s64, c32, s32 = c64_ref[rs, :], s64_ref[rs, :], c32_ref[rs, :], s32_ref[rs, :]
            rope64 = lambda v: jnp.concatenate(
                [_rope(v[:, i:i + LANES], c64, s64, 16) for i in range(0, v.shape[-1], LANES)], axis=1)
            rope32 = lambda v: jnp.concatenate(
                [_rope(v[:, i:i + LANES], c32, s32, 8) for i in range(0, v.shape[-1], LANES)], axis=1)
            km = kv[:, :4 * LANES] + jnp.concatenate([_rope(kpe, c32, s32, 8)] * MLA_HEADS, axis=1)
            zg_o[rs, 0:256] = z.astype(BF16)
            zg_o[rs, 256:512] = gb.astype(BF16)
            qmr_o[rs, :] = rope32(qm).astype(BF16)
            qm_o[rs, :] = qm.astype(BF16)
            km_o[rs, :] = km.astype(BF16)
            vm_o[:, rs] = vm.T.astype(BF16)
            qcr_o[rs, :] = rope64(qc).astype(BF16)
            qc_o[rs, :] = qc.astype(BF16)
            kc_o[rs, :] = rope64(kc).astype(BF16)
            vc_o[:, rs] = vc.T.astype(BF16)
            qdr_o[rs, :] = rope64(qd).astype(BF16)
            qd_o[rs, :] = qd.astype(BF16)
            kd_o[rs, :] = rope64(kd).astype(BF16)
            vd_o[:, rs] = vd.T.astype(BF16)

    nxt = project(0)
    for s in range(nsub):
        acc = nxt
        if s + 1 < nsub:
            nxt = project(s + 1)
        finish(s, acc)


def _inproj(x, mod, lw, l, seq_len, cond_base, rope_tabs, prev_caches=None):
    rows = x.shape[0]
    r = IN_ROW_TILE
    rope = rope_tabs is not None
    aliases = {}
    tiles_per_seq = max(seq_len // r, 1)
    step = 1 if seq_len >= r else 0

    def cond_map(i):
        return (l, cond_base + step * (i // tiles_per_seq), 0, 0)

    row = lambda w: pl.BlockSpec((r, w), lambda i: (i, 0))
    full = lambda a: pl.BlockSpec((None,) + a.shape[1:], lambda i: (l,) + (0,) * (a.ndim - 1))
    in_specs = [row(D_MODEL),
                pl.BlockSpec((None, None, 6, D_MODEL), cond_map),
                full(lw['norm1']), full(lw['w_in']), full(lw['wq']), full(lw['wkv']),
                full(lw['gq']), full(lw['gkv']), full(lw['gqc']), full(lw['gkc']),
                pl.BlockSpec((2 * LANES, 2 * LANES), lambda i: (0, 0))]
    args = [x, mod, lw['norm1'], lw['w_in'], lw['wq'], lw['wkv'], lw['gq'], lw['gkv'], lw['gqc'], lw['gkc'],
            lw['hm']]
    def act(w, transposed):
        if transposed:
            return jax.ShapeDtypeStruct((w, rows), BF16), pl.BlockSpec((w, r), lambda i: (0, i))
        return jax.ShapeDtypeStruct((rows, w), BF16), row(w)

    if rope:
        tab = pl.BlockSpec((r, LANES), lambda i: (i % tiles_per_seq, 0))
        in_specs += [tab] * 4
        args += list(rope_tabs)
        widths = [(512, 0), (512, 0), (512, 0), (512, 0), (256, 1), (256, 0), (256, 0), (128, 0),
                  (128, 1), (256, 0), (256, 0), (128, 0), (128, 1)]
        out_shape, out_specs = map(list, zip(*[act(w, t) for w, t in widths]))
    else:
        widths = [(512, 0), (512, 0), (512, 0), (256, 1), (256, 0), (128, 0), (128, 1), (256, 0),
                  (128, 0), (128, 1)]
        cwidths = [128, MLA_ROPE, 128, 128, 128, 128]
        nb = r // seq_len
        out_shape, out_specs = map(list, zip(*[act(w, t) for w, t in widths]))
        out_shape += [jax.ShapeDtypeStruct((rows // seq_len, DEPTH, seq_len, w), F32) for w in cwidths]
        assert (prev_caches is None) == (l == 0)
        if l == 0:
            out_specs += [pl.BlockSpec((nb, DEPTH, seq_len, w), lambda i: (i, 0, 0, 0)) for w in cwidths]
        else:
            out_specs += [pl.BlockSpec((nb, None, seq_len, w), lambda i: (i, l, 0, 0)) for w in cwidths]
        if prev_caches is not None:
            aliases = {len(args) + k: len(widths) + k for k in range(len(cwidths))}
            in_specs += [pl.BlockSpec(memory_space=pl.ANY)] * len(cwidths)
            args += list(prev_caches)
    return pl.pallas_call(
        functools.partial(_inproj_body, rope, len(aliases)),
        grid=(rows // r,),
        in_specs=in_specs, out_specs=out_specs, out_shape=out_shape,
        input_output_aliases=aliases,
        compiler_params=pltpu.CompilerParams(
            dimension_semantics=("arbitrary",), vmem_limit_bytes=VMEM_LIMIT),
        name="inproj_rope" if rope else "inproj",
    )(*args)


DEN_FLOOR = 2.0 ** -80


def _key_mags(k):
    return jnp.max(jnp.abs(k), axis=0, keepdims=True).astype(F32)


def _score_bound(kmag, qmag):
    col = jnp.sum(qmag.astype(F32) * kmag, axis=1, keepdims=True)
    return jnp.broadcast_to(col, qmag.shape).T[0:1]


def _softmax_pv_t(s_list, vt_list, first, extra=None, bound=None):
    if bound is None:
        m = s_list[0].max(axis=0, keepdims=True)
        for s in s_list[1:]:
            m = jnp.maximum(m, s.max(axis=0, keepdims=True))
    else:
        m = bound
    if extra is not None:
        m = jnp.maximum(m, extra)
    out = None
    for s, vt in zip(s_list, vt_list):
        ones = jnp.ones(vt.shape, BF16)
        vt = jnp.concatenate([vt, ones] if first else [ones, vt], axis=0)
        pv = _dot(vt, jnp.exp2(s - m).astype(BF16))
        out = pv if out is None else out + pv
    den = out[HEAD_DIM:HEAD_DIM + 1] if first else out[0:1]
    if extra is not None:
        den = den + jnp.exp2(extra - m)
    return out * (1.0 / den), den


def _underflowed(dens):
    width = max(d.shape[1] for d in dens)
    low = None
    for d in dens:
        d = jnp.concatenate([d] * (width // d.shape[1]), axis=1)
        low = d if low is None else jnp.minimum(low, d)
    return jnp.logical_not(jnp.min(low) >= DEN_FLOOR)


def _half_masks(q):
    lo = _lane(q.shape) < HEAD_DIM
    zero = jnp.zeros_like(q)
    return jnp.where(lo, q, zero), jnp.where(lo, zero, q)


def _merge_rows(a, b):
    r = lax.broadcasted_iota(jnp.int32, a.shape, 0)
    return jnp.where(r < HEAD_DIM, a, b)


def _sink_row(sink_ref, l, h0, h1, tq):
    c = lax.broadcasted_iota(jnp.int32, (1, 2 * tq), 1)
    return jnp.where(c < tq, sink_ref[l, h0], sink_ref[l, h1]) * LOG2E


SCORE_AHEAD = 3


def _run_groups(stage1, stage2, n, ahead=SCORE_AHEAD):
    out = []
    pending = [stage1(g) for g in range(min(ahead, n))]
    for g in range(n):
        if g + ahead < n:
            pending.append(stage1(g + ahead))
        out.append(stage2(g, pending.pop(0)))
    return out


def _scores(pairs, kmags):
    s = [_dot_nt(k, q) for k, q in pairs]
    b = None
    if kmags is not None:
        kmag, qmag = None, None
        for (_, q), km in zip(pairs, kmags):
            kmag = km if kmag is None else jnp.maximum(kmag, km)
            qmag = jnp.abs(q) if qmag is None else jnp.maximum(qmag, jnp.abs(q))
        b = _score_bound(kmag, qmag)
    return s, b


def _mla_scores(h, q_sets, k_sets, kmags):
    c = slice(h * LANES, (h + 1) * LANES)
    return _scores([(k[:, c], q[:, c]) for q, k in zip(q_sets, k_sets)], kmags)


def _gqa_scores(j, q_sets, k_sets, kmags):
    pairs = []
    for q, k in zip(q_sets, k_sets):
        qa = _half_masks(q[:, 0:LANES])[j]
        qb = _half_masks(q[:, LANES:])[j]
        pairs.append((k[...], jnp.concatenate([qa, qb], axis=0)))
    return _scores(pairs, kmags)


def _store_pair(y_ref, rows, blk, ot):
    y_ref[rows, blk * LANES:(blk + 1) * LANES] = ot.T.astype(BF16)


N_GROUPS = MLA_HEADS + 4


def _store_groups(y_ref, rows, o, tq):
    for p in range(MLA_HEADS // 2):
        _store_pair(y_ref, rows, p, jnp.concatenate([o[2 * p][:HEAD_DIM], o[2 * p + 1][:HEAD_DIM]], axis=0))
    for base, (o0, o1) in ((2, o[4:6]), (4, o[6:8])):
        _store_pair(y_ref, rows, base, _merge_rows(o0[:, :tq], o1[:, :tq]))
        _store_pair(y_ref, rows, base + 1, _merge_rows(o0[:, tq:], o1[:, tq:]))


def _attn_ctx_body(l, nb, t, sink_ref, qm_ref, km_ref, vm_ref, qc_ref, kc_ref, vc_ref, qd_ref, kd_ref, vd_ref,
                   y_ref):
    def stage1(g):
        b, k = divmod(g, N_GROUPS)
        rs = pl.ds(b * t, t)
        if k < MLA_HEADS:
            return _mla_scores(k, [qm_ref.at[rs]], [km_ref.at[rs]], None)
        q_ref, k_ref = (qc_ref, kc_ref) if k < MLA_HEADS + 2 else (qd_ref, kd_ref)
        return _gqa_scores(k % 2, [q_ref.at[rs]], [k_ref.at[rs]], None)

    def stage2(g, scored):
        s, _ = scored
        b, k = divmod(g, N_GROUPS)
        rs = slice(b * t, (b + 1) * t)
        j = k % 2
        hs = slice(j * HEAD_DIM, (j + 1) * HEAD_DIM)
        if k < MLA_HEADS:
            return _softmax_pv_t(s, [vm_ref[k * HEAD_DIM:(k + 1) * HEAD_DIM, rs]], True)[0]
        if k < MLA_HEADS + 2:
            return _softmax_pv_t(s, [vc_ref[hs, rs]], j == 0)[0]
        return _softmax_pv_t(s, [vd_ref[hs, rs]], j == 0, _sink_row(sink_ref, l, 2 * j, 2 * j + 1, t))[0]

    o = _run_groups(stage1, stage2, N_GROUPS * nb)
    for b in range(nb):
        _store_groups(y_ref, slice(b * t, (b + 1) * t), o[N_GROUPS * b:N_GROUPS * (b + 1)], t)


def _attn_ctx(l, t, sink, qm, km, vm, qc, kc, vc, qd, kd, vd):
    rows = qm.shape[0]
    nb = 4
    r = nb * t
    row = lambda a: (pl.BlockSpec((a.shape[0], r), lambda i: (0, i)) if a.shape[1] == rows
                     else pl.BlockSpec((r, a.shape[1]), lambda i: (i, 0)))
    ins = [qm, km, vm, qc, kc, vc, qd, kd, vd]
    return pl.pallas_call(
        functools.partial(_attn_ctx_body, l, nb, t),
        grid=(rows // r,),
        in_specs=[pl.BlockSpec(memory_space=pltpu.SMEM)] + [row(a) for a in ins],
        out_specs=pl.BlockSpec((r, 6 * LANES), lambda i: (i, 0)),
        out_shape=jax.ShapeDtypeStruct((rows, 6 * LANES), BF16),
        compiler_params=pltpu.CompilerParams(
            dimension_semantics=("arbitrary",), vmem_limit_bytes=VMEM_LIMIT),
        name="attn_ctx",
    )(sink, *ins)


def _window_block(i, tq, t, nwb):
    return jnp.clip(i * (tq // LANES) - WINDOW // LANES, 0, t // LANES - nwb)


def _attn_lat_body(l, tq, t, nwb, sink_ref, qmr_ref, qm_ref, km_ref, vm_ref, kx_ref, vx_ref,
                   qcr_ref, qc_ref, kc_ref, vc_ref, kcx_ref, vcx_ref,
                   qdr_ref, qd_ref, kdx_ref, vdx_ref, *rest):
    kd_refs, vd_refs = rest[:nwb], rest[nwb:2 * nwb]
    y_ref, kmag_scr = rest[2 * nwb:]
    i = pl.program_id(1)
    wk = nwb * LANES
    kpos = _window_block(i, tq, t, nwb) * LANES + lax.broadcasted_iota(jnp.int32, (wk, 2 * tq), 0)
    qpos = i * tq + lax.broadcasted_iota(jnp.int32, (wk, 2 * tq), 1) % tq
    valid = jnp.abs(kpos - qpos) <= WINDOW


    @pl.when(i == 0)
    def _():
        for h in range(MLA_HEADS):
            c = slice(h * LANES, (h + 1) * LANES)
            kmag_scr[h:h + 1, :] = _key_mags(km_ref[:, c])
            kmag_scr[MLA_HEADS + h:MLA_HEADS + h + 1, :] = _key_mags(kx_ref[:, c])
        kmag_scr[8:9, :] = _key_mags(kc_ref[...])
        kmag_scr[9:10, :] = _key_mags(kcx_ref[...])
        kmag_scr[10:11, :] = _key_mags(kdx_ref[...])
        kmag_scr[11:, :] = jnp.zeros((kmag_scr.shape[0] - 11, LANES), F32)

    def run(bounded):
        dens = []
        row = lambda r: kmag_scr[r:r + 1, :]

        def stage1(g):
            if g < MLA_HEADS:
                return _mla_scores(g, [qmr_ref, qm_ref], [km_ref, kx_ref],
                                   [row(g), row(MLA_HEADS + g)] if bounded else None)
            if g < MLA_HEADS + 2:
                return _gqa_scores(g % 2, [qcr_ref, qc_ref], [kc_ref, kcx_ref],
                                   [row(8), row(9)] if bounded else None)
            kw = jnp.concatenate([r[...] for r in kd_refs], axis=0)
            (s1, s2), bound = _gqa_scores(g % 2, [qdr_ref, qd_ref], [kw, kdx_ref],
                                          [_key_mags(kw), row(10)] if bounded else None)
            return [jnp.where(valid, s1, NEG_INF), s2], bound

        def stage2(g, scored):
            s, bound = scored
            j = g % 2
            hs = slice(j * HEAD_DIM, (j + 1) * HEAD_DIM)
            if g < MLA_HEADS:
                hs = slice(g * HEAD_DIM, (g + 1) * HEAD_DIM)
                o, den = _softmax_pv_t(s, [vm_ref[hs, :], vx_ref[hs, :]], True, None, bound)
            elif g < MLA_HEADS + 2:
                o, den = _softmax_pv_t(s, [vc_ref[hs, :], vcx_ref[hs, :]], j == 0, None, bound)
            else:
                vw = jnp.concatenate([r[hs, :] for r in vd_refs], axis=1)
                o, den = _softmax_pv_t(s, [vw, vdx_ref[hs, :]], j == 0,
                                       _sink_row(sink_ref, l, 2 * j, 2 * j + 1, tq), bound)
            dens.append(den)
            return o

        _store_groups(y_ref, slice(None), _run_groups(stage1, stage2, N_GROUPS), tq)
        return dens

    @pl.when(_underflowed(run(True)))
    def _():
        run(False)


def _attn_lat(l, nbatch, t, sink, qmr, qm, km, vmt, kx, vxt, qcr, qc, kc, vct, kcx, vcxt, qdr, qd, kd, vdt,
              kdx, vdxt):
    tq = 256
    nq = t // tq
    nwb = (tq + 2 * WINDOW) // LANES
    nkb = t // LANES
    rows = qm.shape[0]
    qs = lambda a: pl.BlockSpec((tq, a.shape[1]), lambda b, i: (b * nq + i, 0))
    ks = lambda a: pl.BlockSpec((t, a.shape[1]), lambda b, i: (b, 0))
    kts = lambda a: pl.BlockSpec((a.shape[0], t), lambda b, i: (0, b))
    xs = lambda a: pl.BlockSpec((None, None) + a.shape[2:], lambda b, i: (l, b, 0, 0))
    cs = lambda a: pl.BlockSpec((None, None) + a.shape[2:], lambda b, i: (b, l, 0, 0))
    wblk = lambda b, i, k: b * nkb + _window_block(i, tq, t, nwb) + k
    kd_specs = [pl.BlockSpec((LANES, LANES), lambda b, i, k=k: (wblk(b, i, k), 0)) for k in range(nwb)]
    vd_specs = [pl.BlockSpec((LANES, LANES), lambda b, i, k=k: (0, wblk(b, i, k))) for k in range(nwb)]
    in_specs = [pl.BlockSpec(memory_space=pltpu.SMEM),
                qs(qmr), qs(qm), ks(km), kts(vmt), xs(kx), xs(vxt),
                qs(qcr), qs(qc), ks(kc), kts(vct), cs(kcx), cs(vcxt),
                qs(qdr), qs(qd), cs(kdx), cs(vdxt)] + kd_specs + vd_specs
    return pl.pallas_call(
        functools.partial(_attn_lat_body, l, tq, t, nwb),
        grid=(nbatch, nq),
        in_specs=in_specs,
        out_specs=pl.BlockSpec((tq, 6 * LANES), lambda b, i: (b * nq + i, 0)),
        out_shape=jax.ShapeDtypeStruct((rows, 6 * LANES), BF16),
        scratch_shapes=[pltpu.VMEM((16, LANES), F32)],
        compiler_params=pltpu.CompilerParams(
            dimension_semantics=("arbitrary", "arbitrary"), vmem_limit_bytes=VMEM_LIMIT),
        name="attn_lat",
    )(sink, qmr, qm, km, vmt, kx, vxt, qcr, qc, kc, vct, kcx, vcxt, qdr, qd, kdx, vdxt,
      *([kd] * nwb), *([vdt] * nwb))


def _post_body(seq_len, final, mod_ref, n2_ref, cwa_ref, wo_ref, cwf_ref, wup_ref, wdn_ref, fn_ref, *rest):
    o_ref, hbuf, gbuf, pbuf, abuf = rest[-5:]
    ins = rest[:-5]
    r = o_ref.shape[0]
    edges = seq_len > r
    assert edges or r % seq_len == 0
    if edges:
        cat = lambda k: jnp.concatenate([ins[3 * k][...], ins[3 * k + 1][...], ins[3 * k + 2][...]], axis=0)
    else:
        cat = lambda k: ins[k][...]
    halo = BF16_ROWS if edges else 0
    n = r + 2 * halo
    grp = r // 8
    nchunk = D_MODEL // LANES
    i = pl.program_id(0)

    zg = cat(1).astype(F32)
    zz, gb = zg[:, :GROUP_WIDTH], zg[:, GROUP_WIDTH:]
    pos = (i * r - halo + lax.broadcasted_iota(jnp.int32, (n, 1), 0)) % seq_len
    prev = jnp.where(pos == 0, 0.0, pltpu.roll(zz, 1, 0))
    nxt = jnp.where(pos == seq_len - 1, 0.0, pltpu.roll(zz, n - 1, 0))
    cwa = cwa_ref[...]
    ya = gb * (prev * cwa[0:1] + zz * cwa[1:2] + nxt * cwa[2:3])
    ycat = jnp.concatenate([ya.astype(BF16), cat(2)], axis=1)
    x1 = cat(0) + mod_ref[2:3, :] * _dot(ycat, wo_ref[...])
    h2 = _rms(x1, n2_ref[...], D_MODEL) * (1.0 + mod_ref[4:5, :]) + mod_ref[3:4, :]
    x1c = x1[halo:halo + r]

    for c in range(nchunk):
        for s in range(8):
            pbuf[c, pl.ds(s, grp, stride=8), :] = h2[halo + s * grp:halo + (s + 1) * grp, c * LANES:(c + 1) * LANES]
    if edges:
        hrow = lax.broadcasted_iota(jnp.int32, (halo, 1), 0)
        edge = jnp.where(hrow == 0, h2[halo - 1:halo], jnp.where(hrow == 1, h2[halo + r:halo + r + 1], 0.0))
        hbuf[0:halo, :] = edge.astype(BF16)
    hbuf[halo:, :] = jnp.concatenate([pbuf[c] for c in range(nchunk)], axis=1).astype(BF16)

    sub = lax.broadcasted_iota(jnp.int32, (8, 1), 0)
    seq_first = (i * r + sub * grp) % seq_len == 0
    seq_last = (i * r + sub * grp + grp - 1) % seq_len == seq_len - 1

    def up(j):
        ga = slice(j * FF_TILE, (j + 1) * FF_TILE)
        va = slice(D_FF + j * FF_TILE, D_FF + (j + 1) * FF_TILE)
        return _dot(hbuf[...], wup_ref[:, ga]), _dot(hbuf[...], wup_ref[:, va])

    def conv(u, cw):
        head = pltpu.roll(u[halo + r - 8:halo + r], 1, 0)
        tail = pltpu.roll(u[halo:halo + 8], 7, 0)
        if edges:
            head = jnp.where(sub == 0, u[0:8], head)
            tail = jnp.where(sub == 7, pltpu.roll(u[0:8], 6, 0), tail)
        prev = jnp.concatenate([jnp.where(seq_first, 0.0, head), u[halo:halo + r - 8]], axis=0)
        nxt = jnp.concatenate([u[halo + 8:halo + r], jnp.where(seq_last, 0.0, tail)], axis=0)
        return prev * cw[0:1] + u[halo:halo + r] * cw[1:2] + nxt * cw[2:3]

    acc = None
    nxt_u = up(0)
    for j in range(N_FF_TILES):
        ua, ub = nxt_u
        if j + 1 < N_FF_TILES:
            nxt_u = up(j + 1)
        ga = slice(j * FF_TILE, (j + 1) * FF_TILE)
        va = slice(D_FF + j * FF_TILE, D_FF + (j + 1) * FF_TILE)
        ua = conv(ua, cwf_ref[:, ga])
        ub = conv(ub, cwf_ref[:, va])
        gbuf[:, ga] = (ua * jax.nn.sigmoid(ua) * ub).astype(BF16)
        if j % DOWN_CHUNK == DOWN_CHUNK - 1 or j == N_FF_TILES - 1:
            lo = (j // DOWN_CHUNK) * DOWN_CHUNK * FF_TILE
            hi = (j + 1) * FF_TILE
            d = _dot(gbuf[:, lo:hi], wdn_ref[lo:hi, :])
            acc = d if acc is None else acc + d

    for c in range(nchunk):
        abuf[c] = acc[:, c * LANES:(c + 1) * LANES]
    g2 = mod_ref[5:6, :]
    for c in range(nchunk):
        cs = slice(c * LANES, (c + 1) * LANES)
        for s in range(8):
            rs = slice(s * grp, (s + 1) * grp)
            o_ref[rs, cs] = x1c[rs, cs] + g2[:, cs] * abuf[c, pl.ds(s, grp, stride=8), :]
    if final:
        o_ref[...] = _rms(o_ref[...], fn_ref[...], D_MODEL)


def _halo_specs(r, w, rows):
    nblk = rows // BF16_ROWS
    per = r // BF16_ROWS
    prev = pl.BlockSpec((BF16_ROWS, w), lambda i: (jnp.maximum(i * per - 1, 0), 0))
    cur = pl.BlockSpec((r, w), lambda i: (i, 0))
    nxt = pl.BlockSpec((BF16_ROWS, w), lambda i: (jnp.minimum((i + 1) * per, nblk - 1), 0))
    return prev, cur, nxt


def _cond_spec(l, seq_len, cond_base):
    r = ROW_TILE
    tiles_per_seq = max(seq_len // r, 1)
    step = 1 if seq_len >= r else 0
    return pl.BlockSpec((None, None, 6, D_MODEL),
                        lambda i: (l, cond_base + step * (i // tiles_per_seq), 0, 0))


def _post(x, mod, lw, l, seq_len, cond_base, zg, y, final_norm, final):
    rows = x.shape[0]
    r = ROW_TILE
    full = lambda a: pl.BlockSpec((None,) + a.shape[1:], lambda i: (l,) + (0,) * (a.ndim - 1),
                                  pipeline_mode=pl.Buffered(1))
    acts = [x, zg, y]
    edges = seq_len > r
    if edges:
        act_specs = [s for a in acts for s in _halo_specs(r, a.shape[1], rows)]
        act_args = [a for a in acts for _ in range(3)]
    else:
        act_specs = [pl.BlockSpec((r, a.shape[1]), lambda i: (i, 0)) for a in acts]
        act_args = acts
    return pl.pallas_call(
        functools.partial(_post_body, seq_len, final),
        grid=(rows // r,),
        in_specs=[_cond_spec(l, seq_len, cond_base), full(lw['norm2']), full(lw['conv_a']), full(lw['w_out']),
                  full(lw['conv_ff']), full(lw['w_up']), full(lw['w_down']),
                  pl.BlockSpec((1, D_MODEL), lambda i: (0, 0))] + act_specs,
        out_specs=pl.BlockSpec((r, D_MODEL), lambda i: (i, 0)),
        out_shape=jax.ShapeDtypeStruct((rows, D_MODEL), F32),
        scratch_shapes=[pltpu.VMEM((r + (BF16_ROWS if edges else 0), D_MODEL), BF16), pltpu.VMEM((r, D_FF), BF16),
                        pltpu.VMEM((D_MODEL // LANES, r, LANES), F32),
                        pltpu.VMEM((D_MODEL // LANES, r, LANES), F32)],
        compiler_params=pltpu.CompilerParams(
            dimension_semantics=("arbitrary",), vmem_limit_bytes=VMEM_LIMIT),
        name="post",
    )(mod, lw['norm2'], lw['conv_a'], lw['w_out'], lw['conv_ff'], lw['w_up'], lw['w_down'], final_norm,
      *act_args)


def _pad_cols(a, n):
    return jnp.pad(a, [(0, 0)] * (a.ndim - 1) + [(0, n - a.shape[-1])])


def _perm_heads(a, axis):
    h = [lax.slice_in_dim(a, k * HEAD_DIM, (k + 1) * HEAD_DIM, axis=axis) for k in range(4)]
    return jnp.concatenate([h[0], h[2], h[1], h[3]], axis=axis)


def _prep_weights(w_in, mla_wq_b, mla_wkv_b, w_out, w_up, conv_ff, w_down, norm1, norm2, conv_a,
                  mla_q_norm, mla_kv_norm, gqa_q_norm, gqa_k_norm):
    L = DEPTH
    a = w_in[..., 0:768]
    cq = _pad_cols(w_in[..., 768:960], 256)
    ckv = w_in[..., 960:1088]
    kpe = _pad_cols(w_in[..., 1088:1120], LANES)
    qc = _perm_heads(w_in[..., 1120:1376], 2)
    kvc = w_in[..., 1376:1632]
    qd = _perm_heads(w_in[..., 1632:1888], 2)
    kvd = w_in[..., 1888:2144]
    w_in_p = jnp.concatenate([a, cq, ckv, kpe, qc, kvc, qd, kvd], axis=-1).astype(BF16)

    wq = mla_wq_b.reshape(L, MLA_Q_LORA, MLA_HEADS, MLA_NOPE + MLA_ROPE)
    wq = jnp.concatenate([wq[..., MLA_NOPE:], wq[..., :MLA_NOPE],
                          jnp.zeros((L, MLA_Q_LORA, MLA_HEADS, LANES - MLA_NOPE - MLA_ROPE), F32)], axis=-1)
    wq = jnp.pad(wq.reshape(L, MLA_Q_LORA, MLA_HEADS * LANES), ((0, 0), (0, 256 - MLA_Q_LORA), (0, 0)))

    wkv = mla_wkv_b.reshape(L, MLA_KV_LORA, MLA_HEADS, MLA_NOPE + MLA_V)
    zk = jnp.zeros((L, MLA_KV_LORA, MLA_HEADS, MLA_ROPE), F32)
    wk = jnp.concatenate([zk, wkv[..., :MLA_NOPE], zk], axis=-1).reshape(L, MLA_KV_LORA, MLA_HEADS * LANES)
    wv = wkv[..., MLA_NOPE:].reshape(L, MLA_KV_LORA, MLA_HEADS * MLA_V)
    wkv_p = jnp.concatenate([wk, wv], axis=-1)

    wo = jnp.concatenate([w_out[:, 0:512], _perm_heads(w_out[:, 512:768], 1),
                          _perm_heads(w_out[:, 768:1024], 1)], axis=1)

    hm = jnp.asarray(np.kron(np.eye(2 * LANES // HEAD_DIM), np.ones((HEAD_DIM, HEAD_DIM))), BF16)
    return {
        'w_in': w_in_p, 'wq': wq.astype(BF16), 'wkv': wkv_p.astype(BF16), 'w_out': wo.astype(BF16),
        'w_up': w_up.astype(BF16), 'conv_ff': conv_ff, 'w_down': w_down.astype(BF16),
        'norm1': norm1.reshape(L, 1, D_MODEL), 'norm2': norm2.reshape(L, 1, D_MODEL), 'conv_a': conv_a,
        'gq': _pad_cols(mla_q_norm, 256).reshape(L, 1, 256), 'gkv': mla_kv_norm.reshape(L, 1, MLA_KV_LORA),
        'gqc': jnp.tile(gqa_q_norm, (1, 4)).reshape(L, 1, 256), 'gkc': jnp.tile(gqa_k_norm, (1, 2)).reshape(L, 1, 128),
        'hm': hm,
    }


def _rope_tables(t):
    rows = t // GRID_W
    row = np.repeat(np.arange(rows, dtype=np.float64), GRID_W)
    col = np.tile(np.arange(GRID_W, dtype=np.float64), rows)

    def tabs(dim):
        half = dim // 2
        inv = np.power(ROPE_THETA, -np.arange(0, half, 2, dtype=np.float64) / half)
        ar = row[:, None] * inv
        ac = col[:, None] * inv
        c = np.concatenate([np.cos(ar), np.cos(ar), np.cos(ac), np.cos(ac)], axis=1)
        s = np.concatenate([-np.sin(ar), np.sin(ar), -np.sin(ac), np.sin(ac)], axis=1)
        return c, s

    c64, s64 = tabs(HEAD_DIM)
    c32, s32 = tabs(MLA_ROPE)
    c64 = np.tile(c64, (1, 2))
    s64 = np.tile(s64, (1, 2))
    c32 = np.concatenate([c32, np.ones((t, LANES - MLA_ROPE))], axis=1)
    s32 = np.concatenate([s32, np.zeros((t, LANES - MLA_ROPE))], axis=1)
    return tuple(jnp.asarray(a.astype(np.float32)) for a in (c64, s64, c32, s32))


def kernel(x_prompt, x_sample, cache_mla_ckv, cache_mla_kpe, cache_gqa_k, cache_gqa_v, cache_swa_k, cache_swa_v,
           c, c_ctx, w_ada, b_ada, norm1, w_in, conv_a, mla_q_norm, mla_wq_b, mla_kv_norm, mla_wkv_b,
           gqa_q_norm, gqa_k_norm, swa_sink, w_out, norm2, w_up, conv_ff, w_down, final_norm):
    B, T, _ = x_prompt.shape
    DB, DT, _ = x_sample.shape
    past = cache_mla_ckv.shape[2]

    lw = _prep_weights(w_in, mla_wq_b, mla_wkv_b, w_out, w_up, conv_ff, w_down, norm1, norm2, conv_a,
                       mla_q_norm, mla_kv_norm, gqa_q_norm, gqa_k_norm)
    rope_tabs = _rope_tables(DT)
    fnorm = final_norm.reshape(1, D_MODEL)

    cond_t = jnp.concatenate([c_ctx[:, None], c.T, jnp.zeros((D_MODEL, 8 - 1 - DB), F32)], axis=1)
    mod = _ada(cond_t, 1 + DB, w_ada, b_ada).reshape(DEPTH, 8, 6, D_MODEL)

    kx, vx = _ctx_mla(cache_mla_ckv, _pad_cols(cache_mla_kpe, LANES), lw['wkv'])
    flat = lambda a: a.reshape(DB, DEPTH, past, 2 * HEAD_DIM).astype(BF16)
    kcx, kdx = flat(cache_gqa_k), flat(cache_swa_k)
    vcx, vdx = jnp.swapaxes(flat(cache_gqa_v), 2, 3), jnp.swapaxes(flat(cache_swa_v), 2, 3)

    xp = x_prompt.reshape(B * T, D_MODEL)
    xs = x_sample.reshape(DB * DT, D_MODEL)
    caches = None
    for l in range(DEPTH):
        final = l == DEPTH - 1
        (zg, qm, km, vm, qc, kc, vc, qd, kd, vd, *caches) = _inproj(xp, mod, lw, l, T, 0, None, caches)
        y = _attn_ctx(l, T, swa_sink, qm, km, vm, qc, kc, vc, qd, kd, vd)
        xp = _post(xp, mod, lw, l, T, 0, zg, y, fnorm, final)
        (zg, qmr, qm, km, vm, qcr, qc, kc, vc, qdr, qd, kd, vd) = _inproj(xs, mod, lw, l, DT, 1, rope_tabs)
        y = _attn_lat(l, DB, DT, swa_sink, qmr, qm, km, vm, kx, vx, qcr, qc, kc, vc, kcx, vcx,
                      qdr, qd, kd, vd, kdx, vdx)
        xs = _post(xs, mod, lw, l, DT, 1, zg, y, fnorm, final)

    heads = lambda a: a.reshape(B, DEPTH, T, 2, HEAD_DIM)
    return (xp.reshape(B, T, D_MODEL), xs.reshape(DB, DT, D_MODEL), caches[0], caches[1],
            heads(caches[2]), heads(caches[3]), heads(caches[4]), heads(caches[5]))
```

```python
import functools
import math

import jax
import jax.numpy as jnp
import numpy as np
from jax import lax
from jax.experimental import pallas as pl
from jax.experimental.pallas import tpu as pltpu

F32 = jnp.float32
BF16 = jnp.bfloat16

D_MODEL = 1024
DEPTH = 2
GRID_W = 64
HEAD_DIM = 64
GROUP_WIDTH = D_MODEL // 4
MLA_HEADS = 4
MLA_NOPE = 64
MLA_ROPE = 32
MLA_V = 64
MLA_Q_LORA = 192
MLA_KV_LORA = 128
WINDOW = 128
D_FF = 2816
ROPE_THETA = 10000.0
EPS = 1e-6
NEG_INF = -1e30
LOG2E = math.log2(math.e)
ATTN_SCALE = HEAD_DIM ** -0.5 * LOG2E
MLA_SCALE = (MLA_NOPE + MLA_ROPE) ** -0.5 * LOG2E

LANES = 128
BF16_ROWS = 16
ROW_TILE = 512
IN_ROW_TILE = 1024
IN_SUB_TILE = 256
FF_TILE = 256
N_FF_TILES = D_FF // FF_TILE
DOWN_CHUNK = 11
IN_COLS_PACKED = 2304
VMEM_LIMIT = 56 * 1024 * 1024

_NT = (((1,), (1,)), ((), ()))


def _dot(a, b):
    return jnp.dot(a, b, preferred_element_type=F32)


def _dot_nt(a, b):
    return lax.dot_general(a, b, _NT, preferred_element_type=F32)


def _rms(x, g, n):
    ms = jnp.sum(x * x, axis=-1, keepdims=True) * (1.0 / n)
    return x * lax.rsqrt(ms + EPS) * g


def _lane(shape):
    return lax.broadcasted_iota(jnp.int32, shape, len(shape) - 1)


def _rope(x, c, s, half):
    w = x.shape[-1]
    lo = (_lane(x.shape) % (2 * half)) < half
    sw = jnp.where(lo, pltpu.roll(x, w - half, 1), pltpu.roll(x, half, 1))
    return x * c + sw * s


def _shift_rows(zz, halo, rows):
    n = zz.shape[0]
    prev = pltpu.roll(zz, 1, 0)[halo:halo + rows]
    nxt = pltpu.roll(zz, n - 1, 0)[halo:halo + rows]
    return prev, nxt


def _seq_pos(tile, rows, seq_len):
    r = lax.broadcasted_iota(jnp.int32, (rows, 1), 0)
    return (tile * rows + r) % seq_len


def _ada_body(ncond, c_ref, w_ref, b_ref, o_ref):
    c = c_ref[...]
    s = c * jax.nn.sigmoid(c)
    tn = w_ref.shape[1]
    w = w_ref[...].reshape(D_MODEL // 8, 8, tn)
    rows = []
    for r in range(ncond):
        part = jnp.sum(w * s[:, r:r + 1].reshape(D_MODEL // 8, 8, 1), axis=0)
        rows.append(jnp.sum(part, axis=0, keepdims=True))
    rows.append(jnp.zeros((8 - ncond, tn), F32))
    o_ref[...] = jnp.concatenate(rows, axis=0) + b_ref[...]


def _ada(cond_t, ncond, w_ada, b_ada):
    tn = 1536
    n = 6 * D_MODEL
    return pl.pallas_call(
        functools.partial(_ada_body, ncond),
        grid=(DEPTH, n // tn),
        in_specs=[pl.BlockSpec((D_MODEL, 8), lambda l, j: (0, 0)),
                  pl.BlockSpec((None, D_MODEL, tn), lambda l, j: (l, 0, j)),
                  pl.BlockSpec((None, 1, tn), lambda l, j: (l, 0, j))],
        out_specs=pl.BlockSpec((None, 8, tn), lambda l, j: (l, 0, j)),
        out_shape=jax.ShapeDtypeStruct((DEPTH, 8, n), F32),
        compiler_params=pltpu.CompilerParams(
            dimension_semantics=("arbitrary", "arbitrary"), vmem_limit_bytes=VMEM_LIMIT),
        name="ada",
    )(cond_t, w_ada, b_ada.reshape(DEPTH, 1, n))


def _ctx_body(ckv_ref, kpe_ref, w_ref, kx_ref, vx_ref):
    kv = _dot(ckv_ref[...].astype(BF16), w_ref[...])
    kpe = kpe_ref[...]
    kx_ref[...] = (kv[:, :4 * LANES] + jnp.concatenate([kpe] * MLA_HEADS, axis=1)).astype(BF16)
    vx_ref[...] = kv[:, 4 * LANES:].T.astype(BF16)


def _ctx_mla(cache_ckv, cache_kpe_pad, wkv_p):
    b, _, s, _ = cache_ckv.shape
    return pl.pallas_call(
        _ctx_body,
        grid=(DEPTH, b),
        in_specs=[pl.BlockSpec((None, None, s, MLA_KV_LORA), lambda l, i: (i, l, 0, 0)),
                  pl.BlockSpec((None, None, s, LANES), lambda l, i: (i, l, 0, 0)),
                  pl.BlockSpec((None, MLA_KV_LORA, 6 * LANES), lambda l, i: (l, 0, 0))],
        out_specs=[pl.BlockSpec((None, None, s, 4 * LANES), lambda l, i: (l, i, 0, 0)),
                   pl.BlockSpec((None, None, 2 * LANES, s), lambda l, i: (l, i, 0, 0))],
        out_shape=[jax.ShapeDtypeStruct((DEPTH, b, s, 4 * LANES), BF16),
                   jax.ShapeDtypeStruct((DEPTH, b, 2 * LANES, s), BF16)],
        compiler_params=pltpu.CompilerParams(
            dimension_semantics=("arbitrary", "arbitrary"), vmem_limit_bytes=VMEM_LIMIT),
        name="ctx_mla",
    )(cache_ckv, cache_kpe_pad, wkv_p)


def _inproj_body(rope, n_alias, x_ref, mod_ref, n1_ref, w_ref, wq_ref, wkv_ref, gq_ref, gkv_ref, gqc_ref, gkc_ref,
                 hm_ref, *rest):
    if rope:
        c64_ref, s64_ref, c32_ref, s32_ref = rest[:4]
        outs = rest[4:]
    else:
        outs = rest[n_alias:]
    sh1 = mod_ref[0:1, :]
    sc1 = mod_ref[1:2, :]
    hm = hm_ref[...]
    sub = IN_SUB_TILE
    nsub = x_ref.shape[0] // sub

    def project(s):
        x = x_ref[s * sub:(s + 1) * sub, :]
        h = _rms(x, n1_ref[...], D_MODEL) * (1.0 + sc1) + sh1
        return _dot(h.astype(BF16), w_ref[...])

    def head_rms(v, g):
        v2 = v * v
        hi = v2.astype(BF16)
        lo = (v2 - hi.astype(F32)).astype(BF16)
        w = v.shape[-1]
        ss = _dot(hi, hm[:w, :w]) + _dot(lo, hm[:w, :w])
        return v * lax.rsqrt(ss * (1.0 / HEAD_DIM) + EPS) * g

    def finish(s, acc):
        rs = slice(s * sub, (s + 1) * sub)
        xa, gb, gc = acc[:, 0:256], acc[:, 256:512], acc[:, 512:768]
        z = gc * xa
        cq = acc[:, 768:1024]
        cqn = _rms(cq, gq_ref[...], MLA_Q_LORA)
        qm = _dot(cqn.astype(BF16), wq_ref[...]) * MLA_SCALE
        ckvn = _rms(acc[:, 1024:1152], gkv_ref[...], MLA_KV_LORA)
        kv = _dot(ckvn.astype(BF16), wkv_ref[...])
        kpe = acc[:, 1152:1280]
        vm = kv[:, 4 * LANES:]
        qc = head_rms(acc[:, 1280:1536], gqc_ref[...]) * ATTN_SCALE
        kc = head_rms(acc[:, 1536:1664], gkc_ref[...])
        vc = acc[:, 1664:1792]
        qd = acc[:, 1792:2048] * ATTN_SCALE
        kd = acc[:, 2048:2176]
        vd = acc[:, 2176:2304]

        if not rope:
            (zg_o, qm_o, km_o, vm_o, qc_o, kc_o, vc_o, qd_o, kd_o, vd_o,
             ckv_c, kpe_c, kc_c, vc_c, kd_c, vd_c) = outs
            km = kv[:, :4 * LANES] + jnp.concatenate([kpe] * MLA_HEADS, axis=1)
            zg_o[rs, 0:256] = z.astype(BF16)
            zg_o[rs, 256:512] = gb.astype(BF16)
            qm_o[rs, :] = qm.astype(BF16)
            km_o[rs, :] = km.astype(BF16)
            vm_o[:, rs] = vm.T.astype(BF16)
            qc_o[rs, :] = qc.astype(BF16)
            kc_o[rs, :] = kc.astype(BF16)
            vc_o[:, rs] = vc.T.astype(BF16)
            qd_o[rs, :] = qd.astype(BF16)
            kd_o[rs, :] = kd.astype(BF16)
            vd_o[:, rs] = vd.T.astype(BF16)
            for o, v in ((ckv_c, ckvn), (kpe_c, kpe[:, :MLA_ROPE]), (kc_c, kc), (vc_c, vc), (kd_c, kd),
                         (vd_c, vd)):
                t = o.shape[-2]
                bs = slice(s * sub // t, (s + 1) * sub // t)
                v = v.reshape(sub // t, t, v.shape[-1])
                if n_alias:
                    o[bs] = v
                else:
                    o[bs, 0] = v
                    o[bs, 1:] = jnp.zeros((sub // t, o.shape[1] - 1) + v.shape[1:], F32)
        else:
            (zg_o, qmr_o, qm_o, km_o, vm_o, qcr_o, qc_o, kc_o, vc_o, qdr_o, qd_o, kd_o, vd_o) = outs
            c64, s64, c32, s32 = c64_ref[rs, :], s64_ref[rs, :], c32_ref[rs, :], s32_ref[rs, :]
            rope64 = lambda v: jnp.concatenate(
                [_rope(v[:, i:i + LANES], c64, s64, 16) for i in range(0, v.shape[-1], LANES)], axis=1)
            rope32 = lambda v: jnp.concatenate(
                [_rope(v[:, i:i + LANES], c32, s32, 8) for i in range(0, v.shape[-1], LANES)], axis=1)
            km = kv[:, :4 * LANES] + jnp.concatenate([_rope(kpe, c32, s32, 8)] * MLA_HEADS, axis=1)
            zg_o[rs, 0:256] = z.astype(BF16)
            zg_o[rs, 256:512] = gb.astype(BF16)
            qmr_o[rs, :] = rope32(qm).astype(BF16)
            qm_o[rs, :] = qm.astype(BF16)
            km_o[rs, :] = km.astype(BF16)
            vm_o[:, rs] = vm.T.astype(BF16)
            qcr_o[rs, :] = rope64(qc).astype(BF16)
            qc_o[rs, :] = qc.astype(BF16)
            kc_o[rs, :] = rope64(kc).astype(BF16)
            vc_o[:, rs] = vc.T.astype(BF16)
            qdr_o[rs, :] = rope64(qd).astype(BF16)
            qd_o[rs, :] = qd.astype(BF16)
            kd_o[rs, :] = rope64(kd).astype(BF16)
            vd_o[:, rs] = vd.T.astype(BF16)

    nxt = project(0)
    for s in range(nsub):
        acc = nxt
        if s + 1 < nsub:
            nxt = project(s + 1)
        finish(s, acc)


def _inproj(x, mod, lw, l, seq_len, cond_base, rope_tabs, prev_caches=None):
    rows = x.shape[0]
    r = IN_ROW_TILE
    rope = rope_tabs is not None
    aliases = {}
    tiles_per_seq = max(seq_len // r, 1)
    step = 1 if seq_len >= r else 0

    def cond_map(i):
        return (l, cond_base + step * (i // tiles_per_seq), 0, 0)

    row = lambda w: pl.BlockSpec((r, w), lambda i: (i, 0))
    full = lambda a: pl.BlockSpec((None,) + a.shape[1:], lambda i: (l,) + (0,) * (a.ndim - 1))
    in_specs = [row(D_MODEL),
                pl.BlockSpec((None, None, 6, D_MODEL), cond_map),
                full(lw['norm1']), full(lw['w_in']), full(lw['wq']), full(lw['wkv']),
                full(lw['gq']), full(lw['gkv']), full(lw['gqc']), full(lw['gkc']),
                pl.BlockSpec((2 * LANES, 2 * LANES), lambda i: (0, 0))]
    args = [x, mod, lw['norm1'], lw['w_in'], lw['wq'], lw['wkv'], lw['gq'], lw['gkv'], lw['gqc'], lw['gkc'],
            lw['hm']]
    def act(w, transposed):
        if transposed:
            return jax.ShapeDtypeStruct((w, rows), BF16), pl.BlockSpec((w, r), lambda i: (0, i))
        return jax.ShapeDtypeStruct((rows, w), BF16), row(w)

    if rope:
        tab = pl.BlockSpec((r, LANES), lambda i: (i % tiles_per_seq, 0))
        in_specs += [tab] * 4
        args += list(rope_tabs)
        widths = [(512, 0), (512, 0), (512, 0), (512, 0), (256, 1), (256, 0), (256, 0), (128, 0),
                  (128, 1), (256, 0), (256, 0), (128, 0), (128, 1)]
        out_shape, out_specs = map(list, zip(*[act(w, t) for w, t in widths]))
    else:
        widths = [(512, 0), (512, 0), (512, 0), (256, 1), (256, 0), (128, 0), (128, 1), (256, 0),
                  (128, 0), (128, 1)]
        cwidths = [128, MLA_ROPE, 128, 128, 128, 128]
        nb = r // seq_len
        out_shape, out_specs = map(list, zip(*[act(w, t) for w, t in widths]))
        out_shape += [jax.ShapeDtypeStruct((rows // seq_len, DEPTH, seq_len, w), F32) for w in cwidths]
        assert (prev_caches is None) == (l == 0)
        if l == 0:
            out_specs += [pl.BlockSpec((nb, DEPTH, seq_len, w), lambda i: (i, 0, 0, 0)) for w in cwidths]
        else:
            out_specs += [pl.BlockSpec((nb, None, seq_len, w), lambda i: (i, l, 0, 0)) for w in cwidths]
        if prev_caches is not None:
            aliases = {len(args) + k: len(widths) + k for k in range(len(cwidths))}
            in_specs += [pl.BlockSpec(memory_space=pl.ANY)] * len(cwidths)
            args += list(prev_caches)
    return pl.pallas_call(
        functools.partial(_inproj_body, rope, len(aliases)),
        grid=(rows // r,),
        in_specs=in_specs, out_specs=out_specs, out_shape=out_shape,
        input_output_aliases=aliases,
        compiler_params=pltpu.CompilerParams(
            dimension_semantics=("arbitrary",), vmem_limit_bytes=VMEM_LIMIT),
        name="inproj_rope" if rope else "inproj",
    )(*args)


DEN_FLOOR = 2.0 ** -80


def _key_mags(k):
    return jnp.max(jnp.abs(k), axis=0, keepdims=True).astype(F32)


def _score_bound(kmag, qmag):
    col = jnp.sum(qmag.astype(F32) * kmag, axis=1, keepdims=True)
    return jnp.broadcast_to(col, qmag.shape).T[0:1]


def _softmax_pv_t(s_list, vt_list, first, extra=None, bound=None):
    if bound is None:
        m = s_list[0].max(axis=0, keepdims=True)
        for s in s_list[1:]:
            m = jnp.maximum(m, s.max(axis=0, keepdims=True))
    else:
        m = bound
    if extra is not None:
        m = jnp.maximum(m, extra)
    out = None
    for s, vt in zip(s_list, vt_list):
        ones = jnp.ones(vt.shape, BF16)
        vt = jnp.concatenate([vt, ones] if first else [ones, vt], axis=0)
        pv = _dot(vt, jnp.exp2(s - m).astype(BF16))
        out = pv if out is None else out + pv
    den = out[HEAD_DIM:HEAD_DIM + 1] if first else out[0:1]
    if extra is not None:
        den = den + jnp.exp2(extra - m)
    return out * (1.0 / den), den


def _underflowed(dens):
    width = max(d.shape[1] for d in dens)
    low = None
    for d in dens:
        d = jnp.concatenate([d] * (width // d.shape[1]), axis=1)
        low = d if low is None else jnp.minimum(low, d)
    return jnp.logical_not(jnp.min(low) >= DEN_FLOOR)


def _half_masks(q):
    lo = _lane(q.shape) < HEAD_DIM
    zero = jnp.zeros_like(q)
    return jnp.where(lo, q, zero), jnp.where(lo, zero, q)


def _merge_rows(a, b):
    r = lax.broadcasted_iota(jnp.int32, a.shape, 0)
    return jnp.where(r < HEAD_DIM, a, b)


def _sink_row(sink_ref, l, h0, h1, tq):
    c = lax.broadcasted_iota(jnp.int32, (1, 2 * tq), 1)
    return jnp.where(c < tq, sink_ref[l, h0], sink_ref[l, h1]) * LOG2E


SCORE_AHEAD = 4


def _run_groups(stage1, stage2, n, ahead=SCORE_AHEAD):
    out = []
    pending = [stage1(g) for g in range(min(ahead, n))]
    for g in range(n):
        if g + ahead < n:
            pending.append(stage1(g + ahead))
        out.append(stage2(g, pending.pop(0)))
    return out


def _scores(pairs, kmags):
    s = [_dot_nt(k, q) for k, q in pairs]
    b = None
    if kmags is not None:
        kmag, qmag = None, None
        for (_, q), km in zip(pairs, kmags):
            kmag = km if kmag is None else jnp.maximum(kmag, km)
            qmag = jnp.abs(q) if qmag is None else jnp.maximum(qmag, jnp.abs(q))
        b = _score_bound(kmag, qmag)
    return s, b


def _mla_scores(h, q_sets, k_sets, kmags):
    c = slice(h * LANES, (h + 1) * LANES)
    return _scores([(k[:, c], q[:, c]) for q, k in zip(q_sets, k_sets)], kmags)


def _gqa_scores(j, q_sets, k_sets, kmags):
    pairs = []
    for q, k in zip(q_sets, k_sets):
        qa = _half_masks(q[:, 0:LANES])[j]
        qb = _half_masks(q[:, LANES:])[j]
        pairs.append((k[...], jnp.concatenate([qa, qb], axis=0)))
    return _scores(pairs, kmags)


def _store_pair(y_ref, rows, blk, ot):
    y_ref[rows, blk * LANES:(blk + 1) * LANES] = ot.T.astype(BF16)


N_GROUPS = MLA_HEADS + 4


def _store_groups(y_ref, rows, o, tq):
    for p in range(MLA_HEADS // 2):
        _store_pair(y_ref, rows, p, jnp.concatenate([o[2 * p][:HEAD_DIM], o[2 * p + 1][:HEAD_DIM]], axis=0))
    for base, (o0, o1) in ((2, o[4:6]), (4, o[6:8])):
        _store_pair(y_ref, rows, base, _merge_rows(o0[:, :tq], o1[:, :tq]))
        _store_pair(y_ref, rows, base + 1, _merge_rows(o0[:, tq:], o1[:, tq:]))


def _attn_ctx_body(l, nb, t, sink_ref, qm_ref, km_ref, vm_ref, qc_ref, kc_ref, vc_ref, qd_ref, kd_ref, vd_ref,
                   y_ref):
    def stage1(g):
        b, k = divmod(g, N_GROUPS)
        rs = pl.ds(b * t, t)
        if k < MLA_HEADS:
            return _mla_scores(k, [qm_ref.at[rs]], [km_ref.at[rs]], None)
        q_ref, k_ref = (qc_ref, kc_ref) if k < MLA_HEADS + 2 else (qd_ref, kd_ref)
        return _gqa_scores(k % 2, [q_ref.at[rs]], [k_ref.at[rs]], None)

    def stage2(g, scored):
        s, _ = scored
        b, k = divmod(g, N_GROUPS)
        rs = slice(b * t, (b + 1) * t)
        j = k % 2
        hs = slice(j * HEAD_DIM, (j + 1) * HEAD_DIM)
        if k < MLA_HEADS:
            return _softmax_pv_t(s, [vm_ref[k * HEAD_DIM:(k + 1) * HEAD_DIM, rs]], True)[0]
        if k < MLA_HEADS + 2:
            return _softmax_pv_t(s, [vc_ref[hs, rs]], j == 0)[0]
        return _softmax_pv_t(s, [vd_ref[hs, rs]], j == 0, _sink_row(sink_ref, l, 2 * j, 2 * j + 1, t))[0]

    o = _run_groups(stage1, stage2, N_GROUPS * nb)
    for b in range(nb):
        _store_groups(y_ref, slice(b * t, (b + 1) * t), o[N_GROUPS * b:N_GROUPS * (b + 1)], t)


def _attn_ctx(l, t, sink, qm, km, vm, qc, kc, vc, qd, kd, vd):
    rows = qm.shape[0]
    nb = 4
    r = nb * t
    row = lambda a: (pl.BlockSpec((a.shape[0], r), lambda i: (0, i)) if a.shape[1] == rows
                     else pl.BlockSpec((r, a.shape[1]), lambda i: (i, 0)))
    ins = [qm, km, vm, qc, kc, vc, qd, kd, vd]
    return pl.pallas_call(
        functools.partial(_attn_ctx_body, l, nb, t),
        grid=(rows // r,),
        in_specs=[pl.BlockSpec(memory_space=pltpu.SMEM)] + [row(a) for a in ins],
        out_specs=pl.BlockSpec((r, 6 * LANES), lambda i: (i, 0)),
        out_shape=jax.ShapeDtypeStruct((rows, 6 * LANES), BF16),
        compiler_params=pltpu.CompilerParams(
            dimension_semantics=("arbitrary",), vmem_limit_bytes=VMEM_LIMIT),
        name="attn_ctx",
    )(sink, *ins)


def _window_block(i, tq, t, nwb):
    return jnp.clip(i * (tq // LANES) - WINDOW // LANES, 0, t // LANES - nwb)


def _attn_lat_body(l, tq, t, nwb, sink_ref, qmr_ref, qm_ref, km_ref, vm_ref, kx_ref, vx_ref,
                   qcr_ref, qc_ref, kc_ref, vc_ref, kcx_ref, vcx_ref,
                   qdr_ref, qd_ref, kdx_ref, vdx_ref, *rest):
    kd_refs, vd_refs = rest[:nwb], rest[nwb:2 * nwb]
    y_ref, kmag_scr = rest[2 * nwb:]
    i = pl.program_id(1)
    wk = nwb * LANES
    kpos = _window_block(i, tq, t, nwb) * LANES + lax.broadcasted_iota(jnp.int32, (wk, 2 * tq), 0)
    qpos = i * tq + lax.broadcasted_iota(jnp.int32, (wk, 2 * tq), 1) % tq
    valid = jnp.abs(kpos - qpos) <= WINDOW


    @pl.when(i == 0)
    def _():
        for h in range(MLA_HEADS):
            c = slice(h * LANES, (h + 1) * LANES)
            kmag_scr[h:h + 1, :] = _key_mags(km_ref[:, c])
            kmag_scr[MLA_HEADS + h:MLA_HEADS + h + 1, :] = _key_mags(kx_ref[:, c])
        kmag_scr[8:9, :] = _key_mags(kc_ref[...])
        kmag_scr[9:10, :] = _key_mags(kcx_ref[...])
        kmag_scr[10:11, :] = _key_mags(kdx_ref[...])
        kmag_scr[11:, :] = jnp.zeros((kmag_scr.shape[0] - 11, LANES), F32)

    def run(bounded):
        dens = []
        row = lambda r: kmag_scr[r:r + 1, :]

        def stage1(g):
            if g < MLA_HEADS:
                return _mla_scores(g, [qmr_ref, qm_ref], [km_ref, kx_ref],
                                   [row(g), row(MLA_HEADS + g)] if bounded else None)
            if g < MLA_HEADS + 2:
                return _gqa_scores(g % 2, [qcr_ref, qc_ref], [kc_ref, kcx_ref],
                                   [row(8), row(9)] if bounded else None)
            kw = jnp.concatenate([r[...] for r in kd_refs], axis=0)
            (s1, s2), bound = _gqa_scores(g % 2, [qdr_ref, qd_ref], [kw, kdx_ref],
                                          [_key_mags(kw), row(10)] if bounded else None)
            return [jnp.where(valid, s1, NEG_INF), s2], bound

        def stage2(g, scored):
            s, bound = scored
            j = g % 2
            hs = slice(j * HEAD_DIM, (j + 1) * HEAD_DIM)
            if g < MLA_HEADS:
                hs = slice(g * HEAD_DIM, (g + 1) * HEAD_DIM)
                o, den = _softmax_pv_t(s, [vm_ref[hs, :], vx_ref[hs, :]], True, None, bound)
            elif g < MLA_HEADS + 2:
                o, den = _softmax_pv_t(s, [vc_ref[hs, :], vcx_ref[hs, :]], j == 0, None, bound)
            else:
                vw = jnp.concatenate([r[hs, :] for r in vd_refs], axis=1)
                o, den = _softmax_pv_t(s, [vw, vdx_ref[hs, :]], j == 0,
                                       _sink_row(sink_ref, l, 2 * j, 2 * j + 1, tq), bound)
            dens.append(den)
            return o

        _store_groups(y_ref, slice(None), _run_groups(stage1, stage2, N_GROUPS), tq)
        return dens

    @pl.when(_underflowed(run(True)))
    def _():
        run(False)


def _attn_lat(l, nbatch, t, sink, qmr, qm, km, vmt, kx, vxt, qcr, qc, kc, vct, kcx, vcxt, qdr, qd, kd, vdt,
              kdx, vdxt):
    tq = 256
    nq = t // tq
    nwb = (tq + 2 * WINDOW) // LANES
    nkb = t // LANES
    rows = qm.shape[0]
    qs = lambda a: pl.BlockSpec((tq, a.shape[1]), lambda b, i: (b * nq + i, 0))
    ks = lambda a: pl.BlockSpec((t, a.shape[1]), lambda b, i: (b, 0))
    kts = lambda a: pl.BlockSpec((a.shape[0], t), lambda b, i: (0, b))
    xs = lambda a: pl.BlockSpec((None, None) + a.shape[2:], lambda b, i: (l, b, 0, 0))
    cs = lambda a: pl.BlockSpec((None, None) + a.shape[2:], lambda b, i: (b, l, 0, 0))
    wblk = lambda b, i, k: b * nkb + _window_block(i, tq, t, nwb) + k
    kd_specs = [pl.BlockSpec((LANES, LANES), lambda b, i, k=k: (wblk(b, i, k), 0)) for k in range(nwb)]
    vd_specs = [pl.BlockSpec((LANES, LANES), lambda b, i, k=k: (0, wblk(b, i, k))) for k in range(nwb)]
    in_specs = [pl.BlockSpec(memory_space=pltpu.SMEM),
                qs(qmr), qs(qm), ks(km), kts(vmt), xs(kx), xs(vxt),
                qs(qcr), qs(qc), ks(kc), kts(vct), cs(kcx), cs(vcxt),
                qs(qdr), qs(qd), cs(kdx), cs(vdxt)] + kd_specs + vd_specs
    return pl.pallas_call(
        functools.partial(_attn_lat_body, l, tq, t, nwb),
        grid=(nbatch, nq),
        in_specs=in_specs,
        out_specs=pl.BlockSpec((tq, 6 * LANES), lambda b, i: (b * nq + i, 0)),
        out_shape=jax.ShapeDtypeStruct((rows, 6 * LANES), BF16),
        scratch_shapes=[pltpu.VMEM((16, LANES), F32)],
        compiler_params=pltpu.CompilerParams(
            dimension_semantics=("arbitrary", "arbitrary"), vmem_limit_bytes=VMEM_LIMIT),
        name="attn_lat",
    )(sink, qmr, qm, km, vmt, kx, vxt, qcr, qc, kc, vct, kcx, vcxt, qdr, qd, kdx, vdxt,
      *([kd] * nwb), *([vdt] * nwb))


def _post_body(seq_len, final, mod_ref, n2_ref, cwa_ref, wo_ref, cwf_ref, wup_ref, wdn_ref, fn_ref, *rest):
    o_ref, hbuf, gbuf, pbuf, abuf = rest[-5:]
    ins = rest[:-5]
    r = o_ref.shape[0]
    edges = seq_len > r
    assert edges or r % seq_len == 0
    if edges:
        cat = lambda k: jnp.concatenate([ins[3 * k][...], ins[3 * k + 1][...], ins[3 * k + 2][...]], axis=0)
    else:
        cat = lambda k: ins[k][...]
    halo = BF16_ROWS if edges else 0
    n = r + 2 * halo
    grp = r // 8
    nchunk = D_MODEL // LANES
    i = pl.program_id(0)

    zg = cat(1).astype(F32)
    zz, gb = zg[:, :GROUP_WIDTH], zg[:, GROUP_WIDTH:]
    pos = (i * r - halo + lax.broadcasted_iota(jnp.int32, (n, 1), 0)) % seq_len
    prev = jnp.where(pos == 0, 0.0, pltpu.roll(zz, 1, 0))
    nxt = jnp.where(pos == seq_len - 1, 0.0, pltpu.roll(zz, n - 1, 0))
    cwa = cwa_ref[...]
    ya = gb * (prev * cwa[0:1] + zz * cwa[1:2] + nxt * cwa[2:3])
    ycat = jnp.concatenate([ya.astype(BF16), cat(2)], axis=1)
    x1 = cat(0) + mod_ref[2:3, :] * _dot(ycat, wo_ref[...])
    h2 = _rms(x1, n2_ref[...], D_MODEL) * (1.0 + mod_ref[4:5, :]) + mod_ref[3:4, :]
    x1c = x1[halo:halo + r]

    for c in range(nchunk):
        for s in range(8):
            pbuf[c, pl.ds(s, grp, stride=8), :] = h2[halo + s * grp:halo + (s + 1) * grp, c * LANES:(c + 1) * LANES]
    if edges:
        hrow = lax.broadcasted_iota(jnp.int32, (halo, 1), 0)
        edge = jnp.where(hrow == 0, h2[halo - 1:halo], jnp.where(hrow == 1, h2[halo + r:halo + r + 1], 0.0))
        hbuf[0:halo, :] = edge.astype(BF16)
    hbuf[halo:, :] = jnp.concatenate([pbuf[c] for c in range(nchunk)], axis=1).astype(BF16)

    sub = lax.broadcasted_iota(jnp.int32, (8, 1), 0)
    seq_first = (i * r + sub * grp) % seq_len == 0
    seq_last = (i * r + sub * grp + grp - 1) % seq_len == seq_len - 1

    def up(j):
        ga = slice(j * FF_TILE, (j + 1) * FF_TILE)
        va = slice(D_FF + j * FF_TILE, D_FF + (j + 1) * FF_TILE)
        return _dot(hbuf[...], wup_ref[:, ga]), _dot(hbuf[...], wup_ref[:, va])

    def conv(u, cw):
        head = pltpu.roll(u[halo + r - 8:halo + r], 1, 0)
        tail = pltpu.roll(u[halo:halo + 8], 7, 0)
        if edges:
            head = jnp.where(sub == 0, u[0:8], head)
            tail = jnp.where(sub == 7, pltpu.roll(u[0:8], 6, 0), tail)
        prev = jnp.concatenate([jnp.where(seq_first, 0.0, head), u[halo:halo + r - 8]], axis=0)
        nxt = jnp.concatenate([u[halo + 8:halo + r], jnp.where(seq_last, 0.0, tail)], axis=0)
        return prev * cw[0:1] + u[halo:halo + r] * cw[1:2] + nxt * cw[2:3]

    acc = None
    nxt_u = up(0)
    for j in range(N_FF_TILES):
        ua, ub = nxt_u
        if j + 1 < N_FF_TILES:
            nxt_u = up(j + 1)
        ga = slice(j * FF_TILE, (j + 1) * FF_TILE)
        va = slice(D_FF + j * FF_TILE, D_FF + (j + 1) * FF_TILE)
        ua = conv(ua, cwf_ref[:, ga])
        ub = conv(ub, cwf_ref[:, va])
        gbuf[:, ga] = (ua * jax.nn.sigmoid(ua) * ub).astype(BF16)
        if j % DOWN_CHUNK == DOWN_CHUNK - 1 or j == N_FF_TILES - 1:
            lo = (j // DOWN_CHUNK) * DOWN_CHUNK * FF_TILE
            hi = (j + 1) * FF_TILE
            d = _dot(gbuf[:, lo:hi], wdn_ref[lo:hi, :])
            acc = d if acc is None else acc + d

    for c in range(nchunk):
        abuf[c] = acc[:, c * LANES:(c + 1) * LANES]
    g2 = mod_ref[5:6, :]
    for c in range(nchunk):
        cs = slice(c * LANES, (c + 1) * LANES)
        for s in range(8):
            rs = slice(s * grp, (s + 1) * grp)
            o_ref[rs, cs] = x1c[rs, cs] + g2[:, cs] * abuf[c, pl.ds(s, grp, stride=8), :]
    if final:
        o_ref[...] = _rms(o_ref[...], fn_ref[...], D_MODEL)


def _halo_specs(r, w, rows):
    nblk = rows // BF16_ROWS
    per = r // BF16_ROWS
    prev = pl.BlockSpec((BF16_ROWS, w), lambda i: (jnp.maximum(i * per - 1, 0), 0))
    cur = pl.BlockSpec((r, w), lambda i: (i, 0))
    nxt = pl.BlockSpec((BF16_ROWS, w), lambda i: (jnp.minimum((i + 1) * per, nblk - 1), 0))
    return prev, cur, nxt


def _cond_spec(l, seq_len, cond_base):
    r = ROW_TILE
    tiles_per_seq = max(seq_len // r, 1)
    step = 1 if seq_len >= r else 0
    return pl.BlockSpec((None, None, 6, D_MODEL),
                        lambda i: (l, cond_base + step * (i // tiles_per_seq), 0, 0))


def _post(x, mod, lw, l, seq_len, cond_base, zg, y, final_norm, final):
    rows = x.shape[0]
    r = ROW_TILE
    full = lambda a: pl.BlockSpec((None,) + a.shape[1:], lambda i: (l,) + (0,) * (a.ndim - 1),
                                  pipeline_mode=pl.Buffered(1))
    acts = [x, zg, y]
    edges = seq_len > r
    if edges:
        act_specs = [s for a in acts for s in _halo_specs(r, a.shape[1], rows)]
        act_args = [a for a in acts for _ in range(3)]
    else:
        act_specs = [pl.BlockSpec((r, a.shape[1]), lambda i: (i, 0)) for a in acts]
        act_args = acts
    return pl.pallas_call(
        functools.partial(_post_body, seq_len, final),
        grid=(rows // r,),
        in_specs=[_cond_spec(l, seq_len, cond_base), full(lw['norm2']), full(lw['conv_a']), full(lw['w_out']),
                  full(lw['conv_ff']), full(lw['w_up']), full(lw['w_down']),
                  pl.BlockSpec((1, D_MODEL), lambda i: (0, 0))] + act_specs,
        out_specs=pl.BlockSpec((r, D_MODEL), lambda i: (i, 0)),
        out_shape=jax.ShapeDtypeStruct((rows, D_MODEL), F32),
        scratch_shapes=[pltpu.VMEM((r + (BF16_ROWS if edges else 0), D_MODEL), BF16), pltpu.VMEM((r, D_FF), BF16),
                        pltpu.VMEM((D_MODEL // LANES, r, LANES), F32),
                        pltpu.VMEM((D_MODEL // LANES, r, LANES), F32)],
        compiler_params=pltpu.CompilerParams(
            dimension_semantics=("arbitrary",), vmem_limit_bytes=VMEM_LIMIT),
        name="post",
    )(mod, lw['norm2'], lw['conv_a'], lw['w_out'], lw['conv_ff'], lw['w_up'], lw['w_down'], final_norm,
      *act_args)


def _pad_cols(a, n):
    return jnp.pad(a, [(0, 0)] * (a.ndim - 1) + [(0, n - a.shape[-1])])


def _perm_heads(a, axis):
    h = [lax.slice_in_dim(a, k * HEAD_DIM, (k + 1) * HEAD_DIM, axis=axis) for k in range(4)]
    return jnp.concatenate([h[0], h[2], h[1], h[3]], axis=axis)


def _prep_weights(w_in, mla_wq_b, mla_wkv_b, w_out, w_up, conv_ff, w_down, norm1, norm2, conv_a,
                  mla_q_norm, mla_kv_norm, gqa_q_norm, gqa_k_norm):
    L = DEPTH
    a = w_in[..., 0:768]
    cq = _pad_cols(w_in[..., 768:960], 256)
    ckv = w_in[..., 960:1088]
    kpe = _pad_cols(w_in[..., 1088:1120], LANES)
    qc = _perm_heads(w_in[..., 1120:1376], 2)
    kvc = w_in[..., 1376:1632]
    qd = _perm_heads(w_in[..., 1632:1888], 2)
    kvd = w_in[..., 1888:2144]
    w_in_p = jnp.concatenate([a, cq, ckv, kpe, qc, kvc, qd, kvd], axis=-1).astype(BF16)

    wq = mla_wq_b.reshape(L, MLA_Q_LORA, MLA_HEADS, MLA_NOPE + MLA_ROPE)
    wq = jnp.concatenate([wq[..., MLA_NOPE:], wq[..., :MLA_NOPE],
                          jnp.zeros((L, MLA_Q_LORA, MLA_HEADS, LANES - MLA_NOPE - MLA_ROPE), F32)], axis=-1)
    wq = jnp.pad(wq.reshape(L, MLA_Q_LORA, MLA_HEADS * LANES), ((0, 0), (0, 256 - MLA_Q_LORA), (0, 0)))

    wkv = mla_wkv_b.reshape(L, MLA_KV_LORA, MLA_HEADS, MLA_NOPE + MLA_V)
    zk = jnp.zeros((L, MLA_KV_LORA, MLA_HEADS, MLA_ROPE), F32)
    wk = jnp.concatenate([zk, wkv[..., :MLA_NOPE], zk], axis=-1).reshape(L, MLA_KV_LORA, MLA_HEADS * LANES)
    wv = wkv[..., MLA_NOPE:].reshape(L, MLA_KV_LORA, MLA_HEADS * MLA_V)
    wkv_p = jnp.concatenate([wk, wv], axis=-1)

    wo = jnp.concatenate([w_out[:, 0:512], _perm_heads(w_out[:, 512:768], 1),
                          _perm_heads(w_out[:, 768:1024], 1)], axis=1)

    hm = jnp.asarray(np.kron(np.eye(2 * LANES // HEAD_DIM), np.ones((HEAD_DIM, HEAD_DIM))), BF16)
    return {
        'w_in': w_in_p, 'wq': wq.astype(BF16), 'wkv': wkv_p.astype(BF16), 'w_out': wo.astype(BF16),
        'w_up': w_up.astype(BF16), 'conv_ff': conv_ff, 'w_down': w_down.astype(BF16),
        'norm1': norm1.reshape(L, 1, D_MODEL), 'norm2': norm2.reshape(L, 1, D_MODEL), 'conv_a': conv_a,
        'gq': _pad_cols(mla_q_norm, 256).reshape(L, 1, 256), 'gkv': mla_kv_norm.reshape(L, 1, MLA_KV_LORA),
        'gqc': jnp.tile(gqa_q_norm, (1, 4)).reshape(L, 1, 256), 'gkc': jnp.tile(gqa_k_norm, (1, 2)).reshape(L, 1, 128),
        'hm': hm,
    }


def _rope_tables(t):
    rows = t // GRID_W
    row = np.repeat(np.arange(rows, dtype=np.float64), GRID_W)
    col = np.tile(np.arange(GRID_W, dtype=np.float64), rows)

    def tabs(dim):
        half = dim // 2
        inv = np.power(ROPE_THETA, -np.arange(0, half, 2, dtype=np.float64) / half)
        ar = row[:, None] * inv
        ac = col[:, None] * inv
        c = np.concatenate([np.cos(ar), np.cos(ar), np.cos(ac), np.cos(ac)], axis=1)
        s = np.concatenate([-np.sin(ar), np.sin(ar), -np.sin(ac), np.sin(ac)], axis=1)
        return c, s

    c64, s64 = tabs(HEAD_DIM)
    c32, s32 = tabs(MLA_ROPE)
    c64 = np.tile(c64, (1, 2))
    s64 = np.tile(s64, (1, 2))
    c32 = np.concatenate([c32, np.ones((t, LANES - MLA_ROPE))], axis=1)
    s32 = np.concatenate([s32, np.zeros((t, LANES - MLA_ROPE))], axis=1)
    return tuple(jnp.asarray(a.astype(np.float32)) for a in (c64, s64, c32, s32))


def kernel(x_prompt, x_sample, cache_mla_ckv, cache_mla_kpe, cache_gqa_k, cache_gqa_v, cache_swa_k, cache_swa_v,
           c, c_ctx, w_ada, b_ada, norm1, w_in, conv_a, mla_q_norm, mla_wq_b, mla_kv_norm, mla_wkv_b,
           gqa_q_norm, gqa_k_norm, swa_sink, w_out, norm2, w_up, conv_ff, w_down, final_norm):
    B, T, _ = x_prompt.shape
    DB, DT, _ = x_sample.shape
    past = cache_mla_ckv.shape[2]

    lw = _prep_weights(w_in, mla_wq_b, mla_wkv_b, w_out, w_up, conv_ff, w_down, norm1, norm2, conv_a,
                       mla_q_norm, mla_kv_norm, gqa_q_norm, gqa_k_norm)
    rope_tabs = _rope_tables(DT)
    fnorm = final_norm.reshape(1, D_MODEL)

    cond_t = jnp.concatenate([c_ctx[:, None], c.T, jnp.zeros((D_MODEL, 8 - 1 - DB), F32)], axis=1)
    mod = _ada(cond_t, 1 + DB, w_ada, b_ada).reshape(DEPTH, 8, 6, D_MODEL)

    kx, vx = _ctx_mla(cache_mla_ckv, _pad_cols(cache_mla_kpe, LANES), lw['wkv'])
    flat = lambda a: a.reshape(DB, DEPTH, past, 2 * HEAD_DIM).astype(BF16)
    kcx, kdx = flat(cache_gqa_k), flat(cache_swa_k)
    vcx, vdx = jnp.swapaxes(flat(cache_gqa_v), 2, 3), jnp.swapaxes(flat(cache_swa_v), 2, 3)

    xp = x_prompt.reshape(B * T, D_MODEL)
    xs = x_sample.reshape(DB * DT, D_MODEL)
    caches = None
    for l in range(DEPTH):
        final = l == DEPTH - 1
        (zg, qm, km, vm, qc, kc, vc, qd, kd, vd, *caches) = _inproj(xp, mod, lw, l, T, 0, None, caches)
        y = _attn_ctx(l, T, swa_sink, qm, km, vm, qc, kc, vc, qd, kd, vd)
        xp = _post(xp, mod, lw, l, T, 0, zg, y, fnorm, final)
        (zg, qmr, qm, km, vm, qcr, qc, kc, vc, qdr, qd, kd, vd) = _inproj(xs, mod, lw, l, DT, 1, rope_tabs)
        y = _attn_lat(l, DB, DT, swa_sink, qmr, qm, km, vm, kx, vx, qcr, qc, kc, vc, kcx, vcx,
                      qdr, qd, kd, vd, kdx, vdx)
        xs = _post(xs, mod, lw, l, DT, 1, zg, y, fnorm, final)

    heads = lambda a: a.reshape(B, DEPTH, T, 2, HEAD_DIM)
    return (xp.reshape(B, T, D_MODEL), xs.reshape(DB, DT, D_MODEL), caches[0], caches[1],
            heads(caches[2]), heads(caches[3]), heads(caches[4]), heads(caches[5]))
```

```python
import functools
import math

import jax
import jax.numpy as jnp
import numpy as np
from jax import lax
from jax.experimental import pallas as pl
from jax.experimental.pallas import tpu as pltpu

F32 = jnp.float32
BF16 = jnp.bfloat16

D_MODEL = 1024
DEPTH = 2
GRID_W = 64
HEAD_DIM = 64
GROUP_WIDTH = D_MODEL // 4
MLA_HEADS = 4
MLA_NOPE = 64
MLA_ROPE = 32
MLA_V = 64
MLA_Q_LORA = 192
MLA_KV_LORA = 128
WINDOW = 128
D_FF = 2816
ROPE_THETA = 10000.0
EPS = 1e-6
NEG_INF = -1e30
LOG2E = math.log2(math.e)
ATTN_SCALE = HEAD_DIM ** -0.5 * LOG2E
MLA_SCALE = (MLA_NOPE + MLA_ROPE) ** -0.5 * LOG2E

LANES = 128
BF16_ROWS = 16
ROW_TILE = 512
IN_ROW_TILE = 1024
IN_SUB_TILE = 256
FF_TILE = 256
N_FF_TILES = D_FF // FF_TILE
DOWN_CHUNK = 11
IN_COLS_PACKED = 2176
VMEM_LIMIT = 56 * 1024 * 1024

_NT = (((1,), (1,)), ((), ()))


def _dot(a, b):
    return jnp.dot(a, b, preferred_element_type=F32)


def _dot_nt(a, b):
    return lax.dot_general(a, b, _NT, preferred_element_type=F32)


def _rms(x, g, n):
    ms = jnp.sum(x * x, axis=-1, keepdims=True) * (1.0 / n)
    return x * lax.rsqrt(ms + EPS) * g


def _lane(shape):
    return lax.broadcasted_iota(jnp.int32, shape, len(shape) - 1)


def _rope(x, c, s, half):
    w = x.shape[-1]
    lo = (_lane(x.shape) % (2 * half)) < half
    sw = jnp.where(lo, pltpu.roll(x, w - half, 1), pltpu.roll(x, half, 1))
    return x * c + sw * s


def _shift_rows(zz, halo, rows):
    n = zz.shape[0]
    prev = pltpu.roll(zz, 1, 0)[halo:halo + rows]
    nxt = pltpu.roll(zz, n - 1, 0)[halo:halo + rows]
    return prev, nxt


def _seq_pos(tile, rows, seq_len):
    r = lax.broadcasted_iota(jnp.int32, (rows, 1), 0)
    return (tile * rows + r) % seq_len


def _ada_body(ncond, c_ref, w_ref, b_ref, o_ref):
    c = c_ref[...]
    s = c * jax.nn.sigmoid(c)
    tn = w_ref.shape[1]
    w = w_ref[...].reshape(D_MODEL // 8, 8, tn)
    rows = []
    for r in range(ncond):
        part = jnp.sum(w * s[:, r:r + 1].reshape(D_MODEL // 8, 8, 1), axis=0)
        rows.append(jnp.sum(part, axis=0, keepdims=True))
    rows.append(jnp.zeros((8 - ncond, tn), F32))
    o_ref[...] = jnp.concatenate(rows, axis=0) + b_ref[...]


def _ada(cond_t, ncond, w_ada, b_ada):
    tn = 1536
    n = 6 * D_MODEL
    return pl.pallas_call(
        functools.partial(_ada_body, ncond),
        grid=(DEPTH, n // tn),
        in_specs=[pl.BlockSpec((D_MODEL, 8), lambda l, j: (0, 0)),
                  pl.BlockSpec((None, D_MODEL, tn), lambda l, j: (l, 0, j)),
                  pl.BlockSpec((None, 1, tn), lambda l, j: (l, 0, j))],
        out_specs=pl.BlockSpec((None, 8, tn), lambda l, j: (l, 0, j)),
        out_shape=jax.ShapeDtypeStruct((DEPTH, 8, n), F32),
        compiler_params=pltpu.CompilerParams(
            dimension_semantics=("arbitrary", "arbitrary"), vmem_limit_bytes=VMEM_LIMIT),
        name="ada",
    )(cond_t, w_ada, b_ada.reshape(DEPTH, 1, n))


def _ctx_body(ckv_ref, kpe_ref, w_ref, kx_ref, vx_ref):
    kv = _dot(ckv_ref[...].astype(BF16), w_ref[...])
    kpe = kpe_ref[...]
    kx_ref[...] = (kv[:, :4 * LANES] + jnp.concatenate([kpe] * MLA_HEADS, axis=1)).astype(BF16)
    vx_ref[...] = kv[:, 4 * LANES:].T.astype(BF16)


def _ctx_mla(cache_ckv, cache_kpe_pad, wkv_p):
    b, _, s, _ = cache_ckv.shape
    return pl.pallas_call(
        _ctx_body,
        grid=(DEPTH, b),
        in_specs=[pl.BlockSpec((None, None, s, MLA_KV_LORA), lambda l, i: (i, l, 0, 0)),
                  pl.BlockSpec((None, None, s, LANES), lambda l, i: (i, l, 0, 0)),
                  pl.BlockSpec((None, MLA_KV_LORA, 6 * LANES), lambda l, i: (l, 0, 0))],
        out_specs=[pl.BlockSpec((None, None, s, 4 * LANES), lambda l, i: (l, i, 0, 0)),
                   pl.BlockSpec((None, None, 2 * LANES, s), lambda l, i: (l, i, 0, 0))],
        out_shape=[jax.ShapeDtypeStruct((DEPTH, b, s, 4 * LANES), BF16),
                   jax.ShapeDtypeStruct((DEPTH, b, 2 * LANES, s), BF16)],
        compiler_params=pltpu.CompilerParams(
            dimension_semantics=("arbitrary", "arbitrary"), vmem_limit_bytes=VMEM_LIMIT),
        name="ctx_mla",
    )(cache_ckv, cache_kpe_pad, wkv_p)


def _inproj_body(rope, n_alias, x_ref, mod_ref, n1_ref, w_ref, wq_ref, wkv_ref, gq_ref, gkv_ref, gqc_ref, gkc_ref,
                 hm_ref, *rest):
    if rope:
        c64_ref, s64_ref, c32_ref, s32_ref = rest[:4]
        outs = rest[4:]
    else:
        outs = rest[n_alias:]
    sh1 = mod_ref[0:1, :]
    sc1 = mod_ref[1:2, :]
    hm = hm_ref[...]
    sub = IN_SUB_TILE
    nsub = x_ref.shape[0] // sub

    def project(s):
        x = x_ref[s * sub:(s + 1) * sub, :]
        h = _rms(x, n1_ref[...], D_MODEL) * (1.0 + sc1) + sh1
        return _dot(h.astype(BF16), w_ref[...])

    def head_rms(v, g):
        w = v.shape[-1]
        ss = _dot((v * v).astype(BF16), hm[:w, :w])
        return v * lax.rsqrt(ss * (1.0 / HEAD_DIM) + EPS) * g

    def finish(s, acc):
        rs = slice(s * sub, (s + 1) * sub)
        xa, gb, gc = acc[:, 0:256], acc[:, 256:512], acc[:, 512:768]
        z = gc * xa
        blk = acc[:, 768:1024]
        cq = jnp.where(_lane(blk.shape) < MLA_Q_LORA, blk, 0.0)
        cqn = _rms(cq, gq_ref[...], MLA_Q_LORA)
        qm = _dot(cqn.astype(BF16), wq_ref[...]) * MLA_SCALE
        ckvn = _rms(acc[:, 1024:1152], gkv_ref[...], MLA_KV_LORA)
        kv = _dot(ckvn.astype(BF16), wkv_ref[...])
        kpe = pltpu.roll(blk[:, LANES:], 2 * LANES - MLA_Q_LORA, 1)
        kpe = jnp.where(_lane(kpe.shape) < MLA_ROPE, kpe, 0.0)
        vm = kv[:, 4 * LANES:]
        qc = head_rms(acc[:, 1152:1408], gqc_ref[...]) * ATTN_SCALE
        kc = head_rms(acc[:, 1408:1536], gkc_ref[...])
        vc = acc[:, 1536:1664]
        qd = acc[:, 1664:1920] * ATTN_SCALE
        kd = acc[:, 1920:2048]
        vd = acc[:, 2048:2176]

        if not rope:
            (zg_o, qm_o, km_o, vm_o, qc_o, kc_o, vc_o, qd_o, kd_o, vd_o,
             ckv_c, kpe_c, kc_c, vc_c, kd_c, vd_c) = outs
            km = kv[:, :4 * LANES] + jnp.concatenate([kpe] * MLA_HEADS, axis=1)
            zg_o[rs, 0:256] = z.astype(BF16)
            zg_o[rs, 256:512] = gb.astype(BF16)
            qm_o[rs, :] = qm.astype(BF16)
            km_o[rs, :] = km.astype(BF16)
            vm_o[:, rs] = vm.T.astype(BF16)
            qc_o[rs, :] = qc.astype(BF16)
            kc_o[rs, :] = kc.astype(BF16)
            vc_o[:, rs] = vc.T.astype(BF16)
            qd_o[rs, :] = qd.astype(BF16)
            kd_o[rs, :] = kd.astype(BF16)
            vd_o[:, rs] = vd.T.astype(BF16)
            for o, v in ((ckv_c, ckvn), (kpe_c, kpe[:, :MLA_ROPE]), (kc_c, kc), (vc_c, vc), (kd_c, kd),
                         (vd_c, vd)):
                t = o.shape[-2]
                bs = slice(s * sub // t, (s + 1) * sub // t)
                v = v.reshape(sub // t, t, v.shape[-1])
                if n_alias:
                    o[bs] = v
                else:
                    o[bs, 0] = v
                    o[bs, 1:] = jnp.zeros((sub // t, o.shape[1] - 1) + v.shape[1:], F32)
        else:
            (zg_o, qmr_o, qm_o, km_o, vm_o, qcr_o, qc_o, kc_o, vc_o, qdr_o, qd_o, kd_o, vd_o) = outs
            c64, s64, c32, s32 = c64_ref[rs, :], s64_ref[rs, :], c32_ref[rs, :], s32_ref[rs, :]
            rope64 = lambda v: jnp.concatenate(
                [_rope(v[:, i:i + LANES], c64, s64, 16) for i in range(0, v.shape[-1], LANES)], axis=1)
            rope32 = lambda v: jnp.concatenate(
                [_rope(v[:, i:i + LANES], c32, s32, 8) for i in range(0, v.shape[-1], LANES)], axis=1)
            km = kv[:, :4 * LANES] + jnp.concatenate([_rope(kpe, c32, s32, 8)] * MLA_HEADS, axis=1)
            zg_o[rs, 0:256] = z.astype(BF16)
            zg_o[rs, 256:512] = gb.astype(BF16)
            qmr_o[rs, :] = rope32(qm).astype(BF16)
            qm_o[rs, :] = qm.astype(BF16)
            km_o[rs, :] = km.astype(BF16)
            vm_o[:, rs] = vm.T.astype(BF16)
            qcr_o[rs, :] = rope64(qc).astype(BF16)
            qc_o[rs, :] = qc.astype(BF16)
            kc_o[rs, :] = rope64(kc).astype(BF16)
            vc_o[:, rs] = vc.T.astype(BF16)
            qdr_o[rs, :] = rope64(qd).astype(BF16)
            qd_o[rs, :] = qd.astype(BF16)
            kd_o[rs, :] = rope64(kd).astype(BF16)
            vd_o[:, rs] = vd.T.astype(BF16)

    nxt = project(0)
    for s in range(nsub):
        acc = nxt
        if s + 1 < nsub:
            nxt = project(s + 1)
        finish(s, acc)


def _inproj(x, mod, lw, l, seq_len, cond_base, rope_tabs, prev_caches=None):
    rows = x.shape[0]
    r = IN_ROW_TILE
    rope = rope_tabs is not None
    aliases = {}
    tiles_per_seq = max(seq_len // r, 1)
    step = 1 if seq_len >= r else 0

    def cond_map(i):
        return (l, cond_base + step * (i // tiles_per_seq), 0, 0)

    row = lambda w: pl.BlockSpec((r, w), lambda i: (i, 0))
    full = lambda a: pl.BlockSpec((None,) + a.shape[1:], lambda i: (l,) + (0,) * (a.ndim - 1))
    in_specs = [row(D_MODEL),
                pl.BlockSpec((None, None, 6, D_MODEL), cond_map),
                full(lw['norm1']), full(lw['w_in']), full(lw['wq']), full(lw['wkv']),
                full(lw['gq']), full(lw['gkv']), full(lw['gqc']), full(lw['gkc']),
                pl.BlockSpec((2 * LANES, 2 * LANES), lambda i: (0, 0))]
    args = [x, mod, lw['norm1'], lw['w_in'], lw['wq'], lw['wkv'], lw['gq'], lw['gkv'], lw['gqc'], lw['gkc'],
            lw['hm']]
    def act(w, transposed):
        if transposed:
            return jax.ShapeDtypeStruct((w, rows), BF16), pl.BlockSpec((w, r), lambda i: (0, i))
        return jax.ShapeDtypeStruct((rows, w), BF16), row(w)

    if rope:
        tab = pl.BlockSpec((r, LANES), lambda i: (i % tiles_per_seq, 0))
        in_specs += [tab] * 4
        args += list(rope_tabs)
        widths = [(512, 0), (512, 0), (512, 0), (512, 0), (256, 1), (256, 0), (256, 0), (128, 0),
                  (128, 1), (256, 0), (256, 0), (128, 0), (128, 1)]
        out_shape, out_specs = map(list, zip(*[act(w, t) for w, t in widths]))
    else:
        widths = [(512, 0), (512, 0), (512, 0), (256, 1), (256, 0), (128, 0), (128, 1), (256, 0),
                  (128, 0), (128, 1)]
        cwidths = [128, MLA_ROPE, 128, 128, 128, 128]
        nb = r // seq_len
        out_shape, out_specs = map(list, zip(*[act(w, t) for w, t in widths]))
        out_shape += [jax.ShapeDtypeStruct((rows // seq_len, DEPTH, seq_len, w), F32) for w in cwidths]
        assert (prev_caches is None) == (l == 0)
        if l == 0:
            out_specs += [pl.BlockSpec((nb, DEPTH, seq_len, w), lambda i: (i, 0, 0, 0)) for w in cwidths]
        else:
            out_specs += [pl.BlockSpec((nb, None, seq_len, w), lambda i: (i, l, 0, 0)) for w in cwidths]
        if prev_caches is not None:
            aliases = {len(args) + k: len(widths) + k for k in range(len(cwidths))}
            in_specs += [pl.BlockSpec(memory_space=pl.ANY)] * len(cwidths)
            args += list(prev_caches)
    return pl.pallas_call(
        functools.partial(_inproj_body, rope, len(aliases)),
        grid=(rows // r,),
        in_specs=in_specs, out_specs=out_specs, out_shape=out_shape,
        input_output_aliases=aliases,
        compiler_params=pltpu.CompilerParams(
            dimension_semantics=("arbitrary",), vmem_limit_bytes=VMEM_LIMIT),
        name="inproj_rope" if rope else "inproj",
    )(*args)


DEN_FLOOR = 2.0 ** -80


def _key_mags(k):
    return jnp.max(jnp.abs(k), axis=0, keepdims=True).astype(F32)


def _score_bound(kmag, qmag):
    col = jnp.sum(qmag.astype(F32) * kmag, axis=1, keepdims=True)
    return jnp.broadcast_to(col, qmag.shape).T[0:1]


def _softmax_pv_t(s_list, vt_list, first, extra=None, bound=None):
    if bound is None:
        m = s_list[0].max(axis=0, keepdims=True)
        for s in s_list[1:]:
            m = jnp.maximum(m, s.max(axis=0, keepdims=True))
    else:
        m = bound
    if extra is not None:
        m = jnp.maximum(m, extra)
    out = None
    for s, vt in zip(s_list, vt_list):
        ones = jnp.ones(vt.shape, BF16)
        vt = jnp.concatenate([vt, ones] if first else [ones, vt], axis=0)
        pv = _dot(vt, jnp.exp2(s - m).astype(BF16))
        out = pv if out is None else out + pv
    den = out[HEAD_DIM:HEAD_DIM + 1] if first else out[0:1]
    if extra is not None:
        den = den + jnp.exp2(extra - m)
    return out * (1.0 / den), den


def _underflowed(dens):
    width = max(d.shape[1] for d in dens)
    low = None
    for d in dens:
        d = jnp.concatenate([d] * (width // d.shape[1]), axis=1)
        low = d if low is None else jnp.minimum(low, d)
    return jnp.logical_not(jnp.min(low) >= DEN_FLOOR)


def _half_masks(q):
    lo = _lane(q.shape) < HEAD_DIM
    zero = jnp.zeros_like(q)
    return jnp.where(lo, q, zero), jnp.where(lo, zero, q)


def _merge_rows(a, b):
    r = lax.broadcasted_iota(jnp.int32, a.shape, 0)
    return jnp.where(r < HEAD_DIM, a, b)


def _sink_row(sink_ref, l, h0, h1, tq):
    c = lax.broadcasted_iota(jnp.int32, (1, 2 * tq), 1)
    return jnp.where(c < tq, sink_ref[l, h0], sink_ref[l, h1]) * LOG2E


SCORE_AHEAD = 4


def _run_groups(stage1, stage2, n, ahead=SCORE_AHEAD):
    out = []
    pending = [stage1(g) for g in range(min(ahead, n))]
    for g in range(n):
        if g + ahead < n:
            pending.append(stage1(g + ahead))
        out.append(stage2(g, pending.pop(0)))
    return out


def _scores(pairs, kmags):
    s = [_dot_nt(k, q) for k, q in pairs]
    b = None
    if kmags is not None:
        kmag, qmag = None, None
        for (_, q), km in zip(pairs, kmags):
            kmag = km if kmag is None else jnp.maximum(kmag, km)
            qmag = jnp.abs(q) if qmag is None else jnp.maximum(qmag, jnp.abs(q))
        b = _score_bound(kmag, qmag)
    return s, b


def _mla_scores(h, q_sets, k_sets, kmags):
    c = slice(h * LANES, (h + 1) * LANES)
    return _scores([(k[:, c], q[:, c]) for q, k in zip(q_sets, k_sets)], kmags)


def _gqa_scores(j, q_sets, k_sets, kmags):
    pairs = []
    for q, k in zip(q_sets, k_sets):
        qa = _half_masks(q[:, 0:LANES])[j]
        qb = _half_masks(q[:, LANES:])[j]
        pairs.append((k[...], jnp.concatenate([qa, qb], axis=0)))
    return _scores(pairs, kmags)


def _store_pair(y_ref, rows, blk, ot):
    y_ref[rows, blk * LANES:(blk + 1) * LANES] = ot.T.astype(BF16)


N_GROUPS = MLA_HEADS + 4


def _store_groups(y_ref, rows, o, tq):
    for p in range(MLA_HEADS // 2):
        _store_pair(y_ref, rows, p, jnp.concatenate([o[2 * p][:HEAD_DIM], o[2 * p + 1][:HEAD_DIM]], axis=0))
    for base, (o0, o1) in ((2, o[4:6]), (4, o[6:8])):
        _store_pair(y_ref, rows, base, _merge_rows(o0[:, :tq], o1[:, :tq]))
        _store_pair(y_ref, rows, base + 1, _merge_rows(o0[:, tq:], o1[:, tq:]))


def _attn_ctx_body(l, nb, t, sink_ref, qm_ref, km_ref, vm_ref, qc_ref, kc_ref, vc_ref, qd_ref, kd_ref, vd_ref,
                   y_ref):
    def stage1(g):
        b, k = divmod(g, N_GROUPS)
        rs = pl.ds(b * t, t)
        if k < MLA_HEADS:
            return _mla_scores(k, [qm_ref.at[rs]], [km_ref.at[rs]], None)
        q_ref, k_ref = (qc_ref, kc_ref) if k < MLA_HEADS + 2 else (qd_ref, kd_ref)
        return _gqa_scores(k % 2, [q_ref.at[rs]], [k_ref.at[rs]], None)

    def stage2(g, scored):
        s, _ = scored
        b, k = divmod(g, N_GROUPS)
        rs = slice(b * t, (b + 1) * t)
        j = k % 2
        hs = slice(j * HEAD_DIM, (j + 1) * HEAD_DIM)
        if k < MLA_HEADS:
            return _softmax_pv_t(s, [vm_ref[k * HEAD_DIM:(k + 1) * HEAD_DIM, rs]], True)[0]
        if k < MLA_HEADS + 2:
            return _softmax_pv_t(s, [vc_ref[hs, rs]], j == 0)[0]
        return _softmax_pv_t(s, [vd_ref[hs, rs]], j == 0, _sink_row(sink_ref, l, 2 * j, 2 * j + 1, t))[0]

    o = _run_groups(stage1, stage2, N_GROUPS * nb)
    for b in range(nb):
        _store_groups(y_ref, slice(b * t, (b + 1) * t), o[N_GROUPS * b:N_GROUPS * (b + 1)], t)


def _attn_ctx(l, t, sink, qm, km, vm, qc, kc, vc, qd, kd, vd):
    rows = qm.shape[0]
    nb = 4
    r = nb * t
    row = lambda a: (pl.BlockSpec((a.shape[0], r), lambda i: (0, i)) if a.shape[1] == rows
                     else pl.BlockSpec((r, a.shape[1]), lambda i: (i, 0)))
    ins = [qm, km, vm, qc, kc, vc, qd, kd, vd]
    return pl.pallas_call(
        functools.partial(_attn_ctx_body, l, nb, t),
        grid=(rows // r,),
        in_specs=[pl.BlockSpec(memory_space=pltpu.SMEM)] + [row(a) for a in ins],
        out_specs=pl.BlockSpec((r, 6 * LANES), lambda i: (i, 0)),
        out_shape=jax.ShapeDtypeStruct((rows, 6 * LANES), BF16),
        compiler_params=pltpu.CompilerParams(
            dimension_semantics=("arbitrary",), vmem_limit_bytes=VMEM_LIMIT),
        name="attn_ctx",
    )(sink, *ins)


def _window_block(i, tq, t, nwb):
    return jnp.clip(i * (tq // LANES) - WINDOW // LANES, 0, t // LANES - nwb)


def _attn_lat_body(l, tq, t, nwb, sink_ref, qmr_ref, qm_ref, km_ref, vm_ref, kx_ref, vx_ref,
                   qcr_ref, qc_ref, kc_ref, vc_ref, kcx_ref, vcx_ref,
                   qdr_ref, qd_ref, kdx_ref, vdx_ref, *rest):
    kd_refs, vd_refs = rest[:nwb], rest[nwb:2 * nwb]
    y_ref, kmag_scr = rest[2 * nwb:]
    i = pl.program_id(1)
    wk = nwb * LANES
    kpos = _window_block(i, tq, t, nwb) * LANES + lax.broadcasted_iota(jnp.int32, (wk, 2 * tq), 0)
    qpos = i * tq + lax.broadcasted_iota(jnp.int32, (wk, 2 * tq), 1) % tq
    valid = jnp.abs(kpos - qpos) <= WINDOW


    @pl.when(i == 0)
    def _():
        for h in range(MLA_HEADS):
            c = slice(h * LANES, (h + 1) * LANES)
            kmag_scr[h:h + 1, :] = _key_mags(km_ref[:, c])
            kmag_scr[MLA_HEADS + h:MLA_HEADS + h + 1, :] = _key_mags(kx_ref[:, c])
        kmag_scr[8:9, :] = _key_mags(kc_ref[...])
        kmag_scr[9:10, :] = _key_mags(kcx_ref[...])
        kmag_scr[10:11, :] = _key_mags(kdx_ref[...])
        kmag_scr[11:, :] = jnp.zeros((kmag_scr.shape[0] - 11, LANES), F32)

    def run(bounded):
        dens = []
        row = lambda r: kmag_scr[r:r + 1, :]

        def stage1(g):
            if g < MLA_HEADS:
                return _mla_scores(g, [qmr_ref, qm_ref], [km_ref, kx_ref],
                                   [row(g), row(MLA_HEADS + g)] if bounded else None)
            if g < MLA_HEADS + 2:
                return _gqa_scores(g % 2, [qcr_ref, qc_ref], [kc_ref, kcx_ref],
                                   [row(8), row(9)] if bounded else None)
            kw = jnp.concatenate([r[...] for r in kd_refs], axis=0)
            (s1, s2), bound = _gqa_scores(g % 2, [qdr_ref, qd_ref], [kw, kdx_ref],
                                          [_key_mags(kw), row(10)] if bounded else None)
            return [jnp.where(valid, s1, NEG_INF), s2], bound

        def stage2(g, scored):
            s, bound = scored
            j = g % 2
            hs = slice(j * HEAD_DIM, (j + 1) * HEAD_DIM)
            if g < MLA_HEADS:
                hs = slice(g * HEAD_DIM, (g + 1) * HEAD_DIM)
                o, den = _softmax_pv_t(s, [vm_ref[hs, :], vx_ref[hs, :]], True, None, bound)
            elif g < MLA_HEADS + 2:
                o, den = _softmax_pv_t(s, [vc_ref[hs, :], vcx_ref[hs, :]], j == 0, None, bound)
            else:
                vw = jnp.concatenate([r[hs, :] for r in vd_refs], axis=1)
                o, den = _softmax_pv_t(s, [vw, vdx_ref[hs, :]], j == 0,
                                       _sink_row(sink_ref, l, 2 * j, 2 * j + 1, tq), bound)
            dens.append(den)
            return o

        _store_groups(y_ref, slice(None), _run_groups(stage1, stage2, N_GROUPS), tq)
        return dens

    @pl.when(_underflowed(run(True)))
    def _():
        run(False)


def _attn_lat(l, nbatch, t, sink, qmr, qm, km, vmt, kx, vxt, qcr, qc, kc, vct, kcx, vcxt, qdr, qd, kd, vdt,
              kdx, vdxt):
    tq = 256
    nq = t // tq
    nwb = (tq + 2 * WINDOW) // LANES
    nkb = t // LANES
    rows = qm.shape[0]
    qs = lambda a: pl.BlockSpec((tq, a.shape[1]), lambda b, i: (b * nq + i, 0))
    ks = lambda a: pl.BlockSpec((t, a.shape[1]), lambda b, i: (b, 0))
    kts = lambda a: pl.BlockSpec((a.shape[0], t), lambda b, i: (0, b))
    xs = lambda a: pl.BlockSpec((None, None) + a.shape[2:], lambda b, i: (l, b, 0, 0))
    cs = lambda a: pl.BlockSpec((None, None) + a.shape[2:], lambda b, i: (b, l, 0, 0))
    wblk = lambda b, i, k: b * nkb + _window_block(i, tq, t, nwb) + k
    kd_specs = [pl.BlockSpec((LANES, LANES), lambda b, i, k=k: (wblk(b, i, k), 0)) for k in range(nwb)]
    vd_specs = [pl.BlockSpec((LANES, LANES), lambda b, i, k=k: (0, wblk(b, i, k))) for k in range(nwb)]
    in_specs = [pl.BlockSpec(memory_space=pltpu.SMEM),
                qs(qmr), qs(qm), ks(km), kts(vmt), xs(kx), xs(vxt),
                qs(qcr), qs(qc), ks(kc), kts(vct), cs(kcx), cs(vcxt),
                qs(qdr), qs(qd), cs(kdx), cs(vdxt)] + kd_specs + vd_specs
    return pl.pallas_call(
        functools.partial(_attn_lat_body, l, tq, t, nwb),
        grid=(nbatch, nq),
        in_specs=in_specs,
        out_specs=pl.BlockSpec((tq, 6 * LANES), lambda b, i: (b * nq + i, 0)),
        out_shape=jax.ShapeDtypeStruct((rows, 6 * LANES), BF16),
        scratch_shapes=[pltpu.VMEM((16, LANES), F32)],
        compiler_params=pltpu.CompilerParams(
            dimension_semantics=("arbitrary", "arbitrary"), vmem_limit_bytes=VMEM_LIMIT),
        name="attn_lat",
    )(sink, qmr, qm, km, vmt, kx, vxt, qcr, qc, kc, vct, kcx, vcxt, qdr, qd, kdx, vdxt,
      *([kd] * nwb), *([vdt] * nwb))


def _post_body(seq_len, final, mod_ref, n2_ref, cwa_ref, wo_ref, cwf_ref, wup_ref, wdn_ref, fn_ref, *rest):
    o_ref, hbuf, gbuf, pbuf, abuf = rest[-5:]
    ins = rest[:-5]
    r = o_ref.shape[0]
    edges = seq_len > r
    assert edges or r % seq_len == 0
    if edges:
        cat = lambda k: jnp.concatenate([ins[3 * k][...], ins[3 * k + 1][...], ins[3 * k + 2][...]], axis=0)
    else:
        cat = lambda k: ins[k][...]
    halo = BF16_ROWS if edges else 0
    n = r + 2 * halo
    grp = r // 8
    nchunk = D_MODEL // LANES
    i = pl.program_id(0)

    zg = cat(1).astype(F32)
    zz, gb = zg[:, :GROUP_WIDTH], zg[:, GROUP_WIDTH:]
    pos = (i * r - halo + lax.broadcasted_iota(jnp.int32, (n, 1), 0)) % seq_len
    prev = jnp.where(pos == 0, 0.0, pltpu.roll(zz, 1, 0))
    nxt = jnp.where(pos == seq_len - 1, 0.0, pltpu.roll(zz, n - 1, 0))
    cwa = cwa_ref[...]
    ya = gb * (prev * cwa[0:1] + zz * cwa[1:2] + nxt * cwa[2:3])
    ycat = jnp.concatenate([ya.astype(BF16), cat(2)], axis=1)
    x1 = cat(0) + mod_ref[2:3, :] * _dot(ycat, wo_ref[...])
    h2 = _rms(x1, n2_ref[...], D_MODEL) * (1.0 + mod_ref[4:5, :]) + mod_ref[3:4, :]
    x1c = x1[halo:halo + r]

    for c in range(nchunk):
        for s in range(8):
            pbuf[c, pl.ds(s, grp, stride=8), :] = h2[halo + s * grp:halo + (s + 1) * grp, c * LANES:(c + 1) * LANES]
    if edges:
        hrow = lax.broadcasted_iota(jnp.int32, (halo, 1), 0)
        edge = jnp.where(hrow == 0, h2[halo - 1:halo], jnp.where(hrow == 1, h2[halo + r:halo + r + 1], 0.0))
        hbuf[0:halo, :] = edge.astype(BF16)
    hbuf[halo:, :] = jnp.concatenate([pbuf[c] for c in range(nchunk)], axis=1).astype(BF16)

    sub = lax.broadcasted_iota(jnp.int32, (8, 1), 0)
    seq_first = (i * r + sub * grp) % seq_len == 0
    seq_last = (i * r + sub * grp + grp - 1) % seq_len == seq_len - 1

    def up(j):
        ga = slice(j * FF_TILE, (j + 1) * FF_TILE)
        va = slice(D_FF + j * FF_TILE, D_FF + (j + 1) * FF_TILE)
        return _dot(hbuf[...], wup_ref[:, ga]), _dot(hbuf[...], wup_ref[:, va])

    def conv(u, cw):
        head = pltpu.roll(u[halo + r - 8:halo + r], 1, 0)
        tail = pltpu.roll(u[halo:halo + 8], 7, 0)
        if edges:
            head = jnp.where(sub == 0, u[0:8], head)
            tail = jnp.where(sub == 7, pltpu.roll(u[0:8], 6, 0), tail)
        prev = jnp.concatenate([jnp.where(seq_first, 0.0, head), u[halo:halo + r - 8]], axis=0)
        nxt = jnp.concatenate([u[halo + 8:halo + r], jnp.where(seq_last, 0.0, tail)], axis=0)
        return prev * cw[0:1] + u[halo:halo + r] * cw[1:2] + nxt * cw[2:3]

    acc = None
    nxt_u = up(0)
    for j in range(N_FF_TILES):
        ua, ub = nxt_u
        if j + 1 < N_FF_TILES:
            nxt_u = up(j + 1)
        ga = slice(j * FF_TILE, (j + 1) * FF_TILE)
        va = slice(D_FF + j * FF_TILE, D_FF + (j + 1) * FF_TILE)
        ua = conv(ua, cwf_ref[:, ga])
        ub = conv(ub, cwf_ref[:, va])
        gbuf[:, ga] = (ua * jax.nn.sigmoid(ua) * ub).astype(BF16)
        if j % DOWN_CHUNK == DOWN_CHUNK - 1 or j == N_FF_TILES - 1:
            lo = (j // DOWN_CHUNK) * DOWN_CHUNK * FF_TILE
            hi = (j + 1) * FF_TILE
            d = _dot(gbuf[:, lo:hi], wdn_ref[lo:hi, :])
            acc = d if acc is None else acc + d

    for c in range(nchunk):
        abuf[c] = acc[:, c * LANES:(c + 1) * LANES]
    g2 = mod_ref[5:6, :]
    for c in range(nchunk):
        cs = slice(c * LANES, (c + 1) * LANES)
        for s in range(8):
            rs = slice(s * grp, (s + 1) * grp)
            o_ref[rs, cs] = x1c[rs, cs] + g2[:, cs] * abuf[c, pl.ds(s, grp, stride=8), :]
    if final:
        o_ref[...] = _rms(o_ref[...], fn_ref[...], D_MODEL)


def _halo_specs(r, w, rows):
    nblk = rows // BF16_ROWS
    per = r // BF16_ROWS
    prev = pl.BlockSpec((BF16_ROWS, w), lambda i: (jnp.maximum(i * per - 1, 0), 0))
    cur = pl.BlockSpec((r, w), lambda i: (i, 0))
    nxt = pl.BlockSpec((BF16_ROWS, w), lambda i: (jnp.minimum((i + 1) * per, nblk - 1), 0))
    return prev, cur, nxt


def _cond_spec(l, seq_len, cond_base):
    r = ROW_TILE
    tiles_per_seq = max(seq_len // r, 1)
    step = 1 if seq_len >= r else 0
    return pl.BlockSpec((None, None, 6, D_MODEL),
                        lambda i: (l, cond_base + step * (i // tiles_per_seq), 0, 0))


def _post(x, mod, lw, l, seq_len, cond_base, zg, y, final_norm, final):
    rows = x.shape[0]
    r = ROW_TILE
    full = lambda a: pl.BlockSpec((None,) + a.shape[1:], lambda i: (l,) + (0,) * (a.ndim - 1),
                                  pipeline_mode=pl.Buffered(1))
    acts = [x, zg, y]
    edges = seq_len > r
    if edges:
        act_specs = [s for a in acts for s in _halo_specs(r, a.shape[1], rows)]
        act_args = [a for a in acts for _ in range(3)]
    else:
        act_specs = [pl.BlockSpec((r, a.shape[1]), lambda i: (i, 0)) for a in acts]
        act_args = acts
    return pl.pallas_call(
        functools.partial(_post_body, seq_len, final),
        grid=(rows // r,),
        in_specs=[_cond_spec(l, seq_len, cond_base), full(lw['norm2']), full(lw['conv_a']), full(lw['w_out']),
                  full(lw['conv_ff']), full(lw['w_up']), full(lw['w_down']),
                  pl.BlockSpec((1, D_MODEL), lambda i: (0, 0))] + act_specs,
        out_specs=pl.BlockSpec((r, D_MODEL), lambda i: (i, 0)),
        out_shape=jax.ShapeDtypeStruct((rows, D_MODEL), F32),
        scratch_shapes=[pltpu.VMEM((r + (BF16_ROWS if edges else 0), D_MODEL), BF16), pltpu.VMEM((r, D_FF), BF16),
                        pltpu.VMEM((D_MODEL // LANES, r, LANES), F32),
                        pltpu.VMEM((D_MODEL // LANES, r, LANES), F32)],
        compiler_params=pltpu.CompilerParams(
            dimension_semantics=("arbitrary",), vmem_limit_bytes=VMEM_LIMIT),
        name="post",
    )(mod, lw['norm2'], lw['conv_a'], lw['w_out'], lw['conv_ff'], lw['w_up'], lw['w_down'], final_norm,
      *act_args)


def _pad_cols(a, n):
    return jnp.pad(a, [(0, 0)] * (a.ndim - 1) + [(0, n - a.shape[-1])])


def _perm_heads(a, axis):
    h = [lax.slice_in_dim(a, k * HEAD_DIM, (k + 1) * HEAD_DIM, axis=axis) for k in range(4)]
    return jnp.concatenate([h[0], h[2], h[1], h[3]], axis=axis)


def _prep_weights(w_in, mla_wq_b, mla_wkv_b, w_out, w_up, conv_ff, w_down, norm1, norm2, conv_a,
                  mla_q_norm, mla_kv_norm, gqa_q_norm, gqa_k_norm):
    L = DEPTH
    a = w_in[..., 0:768]
    cq_kpe = _pad_cols(jnp.concatenate([w_in[..., 768:960], w_in[..., 1088:1120]], axis=-1), 256)
    ckv = w_in[..., 960:1088]
    qc = _perm_heads(w_in[..., 1120:1376], 2)
    kvc = w_in[..., 1376:1632]
    qd = _perm_heads(w_in[..., 1632:1888], 2)
    kvd = w_in[..., 1888:2144]
    w_in_p = jnp.concatenate([a, cq_kpe, ckv, qc, kvc, qd, kvd], axis=-1).astype(BF16)
    assert w_in_p.shape[-1] == IN_COLS_PACKED

    wq = mla_wq_b.reshape(L, MLA_Q_LORA, MLA_HEADS, MLA_NOPE + MLA_ROPE)
    wq = jnp.concatenate([wq[..., MLA_NOPE:], wq[..., :MLA_NOPE],
                          jnp.zeros((L, MLA_Q_LORA, MLA_HEADS, LANES - MLA_NOPE - MLA_ROPE), F32)], axis=-1)
    wq = jnp.pad(wq.reshape(L, MLA_Q_LORA, MLA_HEADS * LANES), ((0, 0), (0, 256 - MLA_Q_LORA), (0, 0)))

    wkv = mla_wkv_b.reshape(L, MLA_KV_LORA, MLA_HEADS, MLA_NOPE + MLA_V)
    zk = jnp.zeros((L, MLA_KV_LORA, MLA_HEADS, MLA_ROPE), F32)
    wk = jnp.concatenate([zk, wkv[..., :MLA_NOPE], zk], axis=-1).reshape(L, MLA_KV_LORA, MLA_HEADS * LANES)
    wv = wkv[..., MLA_NOPE:].reshape(L, MLA_KV_LORA, MLA_HEADS * MLA_V)
    wkv_p = jnp.concatenate([wk, wv], axis=-1)

    wo = jnp.concatenate([w_out[:, 0:512], _perm_heads(w_out[:, 512:768], 1),
                          _perm_heads(w_out[:, 768:1024], 1)], axis=1)

    hm = jnp.asarray(np.kron(np.eye(2 * LANES // HEAD_DIM), np.ones((HEAD_DIM, HEAD_DIM))), BF16)
    return {
        'w_in': w_in_p, 'wq': wq.astype(BF16), 'wkv': wkv_p.astype(BF16), 'w_out': wo.astype(BF16),
        'w_up': w_up.astype(BF16), 'conv_ff': conv_ff, 'w_down': w_down.astype(BF16),
        'norm1': norm1.reshape(L, 1, D_MODEL), 'norm2': norm2.reshape(L, 1, D_MODEL), 'conv_a': conv_a,
        'gq': _pad_cols(mla_q_norm, 256).reshape(L, 1, 256), 'gkv': mla_kv_norm.reshape(L, 1, MLA_KV_LORA),
        'gqc': jnp.tile(gqa_q_norm, (1, 4)).reshape(L, 1, 256), 'gkc': jnp.tile(gqa_k_norm, (1, 2)).reshape(L, 1, 128),
        'hm': hm,
    }


def _rope_tables(t):
    rows = t // GRID_W
    row = np.repeat(np.arange(rows, dtype=np.float64), GRID_W)
    col = np.tile(np.arange(GRID_W, dtype=np.float64), rows)

    def tabs(dim):
        half = dim // 2
        inv = np.power(ROPE_THETA, -np.arange(0, half, 2, dtype=np.float64) / half)
        ar = row[:, None] * inv
        ac = col[:, None] * inv
        c = np.concatenate([np.cos(ar), np.cos(ar), np.cos(ac), np.cos(ac)], axis=1)
        s = np.concatenate([-np.sin(ar), np.sin(ar), -np.sin(ac), np.sin(ac)], axis=1)
        return c, s

    c64, s64 = tabs(HEAD_DIM)
    c32, s32 = tabs(MLA_ROPE)
    c64 = np.tile(c64, (1, 2))
    s64 = np.tile(s64, (1, 2))
    c32 = np.concatenate([c32, np.ones((t, LANES - MLA_ROPE))], axis=1)
    s32 = np.concatenate([s32, np.zeros((t, LANES - MLA_ROPE))], axis=1)
    return tuple(jnp.asarray(a.astype(np.float32)) for a in (c64, s64, c32, s32))


def kernel(x_prompt, x_sample, cache_mla_ckv, cache_mla_kpe, cache_gqa_k, cache_gqa_v, cache_swa_k, cache_swa_v,
           c, c_ctx, w_ada, b_ada, norm1, w_in, conv_a, mla_q_norm, mla_wq_b, mla_kv_norm, mla_wkv_b,
           gqa_q_norm, gqa_k_norm, swa_sink, w_out, norm2, w_up, conv_ff, w_down, final_norm):
    B, T, _ = x_prompt.shape
    DB, DT, _ = x_sample.shape
    past = cache_mla_ckv.shape[2]

    lw = _prep_weights(w_in, mla_wq_b, mla_wkv_b, w_out, w_up, conv_ff, w_down, norm1, norm2, conv_a,
                       mla_q_norm, mla_kv_norm, gqa_q_norm, gqa_k_norm)
    rope_tabs = _rope_tables(DT)
    fnorm = final_norm.reshape(1, D_MODEL)

    cond_t = jnp.concatenate([c_ctx[:, None], c.T, jnp.zeros((D_MODEL, 8 - 1 - DB), F32)], axis=1)
    mod = _ada(cond_t, 1 + DB, w_ada, b_ada).reshape(DEPTH, 8, 6, D_MODEL)

    kx, vx = _ctx_mla(cache_mla_ckv, _pad_cols(cache_mla_kpe, LANES), lw['wkv'])
    flat = lambda a: a.reshape(DB, DEPTH, past, 2 * HEAD_DIM).astype(BF16)
    kcx, kdx = flat(cache_gqa_k), flat(cache_swa_k)
    vcx, vdx = jnp.swapaxes(flat(cache_gqa_v), 2, 3), jnp.swapaxes(flat(cache_swa_v), 2, 3)

    xp = x_prompt.reshape(B * T, D_MODEL)
    xs = x_sample.reshape(DB * DT, D_MODEL)
    caches = None
    for l in range(DEPTH):
        final = l == DEPTH - 1
        (zg, qm, km, vm, qc, kc, vc, qd, kd, vd, *caches) = _inproj(xp, mod, lw, l, T, 0, None, caches)
        y = _attn_ctx(l, T, swa_sink, qm, km, vm, qc, kc, vc, qd, kd, vd)
        xp = _post(xp, mod, lw, l, T, 0, zg, y, fnorm, final)
        (zg, qmr, qm, km, vm, qcr, qc, kc, vc, qdr, qd, kd, vd) = _inproj(xs, mod, lw, l, DT, 1, rope_tabs)
        y = _attn_lat(l, DB, DT, swa_sink, qmr, qm, km, vm, kx, vx, qcr, qc, kc, vc, kcx, vcx,
                      qdr, qd, kd, vd, kdx, vdx)
        xs = _post(xs, mod, lw, l, DT, 1, zg, y, fnorm, final)

    heads = lambda a: a.reshape(B, DEPTH, T, 2, HEAD_DIM)
    return (xp.reshape(B, T, D_MODEL), xs.reshape(DB, DT, D_MODEL), caches[0], caches[1],
            heads(caches[2]), heads(caches[3]), heads(caches[4]), heads(caches[5]))
```

```python
import functools
import math

import jax
import jax.numpy as jnp
import numpy as np
from jax import lax
from jax.experimental import pallas as pl
from jax.experimental.pallas import tpu as pltpu

F32 = jnp.float32
BF16 = jnp.bfloat16

D_MODEL = 1024
DEPTH = 2
GRID_W = 64
HEAD_DIM = 64
GROUP_WIDTH = D_MODEL // 4
MLA_HEADS = 4
MLA_NOPE = 64
MLA_ROPE = 32
MLA_V = 64
MLA_Q_LORA = 192
MLA_KV_LORA = 128
WINDOW = 128
D_FF = 2816
ROPE_THETA = 10000.0
EPS = 1e-6
NEG_INF = -1e30
LOG2E = math.log2(math.e)
ATTN_SCALE = HEAD_DIM ** -0.5 * LOG2E
MLA_SCALE = (MLA_NOPE + MLA_ROPE) ** -0.5 * LOG2E

LANES = 128
BF16_ROWS = 16
ROW_TILE = 512
IN_ROW_TILE = 1024
IN_SUB_TILE = 256
IN_SUB_TILE_ROPE = 128
FF_TILE = 256
N_FF_TILES = D_FF // FF_TILE
DOWN_CHUNK = 11
IN_COLS_PACKED = 2176
VMEM_LIMIT = 56 * 1024 * 1024

_NT = (((1,), (1,)), ((), ()))


def _dot(a, b):
    return jnp.dot(a, b, preferred_element_type=F32)


def _dot_nt(a, b):
    return lax.dot_general(a, b, _NT, preferred_element_type=F32)


def _rms(x, g, n):
    ms = jnp.sum(x * x, axis=-1, keepdims=True) * (1.0 / n)
    return x * lax.rsqrt(ms + EPS) * g


def _lane(shape):
    return lax.broadcasted_iota(jnp.int32, shape, len(shape) - 1)


def _rope(x, c, s, half):
    w = x.shape[-1]
    lo = (_lane(x.shape) % (2 * half)) < half
    sw = jnp.where(lo, pltpu.roll(x, w - half, 1), pltpu.roll(x, half, 1))
    return x * c + sw * s


def _shift_rows(zz, halo, rows):
    n = zz.shape[0]
    prev = pltpu.roll(zz, 1, 0)[halo:halo + rows]
    nxt = pltpu.roll(zz, n - 1, 0)[halo:halo + rows]
    return prev, nxt


def _seq_pos(tile, rows, seq_len):
    r = lax.broadcasted_iota(jnp.int32, (rows, 1), 0)
    return (tile * rows + r) % seq_len


def _ada_body(ncond, c_ref, w_ref, b_ref, o_ref):
    c = c_ref[...]
    s = c * jax.nn.sigmoid(c)
    tn = w_ref.shape[1]
    w = w_ref[...].reshape(D_MODEL // 8, 8, tn)
    rows = []
    for r in range(ncond):
        part = jnp.sum(w * s[:, r:r + 1].reshape(D_MODEL // 8, 8, 1), axis=0)
        rows.append(jnp.sum(part, axis=0, keepdims=True))
    rows.append(jnp.zeros((8 - ncond, tn), F32))
    o_ref[...] = jnp.concatenate(rows, axis=0) + b_ref[...]


def _ada(cond_t, ncond, w_ada, b_ada):
    tn = 1536
    n = 6 * D_MODEL
    return pl.pallas_call(
        functools.partial(_ada_body, ncond),
        grid=(DEPTH, n // tn),
        in_specs=[pl.BlockSpec((D_MODEL, 8), lambda l, j: (0, 0)),
                  pl.BlockSpec((None, D_MODEL, tn), lambda l, j: (l, 0, j)),
                  pl.BlockSpec((None, 1, tn), lambda l, j: (l, 0, j))],
        out_specs=pl.BlockSpec((None, 8, tn), lambda l, j: (l, 0, j)),
        out_shape=jax.ShapeDtypeStruct((DEPTH, 8, n), F32),
        compiler_params=pltpu.CompilerParams(
            dimension_semantics=("arbitrary", "arbitrary"), vmem_limit_bytes=VMEM_LIMIT),
        name="ada",
    )(cond_t, w_ada, b_ada.reshape(DEPTH, 1, n))


def _ctx_body(ckv_ref, kpe_ref, w_ref, kx_ref, vx_ref):
    kv = _dot(ckv_ref[...].astype(BF16), w_ref[...])
    kpe = kpe_ref[...]
    kx_ref[...] = (kv[:, :4 * LANES] + jnp.concatenate([kpe] * MLA_HEADS, axis=1)).astype(BF16)
    vx_ref[...] = kv[:, 4 * LANES:].T.astype(BF16)


def _ctx_mla(cache_ckv, cache_kpe_pad, wkv_p):
    b, _, s, _ = cache_ckv.shape
    return pl.pallas_call(
        _ctx_body,
        grid=(DEPTH, b),
        in_specs=[pl.BlockSpec((None, None, s, MLA_KV_LORA), lambda l, i: (i, l, 0, 0)),
                  pl.BlockSpec((None, None, s, LANES), lambda l, i: (i, l, 0, 0)),
                  pl.BlockSpec((None, MLA_KV_LORA, 6 * LANES), lambda l, i: (l, 0, 0))],
        out_specs=[pl.BlockSpec((None, None, s, 4 * LANES), lambda l, i: (l, i, 0, 0)),
                   pl.BlockSpec((None, None, 2 * LANES, s), lambda l, i: (l, i, 0, 0))],
        out_shape=[jax.ShapeDtypeStruct((DEPTH, b, s, 4 * LANES), BF16),
                   jax.ShapeDtypeStruct((DEPTH, b, 2 * LANES, s), BF16)],
        compiler_params=pltpu.CompilerParams(
            dimension_semantics=("arbitrary", "arbitrary"), vmem_limit_bytes=VMEM_LIMIT),
        name="ctx_mla",
    )(cache_ckv, cache_kpe_pad, wkv_p)


def _inproj_body(rope, n_alias, x_ref, mod_ref, n1_ref, w_ref, wq_ref, wkv_ref, gq_ref, gkv_ref, gqc_ref, gkc_ref,
                 hm_ref, *rest):
    if rope:
        c64_ref, s64_ref, c32_ref, s32_ref = rest[:4]
        outs = rest[4:]
    else:
        outs = rest[n_alias:]
    sh1 = mod_ref[0:1, :]
    sc1 = mod_ref[1:2, :]
    hm = hm_ref[...]
    sub = IN_SUB_TILE_ROPE if rope else IN_SUB_TILE
    nsub = x_ref.shape[0] // sub

    def project(s):
        x = x_ref[s * sub:(s + 1) * sub, :]
        h = _rms(x, n1_ref[...], D_MODEL) * (1.0 + sc1) + sh1
        return _dot(h.astype(BF16), w_ref[...])

    def head_rms(v, g):
        w = v.shape[-1]
        ss = _dot((v * v).astype(BF16), hm[:w, :w])
        return v * lax.rsqrt(ss * (1.0 / HEAD_DIM) + EPS) * g

    def finish(s, acc):
        rs = slice(s * sub, (s + 1) * sub)
        xa, gb, gc = acc[:, 0:256], acc[:, 256:512], acc[:, 512:768]
        z = gc * xa
        blk = acc[:, 768:1024]
        cq = jnp.where(_lane(blk.shape) < MLA_Q_LORA, blk, 0.0)
        cqn = _rms(cq, gq_ref[...], MLA_Q_LORA)
        qm = _dot(cqn.astype(BF16), wq_ref[...]) * MLA_SCALE
        ckvn = _rms(acc[:, 1024:1152], gkv_ref[...], MLA_KV_LORA)
        kv = _dot(ckvn.astype(BF16), wkv_ref[...])
        kpe = pltpu.roll(blk[:, LANES:], 2 * LANES - MLA_Q_LORA, 1)
        kpe = jnp.where(_lane(kpe.shape) < MLA_ROPE, kpe, 0.0)
        vm = kv[:, 4 * LANES:]
        qc = head_rms(acc[:, 1152:1408], gqc_ref[...]) * ATTN_SCALE
        kc = head_rms(acc[:, 1408:1536], gkc_ref[...])
        vc = acc[:, 1536:1664]
        qd = acc[:, 1664:1920] * ATTN_SCALE
        kd = acc[:, 1920:2048]
        vd = acc[:, 2048:2176]

        if not rope:
            (zg_o, qm_o, km_o, vm_o, qc_o, kc_o, vc_o, qd_o, kd_o, vd_o,
             ckv_c, kpe_c, kc_c, vc_c, kd_c, vd_c) = outs
            km = kv[:, :4 * LANES] + jnp.concatenate([kpe] * MLA_HEADS, axis=1)
            zg_o[rs, 0:256] = z.astype(BF16)
            zg_o[rs, 256:512] = gb.astype(BF16)
            qm_o[rs, :] = qm.astype(BF16)
            km_o[rs, :] = km.astype(BF16)
            vm_o[:, rs] = vm.T.astype(BF16)
            qc_o[rs, :] = qc.astype(BF16)
            kc_o[rs, :] = kc.astype(BF16)
            vc_o[:, rs] = vc.T.astype(BF16)
            qd_o[rs, :] = qd.astype(BF16)
            kd_o[rs, :] = kd.astype(BF16)
            vd_o[:, rs] = vd.T.astype(BF16)
            for o, v in ((ckv_c, ckvn), (kpe_c, kpe[:, :MLA_ROPE]), (kc_c, kc), (vc_c, vc), (kd_c, kd),
                         (vd_c, vd)):
                t = o.shape[-2]
                bs = slice(s * sub // t, (s + 1) * sub // t)
                v = v.reshape(sub // t, t, v.shape[-1])
                if n_alias:
                    o[bs] = v
                else:
                    o[bs, 0] = v
                    o[bs, 1:] = jnp.zeros((sub // t, o.shape[1] - 1) + v.shape[1:], F32)
        else:
            (zg_o, qmr_o, qm_o, km_o, vm_o, qcr_o, qc_o, kc_o, vc_o, qdr_o, qd_o, kd_o, vd_o) = outs
            c64, s64, c32, s32 = c64_ref[rs, :], s64_ref[rs, :], c32_ref[rs, :], s32_ref[rs, :]
            rope64 = lambda v: jnp.concatenate(
                [_rope(v[:, i:i + LANES], c64, s64, 16) for i in range(0, v.shape[-1], LANES)], axis=1)
            rope32 = lambda v: jnp.concatenate(
                [_rope(v[:, i:i + LANES], c32, s32, 8) for i in range(0, v.shape[-1], LANES)], axis=1)
            km = kv[:, :4 * LANES] + jnp.concatenate([_rope(kpe, c32, s32, 8)] * MLA_HEADS, axis=1)
            zg_o[rs, 0:256] = z.astype(BF16)
            zg_o[rs, 256:512] = gb.astype(BF16)
            qmr_o[rs, :] = rope32(qm).astype(BF16)
            qm_o[rs, :] = qm.astype(BF16)
            km_o[rs, :] = km.astype(BF16)
            vm_o[:, rs] = vm.T.astype(BF16)
            qcr_o[rs, :] = rope64(qc).astype(BF16)
            qc_o[rs, :] = qc.astype(BF16)
            kc_o[rs, :] = rope64(kc).astype(BF16)
            vc_o[:, rs] = vc.T.astype(BF16)
            qdr_o[rs, :] = rope64(qd).astype(BF16)
            qd_o[rs, :] = qd.astype(BF16)
            kd_o[rs, :] = rope64(kd).astype(BF16)
            vd_o[:, rs] = vd.T.astype(BF16)

    nxt = project(0)
    for s in range(nsub):
        acc = nxt
        if s + 1 < nsub:
            nxt = project(s + 1)
        finish(s, acc)


def _inproj(x, mod, lw, l, seq_len, cond_base, rope_tabs, prev_caches=None):
    rows = x.shape[0]
    r = IN_ROW_TILE
    rope = rope_tabs is not None
    aliases = {}
    tiles_per_seq = max(seq_len // r, 1)
    step = 1 if seq_len >= r else 0

    def cond_map(i):
        return (l, cond_base + step * (i // tiles_per_seq), 0, 0)

    row = lambda w: pl.BlockSpec((r, w), lambda i: (i, 0))
    full = lambda a: pl.BlockSpec((None,) + a.shape[1:], lambda i: (l,) + (0,) * (a.ndim - 1))
    in_specs = [row(D_MODEL),
                pl.BlockSpec((None, None, 6, D_MODEL), cond_map),
                full(lw['norm1']), full(lw['w_in']), full(lw['wq']), full(lw['wkv']),
                full(lw['gq']), full(lw['gkv']), full(lw['gqc']), full(lw['gkc']),
                pl.BlockSpec((2 * LANES, 2 * LANES), lambda i: (0, 0))]
    args = [x, mod, lw['norm1'], lw['w_in'], lw['wq'], lw['wkv'], lw['gq'], lw['gkv'], lw['gqc'], lw['gkc'],
            lw['hm']]
    def act(w, transposed):
        if transposed:
            return jax.ShapeDtypeStruct((w, rows), BF16), pl.BlockSpec((w, r), lambda i: (0, i))
        return jax.ShapeDtypeStruct((rows, w), BF16), row(w)

    if rope:
        tab = pl.BlockSpec((r, LANES), lambda i: (i % tiles_per_seq, 0))
        in_specs += [tab] * 4
        args += list(rope_tabs)
        widths = [(512, 0), (512, 0), (512, 0), (512, 0), (256, 1), (256, 0), (256, 0), (128, 0),
                  (128, 1), (256, 0), (256, 0), (128, 0), (128, 1)]
        out_shape, out_specs = map(list, zip(*[act(w, t) for w, t in widths]))
    else:
        widths = [(512, 0), (512, 0), (512, 0), (256, 1), (256, 0), (128, 0), (128, 1), (256, 0),
                  (128, 0), (128, 1)]
        cwidths = [128, MLA_ROPE, 128, 128, 128, 128]
        nb = r // seq_len
        out_shape, out_specs = map(list, zip(*[act(w, t) for w, t in widths]))
        out_shape += [jax.ShapeDtypeStruct((rows // seq_len, DEPTH, seq_len, w), F32) for w in cwidths]
        assert (prev_caches is None) == (l == 0)
        if l == 0:
            out_specs += [pl.BlockSpec((nb, DEPTH, seq_len, w), lambda i: (i, 0, 0, 0)) for w in cwidths]
        else:
            out_specs += [pl.BlockSpec((nb, None, seq_len, w), lambda i: (i, l, 0, 0)) for w in cwidths]
        if prev_caches is not None:
            aliases = {len(args) + k: len(widths) + k for k in range(len(cwidths))}
            in_specs += [pl.BlockSpec(memory_space=pl.ANY)] * len(cwidths)
            args += list(prev_caches)
    return pl.pallas_call(
        functools.partial(_inproj_body, rope, len(aliases)),
        grid=(rows // r,),
        in_specs=in_specs, out_specs=out_specs, out_shape=out_shape,
        input_output_aliases=aliases,
        compiler_params=pltpu.CompilerParams(
            dimension_semantics=("arbitrary",), vmem_limit_bytes=VMEM_LIMIT),
        name="inproj_rope" if rope else "inproj",
    )(*args)


DEN_FLOOR = 2.0 ** -80


def _key_mags(k):
    return jnp.max(jnp.abs(k), axis=0, keepdims=True).astype(F32)


def _score_bound(kmag, qmag):
    col = jnp.sum(qmag.astype(F32) * kmag, axis=1, keepdims=True)
    return jnp.broadcast_to(col, qmag.shape).T[0:1]


def _softmax_pv_t(s_list, vt_list, first, extra=None, bound=None):
    if bound is None:
        m = s_list[0].max(axis=0, keepdims=True)
        for s in s_list[1:]:
            m = jnp.maximum(m, s.max(axis=0, keepdims=True))
    else:
        m = bound
    if extra is not None:
        m = jnp.maximum(m, extra)
    out = None
    for s, vt in zip(s_list, vt_list):
        ones = jnp.ones(vt.shape, BF16)
        vt = jnp.concatenate([vt, ones] if first else [ones, vt], axis=0)
        pv = _dot(vt, jnp.exp2(s - m).astype(BF16))
        out = pv if out is None else out + pv
    den = out[HEAD_DIM:HEAD_DIM + 1] if first else out[0:1]
    if extra is not None:
        den = den + jnp.exp2(extra - m)
    return out * (1.0 / den), den


def _underflowed(dens):
    width = max(d.shape[1] for d in dens)
    low = None
    for d in dens:
        d = jnp.concatenate([d] * (width // d.shape[1]), axis=1)
        low = d if low is None else jnp.minimum(low, d)
    return jnp.logical_not(jnp.min(low) >= DEN_FLOOR)


def _half_masks(q):
    lo = _lane(q.shape) < HEAD_DIM
    zero = jnp.zeros_like(q)
    return jnp.where(lo, q, zero), jnp.where(lo, zero, q)


def _merge_rows(a, b):
    r = lax.broadcasted_iota(jnp.int32, a.shape, 0)
    return jnp.where(r < HEAD_DIM, a, b)


def _sink_row(sink_ref, l, h0, h1, tq):
    c = lax.broadcasted_iota(jnp.int32, (1, 2 * tq), 1)
    return jnp.where(c < tq, sink_ref[l, h0], sink_ref[l, h1]) * LOG2E


SCORE_AHEAD = 4


def _run_groups(stage1, stage2, n, ahead=SCORE_AHEAD):
    out = []
    pending = [stage1(g) for g in range(min(ahead, n))]
    for g in range(n):
        if g + ahead < n:
            pending.append(stage1(g + ahead))
        out.append(stage2(g, pending.pop(0)))
    return out


def _scores(pairs, kmags):
    s = [_dot_nt(k, q) for k, q in pairs]
    b = None
    if kmags is not None:
        kmag, qmag = None, None
        for (_, q), km in zip(pairs, kmags):
            kmag = km if kmag is None else jnp.maximum(kmag, km)
            qmag = jnp.abs(q) if qmag is None else jnp.maximum(qmag, jnp.abs(q))
        b = _score_bound(kmag, qmag)
    return s, b


def _mla_scores(h, q_sets, k_sets, kmags):
    c = slice(h * LANES, (h + 1) * LANES)
    return _scores([(k[:, c], q[:, c]) for q, k in zip(q_sets, k_sets)], kmags)


def _gqa_scores(j, q_sets, k_sets, kmags):
    pairs = []
    for q, k in zip(q_sets, k_sets):
        qa = _half_masks(q[:, 0:LANES])[j]
        qb = _half_masks(q[:, LANES:])[j]
        pairs.append((k[...], jnp.concatenate([qa, qb], axis=0)))
    return _scores(pairs, kmags)


def _store_pair(y_ref, rows, blk, ot):
    y_ref[rows, blk * LANES:(blk + 1) * LANES] = ot.T.astype(BF16)


N_GROUPS = MLA_HEADS + 4


def _store_groups(y_ref, rows, o, tq):
    for p in range(MLA_HEADS // 2):
        _store_pair(y_ref, rows, p, jnp.concatenate([o[2 * p][:HEAD_DIM], o[2 * p + 1][:HEAD_DIM]], axis=0))
    for base, (o0, o1) in ((2, o[4:6]), (4, o[6:8])):
        _store_pair(y_ref, rows, base, _merge_rows(o0[:, :tq], o1[:, :tq]))
        _store_pair(y_ref, rows, base + 1, _merge_rows(o0[:, tq:], o1[:, tq:]))


def _attn_ctx_body(l, nb, t, sink_ref, qm_ref, km_ref, vm_ref, qc_ref, kc_ref, vc_ref, qd_ref, kd_ref, vd_ref,
                   y_ref):
    def stage1(g):
        b, k = divmod(g, N_GROUPS)
        rs = pl.ds(b * t, t)
        if k < MLA_HEADS:
            return _mla_scores(k, [qm_ref.at[rs]], [km_ref.at[rs]], None)
        q_ref, k_ref = (qc_ref, kc_ref) if k < MLA_HEADS + 2 else (qd_ref, kd_ref)
        return _gqa_scores(k % 2, [q_ref.at[rs]], [k_ref.at[rs]], None)

    def stage2(g, scored):
        s, _ = scored
        b, k = divmod(g, N_GROUPS)
        rs = slice(b * t, (b + 1) * t)
        j = k % 2
        hs = slice(j * HEAD_DIM, (j + 1) * HEAD_DIM)
        if k < MLA_HEADS:
            return _softmax_pv_t(s, [vm_ref[k * HEAD_DIM:(k + 1) * HEAD_DIM, rs]], True)[0]
        if k < MLA_HEADS + 2:
            return _softmax_pv_t(s, [vc_ref[hs, rs]], j == 0)[0]
        return _softmax_pv_t(s, [vd_ref[hs, rs]], j == 0, _sink_row(sink_ref, l, 2 * j, 2 * j + 1, t))[0]

    o = _run_groups(stage1, stage2, N_GROUPS * nb)
    for b in range(nb):
        _store_groups(y_ref, slice(b * t, (b + 1) * t), o[N_GROUPS * b:N_GROUPS * (b + 1)], t)


def _attn_ctx(l, t, sink, qm, km, vm, qc, kc, vc, qd, kd, vd):
    rows = qm.shape[0]
    nb = 8
    r = nb * t
    row = lambda a: (pl.BlockSpec((a.shape[0], r), lambda i: (0, i)) if a.shape[1] == rows
                     else pl.BlockSpec((r, a.shape[1]), lambda i: (i, 0)))
    ins = [qm, km, vm, qc, kc, vc, qd, kd, vd]
    return pl.pallas_call(
        functools.partial(_attn_ctx_body, l, nb, t),
        grid=(rows // r,),
        in_specs=[pl.BlockSpec(memory_space=pltpu.SMEM)] + [row(a) for a in ins],
        out_specs=pl.BlockSpec((r, 6 * LANES), lambda i: (i, 0)),
        out_shape=jax.ShapeDtypeStruct((rows, 6 * LANES), BF16),
        compiler_params=pltpu.CompilerParams(
            dimension_semantics=("arbitrary",), vmem_limit_bytes=VMEM_LIMIT),
        name="attn_ctx",
    )(sink, *ins)


def _window_block(i, tq, t, nwb):
    return jnp.clip(i * (tq // LANES) - WINDOW // LANES, 0, t // LANES - nwb)


def _attn_lat_body(l, tq, t, nwb, sink_ref, qmr_ref, qm_ref, km_ref, vm_ref, kx_ref, vx_ref,
                   qcr_ref, qc_ref, kc_ref, vc_ref, kcx_ref, vcx_ref,
                   qdr_ref, qd_ref, kdx_ref, vdx_ref, *rest):
    kd_refs, vd_refs = rest[:nwb], rest[nwb:2 * nwb]
    y_ref, kmag_scr = rest[2 * nwb:]
    i = pl.program_id(1)
    wk = nwb * LANES
    kpos = _window_block(i, tq, t, nwb) * LANES + lax.broadcasted_iota(jnp.int32, (wk, 2 * tq), 0)
    qpos = i * tq + lax.broadcasted_iota(jnp.int32, (wk, 2 * tq), 1) % tq
    valid = jnp.abs(kpos - qpos) <= WINDOW


    @pl.when(i == 0)
    def _():
        for h in range(MLA_HEADS):
            c = slice(h * LANES, (h + 1) * LANES)
            kmag_scr[h:h + 1, :] = _key_mags(km_ref[:, c])
            kmag_scr[MLA_HEADS + h:MLA_HEADS + h + 1, :] = _key_mags(kx_ref[:, c])
        kmag_scr[8:9, :] = _key_mags(kc_ref[...])
        kmag_scr[9:10, :] = _key_mags(kcx_ref[...])
        kmag_scr[10:11, :] = _key_mags(kdx_ref[...])
        kmag_scr[11:, :] = jnp.zeros((kmag_scr.shape[0] - 11, LANES), F32)

    def run(bounded):
        dens = []
        row = lambda r: kmag_scr[r:r + 1, :]

        def stage1(g):
            if g < MLA_HEADS:
                return _mla_scores(g, [qmr_ref, qm_ref], [km_ref, kx_ref],
                                   [row(g), row(MLA_HEADS + g)] if bounded else None)
            if g < MLA_HEADS + 2:
                return _gqa_scores(g % 2, [qcr_ref, qc_ref], [kc_ref, kcx_ref],
                                   [row(8), row(9)] if bounded else None)
            kw = jnp.concatenate([r[...] for r in kd_refs], axis=0)
            (s1, s2), bound = _gqa_scores(g % 2, [qdr_ref, qd_ref], [kw, kdx_ref],
                                          [_key_mags(kw), row(10)] if bounded else None)
            return [jnp.where(valid, s1, NEG_INF), s2], bound

        def stage2(g, scored):
            s, bound = scored
            j = g % 2
            hs = slice(j * HEAD_DIM, (j + 1) * HEAD_DIM)
            if g < MLA_HEADS:
                hs = slice(g * HEAD_DIM, (g + 1) * HEAD_DIM)
                o, den = _softmax_pv_t(s, [vm_ref[hs, :], vx_ref[hs, :]], True, None, bound)
            elif g < MLA_HEADS + 2:
                o, den = _softmax_pv_t(s, [vc_ref[hs, :], vcx_ref[hs, :]], j == 0, None, bound)
            else:
                vw = jnp.concatenate([r[hs, :] for r in vd_refs], axis=1)
                o, den = _softmax_pv_t(s, [vw, vdx_ref[hs, :]], j == 0,
                                       _sink_row(sink_ref, l, 2 * j, 2 * j + 1, tq), bound)
            dens.append(den)
            return o

        _store_groups(y_ref, slice(None), _run_groups(stage1, stage2, N_GROUPS), tq)
        return dens

    @pl.when(_underflowed(run(True)))
    def _():
        run(False)


def _attn_lat(l, nbatch, t, sink, qmr, qm, km, vmt, kx, vxt, qcr, qc, kc, vct, kcx, vcxt, qdr, qd, kd, vdt,
              kdx, vdxt):
    tq = 256
    nq = t // tq
    nwb = (tq + 2 * WINDOW) // LANES
    nkb = t // LANES
    rows = qm.shape[0]
    qs = lambda a: pl.BlockSpec((tq, a.shape[1]), lambda b, i: (b * nq + i, 0))
    ks = lambda a: pl.BlockSpec((t, a.shape[1]), lambda b, i: (b, 0))
    kts = lambda a: pl.BlockSpec((a.shape[0], t), lambda b, i: (0, b))
    xs = lambda a: pl.BlockSpec((None, None) + a.shape[2:], lambda b, i: (l, b, 0, 0))
    cs = lambda a: pl.BlockSpec((None, None) + a.shape[2:], lambda b, i: (b, l, 0, 0))
    wblk = lambda b, i, k: b * nkb + _window_block(i, tq, t, nwb) + k
    kd_specs = [pl.BlockSpec((LANES, LANES), lambda b, i, k=k: (wblk(b, i, k), 0)) for k in range(nwb)]
    vd_specs = [pl.BlockSpec((LANES, LANES), lambda b, i, k=k: (0, wblk(b, i, k))) for k in range(nwb)]
    in_specs = [pl.BlockSpec(memory_space=pltpu.SMEM),
                qs(qmr), qs(qm), ks(km), kts(vmt), xs(kx), xs(vxt),
                qs(qcr), qs(qc), ks(kc), kts(vct), cs(kcx), cs(vcxt),
                qs(qdr), qs(qd), cs(kdx), cs(vdxt)] + kd_specs + vd_specs
    return pl.pallas_call(
        functools.partial(_attn_lat_body, l, tq, t, nwb),
        grid=(nbatch, nq),
        in_specs=in_specs,
        out_specs=pl.BlockSpec((tq, 6 * LANES), lambda b, i: (b * nq + i, 0)),
        out_shape=jax.ShapeDtypeStruct((rows, 6 * LANES), BF16),
        scratch_shapes=[pltpu.VMEM((16, LANES), F32)],
        compiler_params=pltpu.CompilerParams(
            dimension_semantics=("arbitrary", "arbitrary"), vmem_limit_bytes=VMEM_LIMIT),
        name="attn_lat",
    )(sink, qmr, qm, km, vmt, kx, vxt, qcr, qc, kc, vct, kcx, vcxt, qdr, qd, kdx, vdxt,
      *([kd] * nwb), *([vdt] * nwb))


def _post_body(seq_len, final, mod_ref, n2_ref, cwa_ref, wo_ref, cwf_ref, wup_ref, wdn_ref, fn_ref, *rest):
    o_ref, hbuf, gbuf, pbuf, abuf = rest[-5:]
    ins = rest[:-5]
    r = o_ref.shape[0]
    edges = seq_len > r
    assert edges or r % seq_len == 0
    if edges:
        cat = lambda k: jnp.concatenate([ins[3 * k][...], ins[3 * k + 1][...], ins[3 * k + 2][...]], axis=0)
    else:
        cat = lambda k: ins[k][...]
    halo = BF16_ROWS if edges else 0
    n = r + 2 * halo
    grp = r // 8
    nchunk = D_MODEL // LANES
    i = pl.program_id(0)

    zg = cat(1).astype(F32)
    zz, gb = zg[:, :GROUP_WIDTH], zg[:, GROUP_WIDTH:]
    pos = (i * r - halo + lax.broadcasted_iota(jnp.int32, (n, 1), 0)) % seq_len
    prev = jnp.where(pos == 0, 0.0, pltpu.roll(zz, 1, 0))
    nxt = jnp.where(pos == seq_len - 1, 0.0, pltpu.roll(zz, n - 1, 0))
    cwa = cwa_ref[...]
    ya = gb * (prev * cwa[0:1] + zz * cwa[1:2] + nxt * cwa[2:3])
    ycat = jnp.concatenate([ya.astype(BF16), cat(2)], axis=1)
    x1 = cat(0) + mod_ref[2:3, :] * _dot(ycat, wo_ref[...])
    h2 = _rms(x1, n2_ref[...], D_MODEL) * (1.0 + mod_ref[4:5, :]) + mod_ref[3:4, :]
    x1c = x1[halo:halo + r]

    for c in range(nchunk):
        for s in range(8):
            pbuf[c, pl.ds(s, grp, stride=8), :] = h2[halo + s * grp:halo + (s + 1) * grp, c * LANES:(c + 1) * LANES]
    if edges:
        hrow = lax.broadcasted_iota(jnp.int32, (halo, 1), 0)
        edge = jnp.where(hrow == 0, h2[halo - 1:halo], jnp.where(hrow == 1, h2[halo + r:halo + r + 1], 0.0))
        hbuf[0:halo, :] = edge.astype(BF16)
    hbuf[halo:, :] = jnp.concatenate([pbuf[c] for c in range(nchunk)], axis=1).astype(BF16)

    sub = lax.broadcasted_iota(jnp.int32, (8, 1), 0)
    seq_first = (i * r + sub * grp) % seq_len == 0
    seq_last = (i * r + sub * grp + grp - 1) % seq_len == seq_len - 1

    def up(j):
        ga = slice(j * FF_TILE, (j + 1) * FF_TILE)
        va = slice(D_FF + j * FF_TILE, D_FF + (j + 1) * FF_TILE)
        return _dot(hbuf[...], wup_ref[:, ga]), _dot(hbuf[...], wup_ref[:, va])

    def conv(u, cw):
        head = pltpu.roll(u[halo + r - 8:halo + r], 1, 0)
        tail = pltpu.roll(u[halo:halo + 8], 7, 0)
        if edges:
            head = jnp.where(sub == 0, u[0:8], head)
            tail = jnp.where(sub == 7, pltpu.roll(u[0:8], 6, 0), tail)
        prev = jnp.concatenate([jnp.where(seq_first, 0.0, head), u[halo:halo + r - 8]], axis=0)
        nxt = jnp.concatenate([u[halo + 8:halo + r], jnp.where(seq_last, 0.0, tail)], axis=0)
        return prev * cw[0:1] + u[halo:halo + r] * cw[1:2] + nxt * cw[2:3]

    acc = None
    nxt_u = up(0)
    for j in range(N_FF_TILES):
        ua, ub = nxt_u
        if j + 1 < N_FF_TILES:
            nxt_u = up(j + 1)
        ga = slice(j * FF_TILE, (j + 1) * FF_TILE)
        va = slice(D_FF + j * FF_TILE, D_FF + (j + 1) * FF_TILE)
        ua = conv(ua, cwf_ref[:, ga])
        ub = conv(ub, cwf_ref[:, va])
        gbuf[:, ga] = (ua * jax.nn.sigmoid(ua) * ub).astype(BF16)
        if j % DOWN_CHUNK == DOWN_CHUNK - 1 or j == N_FF_TILES - 1:
            lo = (j // DOWN_CHUNK) * DOWN_CHUNK * FF_TILE
            hi = (j + 1) * FF_TILE
            d = _dot(gbuf[:, lo:hi], wdn_ref[lo:hi, :])
            acc = d if acc is None else acc + d

    for c in range(nchunk):
        abuf[c] = acc[:, c * LANES:(c + 1) * LANES]
    g2 = mod_ref[5:6, :]
    for c in range(nchunk):
        cs = slice(c * LANES, (c + 1) * LANES)
        for s in range(8):
            rs = slice(s * grp, (s + 1) * grp)
            o_ref[rs, cs] = x1c[rs, cs] + g2[:, cs] * abuf[c, pl.ds(s, grp, stride=8), :]
    if final:
        o_ref[...] = _rms(o_ref[...], fn_ref[...], D_MODEL)


def _halo_specs(r, w, rows):
    nblk = rows // BF16_ROWS
    per = r // BF16_ROWS
    prev = pl.BlockSpec((BF16_ROWS, w), lambda i: (jnp.maximum(i * per - 1, 0), 0))
    cur = pl.BlockSpec((r, w), lambda i: (i, 0))
    nxt = pl.BlockSpec((BF16_ROWS, w), lambda i: (jnp.minimum((i + 1) * per, nblk - 1), 0))
    return prev, cur, nxt


def _cond_spec(l, seq_len, cond_base):
    r = ROW_TILE
    tiles_per_seq = max(seq_len // r, 1)
    step = 1 if seq_len >= r else 0
    return pl.BlockSpec((None, None, 6, D_MODEL),
                        lambda i: (l, cond_base + step * (i // tiles_per_seq), 0, 0))


def _post(x, mod, lw, l, seq_len, cond_base, zg, y, final_norm, final):
    rows = x.shape[0]
    r = ROW_TILE
    full = lambda a: pl.BlockSpec((None,) + a.shape[1:], lambda i: (l,) + (0,) * (a.ndim - 1),
                                  pipeline_mode=pl.Buffered(1))
    acts = [x, zg, y]
    edges = seq_len > r
    if edges:
        act_specs = [s for a in acts for s in _halo_specs(r, a.shape[1], rows)]
        act_args = [a for a in acts for _ in range(3)]
    else:
        act_specs = [pl.BlockSpec((r, a.shape[1]), lambda i: (i, 0)) for a in acts]
        act_args = acts
    return pl.pallas_call(
        functools.partial(_post_body, seq_len, final),
        grid=(rows // r,),
        in_specs=[_cond_spec(l, seq_len, cond_base), full(lw['norm2']), full(lw['conv_a']), full(lw['w_out']),
                  full(lw['conv_ff']), full(lw['w_up']), full(lw['w_down']),
                  pl.BlockSpec((1, D_MODEL), lambda i: (0, 0))] + act_specs,
        out_specs=pl.BlockSpec((r, D_MODEL), lambda i: (i, 0)),
        out_shape=jax.ShapeDtypeStruct((rows, D_MODEL), F32),
        scratch_shapes=[pltpu.VMEM((r + (BF16_ROWS if edges else 0), D_MODEL), BF16), pltpu.VMEM((r, D_FF), BF16),
                        pltpu.VMEM((D_MODEL // LANES, r, LANES), F32),
                        pltpu.VMEM((D_MODEL // LANES, r, LANES), F32)],
        compiler_params=pltpu.CompilerParams(
            dimension_semantics=("arbitrary",), vmem_limit_bytes=VMEM_LIMIT),
        name="post",
    )(mod, lw['norm2'], lw['conv_a'], lw['w_out'], lw['conv_ff'], lw['w_up'], lw['w_down'], final_norm,
      *act_args)


def _pad_cols(a, n):
    return jnp.pad(a, [(0, 0)] * (a.ndim - 1) + [(0, n - a.shape[-1])])


def _perm_heads(a, axis):
    h = [lax.slice_in_dim(a, k * HEAD_DIM, (k + 1) * HEAD_DIM, axis=axis) for k in range(4)]
    return jnp.concatenate([h[0], h[2], h[1], h[3]], axis=axis)


def _prep_weights(w_in, mla_wq_b, mla_wkv_b, w_out, w_up, conv_ff, w_down, norm1, norm2, conv_a,
                  mla_q_norm, mla_kv_norm, gqa_q_norm, gqa_k_norm):
    L = DEPTH
    a = w_in[..., 0:768]
    cq_kpe = _pad_cols(jnp.concatenate([w_in[..., 768:960], w_in[..., 1088:1120]], axis=-1), 256)
    ckv = w_in[..., 960:1088]
    qc = _perm_heads(w_in[..., 1120:1376], 2)
    kvc = w_in[..., 1376:1632]
    qd = _perm_heads(w_in[..., 1632:1888], 2)
    kvd = w_in[..., 1888:2144]
    w_in_p = jnp.concatenate([a, cq_kpe, ckv, qc, kvc, qd, kvd], axis=-1).astype(BF16)
    assert w_in_p.shape[-1] == IN_COLS_PACKED

    wq = mla_wq_b.reshape(L, MLA_Q_LORA, MLA_HEADS, MLA_NOPE + MLA_ROPE)
    wq = jnp.concatenate([wq[..., MLA_NOPE:], wq[..., :MLA_NOPE],
                          jnp.zeros((L, MLA_Q_LORA, MLA_HEADS, LANES - MLA_NOPE - MLA_ROPE), F32)], axis=-1)
    wq = jnp.pad(wq.reshape(L, MLA_Q_LORA, MLA_HEADS * LANES), ((0, 0), (0, 256 - MLA_Q_LORA), (0, 0)))

    wkv = mla_wkv_b.reshape(L, MLA_KV_LORA, MLA_HEADS, MLA_NOPE + MLA_V)
    zk = jnp.zeros((L, MLA_KV_LORA, MLA_HEADS, MLA_ROPE), F32)
    wk = jnp.concatenate([zk, wkv[..., :MLA_NOPE], zk], axis=-1).reshape(L, MLA_KV_LORA, MLA_HEADS * LANES)
    wv = wkv[..., MLA_NOPE:].reshape(L, MLA_KV_LORA, MLA_HEADS * MLA_V)
    wkv_p = jnp.concatenate([wk, wv], axis=-1)

    wo = jnp.concatenate([w_out[:, 0:512], _perm_heads(w_out[:, 512:768], 1),
                          _perm_heads(w_out[:, 768:1024], 1)], axis=1)

    hm = jnp.asarray(np.kron(np.eye(2 * LANES // HEAD_DIM), np.ones((HEAD_DIM, HEAD_DIM))), BF16)
    return {
        'w_in': w_in_p, 'wq': wq.astype(BF16), 'wkv': wkv_p.astype(BF16), 'w_out': wo.astype(BF16),
        'w_up': w_up.astype(BF16), 'conv_ff': conv_ff, 'w_down': w_down.astype(BF16),
        'norm1': norm1.reshape(L, 1, D_MODEL), 'norm2': norm2.reshape(L, 1, D_MODEL), 'conv_a': conv_a,
        'gq': _pad_cols(mla_q_norm, 256).reshape(L, 1, 256), 'gkv': mla_kv_norm.reshape(L, 1, MLA_KV_LORA),
        'gqc': jnp.tile(gqa_q_norm, (1, 4)).reshape(L, 1, 256), 'gkc': jnp.tile(gqa_k_norm, (1, 2)).reshape(L, 1, 128),
        'hm': hm,
    }


def _rope_tables(t):
    rows = t // GRID_W
    row = np.repeat(np.arange(rows, dtype=np.float64), GRID_W)
    col = np.tile(np.arange(GRID_W, dtype=np.float64), rows)

    def tabs(dim):
        half = dim // 2
        inv = np.power(ROPE_THETA, -np.arange(0, half, 2, dtype=np.float64) / half)
        ar = row[:, None] * inv
        ac = col[:, None] * inv
        c = np.concatenate([np.cos(ar), np.cos(ar), np.cos(ac), np.cos(ac)], axis=1)
        s = np.concatenate([-np.sin(ar), np.sin(ar), -np.sin(ac), np.sin(ac)], axis=1)
        return c, s

    c64, s64 = tabs(HEAD_DIM)
    c32, s32 = tabs(MLA_ROPE)
    c64 = np.tile(c64, (1, 2))
    s64 = np.tile(s64, (1, 2))
    c32 = np.concatenate([c32, np.ones((t, LANES - MLA_ROPE))], axis=1)
    s32 = np.concatenate([s32, np.zeros((t, LANES - MLA_ROPE))], axis=1)
    return tuple(jnp.asarray(a.astype(np.float32)) for a in (c64, s64, c32, s32))


def kernel(x_prompt, x_sample, cache_mla_ckv, cache_mla_kpe, cache_gqa_k, cache_gqa_v, cache_swa_k, cache_swa_v,
           c, c_ctx, w_ada, b_ada, norm1, w_in, conv_a, mla_q_norm, mla_wq_b, mla_kv_norm, mla_wkv_b,
           gqa_q_norm, gqa_k_norm, swa_sink, w_out, norm2, w_up, conv_ff, w_down, final_norm):
    B, T, _ = x_prompt.shape
    DB, DT, _ = x_sample.shape
    past = cache_mla_ckv.shape[2]

    lw = _prep_weights(w_in, mla_wq_b, mla_wkv_b, w_out, w_up, conv_ff, w_down, norm1, norm2, conv_a,
                       mla_q_norm, mla_kv_norm, gqa_q_norm, gqa_k_norm)
    rope_tabs = _rope_tables(DT)
    fnorm = final_norm.reshape(1, D_MODEL)

    cond_t = jnp.concatenate([c_ctx[:, None], c.T, jnp.zeros((D_MODEL, 8 - 1 - DB), F32)], axis=1)
    mod = _ada(cond_t, 1 + DB, w_ada, b_ada).reshape(DEPTH, 8, 6, D_MODEL)

    kx, vx = _ctx_mla(cache_mla_ckv, _pad_cols(cache_mla_kpe, LANES), lw['wkv'])
    flat = lambda a: a.reshape(DB, DEPTH, past, 2 * HEAD_DIM).astype(BF16)
    kcx, kdx = flat(cache_gqa_k), flat(cache_swa_k)
    vcx, vdx = jnp.swapaxes(flat(cache_gqa_v), 2, 3), jnp.swapaxes(flat(cache_swa_v), 2, 3)

    xp = x_prompt.reshape(B * T, D_MODEL)
    xs = x_sample.reshape(DB * DT, D_MODEL)
    caches = None
    for l in range(DEPTH):
        final = l == DEPTH - 1
        (zg, qm, km, vm, qc, kc, vc, qd, kd, vd, *caches) = _inproj(xp, mod, lw, l, T, 0, None, caches)
        y = _attn_ctx(l, T, swa_sink, qm, km, vm, qc, kc, vc, qd, kd, vd)
        xp = _post(xp, mod, lw, l, T, 0, zg, y, fnorm, final)
        (zg, qmr, qm, km, vm, qcr, qc, kc, vc, qdr, qd, kd, vd) = _inproj(xs, mod, lw, l, DT, 1, rope_tabs)
        y = _attn_lat(l, DB, DT, swa_sink, qmr, qm, km, vm, kx, vx, qcr, qc, kc, vc, kcx, vcx,
                      qdr, qd, kd, vd, kdx, vdx)
        xs = _post(xs, mod, lw, l, DT, 1, zg, y, fnorm, final)

    heads = lambda a: a.reshape(B, DEPTH, T, 2, HEAD_DIM)
    return (xp.reshape(B, T, D_MODEL), xs.reshape(DB, DT, D_MODEL), caches[0], caches[1],
            heads(caches[2]), heads(caches[3]), heads(caches[4]), heads(caches[5]))
```

```python
import functools
import math

import jax
import jax.numpy as jnp
import numpy as np
from jax import lax
from jax.experimental import pallas as pl
from jax.experimental.pallas import tpu as pltpu

F32 = jnp.float32
BF16 = jnp.bfloat16

D_MODEL = 1024
DEPTH = 2
GRID_W = 64
HEAD_DIM = 64
GROUP_WIDTH = D_MODEL // 4
MLA_HEADS = 4
MLA_NOPE = 64
MLA_ROPE = 32
MLA_V = 64
MLA_Q_LORA = 192
MLA_KV_LORA = 128
WINDOW = 128
D_FF = 2816
ROPE_THETA = 10000.0
EPS = 1e-6
NEG_INF = -1e30
LOG2E = math.log2(math.e)
ATTN_SCALE = HEAD_DIM ** -0.5 * LOG2E
MLA_SCALE = (MLA_NOPE + MLA_ROPE) ** -0.5 * LOG2E

LANES = 128
BF16_ROWS = 16
ROW_TILE = 512
IN_ROW_TILE = 1024
IN_SUB_TILE = 256
IN_SUB_TILE_ROPE = 128
FF_TILE = 256
N_FF_TILES = D_FF // FF_TILE
DOWN_CHUNK = 11
IN_COLS_PACKED = 2176
VMEM_LIMIT = 56 * 1024 * 1024

_NT = (((1,), (1,)), ((), ()))


def _dot(a, b):
    return jnp.dot(a, b, preferred_element_type=F32)


def _dot_nt(a, b):
    return lax.dot_general(a, b, _NT, preferred_element_type=F32)


def _rms(x, g, n):
    ms = jnp.sum(x * x, axis=-1, keepdims=True) * (1.0 / n)
    return x * lax.rsqrt(ms + EPS) * g


def _lane(shape):
    return lax.broadcasted_iota(jnp.int32, shape, len(shape) - 1)


def _rope(x, c, s, half):
    w = x.shape[-1]
    lo = (_lane(x.shape) % (2 * half)) < half
    sw = jnp.where(lo, pltpu.roll(x, w - half, 1), pltpu.roll(x, half, 1))
    return x * c + sw * s


def _shift_rows(zz, halo, rows):
    n = zz.shape[0]
    prev = pltpu.roll(zz, 1, 0)[halo:halo + rows]
    nxt = pltpu.roll(zz, n - 1, 0)[halo:halo + rows]
    return prev, nxt


def _seq_pos(tile, rows, seq_len):
    r = lax.broadcasted_iota(jnp.int32, (rows, 1), 0)
    return (tile * rows + r) % seq_len


def _ada_body(ncond, c_ref, w_ref, b_ref, o_ref):
    c = c_ref[...]
    s = c * jax.nn.sigmoid(c)
    tn = w_ref.shape[1]
    w = w_ref[...].reshape(D_MODEL // 8, 8, tn)
    rows = []
    for r in range(ncond):
        part = jnp.sum(w * s[:, r:r + 1].reshape(D_MODEL // 8, 8, 1), axis=0)
        rows.append(jnp.sum(part, axis=0, keepdims=True))
    rows.append(jnp.zeros((8 - ncond, tn), F32))
    o_ref[...] = jnp.concatenate(rows, axis=0) + b_ref[...]


def _ada(cond_t, ncond, w_ada, b_ada):
    tn = 1536
    n = 6 * D_MODEL
    return pl.pallas_call(
        functools.partial(_ada_body, ncond),
        grid=(DEPTH, n // tn),
        in_specs=[pl.BlockSpec((D_MODEL, 8), lambda l, j: (0, 0)),
                  pl.BlockSpec((None, D_MODEL, tn), lambda l, j: (l, 0, j)),
                  pl.BlockSpec((None, 1, tn), lambda l, j: (l, 0, j))],
        out_specs=pl.BlockSpec((None, 8, tn), lambda l, j: (l, 0, j)),
        out_shape=jax.ShapeDtypeStruct((DEPTH, 8, n), F32),
        compiler_params=pltpu.CompilerParams(
            dimension_semantics=("arbitrary", "arbitrary"), vmem_limit_bytes=VMEM_LIMIT),
        name="ada",
    )(cond_t, w_ada, b_ada.reshape(DEPTH, 1, n))


def _ctx_body(ckv_ref, kpe_ref, w_ref, kx_ref, vx_ref):
    kv = _dot(ckv_ref[...].astype(BF16), w_ref[...])
    kpe = kpe_ref[...]
    kx_ref[...] = (kv[:, :4 * LANES] + jnp.concatenate([kpe] * MLA_HEADS, axis=1)).astype(BF16)
    vx_ref[...] = kv[:, 4 * LANES:].T.astype(BF16)


def _ctx_mla(cache_ckv, cache_kpe_pad, wkv_p):
    b, _, s, _ = cache_ckv.shape
    return pl.pallas_call(
        _ctx_body,
        grid=(DEPTH, b),
        in_specs=[pl.BlockSpec((None, None, s, MLA_KV_LORA), lambda l, i: (i, l, 0, 0)),
                  pl.BlockSpec((None, None, s, LANES), lambda l, i: (i, l, 0, 0)),
                  pl.BlockSpec((None, MLA_KV_LORA, 6 * LANES), lambda l, i: (l, 0, 0))],
        out_specs=[pl.BlockSpec((None, None, s, 4 * LANES), lambda l, i: (l, i, 0, 0)),
                   pl.BlockSpec((None, None, 2 * LANES, s), lambda l, i: (l, i, 0, 0))],
        out_shape=[jax.ShapeDtypeStruct((DEPTH, b, s, 4 * LANES), BF16),
                   jax.ShapeDtypeStruct((DEPTH, b, 2 * LANES, s), BF16)],
        compiler_params=pltpu.CompilerParams(
            dimension_semantics=("arbitrary", "arbitrary"), vmem_limit_bytes=VMEM_LIMIT),
        name="ctx_mla",
    )(cache_ckv, cache_kpe_pad, wkv_p)


def _inproj_body(rope, n_alias, x_ref, mod_ref, n1_ref, w_ref, wq_ref, wkv_ref, gq_ref, gkv_ref, gqc_ref, gkc_ref,
                 hm_ref, *rest):
    if rope:
        c64_ref, s64_ref, c32_ref, s32_ref = rest[:4]
        outs = rest[4:]
    else:
        outs = rest[n_alias:]
    sh1 = mod_ref[0:1, :]
    sc1 = mod_ref[1:2, :]
    hm = hm_ref[...]
    sub = IN_SUB_TILE_ROPE if rope else IN_SUB_TILE
    nsub = x_ref.shape[0] // sub

    def project(s):
        x = x_ref[s * sub:(s + 1) * sub, :]
        h = _rms(x, n1_ref[...], D_MODEL) * (1.0 + sc1) + sh1
        return _dot(h.astype(BF16), w_ref[...])

    def head_rms(v, g):
        w = v.shape[-1]
        ss = _dot((v * v).astype(BF16), hm[:w, :w])
        return v * lax.rsqrt(ss * (1.0 / HEAD_DIM) + EPS) * g

    def finish(s, acc):
        rs = slice(s * sub, (s + 1) * sub)
        xa, gb, gc = acc[:, 0:256], acc[:, 256:512], acc[:, 512:768]
        z = gc * xa
        blk = acc[:, 768:1024]
        cq = jnp.where(_lane(blk.shape) < MLA_Q_LORA, blk, 0.0)
        cqn = _rms(cq, gq_ref[...], MLA_Q_LORA)
        qm = _dot(cqn.astype(BF16), wq_ref[...]) * MLA_SCALE
        ckvn = _rms(acc[:, 1024:1152], gkv_ref[...], MLA_KV_LORA)
        kv = _dot(ckvn.astype(BF16), wkv_ref[...])
        kpe = pltpu.roll(blk[:, LANES:], 2 * LANES - MLA_Q_LORA, 1)
        kpe = jnp.where(_lane(kpe.shape) < MLA_ROPE, kpe, 0.0)
        vm = kv[:, 4 * LANES:]
        qc = head_rms(acc[:, 1152:1408], gqc_ref[...]) * ATTN_SCALE
        kc = head_rms(acc[:, 1408:1536], gkc_ref[...])
        vc = acc[:, 1536:1664]
        qd = acc[:, 1664:1920] * ATTN_SCALE
        kd = acc[:, 1920:2048]
        vd = acc[:, 2048:2176]

        if not rope:
            (zg_o, qm_o, km_o, vm_o, qc_o, kc_o, vc_o, qd_o, kd_o, vd_o,
             ckv_c, kpe_c, kc_c, vc_c, kd_c, vd_c) = outs
            km = kv[:, :4 * LANES] + jnp.concatenate([kpe] * MLA_HEADS, axis=1)
            zg_o[rs, 0:256] = z.astype(BF16)
            zg_o[rs, 256:512] = gb.astype(BF16)
            qm_o[rs, :] = qm.astype(BF16)
            km_o[rs, :] = km.astype(BF16)
            vm_o[:, rs] = vm.T.astype(BF16)
            qc_o[rs, :] = qc.astype(BF16)
            kc_o[rs, :] = kc.astype(BF16)
            vc_o[:, rs] = vc.T.astype(BF16)
            qd_o[rs, :] = qd.astype(BF16)
            kd_o[rs, :] = kd.astype(BF16)
            vd_o[:, rs] = vd.T.astype(BF16)
            for o, v in ((ckv_c, ckvn), (kpe_c, kpe[:, :MLA_ROPE]), (kc_c, kc), (vc_c, vc), (kd_c, kd),
                         (vd_c, vd)):
                t = o.shape[-2]
                bs = slice(s * sub // t, (s + 1) * sub // t)
                v = v.reshape(sub // t, t, v.shape[-1])
                if n_alias:
                    o[bs] = v
                else:
                    o[bs, 0] = v
                    o[bs, 1:] = jnp.zeros((sub // t, o.shape[1] - 1) + v.shape[1:], F32)
        else:
            (zg_o, qmr_o, qm_o, km_o, vm_o, qcr_o, qc_o, kc_o, vc_o, qdr_o, qd_o, kd_o, vd_o) = outs
            c64, s64, c32, s32 = c64_ref[rs, :], s64_ref[rs, :], c32_ref[rs, :], s32_ref[rs, :]
            rope64 = lambda v: jnp.concatenate(
                [_rope(v[:, i:i + LANES], c64, s64, 16) for i in range(0, v.shape[-1], LANES)], axis=1)
            rope32 = lambda v: jnp.concatenate(
                [_rope(v[:, i:i + LANES], c32, s32, 8) for i in range(0, v.shape[-1], LANES)], axis=1)
            km = kv[:, :4 * LANES] + jnp.concatenate([_rope(kpe, c32, s32, 8)] * MLA_HEADS, axis=1)
            zg_o[rs, 0:256] = z.astype(BF16)
            zg_o[rs, 256:512] = gb.astype(BF16)
            qmr_o[rs, :] = rope32(qm).astype(BF16)
            qm_o[rs, :] = qm.astype(BF16)
            km_o[rs, :] = km.astype(BF16)
            vm_o[:, rs] = vm.T.astype(BF16)
            qcr_o[rs, :] = rope64(qc).astype(BF16)
            qc_o[rs, :] = qc.astype(BF16)
            kc_o[rs, :] = rope64(kc).astype(BF16)
            vc_o[:, rs] = vc.T.astype(BF16)
            qdr_o[rs, :] = rope64(qd).astype(BF16)
            qd_o[rs, :] = qd.astype(BF16)
            kd_o[rs, :] = rope64(kd).astype(BF16)
            vd_o[:, rs] = vd.T.astype(BF16)

    nxt = project(0)
    for s in range(nsub):
        acc = nxt
        if s + 1 < nsub:
            nxt = project(s + 1)
        finish(s, acc)


def _inproj(x, mod, lw, l, seq_len, cond_base, rope_tabs, prev_caches=None):
    rows = x.shape[0]
    r = IN_ROW_TILE
    rope = rope_tabs is not None
    aliases = {}
    tiles_per_seq = max(seq_len // r, 1)
    step = 1 if seq_len >= r else 0

    def cond_map(i):
        return (l, cond_base + step * (i // tiles_per_seq), 0, 0)

    row = lambda w: pl.BlockSpec((r, w), lambda i: (i, 0))
    full = lambda a: pl.BlockSpec((None,) + a.shape[1:], lambda i: (l,) + (0,) * (a.ndim - 1))
    in_specs = [row(D_MODEL),
                pl.BlockSpec((None, None, 6, D_MODEL), cond_map),
                full(lw['norm1']), full(lw['w_in']), full(lw['wq']), full(lw['wkv']),
                full(lw['gq']), full(lw['gkv']), full(lw['gqc']), full(lw['gkc']),
                pl.BlockSpec((2 * LANES, 2 * LANES), lambda i: (0, 0))]
    args = [x, mod, lw['norm1'], lw['w_in'], lw['wq'], lw['wkv'], lw['gq'], lw['gkv'], lw['gqc'], lw['gkc'],
            lw['hm']]
    def act(w, transposed):
        if transposed:
            return jax.ShapeDtypeStruct((w, rows), BF16), pl.BlockSpec((w, r), lambda i: (0, i))
        return jax.ShapeDtypeStruct((rows, w), BF16), row(w)

    if rope:
        tab = pl.BlockSpec((r, LANES), lambda i: (i % tiles_per_seq, 0))
        in_specs += [tab] * 4
        args += list(rope_tabs)
        widths = [(512, 0), (512, 0), (512, 0), (512, 0), (256, 1), (256, 0), (256, 0), (128, 0),
                  (128, 1), (256, 0), (256, 0), (128, 0), (128, 1)]
        out_shape, out_specs = map(list, zip(*[act(w, t) for w, t in widths]))
    else:
        widths = [(512, 0), (512, 0), (512, 0), (256, 1), (256, 0), (128, 0), (128, 1), (256, 0),
                  (128, 0), (128, 1)]
        cwidths = [128, MLA_ROPE, 128, 128, 128, 128]
        nb = r // seq_len
        out_shape, out_specs = map(list, zip(*[act(w, t) for w, t in widths]))
        out_shape += [jax.ShapeDtypeStruct((rows // seq_len, DEPTH, seq_len, w), F32) for w in cwidths]
        assert (prev_caches is None) == (l == 0)
        if l == 0:
            out_specs += [pl.BlockSpec((nb, DEPTH, seq_len, w), lambda i: (i, 0, 0, 0)) for w in cwidths]
        else:
            out_specs += [pl.BlockSpec((nb, None, seq_len, w), lambda i: (i, l, 0, 0)) for w in cwidths]
        if prev_caches is not None:
            aliases = {len(args) + k: len(widths) + k for k in range(len(cwidths))}
            in_specs += [pl.BlockSpec(memory_space=pl.ANY)] * len(cwidths)
            args += list(prev_caches)
    return pl.pallas_call(
        functools.partial(_inproj_body, rope, len(aliases)),
        grid=(rows // r,),
        in_specs=in_specs, out_specs=out_specs, out_shape=out_shape,
        input_output_aliases=aliases,
        compiler_params=pltpu.CompilerParams(
            dimension_semantics=("arbitrary",), vmem_limit_bytes=VMEM_LIMIT),
        name="inproj_rope" if rope else "inproj",
    )(*args)


DEN_FLOOR = 2.0 ** -80


def _key_mags(k):
    return jnp.max(jnp.abs(k), axis=0, keepdims=True).astype(F32)


def _score_bound(kmag, qmag):
    col = jnp.sum(qmag.astype(F32) * kmag, axis=1, keepdims=True)
    return jnp.broadcast_to(col, qmag.shape).T[0:1]


def _softmax_pv_t(s_list, vt_list, first, extra=None, bound=None):
    if bound is None:
        m = s_list[0].max(axis=0, keepdims=True)
        for s in s_list[1:]:
            m = jnp.maximum(m, s.max(axis=0, keepdims=True))
    else:
        m = bound
    if extra is not None:
        m = jnp.maximum(m, extra)
    out = None
    for s, vt in zip(s_list, vt_list):
        ones = jnp.ones(vt.shape, BF16)
        vt = jnp.concatenate([vt, ones] if first else [ones, vt], axis=0)
        pv = _dot(vt, jnp.exp2(s - m).astype(BF16))
        out = pv if out is None else out + pv
    den = out[HEAD_DIM:HEAD_DIM + 1] if first else out[0:1]
    if extra is not None:
        den = den + jnp.exp2(extra - m)
    return out * (1.0 / den), den


def _underflowed(dens):
    width = max(d.shape[1] for d in dens)
    low = None
    for d in dens:
        d = jnp.concatenate([d] * (width // d.shape[1]), axis=1)
        low = d if low is None else jnp.minimum(low, d)
    return jnp.logical_not(jnp.min(low) >= DEN_FLOOR)


def _half_masks(q):
    lo = _lane(q.shape) < HEAD_DIM
    zero = jnp.zeros_like(q)
    return jnp.where(lo, q, zero), jnp.where(lo, zero, q)


def _merge_rows(a, b):
    r = lax.broadcasted_iota(jnp.int32, a.shape, 0)
    return jnp.where(r < HEAD_DIM, a, b)


def _sink_row(sink_ref, l, h0, h1, tq):
    c = lax.broadcasted_iota(jnp.int32, (1, 2 * tq), 1)
    return jnp.where(c < tq, sink_ref[l, h0], sink_ref[l, h1]) * LOG2E


SCORE_AHEAD = 4


def _run_groups(stage1, stage2, n, ahead=SCORE_AHEAD):
    out = []
    pending = [stage1(g) for g in range(min(ahead, n))]
    for g in range(n):
        if g + ahead < n:
            pending.append(stage1(g + ahead))
        out.append(stage2(g, pending.pop(0)))
    return out


def _scores(pairs, kmags):
    s = [_dot_nt(k, q) for k, q in pairs]
    b = None
    if kmags is not None:
        kmag, qmag = None, None
        for (_, q), km in zip(pairs, kmags):
            kmag = km if kmag is None else jnp.maximum(kmag, km)
            qmag = jnp.abs(q) if qmag is None else jnp.maximum(qmag, jnp.abs(q))
        b = _score_bound(kmag, qmag)
    return s, b


def _mla_scores(h, q_sets, k_sets, kmags):
    c = slice(h * LANES, (h + 1) * LANES)
    return _scores([(k[:, c], q[:, c]) for q, k in zip(q_sets, k_sets)], kmags)


def _gqa_scores(j, q_sets, k_sets, kmags):
    pairs = []
    for q, k in zip(q_sets, k_sets):
        qa = _half_masks(q[:, 0:LANES])[j]
        qb = _half_masks(q[:, LANES:])[j]
        pairs.append((k[...], jnp.concatenate([qa, qb], axis=0)))
    return _scores(pairs, kmags)


def _store_pair(y_ref, rows, blk, ot):
    y_ref[rows, blk * LANES:(blk + 1) * LANES] = ot.T.astype(BF16)


N_GROUPS = MLA_HEADS + 4


def _store_groups(y_ref, rows, o, tq):
    for p in range(MLA_HEADS // 2):
        _store_pair(y_ref, rows, p, jnp.concatenate([o[2 * p][:HEAD_DIM], o[2 * p + 1][:HEAD_DIM]], axis=0))
    for base, (o0, o1) in ((2, o[4:6]), (4, o[6:8])):
        _store_pair(y_ref, rows, base, _merge_rows(o0[:, :tq], o1[:, :tq]))
        _store_pair(y_ref, rows, base + 1, _merge_rows(o0[:, tq:], o1[:, tq:]))


def _attn_ctx_body(l, nb, t, sink_ref, qm_ref, km_ref, vm_ref, qc_ref, kc_ref, vc_ref, qd_ref, kd_ref, vd_ref,
                   y_ref):
    def stage1(g):
        b, k = divmod(g, N_GROUPS)
        rs = pl.ds(b * t, t)
        if k < MLA_HEADS:
            return _mla_scores(k, [qm_ref.at[rs]], [km_ref.at[rs]], None)
        q_ref, k_ref = (qc_ref, kc_ref) if k < MLA_HEADS + 2 else (qd_ref, kd_ref)
        return _gqa_scores(k % 2, [q_ref.at[rs]], [k_ref.at[rs]], None)

    def stage2(g, scored):
        s, _ = scored
        b, k = divmod(g, N_GROUPS)
        rs = slice(b * t, (b + 1) * t)
        j = k % 2
        hs = slice(j * HEAD_DIM, (j + 1) * HEAD_DIM)
        if k < MLA_HEADS:
            return _softmax_pv_t(s, [vm_ref[k * HEAD_DIM:(k + 1) * HEAD_DIM, rs]], True)[0]
        if k < MLA_HEADS + 2:
            return _softmax_pv_t(s, [vc_ref[hs, rs]], j == 0)[0]
        return _softmax_pv_t(s, [vd_ref[hs, rs]], j == 0, _sink_row(sink_ref, l, 2 * j, 2 * j + 1, t))[0]

    o = _run_groups(stage1, stage2, N_GROUPS * nb)
    for b in range(nb):
        _store_groups(y_ref, slice(b * t, (b + 1) * t), o[N_GROUPS * b:N_GROUPS * (b + 1)], t)


def _attn_ctx(l, t, sink, qm, km, vm, qc, kc, vc, qd, kd, vd):
    rows = qm.shape[0]
    nb = 8
    r = nb * t
    row = lambda a: (pl.BlockSpec((a.shape[0], r), lambda i: (0, i)) if a.shape[1] == rows
                     else pl.BlockSpec((r, a.shape[1]), lambda i: (i, 0)))
    ins = [qm, km, vm, qc, kc, vc, qd, kd, vd]
    return pl.pallas_call(
        functools.partial(_attn_ctx_body, l, nb, t),
        grid=(rows // r,),
        in_specs=[pl.BlockSpec(memory_space=pltpu.SMEM)] + [row(a) for a in ins],
        out_specs=pl.BlockSpec((r, 6 * LANES), lambda i: (i, 0)),
        out_shape=jax.ShapeDtypeStruct((rows, 6 * LANES), BF16),
        compiler_params=pltpu.CompilerParams(
            dimension_semantics=("arbitrary",), vmem_limit_bytes=VMEM_LIMIT),
        name="attn_ctx",
    )(sink, *ins)


def _window_block(i, tq, t, nwb):
    return jnp.clip(i * (tq // LANES) - WINDOW // LANES, 0, t // LANES - nwb)


def _attn_lat_body(l, tq, t, nwb, sink_ref, qmr_ref, qm_ref, km_ref, vm_ref, kx_ref, vx_ref,
                   qcr_ref, qc_ref, kc_ref, vc_ref, kcx_ref, vcx_ref,
                   qdr_ref, qd_ref, kdx_ref, vdx_ref, *rest):
    kd_refs, vd_refs = rest[:nwb], rest[nwb:2 * nwb]
    y_ref, kmag_scr = rest[2 * nwb:]
    i = pl.program_id(1)
    wk = nwb * LANES
    kpos = _window_block(i, tq, t, nwb) * LANES + lax.broadcasted_iota(jnp.int32, (wk, 2 * tq), 0)
    qpos = i * tq + lax.broadcasted_iota(jnp.int32, (wk, 2 * tq), 1) % tq
    valid = jnp.abs(kpos - qpos) <= WINDOW


    @pl.when(i == 0)
    def _():
        for h in range(MLA_HEADS):
            c = slice(h * LANES, (h + 1) * LANES)
            kmag_scr[h:h + 1, :] = _key_mags(km_ref[:, c])
            kmag_scr[MLA_HEADS + h:MLA_HEADS + h + 1, :] = _key_mags(kx_ref[:, c])
        kmag_scr[8:9, :] = _key_mags(kc_ref[...])
        kmag_scr[9:10, :] = _key_mags(kcx_ref[...])
        kmag_scr[10:11, :] = _key_mags(kdx_ref[...])
        kmag_scr[11:, :] = jnp.zeros((kmag_scr.shape[0] - 11, LANES), F32)

    def run(bounded):
        dens = []
        row = lambda r: kmag_scr[r:r + 1, :]

        def stage1(g):
            if g < MLA_HEADS:
                return _mla_scores(g, [qmr_ref, qm_ref], [km_ref, kx_ref],
                                   [row(g), row(MLA_HEADS + g)] if bounded else None)
            if g < MLA_HEADS + 2:
                return _gqa_scores(g % 2, [qcr_ref, qc_ref], [kc_ref, kcx_ref],
                                   [row(8), row(9)] if bounded else None)
            kw = jnp.concatenate([r[...] for r in kd_refs], axis=0)
            (s1, s2), bound = _gqa_scores(g % 2, [qdr_ref, qd_ref], [kw, kdx_ref],
                                          [_key_mags(kw), row(10)] if bounded else None)
            return [jnp.where(valid, s1, NEG_INF), s2], bound

        def stage2(g, scored):
            s, bound = scored
            j = g % 2
            hs = slice(j * HEAD_DIM, (j + 1) * HEAD_DIM)
            if g < MLA_HEADS:
                hs = slice(g * HEAD_DIM, (g + 1) * HEAD_DIM)
                o, den = _softmax_pv_t(s, [vm_ref[hs, :], vx_ref[hs, :]], True, None, bound)
            elif g < MLA_HEADS + 2:
                o, den = _softmax_pv_t(s, [vc_ref[hs, :], vcx_ref[hs, :]], j == 0, None, bound)
            else:
                vw = jnp.concatenate([r[hs, :] for r in vd_refs], axis=1)
                o, den = _softmax_pv_t(s, [vw, vdx_ref[hs, :]], j == 0,
                                       _sink_row(sink_ref, l, 2 * j, 2 * j + 1, tq), bound)
            dens.append(den)
            return o

        _store_groups(y_ref, slice(None), _run_groups(stage1, stage2, N_GROUPS), tq)
        return dens

    @pl.when(_underflowed(run(True)))
    def _():
        run(False)


def _attn_lat(l, nbatch, t, sink, qmr, qm, km, vmt, kx, vxt, qcr, qc, kc, vct, kcx, vcxt, qdr, qd, kd, vdt,
              kdx, vdxt):
    tq = 256
    nq = t // tq
    nwb = (tq + 2 * WINDOW) // LANES
    nkb = t // LANES
    rows = qm.shape[0]
    qs = lambda a: pl.BlockSpec((tq, a.shape[1]), lambda b, i: (b * nq + i, 0))
    ks = lambda a: pl.BlockSpec((t, a.shape[1]), lambda b, i: (b, 0))
    kts = lambda a: pl.BlockSpec((a.shape[0], t), lambda b, i: (0, b))
    xs = lambda a: pl.BlockSpec((None, None) + a.shape[2:], lambda b, i: (l, b, 0, 0))
    cs = lambda a: pl.BlockSpec((None, None) + a.shape[2:], lambda b, i: (b, l, 0, 0))
    wblk = lambda b, i, k: b * nkb + _window_block(i, tq, t, nwb) + k
    kd_specs = [pl.BlockSpec((LANES, LANES), lambda b, i, k=k: (wblk(b, i, k), 0)) for k in range(nwb)]
    vd_specs = [pl.BlockSpec((LANES, LANES), lambda b, i, k=k: (0, wblk(b, i, k))) for k in range(nwb)]
    in_specs = [pl.BlockSpec(memory_space=pltpu.SMEM),
                qs(qmr), qs(qm), ks(km), kts(vmt), xs(kx), xs(vxt),
                qs(qcr), qs(qc), ks(kc), kts(vct), cs(kcx), cs(vcxt),
                qs(qdr), qs(qd), cs(kdx), cs(vdxt)] + kd_specs + vd_specs
    return pl.pallas_call(
        functools.partial(_attn_lat_body, l, tq, t, nwb),
        grid=(nbatch, nq),
        in_specs=in_specs,
        out_specs=pl.BlockSpec((tq, 6 * LANES), lambda b, i: (b * nq + i, 0)),
        out_shape=jax.ShapeDtypeStruct((rows, 6 * LANES), BF16),
        scratch_shapes=[pltpu.VMEM((16, LANES), F32)],
        compiler_params=pltpu.CompilerParams(
            dimension_semantics=("arbitrary", "arbitrary"), vmem_limit_bytes=VMEM_LIMIT),
        name="attn_lat",
    )(sink, qmr, qm, km, vmt, kx, vxt, qcr, qc, kc, vct, kcx, vcxt, qdr, qd, kdx, vdxt,
      *([kd] * nwb), *([vdt] * nwb))


def _post_body(seq_len, final, mod_ref, n2_ref, cwa_ref, wo_ref, cwf_ref, wup_ref, wdn_ref, fn_ref, *rest):
    o_ref, hbuf, gbuf, pbuf, abuf = rest[-5:]
    ins = rest[:-5]
    r = o_ref.shape[0]
    edges = seq_len > r
    assert edges or r % seq_len == 0
    if edges:
        cat = lambda k: jnp.concatenate([ins[3 * k][...], ins[3 * k + 1][...], ins[3 * k + 2][...]], axis=0)
    else:
        cat = lambda k: ins[k][...]
    halo = BF16_ROWS if edges else 0
    n = r + 2 * halo
    grp = r // 8
    nchunk = D_MODEL // LANES
    i = pl.program_id(0)

    zg = cat(1).astype(F32)
    zz, gb = zg[:, :GROUP_WIDTH], zg[:, GROUP_WIDTH:]
    pos = (i * r - halo + lax.broadcasted_iota(jnp.int32, (n, 1), 0)) % seq_len
    prev = jnp.where(pos == 0, 0.0, pltpu.roll(zz, 1, 0))
    nxt = jnp.where(pos == seq_len - 1, 0.0, pltpu.roll(zz, n - 1, 0))
    cwa = cwa_ref[...]
    ya = gb * (prev * cwa[0:1] + zz * cwa[1:2] + nxt * cwa[2:3])
    ycat = jnp.concatenate([ya.astype(BF16), cat(2)], axis=1)
    xin = cat(0)
    nq = 4
    cuts = [0] + [halo + k * (r // nq) for k in range(1, nq)] + [n]
    ys = [_dot(ycat[cuts[k]:cuts[k + 1]], wo_ref[...]) for k in range(nq)]
    x1s, h2s = [], []
    for k, y in enumerate(ys):
        x1q = xin[cuts[k]:cuts[k + 1]] + mod_ref[2:3, :] * y
        h2q = _rms(x1q, n2_ref[...], D_MODEL) * (1.0 + mod_ref[4:5, :]) + mod_ref[3:4, :]
        for s in range(k * 8 // nq, (k + 1) * 8 // nq):
            lo = halo + s * grp - cuts[k]
            for c in range(nchunk):
                pbuf[c, pl.ds(s, grp, stride=8), :] = h2q[lo:lo + grp, c * LANES:(c + 1) * LANES]
        x1s.append(x1q)
        h2s.append(h2q)
    x1c = jnp.concatenate([x1s[0][halo:]] + x1s[1:-1] + [x1s[-1][:r // nq]], axis=0)
    if edges:
        hrow = lax.broadcasted_iota(jnp.int32, (halo, 1), 0)
        edge = jnp.where(hrow == 0, h2s[0][halo - 1:halo],
                         jnp.where(hrow == 1, h2s[-1][r // nq:r // nq + 1], 0.0))
        hbuf[0:halo, :] = edge.astype(BF16)
    hbuf[halo:, :] = jnp.concatenate([pbuf[c] for c in range(nchunk)], axis=1).astype(BF16)

    sub = lax.broadcasted_iota(jnp.int32, (8, 1), 0)
    seq_first = (i * r + sub * grp) % seq_len == 0
    seq_last = (i * r + sub * grp + grp - 1) % seq_len == seq_len - 1

    def up(j):
        ga = slice(j * FF_TILE, (j + 1) * FF_TILE)
        va = slice(D_FF + j * FF_TILE, D_FF + (j + 1) * FF_TILE)
        return _dot(hbuf[...], wup_ref[:, ga]), _dot(hbuf[...], wup_ref[:, va])

    def conv(u, cw):
        head = pltpu.roll(u[halo + r - 8:halo + r], 1, 0)
        tail = pltpu.roll(u[halo:halo + 8], 7, 0)
        if edges:
            head = jnp.where(sub == 0, u[0:8], head)
            tail = jnp.where(sub == 7, pltpu.roll(u[0:8], 6, 0), tail)
        prev = jnp.concatenate([jnp.where(seq_first, 0.0, head), u[halo:halo + r - 8]], axis=0)
        nxt = jnp.concatenate([u[halo + 8:halo + r], jnp.where(seq_last, 0.0, tail)], axis=0)
        return prev * cw[0:1] + u[halo:halo + r] * cw[1:2] + nxt * cw[2:3]

    acc = None
    nxt_u = up(0)
    for j in range(N_FF_TILES):
        ua, ub = nxt_u
        if j + 1 < N_FF_TILES:
            nxt_u = up(j + 1)
        ga = slice(j * FF_TILE, (j + 1) * FF_TILE)
        va = slice(D_FF + j * FF_TILE, D_FF + (j + 1) * FF_TILE)
        ua = conv(ua, cwf_ref[:, ga])
        ub = conv(ub, cwf_ref[:, va])
        gbuf[:, ga] = (ua * jax.nn.sigmoid(ua) * ub).astype(BF16)
        if j % DOWN_CHUNK == DOWN_CHUNK - 1 or j == N_FF_TILES - 1:
            lo = (j // DOWN_CHUNK) * DOWN_CHUNK * FF_TILE
            hi = (j + 1) * FF_TILE
            d = _dot(gbuf[:, lo:hi], wdn_ref[lo:hi, :])
            acc = d if acc is None else acc + d

    for c in range(nchunk):
        abuf[c] = acc[:, c * LANES:(c + 1) * LANES]
    g2 = mod_ref[5:6, :]
    for c in range(nchunk):
        cs = slice(c * LANES, (c + 1) * LANES)
        for s in range(8):
            rs = slice(s * grp, (s + 1) * grp)
            o_ref[rs, cs] = x1c[rs, cs] + g2[:, cs] * abuf[c, pl.ds(s, grp, stride=8), :]
    if final:
        o_ref[...] = _rms(o_ref[...], fn_ref[...], D_MODEL)


def _halo_specs(r, w, rows):
    nblk = rows // BF16_ROWS
    per = r // BF16_ROWS
    prev = pl.BlockSpec((BF16_ROWS, w), lambda i: (jnp.maximum(i * per - 1, 0), 0))
    cur = pl.BlockSpec((r, w), lambda i: (i, 0))
    nxt = pl.BlockSpec((BF16_ROWS, w), lambda i: (jnp.minimum((i + 1) * per, nblk - 1), 0))
    return prev, cur, nxt


def _cond_spec(l, seq_len, cond_base):
    r = ROW_TILE
    tiles_per_seq = max(seq_len // r, 1)
    step = 1 if seq_len >= r else 0
    return pl.BlockSpec((None, None, 6, D_MODEL),
                        lambda i: (l, cond_base + step * (i // tiles_per_seq), 0, 0))


def _post(x, mod, lw, l, seq_len, cond_base, zg, y, final_norm, final):
    rows = x.shape[0]
    r = ROW_TILE
    full = lambda a: pl.BlockSpec((None,) + a.shape[1:], lambda i: (l,) + (0,) * (a.ndim - 1),
                                  pipeline_mode=pl.Buffered(1))
    acts = [x, zg, y]
    edges = seq_len > r
    if edges:
        act_specs = [s for a in acts for s in _halo_specs(r, a.shape[1], rows)]
        act_args = [a for a in acts for _ in range(3)]
    else:
        act_specs = [pl.BlockSpec((r, a.shape[1]), lambda i: (i, 0)) for a in acts]
        act_args = acts
    return pl.pallas_call(
        functools.partial(_post_body, seq_len, final),
        grid=(rows // r,),
        in_specs=[_cond_spec(l, seq_len, cond_base), full(lw['norm2']), full(lw['conv_a']), full(lw['w_out']),
                  full(lw['conv_ff']), full(lw['w_up']), full(lw['w_down']),
                  pl.BlockSpec((1, D_MODEL), lambda i: (0, 0))] + act_specs,
        out_specs=pl.BlockSpec((r, D_MODEL), lambda i: (i, 0)),
        out_shape=jax.ShapeDtypeStruct((rows, D_MODEL), F32),
        scratch_shapes=[pltpu.VMEM((r + (BF16_ROWS if edges else 0), D_MODEL), BF16), pltpu.VMEM((r, D_FF), BF16),
                        pltpu.VMEM((D_MODEL // LANES, r, LANES), F32),
                        pltpu.VMEM((D_MODEL // LANES, r, LANES), F32)],
        compiler_params=pltpu.CompilerParams(
            dimension_semantics=("arbitrary",), vmem_limit_bytes=VMEM_LIMIT),
        name="post",
    )(mod, lw['norm2'], lw['conv_a'], lw['w_out'], lw['conv_ff'], lw['w_up'], lw['w_down'], final_norm,
      *act_args)


def _pad_cols(a, n):
    return jnp.pad(a, [(0, 0)] * (a.ndim - 1) + [(0, n - a.shape[-1])])


def _perm_heads(a, axis):
    h = [lax.slice_in_dim(a, k * HEAD_DIM, (k + 1) * HEAD_DIM, axis=axis) for k in range(4)]
    return jnp.concatenate([h[0], h[2], h[1], h[3]], axis=axis)


def _prep_weights(w_in, mla_wq_b, mla_wkv_b, w_out, w_up, conv_ff, w_down, norm1, norm2, conv_a,
                  mla_q_norm, mla_kv_norm, gqa_q_norm, gqa_k_norm):
    L = DEPTH
    a = w_in[..., 0:768]
    cq_kpe = _pad_cols(jnp.concatenate([w_in[..., 768:960], w_in[..., 1088:1120]], axis=-1), 256)
    ckv = w_in[..., 960:1088]
    qc = _perm_heads(w_in[..., 1120:1376], 2)
    kvc = w_in[..., 1376:1632]
    qd = _perm_heads(w_in[..., 1632:1888], 2)
    kvd = w_in[..., 1888:2144]
    w_in_p = jnp.concatenate([a, cq_kpe, ckv, qc, kvc, qd, kvd], axis=-1).astype(BF16)
    assert w_in_p.shape[-1] == IN_COLS_PACKED

    wq = mla_wq_b.reshape(L, MLA_Q_LORA, MLA_HEADS, MLA_NOPE + MLA_ROPE)
    wq = jnp.concatenate([wq[..., MLA_NOPE:], wq[..., :MLA_NOPE],
                          jnp.zeros((L, MLA_Q_LORA, MLA_HEADS, LANES - MLA_NOPE - MLA_ROPE), F32)], axis=-1)
    wq = jnp.pad(wq.reshape(L, MLA_Q_LORA, MLA_HEADS * LANES), ((0, 0), (0, 256 - MLA_Q_LORA), (0, 0)))

    wkv = mla_wkv_b.reshape(L, MLA_KV_LORA, MLA_HEADS, MLA_NOPE + MLA_V)
    zk = jnp.zeros((L, MLA_KV_LORA, MLA_HEADS, MLA_ROPE), F32)
    wk = jnp.concatenate([zk, wkv[..., :MLA_NOPE], zk], axis=-1).reshape(L, MLA_KV_LORA, MLA_HEADS * LANES)
    wv = wkv[..., MLA_NOPE:].reshape(L, MLA_KV_LORA, MLA_HEADS * MLA_V)
    wkv_p = jnp.concatenate([wk, wv], axis=-1)

    wo = jnp.concatenate([w_out[:, 0:512], _perm_heads(w_out[:, 512:768], 1),
                          _perm_heads(w_out[:, 768:1024], 1)], axis=1)

    hm = jnp.asarray(np.kron(np.eye(2 * LANES // HEAD_DIM), np.ones((HEAD_DIM, HEAD_DIM))), BF16)
    return {
        'w_in': w_in_p, 'wq': wq.astype(BF16), 'wkv': wkv_p.astype(BF16), 'w_out': wo.astype(BF16),
        'w_up': w_up.astype(BF16), 'conv_ff': conv_ff, 'w_down': w_down.astype(BF16),
        'norm1': norm1.reshape(L, 1, D_MODEL), 'norm2': norm2.reshape(L, 1, D_MODEL), 'conv_a': conv_a,
        'gq': _pad_cols(mla_q_norm, 256).reshape(L, 1, 256), 'gkv': mla_kv_norm.reshape(L, 1, MLA_KV_LORA),
        'gqc': jnp.tile(gqa_q_norm, (1, 4)).reshape(L, 1, 256), 'gkc': jnp.tile(gqa_k_norm, (1, 2)).reshape(L, 1, 128),
        'hm': hm,
    }


def _rope_tables(t):
    rows = t // GRID_W
    row = np.repeat(np.arange(rows, dtype=np.float64), GRID_W)
    col = np.tile(np.arange(GRID_W, dtype=np.float64), rows)

    def tabs(dim):
        half = dim // 2
        inv = np.power(ROPE_THETA, -np.arange(0, half, 2, dtype=np.float64) / half)
        ar = row[:, None] * inv
        ac = col[:, None] * inv
        c = np.concatenate([np.cos(ar), np.cos(ar), np.cos(ac), np.cos(ac)], axis=1)
        s = np.concatenate([-np.sin(ar), np.sin(ar), -np.sin(ac), np.sin(ac)], axis=1)
        return c, s

    c64, s64 = tabs(HEAD_DIM)
    c32, s32 = tabs(MLA_ROPE)
    c64 = np.tile(c64, (1, 2))
    s64 = np.tile(s64, (1, 2))
    c32 = np.concatenate([c32, np.ones((t, LANES - MLA_ROPE))], axis=1)
    s32 = np.concatenate([s32, np.zeros((t, LANES - MLA_ROPE))], axis=1)
    return tuple(jnp.asarray(a.astype(np.float32)) for a in (c64, s64, c32, s32))


def kernel(x_prompt, x_sample, cache_mla_ckv, cache_mla_kpe, cache_gqa_k, cache_gqa_v, cache_swa_k, cache_swa_v,
           c, c_ctx, w_ada, b_ada, norm1, w_in, conv_a, mla_q_norm, mla_wq_b, mla_kv_norm, mla_wkv_b,
           gqa_q_norm, gqa_k_norm, swa_sink, w_out, norm2, w_up, conv_ff, w_down, final_norm):
    B, T, _ = x_prompt.shape
    DB, DT, _ = x_sample.shape
    past = cache_mla_ckv.shape[2]

    lw = _prep_weights(w_in, mla_wq_b, mla_wkv_b, w_out, w_up, conv_ff, w_down, norm1, norm2, conv_a,
                       mla_q_norm, mla_kv_norm, gqa_q_norm, gqa_k_norm)
    rope_tabs = _rope_tables(DT)
    fnorm = final_norm.reshape(1, D_MODEL)

    cond_t = jnp.concatenate([c_ctx[:, None], c.T, jnp.zeros((D_MODEL, 8 - 1 - DB), F32)], axis=1)
    mod = _ada(cond_t, 1 + DB, w_ada, b_ada).reshape(DEPTH, 8, 6, D_MODEL)

    kx, vx = _ctx_mla(cache_mla_ckv, _pad_cols(cache_mla_kpe, LANES), lw['wkv'])
    flat = lambda a: a.reshape(DB, DEPTH, past, 2 * HEAD_DIM).astype(BF16)
    kcx, kdx = flat(cache_gqa_k), flat(cache_swa_k)
    vcx, vdx = jnp.swapaxes(flat(cache_gqa_v), 2, 3), jnp.swapaxes(flat(cache_swa_v), 2, 3)

    xp = x_prompt.reshape(B * T, D_MODEL)
    xs = x_sample.reshape(DB * DT, D_MODEL)
    caches = None
    for l in range(DEPTH):
        final = l == DEPTH - 1
        (zg, qm, km, vm, qc, kc, vc, qd, kd, vd, *caches) = _inproj(xp, mod, lw, l, T, 0, None, caches)
        y = _attn_ctx(l, T, swa_sink, qm, km, vm, qc, kc, vc, qd, kd, vd)
        xp = _post(xp, mod, lw, l, T, 0, zg, y, fnorm, final)
        (zg, qmr, qm, km, vm, qcr, qc, kc, vc, qdr, qd, kd, vd) = _inproj(xs, mod, lw, l, DT, 1, rope_tabs)
        y = _attn_lat(l, DB, DT, swa_sink, qmr, qm, km, vm, kx, vx, qcr, qc, kc, vc, kcx, vcx,
                      qdr, qd, kd, vd, kdx, vdx)
        xs = _post(xs, mod, lw, l, DT, 1, zg, y, fnorm, final)

    heads = lambda a: a.reshape(B, DEPTH, T, 2, HEAD_DIM)
    return (xp.reshape(B, T, D_MODEL), xs.reshape(DB, DT, D_MODEL), caches[0], caches[1],
            heads(caches[2]), heads(caches[3]), heads(caches[4]), heads(caches[5]))
```

```python
import functools
import math

import jax
import jax.numpy as jnp
import numpy as np
from jax import lax
from jax.experimental import pallas as pl
from jax.experimental.pallas import tpu as pltpu

F32 = jnp.float32
BF16 = jnp.bfloat16

D_MODEL = 1024
DEPTH = 2
GRID_W = 64
HEAD_DIM = 64
GROUP_WIDTH = D_MODEL // 4
MLA_HEADS = 4
MLA_NOPE = 64
MLA_ROPE = 32
MLA_V = 64
MLA_Q_LORA = 192
MLA_KV_LORA = 128
WINDOW = 128
D_FF = 2816
ROPE_THETA = 10000.0
EPS = 1e-6
NEG_INF = -1e30
LOG2E = math.log2(math.e)
ATTN_SCALE = HEAD_DIM ** -0.5 * LOG2E
MLA_SCALE = (MLA_NOPE + MLA_ROPE) ** -0.5 * LOG2E

LANES = 128
BF16_ROWS = 16
ROW_TILE = 512
IN_ROW_TILE = 1024
IN_SUB_TILE = 256
IN_SUB_TILE_ROPE = 128
FF_TILE = 256
N_FF_TILES = D_FF // FF_TILE
DOWN_CHUNK = 11
IN_COLS_PACKED = 2176
VMEM_LIMIT = 56 * 1024 * 1024

_NT = (((1,), (1,)), ((), ()))


def _dot(a, b):
    return jnp.dot(a, b, preferred_element_type=F32)


def _dot_nt(a, b):
    return lax.dot_general(a, b, _NT, preferred_element_type=F32)


def _rms(x, g, n):
    ms = jnp.sum(x * x, axis=-1, keepdims=True) * (1.0 / n)
    return x * lax.rsqrt(ms + EPS) * g


def _lane(shape):
    return lax.broadcasted_iota(jnp.int32, shape, len(shape) - 1)


def _rope(x, c, s, half):
    w = x.shape[-1]
    lo = (_lane(x.shape) % (2 * half)) < half
    sw = jnp.where(lo, pltpu.roll(x, w - half, 1), pltpu.roll(x, half, 1))
    return x * c + sw * s


def _shift_rows(zz, halo, rows):
    n = zz.shape[0]
    prev = pltpu.roll(zz, 1, 0)[halo:halo + rows]
    nxt = pltpu.roll(zz, n - 1, 0)[halo:halo + rows]
    return prev, nxt


def _seq_pos(tile, rows, seq_len):
    r = lax.broadcasted_iota(jnp.int32, (rows, 1), 0)
    return (tile * rows + r) % seq_len


def _ada_body(ncond, c_ref, w_ref, b_ref, o_ref):
    c = c_ref[...]
    s = c * jax.nn.sigmoid(c)
    tn = w_ref.shape[1]
    w = w_ref[...].reshape(D_MODEL // 8, 8, tn)
    rows = []
    for r in range(ncond):
        part = jnp.sum(w * s[:, r:r + 1].reshape(D_MODEL // 8, 8, 1), axis=0)
        rows.append(jnp.sum(part, axis=0, keepdims=True))
    rows.append(jnp.zeros((8 - ncond, tn), F32))
    o_ref[...] = jnp.concatenate(rows, axis=0) + b_ref[...]


def _ada(cond_t, ncond, w_ada, b_ada):
    tn = 1536
    n = 6 * D_MODEL
    return pl.pallas_call(
        functools.partial(_ada_body, ncond),
        grid=(DEPTH, n // tn),
        in_specs=[pl.BlockSpec((D_MODEL, 8), lambda l, j: (0, 0)),
                  pl.BlockSpec((None, D_MODEL, tn), lambda l, j: (l, 0, j)),
                  pl.BlockSpec((None, 1, tn), lambda l, j: (l, 0, j))],
        out_specs=pl.BlockSpec((None, 8, tn), lambda l, j: (l, 0, j)),
        out_shape=jax.ShapeDtypeStruct((DEPTH, 8, n), F32),
        compiler_params=pltpu.CompilerParams(
            dimension_semantics=("arbitrary", "arbitrary"), vmem_limit_bytes=VMEM_LIMIT),
        name="ada",
    )(cond_t, w_ada, b_ada.reshape(DEPTH, 1, n))


def _ctx_body(ckv_ref, kpe_ref, w_ref, kx_ref, vx_ref):
    kv = _dot(ckv_ref[...].astype(BF16), w_ref[...])
    kpe = kpe_ref[...]
    kx_ref[...] = (kv[:, :4 * LANES] + jnp.concatenate([kpe] * MLA_HEADS, axis=1)).astype(BF16)
    vx_ref[...] = kv[:, 4 * LANES:].T.astype(BF16)


def _ctx_mla(cache_ckv, cache_kpe_pad, wkv_p):
    b, _, s, _ = cache_ckv.shape
    return pl.pallas_call(
        _ctx_body,
        grid=(DEPTH, b),
        in_specs=[pl.BlockSpec((None, None, s, MLA_KV_LORA), lambda l, i: (i, l, 0, 0)),
                  pl.BlockSpec((None, None, s, LANES), lambda l, i: (i, l, 0, 0)),
                  pl.BlockSpec((None, MLA_KV_LORA, 6 * LANES), lambda l, i: (l, 0, 0))],
        out_specs=[pl.BlockSpec((None, None, s, 4 * LANES), lambda l, i: (l, i, 0, 0)),
                   pl.BlockSpec((None, None, 2 * LANES, s), lambda l, i: (l, i, 0, 0))],
        out_shape=[jax.ShapeDtypeStruct((DEPTH, b, s, 4 * LANES), BF16),
                   jax.ShapeDtypeStruct((DEPTH, b, 2 * LANES, s), BF16)],
        compiler_params=pltpu.CompilerParams(
            dimension_semantics=("arbitrary", "arbitrary"), vmem_limit_bytes=VMEM_LIMIT),
        name="ctx_mla",
    )(cache_ckv, cache_kpe_pad, wkv_p)


def _inproj_body(rope, n_alias, x_ref, mod_ref, n1_ref, w_ref, wq_ref, wkv_ref, gq_ref, gkv_ref, gqc_ref, gkc_ref,
                 hm_ref, *rest):
    if rope:
        c64_ref, s64_ref, c32_ref, s32_ref = rest[:4]
        outs = rest[4:]
    else:
        outs = rest[n_alias:]
    sh1 = mod_ref[0:1, :]
    sc1 = mod_ref[1:2, :]
    hm = hm_ref[...]
    sub = IN_SUB_TILE_ROPE if rope else IN_SUB_TILE
    nsub = x_ref.shape[0] // sub

    def project(s):
        x = x_ref[s * sub:(s + 1) * sub, :]
        h = _rms(x, n1_ref[...], D_MODEL) * (1.0 + sc1) + sh1
        return _dot(h.astype(BF16), w_ref[...])

    def head_rms(v, g):
        w = v.shape[-1]
        ss = _dot((v * v).astype(BF16), hm[:w, :w])
        return v * lax.rsqrt(ss * (1.0 / HEAD_DIM) + EPS) * g

    def finish(s, acc):
        rs = slice(s * sub, (s + 1) * sub)
        xa, gb, gc = acc[:, 0:256], acc[:, 256:512], acc[:, 512:768]
        z = gc * xa
        blk = acc[:, 768:1024]
        cq = jnp.where(_lane(blk.shape) < MLA_Q_LORA, blk, 0.0)
        cqn = _rms(cq, gq_ref[...], MLA_Q_LORA)
        qm = _dot(cqn.astype(BF16), wq_ref[...]) * MLA_SCALE
        ckvn = _rms(acc[:, 1024:1152], gkv_ref[...], MLA_KV_LORA)
        kv = _dot(ckvn.astype(BF16), wkv_ref[...])
        kpe = pltpu.roll(blk[:, LANES:], 2 * LANES - MLA_Q_LORA, 1)
        kpe = jnp.where(_lane(kpe.shape) < MLA_ROPE, kpe, 0.0)
        vm = kv[:, 4 * LANES:]
        qc = head_rms(acc[:, 1152:1408], gqc_ref[...]) * ATTN_SCALE
        kc = head_rms(acc[:, 1408:1536], gkc_ref[...])
        vc = acc[:, 1536:1664]
        qd = acc[:, 1664:1920] * ATTN_SCALE
        kd = acc[:, 1920:2048]
        vd = acc[:, 2048:2176]

        if not rope:
            (zg_o, qm_o, km_o, vm_o, qc_o, kc_o, vc_o, qd_o, kd_o, vd_o,
             ckv_c, kpe_c, kc_c, vc_c, kd_c, vd_c) = outs
            km = kv[:, :4 * LANES] + jnp.concatenate([kpe] * MLA_HEADS, axis=1)
            zg_o[rs, 0:256] = z.astype(BF16)
            zg_o[rs, 256:512] = gb.astype(BF16)
            qm_o[rs, :] = qm.astype(BF16)
            km_o[rs, :] = km.astype(BF16)
            vm_o[:, rs] = vm.T.astype(BF16)
            qc_o[rs, :] = qc.astype(BF16)
            kc_o[rs, :] = kc.astype(BF16)
            vc_o[:, rs] = vc.T.astype(BF16)
            qd_o[rs, :] = qd.astype(BF16)
            kd_o[rs, :] = kd.astype(BF16)
            vd_o[:, rs] = vd.T.astype(BF16)
            for o, v in ((ckv_c, ckvn), (kpe_c, kpe[:, :MLA_ROPE]), (kc_c, kc), (vc_c, vc), (kd_c, kd),
                         (vd_c, vd)):
                t = o.shape[-2]
                bs = slice(s * sub // t, (s + 1) * sub // t)
                v = v.reshape(sub // t, t, v.shape[-1])
                if n_alias:
                    o[bs] = v
                else:
                    o[bs, 0] = v
                    o[bs, 1:] = jnp.zeros((sub // t, o.shape[1] - 1) + v.shape[1:], F32)
        else:
            (zg_o, qmr_o, qm_o, km_o, vm_o, qcr_o, qc_o, kc_o, vc_o, qdr_o, qd_o, kd_o, vd_o) = outs
            c64, s64, c32, s32 = c64_ref[rs, :], s64_ref[rs, :], c32_ref[rs, :], s32_ref[rs, :]
            rope64 = lambda v: jnp.concatenate(
                [_rope(v[:, i:i + LANES], c64, s64, 16) for i in range(0, v.shape[-1], LANES)], axis=1)
            rope32 = lambda v: jnp.concatenate(
                [_rope(v[:, i:i + LANES], c32, s32, 8) for i in range(0, v.shape[-1], LANES)], axis=1)
            km = kv[:, :4 * LANES] + jnp.concatenate([_rope(kpe, c32, s32, 8)] * MLA_HEADS, axis=1)
            zg_o[rs, 0:256] = z.astype(BF16)
            zg_o[rs, 256:512] = gb.astype(BF16)
            qmr_o[rs, :] = rope32(qm).astype(BF16)
            qm_o[rs, :] = qm.astype(BF16)
            km_o[rs, :] = km.astype(BF16)
            vm_o[:, rs] = vm.T.astype(BF16)
            qcr_o[rs, :] = rope64(qc).astype(BF16)
            qc_o[rs, :] = qc.astype(BF16)
            kc_o[rs, :] = rope64(kc).astype(BF16)
            vc_o[:, rs] = vc.T.astype(BF16)
            qdr_o[rs, :] = rope64(qd).astype(BF16)
            qd_o[rs, :] = qd.astype(BF16)
            kd_o[rs, :] = rope64(kd).astype(BF16)
            vd_o[:, rs] = vd.T.astype(BF16)

    nxt = project(0)
    for s in range(nsub):
        acc = nxt
        if s + 1 < nsub:
            nxt = project(s + 1)
        finish(s, acc)


def _inproj(x, mod, lw, l, seq_len, cond_base, rope_tabs, prev_caches=None):
    rows = x.shape[0]
    r = IN_ROW_TILE
    rope = rope_tabs is not None
    aliases = {}
    tiles_per_seq = max(seq_len // r, 1)
    step = 1 if seq_len >= r else 0

    def cond_map(i):
        return (l, cond_base + step * (i // tiles_per_seq), 0, 0)

    row = lambda w: pl.BlockSpec((r, w), lambda i: (i, 0))
    full = lambda a: pl.BlockSpec((None,) + a.shape[1:], lambda i: (l,) + (0,) * (a.ndim - 1))
    in_specs = [row(D_MODEL),
                pl.BlockSpec((None, None, 6, D_MODEL), cond_map),
                full(lw['norm1']), full(lw['w_in']), full(lw['wq']), full(lw['wkv']),
                full(lw['gq']), full(lw['gkv']), full(lw['gqc']), full(lw['gkc']),
                pl.BlockSpec((2 * LANES, 2 * LANES), lambda i: (0, 0))]
    args = [x, mod, lw['norm1'], lw['w_in'], lw['wq'], lw['wkv'], lw['gq'], lw['gkv'], lw['gqc'], lw['gkc'],
            lw['hm']]
    def act(w, transposed):
        if transposed:
            return jax.ShapeDtypeStruct((w, rows), BF16), pl.BlockSpec((w, r), lambda i: (0, i))
        return jax.ShapeDtypeStruct((rows, w), BF16), row(w)

    if rope:
        tab = pl.BlockSpec((r, LANES), lambda i: (i % tiles_per_seq, 0))
        in_specs += [tab] * 4
        args += list(rope_tabs)
        widths = [(512, 0), (512, 0), (512, 0), (512, 0), (256, 1), (256, 0), (256, 0), (128, 0),
                  (128, 1), (256, 0), (256, 0), (128, 0), (128, 1)]
        out_shape, out_specs = map(list, zip(*[act(w, t) for w, t in widths]))
    else:
        widths = [(512, 0), (512, 0), (512, 0), (256, 1), (256, 0), (128, 0), (128, 1), (256, 0),
                  (128, 0), (128, 1)]
        cwidths = [128, MLA_ROPE, 128, 128, 128, 128]
        nb = r // seq_len
        out_shape, out_specs = map(list, zip(*[act(w, t) for w, t in widths]))
        out_shape += [jax.ShapeDtypeStruct((rows // seq_len, DEPTH, seq_len, w), F32) for w in cwidths]
        assert (prev_caches is None) == (l == 0)
        if l == 0:
            out_specs += [pl.BlockSpec((nb, DEPTH, seq_len, w), lambda i: (i, 0, 0, 0)) for w in cwidths]
        else:
            out_specs += [pl.BlockSpec((nb, None, seq_len, w), lambda i: (i, l, 0, 0)) for w in cwidths]
        if prev_caches is not None:
            aliases = {len(args) + k: len(widths) + k for k in range(len(cwidths))}
            in_specs += [pl.BlockSpec(memory_space=pl.ANY)] * len(cwidths)
            args += list(prev_caches)
    return pl.pallas_call(
        functools.partial(_inproj_body, rope, len(aliases)),
        grid=(rows // r,),
        in_specs=in_specs, out_specs=out_specs, out_shape=out_shape,
        input_output_aliases=aliases,
        compiler_params=pltpu.CompilerParams(
            dimension_semantics=("arbitrary",), vmem_limit_bytes=VMEM_LIMIT),
        name="inproj_rope" if rope else "inproj",
    )(*args)


DEN_FLOOR = 2.0 ** -80


def _key_mags(k):
    return jnp.max(jnp.abs(k), axis=0, keepdims=True).astype(F32)


def _score_bound(kmag, qmag):
    col = jnp.sum(qmag.astype(F32) * kmag, axis=1, keepdims=True)
    return jnp.broadcast_to(col, qmag.shape).T[0:1]


def _softmax_pv_t(s_list, vt_list, first, extra=None, bound=None):
    if bound is None:
        m = s_list[0].max(axis=0, keepdims=True)
        for s in s_list[1:]:
            m = jnp.maximum(m, s.max(axis=0, keepdims=True))
    else:
        m = bound
    if extra is not None:
        m = jnp.maximum(m, extra)
    out = None
    for s, vt in zip(s_list, vt_list):
        ones = jnp.ones(vt.shape, BF16)
        vt = jnp.concatenate([vt, ones] if first else [ones, vt], axis=0)
        pv = _dot(vt, jnp.exp2(s - m).astype(BF16))
        out = pv if out is None else out + pv
    den = out[HEAD_DIM:HEAD_DIM + 1] if first else out[0:1]
    if extra is not None:
        den = den + jnp.exp2(extra - m)
    return out * (1.0 / den), den


def _underflowed(dens):
    width = max(d.shape[1] for d in dens)
    low = None
    for d in dens:
        d = jnp.concatenate([d] * (width // d.shape[1]), axis=1)
        low = d if low is None else jnp.minimum(low, d)
    return jnp.logical_not(jnp.min(low) >= DEN_FLOOR)


def _half_masks(q):
    lo = _lane(q.shape) < HEAD_DIM
    zero = jnp.zeros_like(q)
    return jnp.where(lo, q, zero), jnp.where(lo, zero, q)


def _merge_rows(a, b):
    r = lax.broadcasted_iota(jnp.int32, a.shape, 0)
    return jnp.where(r < HEAD_DIM, a, b)


def _sink_row(sink_ref, l, h0, h1, tq):
    c = lax.broadcasted_iota(jnp.int32, (1, 2 * tq), 1)
    return jnp.where(c < tq, sink_ref[l, h0], sink_ref[l, h1]) * LOG2E


SCORE_AHEAD = 4


def _run_groups(stage1, stage2, n, ahead=SCORE_AHEAD):
    out = []
    pending = [stage1(g) for g in range(min(ahead, n))]
    for g in range(n):
        if g + ahead < n:
            pending.append(stage1(g + ahead))
        out.append(stage2(g, pending.pop(0)))
    return out


def _scores(pairs, kmags):
    s = [_dot_nt(k, q) for k, q in pairs]
    b = None
    if kmags is not None:
        kmag, qmag = None, None
        for (_, q), km in zip(pairs, kmags):
            kmag = km if kmag is None else jnp.maximum(kmag, km)
            qmag = jnp.abs(q) if qmag is None else jnp.maximum(qmag, jnp.abs(q))
        b = _score_bound(kmag, qmag)
    return s, b


def _mla_scores(h, q_sets, k_sets, kmags):
    c = slice(h * LANES, (h + 1) * LANES)
    return _scores([(k[:, c], q[:, c]) for q, k in zip(q_sets, k_sets)], kmags)


def _gqa_scores(j, q_sets, k_sets, kmags):
    pairs = []
    for q, k in zip(q_sets, k_sets):
        qa = _half_masks(q[:, 0:LANES])[j]
        qb = _half_masks(q[:, LANES:])[j]
        pairs.append((k[...], jnp.concatenate([qa, qb], axis=0)))
    return _scores(pairs, kmags)


def _store_pair(y_ref, rows, blk, ot):
    y_ref[rows, blk * LANES:(blk + 1) * LANES] = ot.T.astype(BF16)


N_GROUPS = MLA_HEADS + 4


def _store_groups(y_ref, rows, o, tq):
    for p in range(MLA_HEADS // 2):
        _store_pair(y_ref, rows, p, jnp.concatenate([o[2 * p][:HEAD_DIM], o[2 * p + 1][:HEAD_DIM]], axis=0))
    for base, (o0, o1) in ((2, o[4:6]), (4, o[6:8])):
        _store_pair(y_ref, rows, base, _merge_rows(o0[:, :tq], o1[:, :tq]))
        _store_pair(y_ref, rows, base + 1, _merge_rows(o0[:, tq:], o1[:, tq:]))


def _attn_ctx_body(l, nb, t, sink_ref, qm_ref, km_ref, vm_ref, qc_ref, kc_ref, vc_ref, qd_ref, kd_ref, vd_ref,
                   y_ref):
    def stage1(g):
        b, k = divmod(g, N_GROUPS)
        rs = pl.ds(b * t, t)
        if k < MLA_HEADS:
            return _mla_scores(k, [qm_ref.at[rs]], [km_ref.at[rs]], None)
        q_ref, k_ref = (qc_ref, kc_ref) if k < MLA_HEADS + 2 else (qd_ref, kd_ref)
        return _gqa_scores(k % 2, [q_ref.at[rs]], [k_ref.at[rs]], None)

    def stage2(g, scored):
        s, _ = scored
        b, k = divmod(g, N_GROUPS)
        rs = slice(b * t, (b + 1) * t)
        j = k % 2
        hs = slice(j * HEAD_DIM, (j + 1) * HEAD_DIM)
        if k < MLA_HEADS:
            return _softmax_pv_t(s, [vm_ref[k * HEAD_DIM:(k + 1) * HEAD_DIM, rs]], True)[0]
        if k < MLA_HEADS + 2:
            return _softmax_pv_t(s, [vc_ref[hs, rs]], j == 0)[0]
        return _softmax_pv_t(s, [vd_ref[hs, rs]], j == 0, _sink_row(sink_ref, l, 2 * j, 2 * j + 1, t))[0]

    o = _run_groups(stage1, stage2, N_GROUPS * nb)
    for b in range(nb):
        _store_groups(y_ref, slice(b * t, (b + 1) * t), o[N_GROUPS * b:N_GROUPS * (b + 1)], t)


def _attn_ctx(l, t, sink, qm, km, vm, qc, kc, vc, qd, kd, vd):
    rows = qm.shape[0]
    nb = 8
    r = nb * t
    row = lambda a: (pl.BlockSpec((a.shape[0], r), lambda i: (0, i)) if a.shape[1] == rows
                     else pl.BlockSpec((r, a.shape[1]), lambda i: (i, 0)))
    ins = [qm, km, vm, qc, kc, vc, qd, kd, vd]
    return pl.pallas_call(
        functools.partial(_attn_ctx_body, l, nb, t),
        grid=(rows // r,),
        in_specs=[pl.BlockSpec(memory_space=pltpu.SMEM)] + [row(a) for a in ins],
        out_specs=pl.BlockSpec((r, 6 * LANES), lambda i: (i, 0)),
        out_shape=jax.ShapeDtypeStruct((rows, 6 * LANES), BF16),
        compiler_params=pltpu.CompilerParams(
            dimension_semantics=("arbitrary",), vmem_limit_bytes=VMEM_LIMIT),
        name="attn_ctx",
    )(sink, *ins)


def _window_block(i, tq, t, nwb):
    return jnp.clip(i * (tq // LANES) - WINDOW // LANES, 0, t // LANES - nwb)


def _attn_lat_body(l, tq, t, nwb, sink_ref, qmr_ref, qm_ref, km_ref, vm_ref, kx_ref, vx_ref,
                   qcr_ref, qc_ref, kc_ref, vc_ref, kcx_ref, vcx_ref,
                   qdr_ref, qd_ref, kdx_ref, vdx_ref, *rest):
    kd_refs, vd_refs = rest[:nwb], rest[nwb:2 * nwb]
    y_ref, kmag_scr = rest[2 * nwb:]
    i = pl.program_id(1)
    wk = nwb * LANES
    kpos = _window_block(i, tq, t, nwb) * LANES + lax.broadcasted_iota(jnp.int32, (wk, 2 * tq), 0)
    qpos = i * tq + lax.broadcasted_iota(jnp.int32, (wk, 2 * tq), 1) % tq
    valid = jnp.abs(kpos - qpos) <= WINDOW


    @pl.when(i == 0)
    def _():
        for h in range(MLA_HEADS):
            c = slice(h * LANES, (h + 1) * LANES)
            kmag_scr[h:h + 1, :] = _key_mags(km_ref[:, c])
            kmag_scr[MLA_HEADS + h:MLA_HEADS + h + 1, :] = _key_mags(kx_ref[:, c])
        kmag_scr[8:9, :] = _key_mags(kc_ref[...])
        kmag_scr[9:10, :] = _key_mags(kcx_ref[...])
        kmag_scr[10:11, :] = _key_mags(kdx_ref[...])
        kmag_scr[11:, :] = jnp.zeros((kmag_scr.shape[0] - 11, LANES), F32)

    def run(bounded):
        dens = []
        row = lambda r: kmag_scr[r:r + 1, :]

        def stage1(g):
            if g < MLA_HEADS:
                return _mla_scores(g, [qmr_ref, qm_ref], [km_ref, kx_ref],
                                   [row(g), row(MLA_HEADS + g)] if bounded else None)
            if g < MLA_HEADS + 2:
                return _gqa_scores(g % 2, [qcr_ref, qc_ref], [kc_ref, kcx_ref],
                                   [row(8), row(9)] if bounded else None)
            kw = jnp.concatenate([r[...] for r in kd_refs], axis=0)
            (s1, s2), bound = _gqa_scores(g % 2, [qdr_ref, qd_ref], [kw, kdx_ref],
                                          [_key_mags(kw), row(10)] if bounded else None)
            return [jnp.where(valid, s1, NEG_INF), s2], bound

        def stage2(g, scored):
            s, bound = scored
            j = g % 2
            hs = slice(j * HEAD_DIM, (j + 1) * HEAD_DIM)
            if g < MLA_HEADS:
                hs = slice(g * HEAD_DIM, (g + 1) * HEAD_DIM)
                o, den = _softmax_pv_t(s, [vm_ref[hs, :], vx_ref[hs, :]], True, None, bound)
            elif g < MLA_HEADS + 2:
                o, den = _softmax_pv_t(s, [vc_ref[hs, :], vcx_ref[hs, :]], j == 0, None, bound)
            else:
                vw = jnp.concatenate([r[hs, :] for r in vd_refs], axis=1)
                o, den = _softmax_pv_t(s, [vw, vdx_ref[hs, :]], j == 0,
                                       _sink_row(sink_ref, l, 2 * j, 2 * j + 1, tq), bound)
            dens.append(den)
            return o

        _store_groups(y_ref, slice(None), _run_groups(stage1, stage2, N_GROUPS), tq)
        return dens

    @pl.when(_underflowed(run(True)))
    def _():
        run(False)


def _attn_lat(l, nbatch, t, sink, qmr, qm, km, vmt, kx, vxt, qcr, qc, kc, vct, kcx, vcxt, qdr, qd, kd, vdt,
              kdx, vdxt):
    tq = 256
    nq = t // tq
    nwb = (tq + 2 * WINDOW) // LANES
    nkb = t // LANES
    rows = qm.shape[0]
    qs = lambda a: pl.BlockSpec((tq, a.shape[1]), lambda b, i: (b * nq + i, 0))
    ks = lambda a: pl.BlockSpec((t, a.shape[1]), lambda b, i: (b, 0))
    kts = lambda a: pl.BlockSpec((a.shape[0], t), lambda b, i: (0, b))
    xs = lambda a: pl.BlockSpec((None, None) + a.shape[2:], lambda b, i: (l, b, 0, 0))
    cs = lambda a: pl.BlockSpec((None, None) + a.shape[2:], lambda b, i: (b, l, 0, 0))
    wblk = lambda b, i, k: b * nkb + _window_block(i, tq, t, nwb) + k
    kd_specs = [pl.BlockSpec((LANES, LANES), lambda b, i, k=k: (wblk(b, i, k), 0)) for k in range(nwb)]
    vd_specs = [pl.BlockSpec((LANES, LANES), lambda b, i, k=k: (0, wblk(b, i, k))) for k in range(nwb)]
    in_specs = [pl.BlockSpec(memory_space=pltpu.SMEM),
                qs(qmr), qs(qm), ks(km), kts(vmt), xs(kx), xs(vxt),
                qs(qcr), qs(qc), ks(kc), kts(vct), cs(kcx), cs(vcxt),
                qs(qdr), qs(qd), cs(kdx), cs(vdxt)] + kd_specs + vd_specs
    return pl.pallas_call(
        functools.partial(_attn_lat_body, l, tq, t, nwb),
        grid=(nbatch, nq),
        in_specs=in_specs,
        out_specs=pl.BlockSpec((tq, 6 * LANES), lambda b, i: (b * nq + i, 0)),
        out_shape=jax.ShapeDtypeStruct((rows, 6 * LANES), BF16),
        scratch_shapes=[pltpu.VMEM((16, LANES), F32)],
        compiler_params=pltpu.CompilerParams(
            dimension_semantics=("arbitrary", "arbitrary"), vmem_limit_bytes=VMEM_LIMIT),
        name="attn_lat",
    )(sink, qmr, qm, km, vmt, kx, vxt, qcr, qc, kc, vct, kcx, vcxt, qdr, qd, kdx, vdxt,
      *([kd] * nwb), *([vdt] * nwb))


def _post_body(seq_len, final, mod_ref, n2_ref, cwa_ref, wo_ref, cwf_ref, wup_ref, wdn_ref, fn_ref, *rest):
    o_ref, hbuf, gbuf, pbuf, abuf = rest[-5:]
    ins = rest[:-5]
    r = o_ref.shape[0]
    edges = seq_len > r
    assert edges or r % seq_len == 0
    if edges:
        cat = lambda k: jnp.concatenate([ins[3 * k][...], ins[3 * k + 1][...], ins[3 * k + 2][...]], axis=0)
    else:
        cat = lambda k: ins[k][...]
    halo = BF16_ROWS if edges else 0
    n = r + 2 * halo
    grp = r // 8
    nchunk = D_MODEL // LANES
    i = pl.program_id(0)

    zg = cat(1).astype(F32)
    zz, gb = zg[:, :GROUP_WIDTH], zg[:, GROUP_WIDTH:]
    pos = (i * r - halo + lax.broadcasted_iota(jnp.int32, (n, 1), 0)) % seq_len
    prev = jnp.where(pos == 0, 0.0, pltpu.roll(zz, 1, 0))
    nxt = jnp.where(pos == seq_len - 1, 0.0, pltpu.roll(zz, n - 1, 0))
    cwa = cwa_ref[...]
    ya = gb * (prev * cwa[0:1] + zz * cwa[1:2] + nxt * cwa[2:3])
    ycat = jnp.concatenate([ya.astype(BF16), cat(2)], axis=1)
    x1 = cat(0) + mod_ref[2:3, :] * _dot(ycat, wo_ref[...])
    h2 = _rms(x1, n2_ref[...], D_MODEL) * (1.0 + mod_ref[4:5, :]) + mod_ref[3:4, :]
    x1c = x1[halo:halo + r]

    for c in range(nchunk):
        for s in range(8):
            pbuf[c, pl.ds(s, grp, stride=8), :] = h2[halo + s * grp:halo + (s + 1) * grp, c * LANES:(c + 1) * LANES]
    if edges:
        hrow = lax.broadcasted_iota(jnp.int32, (halo, 1), 0)
        edge = jnp.where(hrow == 0, h2[halo - 1:halo], jnp.where(hrow == 1, h2[halo + r:halo + r + 1], 0.0))
        hbuf[0:halo, :] = edge.astype(BF16)
    hbuf[halo:, :] = jnp.concatenate([pbuf[c] for c in range(nchunk)], axis=1).astype(BF16)

    sub = lax.broadcasted_iota(jnp.int32, (8, 1), 0)
    seq_first = (i * r + sub * grp) % seq_len == 0
    seq_last = (i * r + sub * grp + grp - 1) % seq_len == seq_len - 1

    def up(j):
        ga = slice(j * FF_TILE, (j + 1) * FF_TILE)
        va = slice(D_FF + j * FF_TILE, D_FF + (j + 1) * FF_TILE)
        return _dot(hbuf[...], wup_ref[:, ga]), _dot(hbuf[...], wup_ref[:, va])

    def conv(u, cw):
        head = pltpu.roll(u[halo + r - 8:halo + r], 1, 0)
        tail = pltpu.roll(u[halo:halo + 8], 7, 0)
        if edges:
            head = jnp.where(sub == 0, u[0:8], head)
            tail = jnp.where(sub == 7, pltpu.roll(u[0:8], 6, 0), tail)
        prev = jnp.concatenate([jnp.where(seq_first, 0.0, head), u[halo:halo + r - 8]], axis=0)
        nxt = jnp.concatenate([u[halo + 8:halo + r], jnp.where(seq_last, 0.0, tail)], axis=0)
        return prev * cw[0:1] + u[halo:halo + r] * cw[1:2] + nxt * cw[2:3]

    acc = None
    nxt_u = up(0)
    for j in range(N_FF_TILES):
        ua, ub = nxt_u
        if j + 1 < N_FF_TILES:
            nxt_u = up(j + 1)
        ga = slice(j * FF_TILE, (j + 1) * FF_TILE)
        va = slice(D_FF + j * FF_TILE, D_FF + (j + 1) * FF_TILE)
        ua = conv(ua, cwf_ref[:, ga])
        ub = conv(ub, cwf_ref[:, va])
        gbuf[:, ga] = (ua * jax.nn.sigmoid(ua) * ub).astype(BF16)
        if j % DOWN_CHUNK == DOWN_CHUNK - 1 or j == N_FF_TILES - 1:
            lo = (j // DOWN_CHUNK) * DOWN_CHUNK * FF_TILE
            hi = (j + 1) * FF_TILE
            d = _dot(gbuf[:, lo:hi], wdn_ref[lo:hi, :])
            acc = d if acc is None else acc + d

    for c in range(nchunk):
        abuf[c] = acc[:, c * LANES:(c + 1) * LANES]
    g2 = mod_ref[5:6, :]
    for c in range(nchunk):
        cs = slice(c * LANES, (c + 1) * LANES)
        for s in range(8):
            rs = slice(s * grp, (s + 1) * grp)
            o_ref[rs, cs] = x1c[rs, cs] + g2[:, cs] * abuf[c, pl.ds(s, grp, stride=8), :]
    if final:
        o_ref[...] = _rms(o_ref[...], fn_ref[...], D_MODEL)


def _halo_specs(r, w, rows):
    nblk = rows // BF16_ROWS
    per = r // BF16_ROWS
    prev = pl.BlockSpec((BF16_ROWS, w), lambda i: (jnp.maximum(i * per - 1, 0), 0))
    cur = pl.BlockSpec((r, w), lambda i: (i, 0))
    nxt = pl.BlockSpec((BF16_ROWS, w), lambda i: (jnp.minimum((i + 1) * per, nblk - 1), 0))
    return prev, cur, nxt


def _cond_spec(l, seq_len, cond_base):
    r = ROW_TILE
    tiles_per_seq = max(seq_len // r, 1)
    step = 1 if seq_len >= r else 0
    return pl.BlockSpec((None, None, 6, D_MODEL),
                        lambda i: (l, cond_base + step * (i // tiles_per_seq), 0, 0))


def _post(x, mod, lw, l, seq_len, cond_base, zg, y, final_norm, final):
    rows = x.shape[0]
    r = ROW_TILE
    full = lambda a: pl.BlockSpec((None,) + a.shape[1:], lambda i: (l,) + (0,) * (a.ndim - 1),
                                  pipeline_mode=pl.Buffered(1))
    acts = [x, zg, y]
    edges = seq_len > r
    if edges:
        act_specs = [s for a in acts for s in _halo_specs(r, a.shape[1], rows)]
        act_args = [a for a in acts for _ in range(3)]
    else:
        act_specs = [pl.BlockSpec((r, a.shape[1]), lambda i: (i, 0)) for a in acts]
        act_args = acts
    return pl.pallas_call(
        functools.partial(_post_body, seq_len, final),
        grid=(rows // r,),
        in_specs=[_cond_spec(l, seq_len, cond_base), full(lw['norm2']), full(lw['conv_a']), full(lw['w_out']),
                  full(lw['conv_ff']), full(lw['w_up']), full(lw['w_down']),
                  pl.BlockSpec((1, D_MODEL), lambda i: (0, 0))] + act_specs,
        out_specs=pl.BlockSpec((r, D_MODEL), lambda i: (i, 0)),
        out_shape=jax.ShapeDtypeStruct((rows, D_MODEL), F32),
        scratch_shapes=[pltpu.VMEM((r + (BF16_ROWS if edges else 0), D_MODEL), BF16), pltpu.VMEM((r, D_FF), BF16),
                        pltpu.VMEM((D_MODEL // LANES, r, LANES), F32),
                        pltpu.VMEM((D_MODEL // LANES, r, LANES), F32)],
        compiler_params=pltpu.CompilerParams(
            dimension_semantics=("arbitrary",), vmem_limit_bytes=VMEM_LIMIT),
        name="post",
    )(mod, lw['norm2'], lw['conv_a'], lw['w_out'], lw['conv_ff'], lw['w_up'], lw['w_down'], final_norm,
      *act_args)


def _pad_cols(a, n):
    return jnp.pad(a, [(0, 0)] * (a.ndim - 1) + [(0, n - a.shape[-1])])


def _perm_heads(a, axis):
    h = [lax.slice_in_dim(a, k * HEAD_DIM, (k + 1) * HEAD_DIM, axis=axis) for k in range(4)]
    return jnp.concatenate([h[0], h[2], h[1], h[3]], axis=axis)


def _pack_w_in_body(w_ref, o_ref):
    w = w_ref[...]
    o_ref[...] = jnp.concatenate(
        [w[:, 0:768], w[:, 768:960], w[:, 1088:1120], jnp.zeros((w.shape[0], 32), F32), w[:, 960:1088],
         _perm_heads(w[:, 1120:1376], 1), w[:, 1376:1632], _perm_heads(w[:, 1632:1888], 1), w[:, 1888:2144]],
        axis=1).astype(BF16)


def _pack_w_in(w_in):
    rt = 512
    L, k, n = w_in.shape
    return pl.pallas_call(
        _pack_w_in_body,
        grid=(L, k // rt),
        in_specs=[pl.BlockSpec((None, rt, n), lambda l, i: (l, i, 0))],
        out_specs=pl.BlockSpec((None, rt, IN_COLS_PACKED), lambda l, i: (l, i, 0)),
        out_shape=jax.ShapeDtypeStruct((L, k, IN_COLS_PACKED), BF16),
        compiler_params=pltpu.CompilerParams(
            dimension_semantics=("arbitrary", "arbitrary"), vmem_limit_bytes=VMEM_LIMIT),
        name="pack_w_in",
    )(w_in)


def _prep_weights(w_in, mla_wq_b, mla_wkv_b, w_out, w_up, conv_ff, w_down, norm1, norm2, conv_a,
                  mla_q_norm, mla_kv_norm, gqa_q_norm, gqa_k_norm):
    L = DEPTH
    w_in_p = _pack_w_in(w_in)

    wq = mla_wq_b.reshape(L, MLA_Q_LORA, MLA_HEADS, MLA_NOPE + MLA_ROPE)
    wq = jnp.concatenate([wq[..., MLA_NOPE:], wq[..., :MLA_NOPE],
                          jnp.zeros((L, MLA_Q_LORA, MLA_HEADS, LANES - MLA_NOPE - MLA_ROPE), F32)], axis=-1)
    wq = jnp.pad(wq.reshape(L, MLA_Q_LORA, MLA_HEADS * LANES), ((0, 0), (0, 256 - MLA_Q_LORA), (0, 0)))

    wkv = mla_wkv_b.reshape(L, MLA_KV_LORA, MLA_HEADS, MLA_NOPE + MLA_V)
    zk = jnp.zeros((L, MLA_KV_LORA, MLA_HEADS, MLA_ROPE), F32)
    wk = jnp.concatenate([zk, wkv[..., :MLA_NOPE], zk], axis=-1).reshape(L, MLA_KV_LORA, MLA_HEADS * LANES)
    wv = wkv[..., MLA_NOPE:].reshape(L, MLA_KV_LORA, MLA_HEADS * MLA_V)
    wkv_p = jnp.concatenate([wk, wv], axis=-1)

    wo = jnp.concatenate([w_out[:, 0:512], _perm_heads(w_out[:, 512:768], 1),
                          _perm_heads(w_out[:, 768:1024], 1)], axis=1)

    hm = jnp.asarray(np.kron(np.eye(2 * LANES // HEAD_DIM), np.ones((HEAD_DIM, HEAD_DIM))), BF16)
    return {
        'w_in': w_in_p, 'wq': wq.astype(BF16), 'wkv': wkv_p.astype(BF16), 'w_out': wo.astype(BF16),
        'w_up': w_up.astype(BF16), 'conv_ff': conv_ff, 'w_down': w_down.astype(BF16),
        'norm1': norm1.reshape(L, 1, D_MODEL), 'norm2': norm2.reshape(L, 1, D_MODEL), 'conv_a': conv_a,
        'gq': _pad_cols(mla_q_norm, 256).reshape(L, 1, 256), 'gkv': mla_kv_norm.reshape(L, 1, MLA_KV_LORA),
        'gqc': jnp.tile(gqa_q_norm, (1, 4)).reshape(L, 1, 256), 'gkc': jnp.tile(gqa_k_norm, (1, 2)).reshape(L, 1, 128),
        'hm': hm,
    }


def _rope_tables(t):
    rows = t // GRID_W
    row = np.repeat(np.arange(rows, dtype=np.float64), GRID_W)
    col = np.tile(np.arange(GRID_W, dtype=np.float64), rows)

    def tabs(dim):
        half = dim // 2
        inv = np.power(ROPE_THETA, -np.arange(0, half, 2, dtype=np.float64) / half)
        ar = row[:, None] * inv
        ac = col[:, None] * inv
        c = np.concatenate([np.cos(ar), np.cos(ar), np.cos(ac), np.cos(ac)], axis=1)
        s = np.concatenate([-np.sin(ar), np.sin(ar), -np.sin(ac), np.sin(ac)], axis=1)
        return c, s

    c64, s64 = tabs(HEAD_DIM)
    c32, s32 = tabs(MLA_ROPE)
    c64 = np.tile(c64, (1, 2))
    s64 = np.tile(s64, (1, 2))
    c32 = np.concatenate([c32, np.ones((t, LANES - MLA_ROPE))], axis=1)
    s32 = np.concatenate([s32, np.zeros((t, LANES - MLA_ROPE))], axis=1)
    return tuple(jnp.asarray(a.astype(np.float32)) for a in (c64, s64, c32, s32))


def kernel(x_prompt, x_sample, cache_mla_ckv, cache_mla_kpe, cache_gqa_k, cache_gqa_v, cache_swa_k, cache_swa_v,
           c, c_ctx, w_ada, b_ada, norm1, w_in, conv_a, mla_q_norm, mla_wq_b, mla_kv_norm, mla_wkv_b,
           gqa_q_norm, gqa_k_norm, swa_sink, w_out, norm2, w_up, conv_ff, w_down, final_norm):
    B, T, _ = x_prompt.shape
    DB, DT, _ = x_sample.shape
    past = cache_mla_ckv.shape[2]

    lw = _prep_weights(w_in, mla_wq_b, mla_wkv_b, w_out, w_up, conv_ff, w_down, norm1, norm2, conv_a,
                       mla_q_norm, mla_kv_norm, gqa_q_norm, gqa_k_norm)
    rope_tabs = _rope_tables(DT)
    fnorm = final_norm.reshape(1, D_MODEL)

    cond_t = jnp.concatenate([c_ctx[:, None], c.T, jnp.zeros((D_MODEL, 8 - 1 - DB), F32)], axis=1)
    mod = _ada(cond_t, 1 + DB, w_ada, b_ada).reshape(DEPTH, 8, 6, D_MODEL)

    kx, vx = _ctx_mla(cache_mla_ckv, _pad_cols(cache_mla_kpe, LANES), lw['wkv'])
    flat = lambda a: a.reshape(DB, DEPTH, past, 2 * HEAD_DIM).astype(BF16)
    kcx, kdx = flat(cache_gqa_k), flat(cache_swa_k)
    vcx, vdx = jnp.swapaxes(flat(cache_gqa_v), 2, 3), jnp.swapaxes(flat(cache_swa_v), 2, 3)

    xp = x_prompt.reshape(B * T, D_MODEL)
    xs = x_sample.reshape(DB * DT, D_MODEL)
    caches = None
    for l in range(DEPTH):
        final = l == DEPTH - 1
        (zg, qm, km, vm, qc, kc, vc, qd, kd, vd, *caches) = _inproj(xp, mod, lw, l, T, 0, None, caches)
        y = _attn_ctx(l, T, swa_sink, qm, km, vm, qc, kc, vc, qd, kd, vd)
        xp = _post(xp, mod, lw, l, T, 0, zg, y, fnorm, final)
        (zg, qmr, qm, km, vm, qcr, qc, kc, vc, qdr, qd, kd, vd) = _inproj(xs, mod, lw, l, DT, 1, rope_tabs)
        y = _attn_lat(l, DB, DT, swa_sink, qmr, qm, km, vm, kx, vx, qcr, qc, kc, vc, kcx, vcx,
                      qdr, qd, kd, vd, kdx, vdx)
        xs = _post(xs, mod, lw, l, DT, 1, zg, y, fnorm, final)

    heads = lambda a: a.reshape(B, DEPTH, T, 2, HEAD_DIM)
    return (xp.reshape(B, T, D_MODEL), xs.reshape(DB, DT, D_MODEL), caches[0], caches[1],
            heads(caches[2]), heads(caches[3]), heads(caches[4]), heads(caches[5]))
```

```python
import functools
import math

import jax
import jax.numpy as jnp
import numpy as np
from jax import lax
from jax.experimental import pallas as pl
from jax.experimental.pallas import tpu as pltpu

F32 = jnp.float32
BF16 = jnp.bfloat16

D_MODEL = 1024
DEPTH = 2
GRID_W = 64
HEAD_DIM = 64
GROUP_WIDTH = D_MODEL // 4
MLA_HEADS = 4
MLA_NOPE = 64
MLA_ROPE = 32
MLA_V = 64
MLA_Q_LORA = 192
MLA_KV_LORA = 128
WINDOW = 128
D_FF = 2816
ROPE_THETA = 10000.0
EPS = 1e-6
NEG_INF = -1e30
LOG2E = math.log2(math.e)
ATTN_SCALE = HEAD_DIM ** -0.5 * LOG2E
MLA_SCALE = (MLA_NOPE + MLA_ROPE) ** -0.5 * LOG2E

LANES = 128
BF16_ROWS = 16
MXU_WIDTH = 256
VMEM_LIMIT = 56 * 1024 * 1024

ROW_TILE = 512
IN_ROW_TILE = 1024
IN_SUB_TILE = 256
IN_SUB_TILE_ROPE = 128
FF_TILE = MXU_WIDTH
N_FF_TILES = D_FF // FF_TILE
IN_COLS_PACKED = 2176

_NT = (((1,), (1,)), ((), ()))


def _dot(a, b):
    return jnp.dot(a, b, preferred_element_type=F32)


def _dot_nt(a, b):
    return lax.dot_general(a, b, _NT, preferred_element_type=F32)


def _rms(x, g, n):
    ms = jnp.sum(x * x, axis=-1, keepdims=True) * (1.0 / n)
    return x * lax.rsqrt(ms + EPS) * g


def _lane(shape):
    return lax.broadcasted_iota(jnp.int32, shape, len(shape) - 1)


def _rope(x, c, s, half):
    w = x.shape[-1]
    lo = (_lane(x.shape) % (2 * half)) < half
    sw = jnp.where(lo, pltpu.roll(x, w - half, 1), pltpu.roll(x, half, 1))
    return x * c + sw * s


def _ada_body(ncond, c_ref, w_ref, b_ref, o_ref):
    c = c_ref[...]
    s = c * jax.nn.sigmoid(c)
    tn = w_ref.shape[1]
    w = w_ref[...].reshape(D_MODEL // 8, 8, tn)
    rows = []
    for r in range(ncond):
        part = jnp.sum(w * s[:, r:r + 1].reshape(D_MODEL // 8, 8, 1), axis=0)
        rows.append(jnp.sum(part, axis=0, keepdims=True))
    rows.append(jnp.zeros((8 - ncond, tn), F32))
    o_ref[...] = jnp.concatenate(rows, axis=0) + b_ref[...]


def _ada(cond_t, ncond, w_ada, b_ada):
    tn = 1536
    n = 6 * D_MODEL
    return pl.pallas_call(
        functools.partial(_ada_body, ncond),
        grid=(DEPTH, n // tn),
        in_specs=[pl.BlockSpec((D_MODEL, 8), lambda l, j: (0, 0)),
                  pl.BlockSpec((None, D_MODEL, tn), lambda l, j: (l, 0, j)),
                  pl.BlockSpec((None, 1, tn), lambda l, j: (l, 0, j))],
        out_specs=pl.BlockSpec((None, 8, tn), lambda l, j: (l, 0, j)),
        out_shape=jax.ShapeDtypeStruct((DEPTH, 8, n), F32),
        compiler_params=pltpu.CompilerParams(
            dimension_semantics=("arbitrary", "arbitrary"), vmem_limit_bytes=VMEM_LIMIT),
        name="ada",
    )(cond_t, w_ada, b_ada.reshape(DEPTH, 1, n))


def _ctx_body(ckv_ref, kpe_ref, w_ref, kx_ref, vx_ref):
    kv = _dot(ckv_ref[...].astype(BF16), w_ref[...])
    kpe = kpe_ref[...]
    kx_ref[...] = (kv[:, :4 * LANES] + jnp.concatenate([kpe] * MLA_HEADS, axis=1)).astype(BF16)
    vx_ref[...] = kv[:, 4 * LANES:].T.astype(BF16)


def _ctx_mla(cache_ckv, cache_kpe_pad, wkv_p):
    b, _, s, _ = cache_ckv.shape
    return pl.pallas_call(
        _ctx_body,
        grid=(DEPTH, b),
        in_specs=[pl.BlockSpec((None, None, s, MLA_KV_LORA), lambda l, i: (i, l, 0, 0)),
                  pl.BlockSpec((None, None, s, LANES), lambda l, i: (i, l, 0, 0)),
                  pl.BlockSpec((None, MLA_KV_LORA, 6 * LANES), lambda l, i: (l, 0, 0))],
        out_specs=[pl.BlockSpec((None, None, s, 4 * LANES), lambda l, i: (l, i, 0, 0)),
                   pl.BlockSpec((None, None, 2 * LANES, s), lambda l, i: (l, i, 0, 0))],
        out_shape=[jax.ShapeDtypeStruct((DEPTH, b, s, 4 * LANES), BF16),
                   jax.ShapeDtypeStruct((DEPTH, b, 2 * LANES, s), BF16)],
        compiler_params=pltpu.CompilerParams(
            dimension_semantics=("arbitrary", "arbitrary"), vmem_limit_bytes=VMEM_LIMIT),
        name="ctx_mla",
    )(cache_ckv, cache_kpe_pad, wkv_p)


def _inproj_body(rope, n_alias, x_ref, mod_ref, n1_ref, w_ref, wq_ref, wkv_ref, gq_ref, gkv_ref, gqc_ref, gkc_ref,
                 hm_ref, *rest):
    if rope:
        c64_ref, s64_ref, c32_ref, s32_ref = rest[:4]
        outs = rest[4:]
    else:
        outs = rest[n_alias:]
    sh1 = mod_ref[0:1, :]
    sc1 = mod_ref[1:2, :]
    hm = hm_ref[...]
    sub = IN_SUB_TILE_ROPE if rope else IN_SUB_TILE
    nsub = x_ref.shape[0] // sub

    def project(s):
        x = x_ref[s * sub:(s + 1) * sub, :]
        h = _rms(x, n1_ref[...], D_MODEL) * (1.0 + sc1) + sh1
        return _dot(h.astype(BF16), w_ref[...])

    def head_rms(v, g):
        w = v.shape[-1]
        ss = _dot((v * v).astype(BF16), hm[:w, :w])
        return v * lax.rsqrt(ss * (1.0 / HEAD_DIM) + EPS) * g

    def finish(s, acc):
        rs = slice(s * sub, (s + 1) * sub)
        xa, gb, gc = acc[:, 0:256], acc[:, 256:512], acc[:, 512:768]
        z = gc * xa
        blk = acc[:, 768:1024]
        cq = jnp.where(_lane(blk.shape) < MLA_Q_LORA, blk, 0.0)
        cqn = _rms(cq, gq_ref[...], MLA_Q_LORA)
        qm = _dot(cqn.astype(BF16), wq_ref[...]) * MLA_SCALE
        ckvn = _rms(acc[:, 1024:1152], gkv_ref[...], MLA_KV_LORA)
        kv = _dot(ckvn.astype(BF16), wkv_ref[...])
        kpe = pltpu.roll(blk[:, LANES:], 2 * LANES - MLA_Q_LORA, 1)
        kpe = jnp.where(_lane(kpe.shape) < MLA_ROPE, kpe, 0.0)
        vm = kv[:, 4 * LANES:]
        qc = head_rms(acc[:, 1152:1408], gqc_ref[...]) * ATTN_SCALE
        kc = head_rms(acc[:, 1408:1536], gkc_ref[...])
        vc = acc[:, 1536:1664]
        qd = acc[:, 1664:1920] * ATTN_SCALE
        kd = acc[:, 1920:2048]
        vd = acc[:, 2048:2176]

        if not rope:
            (zg_o, qm_o, km_o, vm_o, qc_o, kc_o, vc_o, qd_o, kd_o, vd_o,
             ckv_c, kpe_c, kc_c, vc_c, kd_c, vd_c) = outs
            km = kv[:, :4 * LANES] + jnp.concatenate([kpe] * MLA_HEADS, axis=1)
            zg_o[rs, 0:256] = z.astype(BF16)
            zg_o[rs, 256:512] = gb.astype(BF16)
            qm_o[rs, :] = qm.astype(BF16)
            km_o[rs, :] = km.astype(BF16)
            vm_o[:, rs] = vm.T.astype(BF16)
            qc_o[rs, :] = qc.astype(BF16)
            kc_o[rs, :] = kc.astype(BF16)
            vc_o[:, rs] = vc.T.astype(BF16)
            qd_o[rs, :] = qd.astype(BF16)
            kd_o[rs, :] = kd.astype(BF16)
            vd_o[:, rs] = vd.T.astype(BF16)
            for o, v in ((ckv_c, ckvn), (kpe_c, kpe[:, :MLA_ROPE]), (kc_c, kc), (vc_c, vc), (kd_c, kd),
                         (vd_c, vd)):
                t = o.shape[-2]
                bs = slice(s * sub // t, (s + 1) * sub // t)
                v = v.reshape(sub // t, t, v.shape[-1])
                if n_alias:
                    o[bs] = v
                else:
                    o[bs, 0] = v
                    o[bs, 1:] = jnp.zeros((sub // t, o.shape[1] - 1) + v.shape[1:], F32)
        else:
            (zg_o, qmr_o, qm_o, km_o, vm_o, qcr_o, qc_o, kc_o, vc_o, qdr_o, qd_o, kd_o, vd_o) = outs
            c64, s64, c32, s32 = c64_ref[rs, :], s64_ref[rs, :], c32_ref[rs, :], s32_ref[rs, :]
            rope64 = lambda v: jnp.concatenate(
                [_rope(v[:, i:i + LANES], c64, s64, 16) for i in range(0, v.shape[-1], LANES)], axis=1)
            rope32 = lambda v: jnp.concatenate(
                [_rope(v[:, i:i + LANES], c32, s32, 8) for i in range(0, v.shape[-1], LANES)], axis=1)
            km = kv[:, :4 * LANES] + jnp.concatenate([_rope(kpe, c32, s32, 8)] * MLA_HEADS, axis=1)
            zg_o[rs, 0:256] = z.astype(BF16)
            zg_o[rs, 256:512] = gb.astype(BF16)
            qmr_o[rs, :] = rope32(qm).astype(BF16)
            qm_o[rs, :] = qm.astype(BF16)
            km_o[rs, :] = km.astype(BF16)
            vm_o[:, rs] = vm.T.astype(BF16)
            qcr_o[rs, :] = rope64(qc).astype(BF16)
            qc_o[rs, :] = qc.astype(BF16)
            kc_o[rs, :] = rope64(kc).astype(BF16)
            vc_o[:, rs] = vc.T.astype(BF16)
            qdr_o[rs, :] = rope64(qd).astype(BF16)
            qd_o[rs, :] = qd.astype(BF16)
            kd_o[rs, :] = rope64(kd).astype(BF16)
            vd_o[:, rs] = vd.T.astype(BF16)

    nxt = project(0)
    for s in range(nsub):
        acc = nxt
        if s + 1 < nsub:
            nxt = project(s + 1)
        finish(s, acc)


def _inproj(x, mod, lw, l, seq_len, cond_base, rope_tabs, prev_caches=None):
    rows = x.shape[0]
    r = IN_ROW_TILE
    rope = rope_tabs is not None
    aliases = {}
    tiles_per_seq = max(seq_len // r, 1)
    step = 1 if seq_len >= r else 0

    def cond_map(i):
        return (l, cond_base + step * (i // tiles_per_seq), 0, 0)

    row = lambda w: pl.BlockSpec((r, w), lambda i: (i, 0))
    full = lambda a: pl.BlockSpec((None,) + a.shape[1:], lambda i: (l,) + (0,) * (a.ndim - 1))
    in_specs = [row(D_MODEL),
                pl.BlockSpec((None, None, 6, D_MODEL), cond_map),
                full(lw['norm1']), full(lw['w_in']), full(lw['wq']), full(lw['wkv']),
                full(lw['gq']), full(lw['gkv']), full(lw['gqc']), full(lw['gkc']),
                pl.BlockSpec((2 * LANES, 2 * LANES), lambda i: (0, 0))]
    args = [x, mod, lw['norm1'], lw['w_in'], lw['wq'], lw['wkv'], lw['gq'], lw['gkv'], lw['gqc'], lw['gkc'],
            lw['hm']]
    def act(w, transposed):
        if transposed:
            return jax.ShapeDtypeStruct((w, rows), BF16), pl.BlockSpec((w, r), lambda i: (0, i))
        return jax.ShapeDtypeStruct((rows, w), BF16), row(w)

    if rope:
        tab = pl.BlockSpec((r, LANES), lambda i: (i % tiles_per_seq, 0))
        in_specs += [tab] * 4
        args += list(rope_tabs)
        widths = [(512, 0), (512, 0), (512, 0), (512, 0), (256, 1), (256, 0), (256, 0), (128, 0),
                  (128, 1), (256, 0), (256, 0), (128, 0), (128, 1)]
        out_shape, out_specs = map(list, zip(*[act(w, t) for w, t in widths]))
    else:
        widths = [(512, 0), (512, 0), (512, 0), (256, 1), (256, 0), (128, 0), (128, 1), (256, 0),
                  (128, 0), (128, 1)]
        cwidths = [128, MLA_ROPE, 128, 128, 128, 128]
        nb = r // seq_len
        out_shape, out_specs = map(list, zip(*[act(w, t) for w, t in widths]))
        out_shape += [jax.ShapeDtypeStruct((rows // seq_len, DEPTH, seq_len, w), F32) for w in cwidths]
        assert (prev_caches is None) == (l == 0)
        if l == 0:
            out_specs += [pl.BlockSpec((nb, DEPTH, seq_len, w), lambda i: (i, 0, 0, 0)) for w in cwidths]
        else:
            out_specs += [pl.BlockSpec((nb, None, seq_len, w), lambda i: (i, l, 0, 0)) for w in cwidths]
        if prev_caches is not None:
            aliases = {len(args) + k: len(widths) + k for k in range(len(cwidths))}
            in_specs += [pl.BlockSpec(memory_space=pl.ANY)] * len(cwidths)
            args += list(prev_caches)
    return pl.pallas_call(
        functools.partial(_inproj_body, rope, len(aliases)),
        grid=(rows // r,),
        in_specs=in_specs, out_specs=out_specs, out_shape=out_shape,
        input_output_aliases=aliases,
        compiler_params=pltpu.CompilerParams(
            dimension_semantics=("arbitrary",), vmem_limit_bytes=VMEM_LIMIT),
        name="inproj_rope" if rope else "inproj",
    )(*args)


DEN_FLOOR = 2.0 ** -80


def _key_mags(k):
    return jnp.max(jnp.abs(k), axis=0, keepdims=True).astype(F32)


def _score_bound(kmag, qmag):
    col = jnp.sum(qmag.astype(F32) * kmag, axis=1, keepdims=True)
    return jnp.broadcast_to(col, qmag.shape).T[0:1]


def _softmax_pv_t(s_list, vt_list, first, extra=None, bound=None):
    if bound is None:
        m = s_list[0].max(axis=0, keepdims=True)
        for s in s_list[1:]:
            m = jnp.maximum(m, s.max(axis=0, keepdims=True))
    else:
        m = bound
    if extra is not None:
        m = jnp.maximum(m, extra)
    out = None
    for s, vt in zip(s_list, vt_list):
        ones = jnp.ones(vt.shape, BF16)
        vt = jnp.concatenate([vt, ones] if first else [ones, vt], axis=0)
        pv = _dot(vt, jnp.exp2(s - m).astype(BF16))
        out = pv if out is None else out + pv
    den = out[HEAD_DIM:HEAD_DIM + 1] if first else out[0:1]
    if extra is not None:
        den = den + jnp.exp2(extra - m)
    return out * (1.0 / den), den


def _underflowed(dens):
    width = max(d.shape[1] for d in dens)
    low = None
    for d in dens:
        d = jnp.concatenate([d] * (width // d.shape[1]), axis=1)
        low = d if low is None else jnp.minimum(low, d)
    return jnp.logical_not(jnp.min(low) >= DEN_FLOOR)


def _half_masks(q):
    lo = _lane(q.shape) < HEAD_DIM
    zero = jnp.zeros_like(q)
    return jnp.where(lo, q, zero), jnp.where(lo, zero, q)


def _merge_rows(a, b):
    r = lax.broadcasted_iota(jnp.int32, a.shape, 0)
    return jnp.where(r < HEAD_DIM, a, b)


def _sink_row(sink_ref, l, h0, h1, tq):
    c = lax.broadcasted_iota(jnp.int32, (1, 2 * tq), 1)
    return jnp.where(c < tq, sink_ref[l, h0], sink_ref[l, h1]) * LOG2E


SCORE_AHEAD = 4


def _run_groups(stage1, stage2, n, ahead=SCORE_AHEAD):
    out = []
    pending = [stage1(g) for g in range(min(ahead, n))]
    for g in range(n):
        if g + ahead < n:
            pending.append(stage1(g + ahead))
        out.append(stage2(g, pending.pop(0)))
    return out


def _scores(pairs, kmags):
    s = [_dot_nt(k, q) for k, q in pairs]
    b = None
    if kmags is not None:
        kmag, qmag = None, None
        for (_, q), km in zip(pairs, kmags):
            kmag = km if kmag is None else jnp.maximum(kmag, km)
            qmag = jnp.abs(q) if qmag is None else jnp.maximum(qmag, jnp.abs(q))
        b = _score_bound(kmag, qmag)
    return s, b


def _mla_scores(h, q_sets, k_sets, kmags):
    c = slice(h * LANES, (h + 1) * LANES)
    return _scores([(k[:, c], q[:, c]) for q, k in zip(q_sets, k_sets)], kmags)


def _gqa_scores(j, q_sets, k_sets, kmags):
    pairs = []
    for q, k in zip(q_sets, k_sets):
        qa = _half_masks(q[:, 0:LANES])[j]
        qb = _half_masks(q[:, LANES:])[j]
        pairs.append((k[...], jnp.concatenate([qa, qb], axis=0)))
    return _scores(pairs, kmags)


def _store_pair(y_ref, rows, blk, ot):
    y_ref[rows, blk * LANES:(blk + 1) * LANES] = ot.T.astype(BF16)


N_GROUPS = MLA_HEADS + 4


def _store_groups(y_ref, rows, o, tq):
    for p in range(MLA_HEADS // 2):
        _store_pair(y_ref, rows, p, jnp.concatenate([o[2 * p][:HEAD_DIM], o[2 * p + 1][:HEAD_DIM]], axis=0))
    for base, (o0, o1) in ((2, o[4:6]), (4, o[6:8])):
        _store_pair(y_ref, rows, base, _merge_rows(o0[:, :tq], o1[:, :tq]))
        _store_pair(y_ref, rows, base + 1, _merge_rows(o0[:, tq:], o1[:, tq:]))


def _attn_ctx_body(l, nb, t, sink_ref, qm_ref, km_ref, vm_ref, qc_ref, kc_ref, vc_ref, qd_ref, kd_ref, vd_ref,
                   y_ref):
    def stage1(g):
        b, k = divmod(g, N_GROUPS)
        rs = pl.ds(b * t, t)
        if k < MLA_HEADS:
            return _mla_scores(k, [qm_ref.at[rs]], [km_ref.at[rs]], None)
        q_ref, k_ref = (qc_ref, kc_ref) if k < MLA_HEADS + 2 else (qd_ref, kd_ref)
        return _gqa_scores(k % 2, [q_ref.at[rs]], [k_ref.at[rs]], None)

    def stage2(g, scored):
        s, _ = scored
        b, k = divmod(g, N_GROUPS)
        rs = slice(b * t, (b + 1) * t)
        j = k % 2
        hs = slice(j * HEAD_DIM, (j + 1) * HEAD_DIM)
        if k < MLA_HEADS:
            return _softmax_pv_t(s, [vm_ref[k * HEAD_DIM:(k + 1) * HEAD_DIM, rs]], True)[0]
        if k < MLA_HEADS + 2:
            return _softmax_pv_t(s, [vc_ref[hs, rs]], j == 0)[0]
        return _softmax_pv_t(s, [vd_ref[hs, rs]], j == 0, _sink_row(sink_ref, l, 2 * j, 2 * j + 1, t))[0]

    o = _run_groups(stage1, stage2, N_GROUPS * nb)
    for b in range(nb):
        _store_groups(y_ref, slice(b * t, (b + 1) * t), o[N_GROUPS * b:N_GROUPS * (b + 1)], t)


def _attn_ctx(l, t, sink, qm, km, vm, qc, kc, vc, qd, kd, vd):
    rows = qm.shape[0]
    nb = 8
    r = nb * t
    row = lambda a: (pl.BlockSpec((a.shape[0], r), lambda i: (0, i)) if a.shape[1] == rows
                     else pl.BlockSpec((r, a.shape[1]), lambda i: (i, 0)))
    ins = [qm, km, vm, qc, kc, vc, qd, kd, vd]
    return pl.pallas_call(
        functools.partial(_attn_ctx_body, l, nb, t),
        grid=(rows // r,),
        in_specs=[pl.BlockSpec(memory_space=pltpu.SMEM)] + [row(a) for a in ins],
        out_specs=pl.BlockSpec((r, 6 * LANES), lambda i: (i, 0)),
        out_shape=jax.ShapeDtypeStruct((rows, 6 * LANES), BF16),
        compiler_params=pltpu.CompilerParams(
            dimension_semantics=("arbitrary",), vmem_limit_bytes=VMEM_LIMIT),
        name="attn_ctx",
    )(sink, *ins)


def _window_block(i, tq, t, nwb):
    return jnp.clip(i * (tq // LANES) - WINDOW // LANES, 0, t // LANES - nwb)


def _attn_lat_body(l, tq, t, nwb, sink_ref, qmr_ref, qm_ref, km_ref, vm_ref, kx_ref, vx_ref,
                   qcr_ref, qc_ref, kc_ref, vc_ref, kcx_ref, vcx_ref,
                   qdr_ref, qd_ref, kdx_ref, vdx_ref, *rest):
    kd_refs, vd_refs = rest[:nwb], rest[nwb:2 * nwb]
    y_ref, kmag_scr = rest[2 * nwb:]
    i = pl.program_id(1)
    wk = nwb * LANES
    kpos = _window_block(i, tq, t, nwb) * LANES + lax.broadcasted_iota(jnp.int32, (wk, 2 * tq), 0)
    qpos = i * tq + lax.broadcasted_iota(jnp.int32, (wk, 2 * tq), 1) % tq
    valid = jnp.abs(kpos - qpos) <= WINDOW


    @pl.when(i == 0)
    def _():
        for h in range(MLA_HEADS):
            c = slice(h * LANES, (h + 1) * LANES)
            kmag_scr[h:h + 1, :] = _key_mags(km_ref[:, c])
            kmag_scr[MLA_HEADS + h:MLA_HEADS + h + 1, :] = _key_mags(kx_ref[:, c])
        kmag_scr[8:9, :] = _key_mags(kc_ref[...])
        kmag_scr[9:10, :] = _key_mags(kcx_ref[...])
        kmag_scr[10:11, :] = _key_mags(kdx_ref[...])
        kmag_scr[11:, :] = jnp.zeros((kmag_scr.shape[0] - 11, LANES), F32)

    def run(bounded):
        dens = []
        row = lambda r: kmag_scr[r:r + 1, :]

        def stage1(g):
            if g < MLA_HEADS:
                return _mla_scores(g, [qmr_ref, qm_ref], [km_ref, kx_ref],
                                   [row(g), row(MLA_HEADS + g)] if bounded else None)
            if g < MLA_HEADS + 2:
                return _gqa_scores(g % 2, [qcr_ref, qc_ref], [kc_ref, kcx_ref],
                                   [row(8), row(9)] if bounded else None)
            kw = jnp.concatenate([r[...] for r in kd_refs], axis=0)
            (s1, s2), bound = _gqa_scores(g % 2, [qdr_ref, qd_ref], [kw, kdx_ref],
                                          [_key_mags(kw), row(10)] if bounded else None)
            return [jnp.where(valid, s1, NEG_INF), s2], bound

        def stage2(g, scored):
            s, bound = scored
            j = g % 2
            hs = slice(j * HEAD_DIM, (j + 1) * HEAD_DIM)
            if g < MLA_HEADS:
                hs = slice(g * HEAD_DIM, (g + 1) * HEAD_DIM)
                o, den = _softmax_pv_t(s, [vm_ref[hs, :], vx_ref[hs, :]], True, None, bound)
            elif g < MLA_HEADS + 2:
                o, den = _softmax_pv_t(s, [vc_ref[hs, :], vcx_ref[hs, :]], j == 0, None, bound)
            else:
                vw = jnp.concatenate([r[hs, :] for r in vd_refs], axis=1)
                o, den = _softmax_pv_t(s, [vw, vdx_ref[hs, :]], j == 0,
                                       _sink_row(sink_ref, l, 2 * j, 2 * j + 1, tq), bound)
            dens.append(den)
            return o

        _store_groups(y_ref, slice(None), _run_groups(stage1, stage2, N_GROUPS), tq)
        return dens

    @pl.when(_underflowed(run(True)))
    def _():
        run(False)


def _attn_lat(l, nbatch, t, sink, qmr, qm, km, vmt, kx, vxt, qcr, qc, kc, vct, kcx, vcxt, qdr, qd, kd, vdt,
              kdx, vdxt):
    tq = 256
    nq = t // tq
    nwb = (tq + 2 * WINDOW) // LANES
    nkb = t // LANES
    rows = qm.shape[0]
    qs = lambda a: pl.BlockSpec((tq, a.shape[1]), lambda b, i: (b * nq + i, 0))
    ks = lambda a: pl.BlockSpec((t, a.shape[1]), lambda b, i: (b, 0))
    kts = lambda a: pl.BlockSpec((a.shape[0], t), lambda b, i: (0, b))
    xs = lambda a: pl.BlockSpec((None, None) + a.shape[2:], lambda b, i: (l, b, 0, 0))
    cs = lambda a: pl.BlockSpec((None, None) + a.shape[2:], lambda b, i: (b, l, 0, 0))
    wblk = lambda b, i, k: b * nkb + _window_block(i, tq, t, nwb) + k
    kd_specs = [pl.BlockSpec((LANES, LANES), lambda b, i, k=k: (wblk(b, i, k), 0)) for k in range(nwb)]
    vd_specs = [pl.BlockSpec((LANES, LANES), lambda b, i, k=k: (0, wblk(b, i, k))) for k in range(nwb)]
    in_specs = [pl.BlockSpec(memory_space=pltpu.SMEM),
                qs(qmr), qs(qm), ks(km), kts(vmt), xs(kx), xs(vxt),
                qs(qcr), qs(qc), ks(kc), kts(vct), cs(kcx), cs(vcxt),
                qs(qdr), qs(qd), cs(kdx), cs(vdxt)] + kd_specs + vd_specs
    return pl.pallas_call(
        functools.partial(_attn_lat_body, l, tq, t, nwb),
        grid=(nbatch, nq),
        in_specs=in_specs,
        out_specs=pl.BlockSpec((tq, 6 * LANES), lambda b, i: (b * nq + i, 0)),
        out_shape=jax.ShapeDtypeStruct((rows, 6 * LANES), BF16),
        scratch_shapes=[pltpu.VMEM((16, LANES), F32)],
        compiler_params=pltpu.CompilerParams(
            dimension_semantics=("arbitrary", "arbitrary"), vmem_limit_bytes=VMEM_LIMIT),
        name="attn_lat",
    )(sink, qmr, qm, km, vmt, kx, vxt, qcr, qc, kc, vct, kcx, vcxt, qdr, qd, kdx, vdxt,
      *([kd] * nwb), *([vdt] * nwb))


def _post_body(seq_len, final, mod_ref, n2_ref, cwa_ref, wo_ref, cwf_ref, wup_ref, wdn_ref, fn_ref, *rest):
    o_ref, hbuf, gbuf, pbuf, abuf = rest[-5:]
    ins = rest[:-5]
    r = o_ref.shape[0]
    edges = seq_len > r
    assert edges or r % seq_len == 0
    if edges:
        cat = lambda k: jnp.concatenate([ins[3 * k][...], ins[3 * k + 1][...], ins[3 * k + 2][...]], axis=0)
    else:
        cat = lambda k: ins[k][...]
    halo = BF16_ROWS if edges else 0
    n = r + 2 * halo
    grp = r // 8
    nchunk = D_MODEL // LANES
    i = pl.program_id(0)

    zg = cat(1).astype(F32)
    zz, gb = zg[:, :GROUP_WIDTH], zg[:, GROUP_WIDTH:]
    pos = (i * r - halo + lax.broadcasted_iota(jnp.int32, (n, 1), 0)) % seq_len
    prev = jnp.where(pos == 0, 0.0, pltpu.roll(zz, 1, 0))
    nxt = jnp.where(pos == seq_len - 1, 0.0, pltpu.roll(zz, n - 1, 0))
    cwa = cwa_ref[...]
    ya = gb * (prev * cwa[0:1] + zz * cwa[1:2] + nxt * cwa[2:3])
    ycat = jnp.concatenate([ya.astype(BF16), cat(2)], axis=1)
    x1 = cat(0) + mod_ref[2:3, :] * _dot(ycat, wo_ref[...])
    h2 = _rms(x1, n2_ref[...], D_MODEL) * (1.0 + mod_ref[4:5, :]) + mod_ref[3:4, :]
    x1c = x1[halo:halo + r]

    for c in range(nchunk):
        for s in range(8):
            pbuf[c, pl.ds(s, grp, stride=8), :] = h2[halo + s * grp:halo + (s + 1) * grp, c * LANES:(c + 1) * LANES]
    if edges:
        hrow = lax.broadcasted_iota(jnp.int32, (halo, 1), 0)
        edge = jnp.where(hrow == 0, h2[halo - 1:halo], jnp.where(hrow == 1, h2[halo + r:halo + r + 1], 0.0))
        hbuf[0:halo, :] = edge.astype(BF16)
    hbuf[halo:, :] = jnp.concatenate([pbuf[c] for c in range(nchunk)], axis=1).astype(BF16)

    sub = lax.broadcasted_iota(jnp.int32, (8, 1), 0)
    seq_first = (i * r + sub * grp) % seq_len == 0
    seq_last = (i * r + sub * grp + grp - 1) % seq_len == seq_len - 1

    def up(j):
        ga = slice(j * FF_TILE, (j + 1) * FF_TILE)
        va = slice(D_FF + j * FF_TILE, D_FF + (j + 1) * FF_TILE)
        return _dot(hbuf[...], wup_ref[:, ga]), _dot(hbuf[...], wup_ref[:, va])

    def conv(u, cw):
        head = pltpu.roll(u[halo + r - 8:halo + r], 1, 0)
        tail = pltpu.roll(u[halo:halo + 8], 7, 0)
        if edges:
            head = jnp.where(sub == 0, u[0:8], head)
            tail = jnp.where(sub == 7, pltpu.roll(u[0:8], 6, 0), tail)
        prev = jnp.concatenate([jnp.where(seq_first, 0.0, head), u[halo:halo + r - 8]], axis=0)
        nxt = jnp.concatenate([u[halo + 8:halo + r], jnp.where(seq_last, 0.0, tail)], axis=0)
        return prev * cw[0:1] + u[halo:halo + r] * cw[1:2] + nxt * cw[2:3]

    nxt_u = up(0)
    for j in range(N_FF_TILES):
        ua, ub = nxt_u
        if j + 1 < N_FF_TILES:
            nxt_u = up(j + 1)
        ga = slice(j * FF_TILE, (j + 1) * FF_TILE)
        va = slice(D_FF + j * FF_TILE, D_FF + (j + 1) * FF_TILE)
        ua = conv(ua, cwf_ref[:, ga])
        ub = conv(ub, cwf_ref[:, va])
        gbuf[:, ga] = (ua * jax.nn.sigmoid(ua) * ub).astype(BF16)
    acc = _dot(gbuf[...], wdn_ref[...])

    for c in range(nchunk):
        abuf[c] = acc[:, c * LANES:(c + 1) * LANES]
    g2 = mod_ref[5:6, :]
    for c in range(nchunk):
        cs = slice(c * LANES, (c + 1) * LANES)
        for s in range(8):
            rs = slice(s * grp, (s + 1) * grp)
            o_ref[rs, cs] = x1c[rs, cs] + g2[:, cs] * abuf[c, pl.ds(s, grp, stride=8), :]
    if final:
        o_ref[...] = _rms(o_ref[...], fn_ref[...], D_MODEL)


def _halo_specs(r, w, rows):
    nblk = rows // BF16_ROWS
    per = r // BF16_ROWS
    prev = pl.BlockSpec((BF16_ROWS, w), lambda i: (jnp.maximum(i * per - 1, 0), 0))
    cur = pl.BlockSpec((r, w), lambda i: (i, 0))
    nxt = pl.BlockSpec((BF16_ROWS, w), lambda i: (jnp.minimum((i + 1) * per, nblk - 1), 0))
    return prev, cur, nxt


def _cond_spec(l, seq_len, cond_base):
    r = ROW_TILE
    tiles_per_seq = max(seq_len // r, 1)
    step = 1 if seq_len >= r else 0
    return pl.BlockSpec((None, None, 6, D_MODEL),
                        lambda i: (l, cond_base + step * (i // tiles_per_seq), 0, 0))


def _post(x, mod, lw, l, seq_len, cond_base, zg, y, final_norm, final):
    rows = x.shape[0]
    r = ROW_TILE
    full = lambda a: pl.BlockSpec((None,) + a.shape[1:], lambda i: (l,) + (0,) * (a.ndim - 1),
                                  pipeline_mode=pl.Buffered(1))
    acts = [x, zg, y]
    edges = seq_len > r
    if edges:
        act_specs = [s for a in acts for s in _halo_specs(r, a.shape[1], rows)]
        act_args = [a for a in acts for _ in range(3)]
    else:
        act_specs = [pl.BlockSpec((r, a.shape[1]), lambda i: (i, 0)) for a in acts]
        act_args = acts
    return pl.pallas_call(
        functools.partial(_post_body, seq_len, final),
        grid=(rows // r,),
        in_specs=[_cond_spec(l, seq_len, cond_base), full(lw['norm2']), full(lw['conv_a']), full(lw['w_out']),
                  full(lw['conv_ff']), full(lw['w_up']), full(lw['w_down']),
                  pl.BlockSpec((1, D_MODEL), lambda i: (0, 0))] + act_specs,
        out_specs=pl.BlockSpec((r, D_MODEL), lambda i: (i, 0)),
        out_shape=jax.ShapeDtypeStruct((rows, D_MODEL), F32),
        scratch_shapes=[pltpu.VMEM((r + (BF16_ROWS if edges else 0), D_MODEL), BF16), pltpu.VMEM((r, D_FF), BF16),
                        pltpu.VMEM((D_MODEL // LANES, r, LANES), F32),
                        pltpu.VMEM((D_MODEL // LANES, r, LANES), F32)],
        compiler_params=pltpu.CompilerParams(
            dimension_semantics=("arbitrary",), vmem_limit_bytes=VMEM_LIMIT),
        name="post",
    )(mod, lw['norm2'], lw['conv_a'], lw['w_out'], lw['conv_ff'], lw['w_up'], lw['w_down'], final_norm,
      *act_args)


def _pad_cols(a, n):
    return jnp.pad(a, [(0, 0)] * (a.ndim - 1) + [(0, n - a.shape[-1])])


def _perm_heads(a, axis):
    h = [lax.slice_in_dim(a, k * HEAD_DIM, (k + 1) * HEAD_DIM, axis=axis) for k in range(4)]
    return jnp.concatenate([h[0], h[2], h[1], h[3]], axis=axis)


def _prep_weights(w_in, mla_wq_b, mla_wkv_b, w_out, w_up, conv_ff, w_down, norm1, norm2, conv_a,
                  mla_q_norm, mla_kv_norm, gqa_q_norm, gqa_k_norm):
    L = DEPTH
    a = w_in[..., 0:768]
    cq_kpe = _pad_cols(jnp.concatenate([w_in[..., 768:960], w_in[..., 1088:1120]], axis=-1), 256)
    ckv = w_in[..., 960:1088]
    qc = _perm_heads(w_in[..., 1120:1376], 2)
    kvc = w_in[..., 1376:1632]
    qd = _perm_heads(w_in[..., 1632:1888], 2)
    kvd = w_in[..., 1888:2144]
    w_in_p = jnp.concatenate([a, cq_kpe, ckv, qc, kvc, qd, kvd], axis=-1).astype(BF16)
    assert w_in_p.shape[-1] == IN_COLS_PACKED

    wq = mla_wq_b.reshape(L, MLA_Q_LORA, MLA_HEADS, MLA_NOPE + MLA_ROPE)
    wq = jnp.concatenate([wq[..., MLA_NOPE:], wq[..., :MLA_NOPE],
                          jnp.zeros((L, MLA_Q_LORA, MLA_HEADS, LANES - MLA_NOPE - MLA_ROPE), F32)], axis=-1)
    wq = jnp.pad(wq.reshape(L, MLA_Q_LORA, MLA_HEADS * LANES), ((0, 0), (0, 256 - MLA_Q_LORA), (0, 0)))

    wkv = mla_wkv_b.reshape(L, MLA_KV_LORA, MLA_HEADS, MLA_NOPE + MLA_V)
    zk = jnp.zeros((L, MLA_KV_LORA, MLA_HEADS, MLA_ROPE), F32)
    wk = jnp.concatenate([zk, wkv[..., :MLA_NOPE], zk], axis=-1).reshape(L, MLA_KV_LORA, MLA_HEADS * LANES)
    wv = wkv[..., MLA_NOPE:].reshape(L, MLA_KV_LORA, MLA_HEADS * MLA_V)
    wkv_p = jnp.concatenate([wk, wv], axis=-1)

    wo = jnp.concatenate([w_out[:, 0:512], _perm_heads(w_out[:, 512:768], 1),
                          _perm_heads(w_out[:, 768:1024], 1)], axis=1)

    hm = jnp.asarray(np.kron(np.eye(2 * LANES // HEAD_DIM), np.ones((HEAD_DIM, HEAD_DIM))), BF16)
    return {
        'w_in': w_in_p, 'wq': wq.astype(BF16), 'wkv': wkv_p.astype(BF16), 'w_out': wo.astype(BF16),
        'w_up': w_up.astype(BF16), 'conv_ff': conv_ff, 'w_down': w_down.astype(BF16),
        'norm1': norm1.reshape(L, 1, D_MODEL), 'norm2': norm2.reshape(L, 1, D_MODEL), 'conv_a': conv_a,
        'gq': _pad_cols(mla_q_norm, 256).reshape(L, 1, 256), 'gkv': mla_kv_norm.reshape(L, 1, MLA_KV_LORA),
        'gqc': jnp.tile(gqa_q_norm, (1, 4)).reshape(L, 1, 256), 'gkc': jnp.tile(gqa_k_norm, (1, 2)).reshape(L, 1, 128),
        'hm': hm,
    }


def _rope_tables(t):
    rows = t // GRID_W
    row = np.repeat(np.arange(rows, dtype=np.float64), GRID_W)
    col = np.tile(np.arange(GRID_W, dtype=np.float64), rows)

    def tabs(dim):
        half = dim // 2
        inv = np.power(ROPE_THETA, -np.arange(0, half, 2, dtype=np.float64) / half)
        ar = row[:, None] * inv
        ac = col[:, None] * inv
        c = np.concatenate([np.cos(ar), np.cos(ar), np.cos(ac), np.cos(ac)], axis=1)
        s = np.concatenate([-np.sin(ar), np.sin(ar), -np.sin(ac), np.sin(ac)], axis=1)
        return c, s

    c64, s64 = tabs(HEAD_DIM)
    c32, s32 = tabs(MLA_ROPE)
    c64 = np.tile(c64, (1, 2))
    s64 = np.tile(s64, (1, 2))
    c32 = np.concatenate([c32, np.ones((t, LANES - MLA_ROPE))], axis=1)
    s32 = np.concatenate([s32, np.zeros((t, LANES - MLA_ROPE))], axis=1)
    return tuple(jnp.asarray(a.astype(np.float32)) for a in (c64, s64, c32, s32))


def kernel(x_prompt, x_sample, cache_mla_ckv, cache_mla_kpe, cache_gqa_k, cache_gqa_v, cache_swa_k, cache_swa_v,
           c, c_ctx, w_ada, b_ada, norm1, w_in, conv_a, mla_q_norm, mla_wq_b, mla_kv_norm, mla_wkv_b,
           gqa_q_norm, gqa_k_norm, swa_sink, w_out, norm2, w_up, conv_ff, w_down, final_norm):
    B, T, _ = x_prompt.shape
    DB, DT, _ = x_sample.shape
    past = cache_mla_ckv.shape[2]

    lw = _prep_weights(w_in, mla_wq_b, mla_wkv_b, w_out, w_up, conv_ff, w_down, norm1, norm2, conv_a,
                       mla_q_norm, mla_kv_norm, gqa_q_norm, gqa_k_norm)
    rope_tabs = _rope_tables(DT)
    fnorm = final_norm.reshape(1, D_MODEL)

    cond_t = jnp.concatenate([c_ctx[:, None], c.T, jnp.zeros((D_MODEL, 8 - 1 - DB), F32)], axis=1)
    mod = _ada(cond_t, 1 + DB, w_ada, b_ada).reshape(DEPTH, 8, 6, D_MODEL)

    kx, vx = _ctx_mla(cache_mla_ckv, _pad_cols(cache_mla_kpe, LANES), lw['wkv'])
    flat = lambda a: a.reshape(DB, DEPTH, past, 2 * HEAD_DIM).astype(BF16)
    kcx, kdx = flat(cache_gqa_k), flat(cache_swa_k)
    vcx, vdx = jnp.swapaxes(flat(cache_gqa_v), 2, 3), jnp.swapaxes(flat(cache_swa_v), 2, 3)

    xp = x_prompt.reshape(B * T, D_MODEL)
    xs = x_sample.reshape(DB * DT, D_MODEL)
    caches = None
    for l in range(DEPTH):
        final = l == DEPTH - 1
        (zg, qm, km, vm, qc, kc, vc, qd, kd, vd, *caches) = _inproj(xp, mod, lw, l, T, 0, None, caches)
        y = _attn_ctx(l, T, swa_sink, qm, km, vm, qc, kc, vc, qd, kd, vd)
        xp = _post(xp, mod, lw, l, T, 0, zg, y, fnorm, final)
        (zg, qmr, qm, km, vm, qcr, qc, kc, vc, qdr, qd, kd, vd) = _inproj(xs, mod, lw, l, DT, 1, rope_tabs)
        y = _attn_lat(l, DB, DT, swa_sink, qmr, qm, km, vm, kx, vx, qcr, qc, kc, vc, kcx, vcx,
                      qdr, qd, kd, vd, kdx, vdx)
        xs = _post(xs, mod, lw, l, DT, 1, zg, y, fnorm, final)

    heads = lambda a: a.reshape(B, DEPTH, T, 2, HEAD_DIM)
    return (xp.reshape(B, T, D_MODEL), xs.reshape(DB, DT, D_MODEL), caches[0], caches[1],
            heads(caches[2]), heads(caches[3]), heads(caches[4]), heads(caches[5]))
```

```python
import functools
import math

import jax
import jax.numpy as jnp
import numpy as np
from jax import lax
from jax.experimental import pallas as pl
from jax.experimental.pallas import tpu as pltpu

F32 = jnp.float32
BF16 = jnp.bfloat16

D_MODEL = 1024
DEPTH = 2
GRID_W = 64
HEAD_DIM = 64
GROUP_WIDTH = D_MODEL // 4
MLA_HEADS = 4
MLA_NOPE = 64
MLA_ROPE = 32
MLA_V = 64
MLA_Q_LORA = 192
MLA_KV_LORA = 128
WINDOW = 128
D_FF = 2816
ROPE_THETA = 10000.0
EPS = 1e-6
NEG_INF = -1e30
LOG2E = math.log2(math.e)
ATTN_SCALE = HEAD_DIM ** -0.5 * LOG2E
MLA_SCALE = (MLA_NOPE + MLA_ROPE) ** -0.5 * LOG2E

LANES = 128
BF16_ROWS = 16
MXU_WIDTH = 256
VMEM_LIMIT = 56 * 1024 * 1024

ROW_TILE = 512
IN_ROW_TILE = 1024
IN_SUB_TILE = 256
IN_SUB_TILE_ROPE = 128
FF_TILE = MXU_WIDTH
N_FF_TILES = D_FF // FF_TILE
IN_COLS_PACKED = 2176

_NT = (((1,), (1,)), ((), ()))


def _dot(a, b):
    return jnp.dot(a, b, preferred_element_type=F32)


def _dot_nt(a, b):
    return lax.dot_general(a, b, _NT, preferred_element_type=F32)


def _rms(x, g, n):
    ms = jnp.sum(x * x, axis=-1, keepdims=True) * (1.0 / n)
    return x * lax.rsqrt(ms + EPS) * g


def _lane(shape):
    return lax.broadcasted_iota(jnp.int32, shape, len(shape) - 1)


def _rope(x, c, s, half):
    w = x.shape[-1]
    lo = (_lane(x.shape) % (2 * half)) < half
    sw = jnp.where(lo, pltpu.roll(x, w - half, 1), pltpu.roll(x, half, 1))
    return x * c + sw * s


def _ada_body(ncond, c_ref, w_ref, b_ref, o_ref):
    c = c_ref[...]
    s = c * jax.nn.sigmoid(c)
    tn = w_ref.shape[1]
    w = w_ref[...].reshape(D_MODEL // 8, 8, tn)
    rows = []
    for r in range(ncond):
        part = jnp.sum(w * s[:, r:r + 1].reshape(D_MODEL // 8, 8, 1), axis=0)
        rows.append(jnp.sum(part, axis=0, keepdims=True))
    rows.append(jnp.zeros((8 - ncond, tn), F32))
    o_ref[...] = jnp.concatenate(rows, axis=0) + b_ref[...]


def _ada(cond_t, ncond, w_ada, b_ada):
    tn = 1536
    n = 6 * D_MODEL
    return pl.pallas_call(
        functools.partial(_ada_body, ncond),
        grid=(DEPTH, n // tn),
        in_specs=[pl.BlockSpec((D_MODEL, 8), lambda l, j: (0, 0)),
                  pl.BlockSpec((None, D_MODEL, tn), lambda l, j: (l, 0, j)),
                  pl.BlockSpec((None, 1, tn), lambda l, j: (l, 0, j))],
        out_specs=pl.BlockSpec((None, 8, tn), lambda l, j: (l, 0, j)),
        out_shape=jax.ShapeDtypeStruct((DEPTH, 8, n), F32),
        compiler_params=pltpu.CompilerParams(
            dimension_semantics=("arbitrary", "arbitrary"), vmem_limit_bytes=VMEM_LIMIT),
        name="ada",
    )(cond_t, w_ada, b_ada.reshape(DEPTH, 1, n))


def _ctx_body(ckv_ref, kpe_ref, w_ref, kx_ref, vx_ref):
    kv = _dot(ckv_ref[...].astype(BF16), w_ref[...])
    kpe = kpe_ref[...]
    kx_ref[...] = (kv[:, :4 * LANES] + jnp.concatenate([kpe] * MLA_HEADS, axis=1)).astype(BF16)
    vx_ref[...] = kv[:, 4 * LANES:].T.astype(BF16)


def _ctx_mla(cache_ckv, cache_kpe_pad, wkv_p):
    b, _, s, _ = cache_ckv.shape
    return pl.pallas_call(
        _ctx_body,
        grid=(DEPTH, b),
        in_specs=[pl.BlockSpec((None, None, s, MLA_KV_LORA), lambda l, i: (i, l, 0, 0)),
                  pl.BlockSpec((None, None, s, LANES), lambda l, i: (i, l, 0, 0)),
                  pl.BlockSpec((None, MLA_KV_LORA, 6 * LANES), lambda l, i: (l, 0, 0))],
        out_specs=[pl.BlockSpec((None, None, s, 4 * LANES), lambda l, i: (l, i, 0, 0)),
                   pl.BlockSpec((None, None, 2 * LANES, s), lambda l, i: (l, i, 0, 0))],
        out_shape=[jax.ShapeDtypeStruct((DEPTH, b, s, 4 * LANES), BF16),
                   jax.ShapeDtypeStruct((DEPTH, b, 2 * LANES, s), BF16)],
        compiler_params=pltpu.CompilerParams(
            dimension_semantics=("arbitrary", "arbitrary"), vmem_limit_bytes=VMEM_LIMIT),
        name="ctx_mla",
    )(cache_ckv, cache_kpe_pad, wkv_p)


def _inproj_body(rope, n_alias, x_ref, mod_ref, n1_ref, w_ref, wq_ref, wkv_ref, gq_ref, gkv_ref, gqc_ref, gkc_ref,
                 hm_ref, *rest):
    if rope:
        c64_ref, s64_ref, c32_ref, s32_ref = rest[:4]
        outs = rest[4:]
    else:
        outs = rest[n_alias:]
    sh1 = mod_ref[0:1, :]
    sc1 = mod_ref[1:2, :]
    hm = hm_ref[...]
    sub = IN_SUB_TILE_ROPE if rope else IN_SUB_TILE
    nsub = x_ref.shape[0] // sub

    def project(s):
        x = x_ref[s * sub:(s + 1) * sub, :]
        h = _rms(x, n1_ref[...], D_MODEL) * (1.0 + sc1) + sh1
        return _dot(h.astype(BF16), w_ref[...])

    def head_rms(v, g):
        w = v.shape[-1]
        ss = _dot((v * v).astype(BF16), hm[:w, :w])
        return v * lax.rsqrt(ss * (1.0 / HEAD_DIM) + EPS) * g

    def finish(s, acc):
        rs = slice(s * sub, (s + 1) * sub)
        xa, gb, gc = acc[:, 0:256], acc[:, 256:512], acc[:, 512:768]
        z = gc * xa
        blk = acc[:, 768:1024]
        cq = jnp.where(_lane(blk.shape) < MLA_Q_LORA, blk, 0.0)
        cqn = _rms(cq, gq_ref[...], MLA_Q_LORA)
        qm = _dot(cqn.astype(BF16), wq_ref[...]) * MLA_SCALE
        ckvn = _rms(acc[:, 1024:1152], gkv_ref[...], MLA_KV_LORA)
        kv = _dot(ckvn.astype(BF16), wkv_ref[...])
        kpe = pltpu.roll(blk[:, LANES:], 2 * LANES - MLA_Q_LORA, 1)
        kpe = jnp.where(_lane(kpe.shape) < MLA_ROPE, kpe, 0.0)
        vm = kv[:, 4 * LANES:]
        qc = head_rms(acc[:, 1152:1408], gqc_ref[...]) * ATTN_SCALE
        kc = head_rms(acc[:, 1408:1536], gkc_ref[...])
        vc = acc[:, 1536:1664]
        qd = acc[:, 1664:1920] * ATTN_SCALE
        kd = acc[:, 1920:2048]
        vd = acc[:, 2048:2176]

        if not rope:
            (zg_o, qm_o, km_o, vm_o, qc_o, kc_o, vc_o, qd_o, kd_o, vd_o,
             ckv_c, kpe_c, kc_c, vc_c, kd_c, vd_c) = outs
            km = kv[:, :4 * LANES] + jnp.concatenate([kpe] * MLA_HEADS, axis=1)
            zg_o[rs, 0:256] = z.astype(BF16)
            zg_o[rs, 256:512] = gb.astype(BF16)
            qm_o[rs, :] = qm.astype(BF16)
            km_o[rs, :] = km.astype(BF16)
            vm_o[:, rs] = vm.T.astype(BF16)
            qc_o[rs, :] = qc.astype(BF16)
            kc_o[rs, :] = kc.astype(BF16)
            vc_o[:, rs] = vc.T.astype(BF16)
            qd_o[rs, :] = qd.astype(BF16)
            kd_o[rs, :] = kd.astype(BF16)
            vd_o[:, rs] = vd.T.astype(BF16)
            for o, v in ((ckv_c, ckvn), (kpe_c, kpe[:, :MLA_ROPE]), (kc_c, kc), (vc_c, vc), (kd_c, kd),
                         (vd_c, vd)):
                t = o.shape[-2]
                bs = slice(s * sub // t, (s + 1) * sub // t)
                v = v.reshape(sub // t, t, v.shape[-1])
                if n_alias:
                    o[bs] = v
                else:
                    o[bs, 0] = v
                    o[bs, 1:] = jnp.zeros((sub // t, o.shape[1] - 1) + v.shape[1:], F32)
        else:
            (zg_o, qmr_o, qm_o, km_o, vm_o, qcr_o, qc_o, kc_o, vc_o, qdr_o, qd_o, kd_o, vd_o) = outs
            c64, s64, c32, s32 = c64_ref[rs, :], s64_ref[rs, :], c32_ref[rs, :], s32_ref[rs, :]
            rope64 = lambda v: jnp.concatenate(
                [_rope(v[:, i:i + LANES], c64, s64, 16) for i in range(0, v.shape[-1], LANES)], axis=1)
            rope32 = lambda v: jnp.concatenate(
                [_rope(v[:, i:i + LANES], c32, s32, 8) for i in range(0, v.shape[-1], LANES)], axis=1)
            km = kv[:, :4 * LANES] + jnp.concatenate([_rope(kpe, c32, s32, 8)] * MLA_HEADS, axis=1)
            zg_o[rs, 0:256] = z.astype(BF16)
            zg_o[rs, 256:512] = gb.astype(BF16)
            qmr_o[rs, :] = rope32(qm).astype(BF16)
            qm_o[rs, :] = qm.astype(BF16)
            km_o[rs, :] = km.astype(BF16)
            vm_o[:, rs] = vm.T.astype(BF16)
            qcr_o[rs, :] = rope64(qc).astype(BF16)
            qc_o[rs, :] = qc.astype(BF16)
            kc_o[rs, :] = rope64(kc).astype(BF16)
            vc_o[:, rs] = vc.T.astype(BF16)
            qdr_o[rs, :] = rope64(qd).astype(BF16)
            qd_o[rs, :] = qd.astype(BF16)
            kd_o[rs, :] = rope64(kd).astype(BF16)
            vd_o[:, rs] = vd.T.astype(BF16)

    nxt = project(0)
    for s in range(nsub):
        acc = nxt
        if s + 1 < nsub:
            nxt = project(s + 1)
        finish(s, acc)


def _inproj(x, mod, lw, l, seq_len, cond_base, rope_tabs, prev_caches=None):
    rows = x.shape[0]
    r = IN_ROW_TILE
    rope = rope_tabs is not None
    aliases = {}
    tiles_per_seq = max(seq_len // r, 1)
    step = 1 if seq_len >= r else 0

    def cond_map(i):
        return (l, cond_base + step * (i // tiles_per_seq), 0, 0)

    row = lambda w: pl.BlockSpec((r, w), lambda i: (i, 0))
    full = lambda a: pl.BlockSpec((None,) + a.shape[1:], lambda i: (l,) + (0,) * (a.ndim - 1))
    in_specs = [row(D_MODEL),
                pl.BlockSpec((None, None, 6, D_MODEL), cond_map),
                full(lw['norm1']), full(lw['w_in']), full(lw['wq']), full(lw['wkv']),
                full(lw['gq']), full(lw['gkv']), full(lw['gqc']), full(lw['gkc']),
                pl.BlockSpec((2 * LANES, 2 * LANES), lambda i: (0, 0))]
    args = [x, mod, lw['norm1'], lw['w_in'], lw['wq'], lw['wkv'], lw['gq'], lw['gkv'], lw['gqc'], lw['gkc'],
            lw['hm']]
    def act(w, transposed):
        if transposed:
            return jax.ShapeDtypeStruct((w, rows), BF16), pl.BlockSpec((w, r), lambda i: (0, i))
        return jax.ShapeDtypeStruct((rows, w), BF16), row(w)

    if rope:
        tab = pl.BlockSpec((r, LANES), lambda i: (i % tiles_per_seq, 0))
        in_specs += [tab] * 4
        args += list(rope_tabs)
        widths = [(512, 0), (512, 0), (512, 0), (512, 0), (256, 1), (256, 0), (256, 0), (128, 0),
                  (128, 1), (256, 0), (256, 0), (128, 0), (128, 1)]
        out_shape, out_specs = map(list, zip(*[act(w, t) for w, t in widths]))
    else:
        widths = [(512, 0), (512, 0), (512, 0), (256, 1), (256, 0), (128, 0), (128, 1), (256, 0),
                  (128, 0), (128, 1)]
        cwidths = [128, MLA_ROPE, 128, 128, 128, 128]
        nb = r // seq_len
        out_shape, out_specs = map(list, zip(*[act(w, t) for w, t in widths]))
        out_shape += [jax.ShapeDtypeStruct((rows // seq_len, DEPTH, seq_len, w), F32) for w in cwidths]
        assert (prev_caches is None) == (l == 0)
        if l == 0:
            out_specs += [pl.BlockSpec((nb, DEPTH, seq_len, w), lambda i: (i, 0, 0, 0)) for w in cwidths]
        else:
            out_specs += [pl.BlockSpec((nb, None, seq_len, w), lambda i: (i, l, 0, 0)) for w in cwidths]
        if prev_caches is not None:
            aliases = {len(args) + k: len(widths) + k for k in range(len(cwidths))}
            in_specs += [pl.BlockSpec(memory_space=pl.ANY)] * len(cwidths)
            args += list(prev_caches)
    return pl.pallas_call(
        functools.partial(_inproj_body, rope, len(aliases)),
        grid=(rows // r,),
        in_specs=in_specs, out_specs=out_specs, out_shape=out_shape,
        input_output_aliases=aliases,
        compiler_params=pltpu.CompilerParams(
            dimension_semantics=("arbitrary",), vmem_limit_bytes=VMEM_LIMIT),
        name="inproj_rope" if rope else "inproj",
    )(*args)


DEN_FLOOR = 2.0 ** -80


def _key_mags(k):
    return jnp.max(jnp.abs(k), axis=0, keepdims=True).astype(F32)


def _score_bound(kmag, qmag):
    col = jnp.sum(qmag.astype(F32) * kmag, axis=1, keepdims=True)
    return jnp.broadcast_to(col, qmag.shape).T[0:1]


def _softmax_pv_t(s_list, vt_list, first, extra=None, bound=None):
    if bound is None:
        m = s_list[0].max(axis=0, keepdims=True)
        for s in s_list[1:]:
            m = jnp.maximum(m, s.max(axis=0, keepdims=True))
    else:
        m = bound
    if extra is not None:
        m = jnp.maximum(m, extra)
    out = None
    for s, vt in zip(s_list, vt_list):
        ones = jnp.ones(vt.shape, BF16)
        vt = jnp.concatenate([vt, ones] if first else [ones, vt], axis=0)
        pv = _dot(vt, jnp.exp2(s - m).astype(BF16))
        out = pv if out is None else out + pv
    den = out[HEAD_DIM:HEAD_DIM + 1] if first else out[0:1]
    if extra is not None:
        den = den + jnp.exp2(extra - m)
    return out * (1.0 / den), den


def _underflowed(dens):
    width = max(d.shape[1] for d in dens)
    low = None
    for d in dens:
        d = jnp.concatenate([d] * (width // d.shape[1]), axis=1)
        low = d if low is None else jnp.minimum(low, d)
    return jnp.logical_not(jnp.min(low) >= DEN_FLOOR)


def _half_masks(q):
    lo = _lane(q.shape) < HEAD_DIM
    zero = jnp.zeros_like(q)
    return jnp.where(lo, q, zero), jnp.where(lo, zero, q)


def _merge_rows(a, b):
    r = lax.broadcasted_iota(jnp.int32, a.shape, 0)
    return jnp.where(r < HEAD_DIM, a, b)


def _sink_row(sink_ref, l, h0, h1, tq):
    c = lax.broadcasted_iota(jnp.int32, (1, 2 * tq), 1)
    return jnp.where(c < tq, sink_ref[l, h0], sink_ref[l, h1]) * LOG2E


SCORE_AHEAD = 4


def _run_groups(stage1, stage2, n, ahead=SCORE_AHEAD):
    out = []
    pending = [stage1(g) for g in range(min(ahead, n))]
    for g in range(n):
        if g + ahead < n:
            pending.append(stage1(g + ahead))
        out.append(stage2(g, pending.pop(0)))
    return out


def _scores(pairs, kmags):
    s = [_dot_nt(k, q) for k, q in pairs]
    b = None
    if kmags is not None:
        kmag, qmag = None, None
        for (_, q), km in zip(pairs, kmags):
            kmag = km if kmag is None else jnp.maximum(kmag, km)
            qmag = jnp.abs(q) if qmag is None else jnp.maximum(qmag, jnp.abs(q))
        b = _score_bound(kmag, qmag)
    return s, b


def _mla_scores(h, q_sets, k_sets, kmags):
    c = slice(h * LANES, (h + 1) * LANES)
    return _scores([(k[:, c], q[:, c]) for q, k in zip(q_sets, k_sets)], kmags)


def _gqa_scores(j, q_sets, k_sets, kmags):
    pairs = []
    for q, k in zip(q_sets, k_sets):
        qa = _half_masks(q[:, 0:LANES])[j]
        qb = _half_masks(q[:, LANES:])[j]
        pairs.append((k[...], jnp.concatenate([qa, qb], axis=0)))
    return _scores(pairs, kmags)


def _store_pair(y_ref, rows, blk, ot):
    y_ref[rows, blk * LANES:(blk + 1) * LANES] = ot.T.astype(BF16)


N_GROUPS = MLA_HEADS + 4


def _store_groups(y_ref, rows, o, tq):
    for p in range(MLA_HEADS // 2):
        _store_pair(y_ref, rows, p, jnp.concatenate([o[2 * p][:HEAD_DIM], o[2 * p + 1][:HEAD_DIM]], axis=0))
    for base, (o0, o1) in ((2, o[4:6]), (4, o[6:8])):
        _store_pair(y_ref, rows, base, _merge_rows(o0[:, :tq], o1[:, :tq]))
        _store_pair(y_ref, rows, base + 1, _merge_rows(o0[:, tq:], o1[:, tq:]))


def _attn_ctx_body(l, nb, t, sink_ref, qm_ref, km_ref, vm_ref, qc_ref, kc_ref, vc_ref, qd_ref, kd_ref, vd_ref,
                   y_ref):
    def stage1(g):
        b, k = divmod(g, N_GROUPS)
        rs = pl.ds(b * t, t)
        if k < MLA_HEADS:
            return _mla_scores(k, [qm_ref.at[rs]], [km_ref.at[rs]], None)
        q_ref, k_ref = (qc_ref, kc_ref) if k < MLA_HEADS + 2 else (qd_ref, kd_ref)
        return _gqa_scores(k % 2, [q_ref.at[rs]], [k_ref.at[rs]], None)

    def stage2(g, scored):
        s, _ = scored
        b, k = divmod(g, N_GROUPS)
        rs = slice(b * t, (b + 1) * t)
        j = k % 2
        hs = slice(j * HEAD_DIM, (j + 1) * HEAD_DIM)
        if k < MLA_HEADS:
            return _softmax_pv_t(s, [vm_ref[k * HEAD_DIM:(k + 1) * HEAD_DIM, rs]], True)[0]
        if k < MLA_HEADS + 2:
            return _softmax_pv_t(s, [vc_ref[hs, rs]], j == 0)[0]
        return _softmax_pv_t(s, [vd_ref[hs, rs]], j == 0, _sink_row(sink_ref, l, 2 * j, 2 * j + 1, t))[0]

    o = _run_groups(stage1, stage2, N_GROUPS * nb)
    for b in range(nb):
        _store_groups(y_ref, slice(b * t, (b + 1) * t), o[N_GROUPS * b:N_GROUPS * (b + 1)], t)


def _attn_ctx(l, t, sink, qm, km, vm, qc, kc, vc, qd, kd, vd):
    rows = qm.shape[0]
    nb = 8
    r = nb * t
    row = lambda a: (pl.BlockSpec((a.shape[0], r), lambda i: (0, i)) if a.shape[1] == rows
                     else pl.BlockSpec((r, a.shape[1]), lambda i: (i, 0)))
    ins = [qm, km, vm, qc, kc, vc, qd, kd, vd]
    return pl.pallas_call(
        functools.partial(_attn_ctx_body, l, nb, t),
        grid=(rows // r,),
        in_specs=[pl.BlockSpec(memory_space=pltpu.SMEM)] + [row(a) for a in ins],
        out_specs=pl.BlockSpec((r, 6 * LANES), lambda i: (i, 0)),
        out_shape=jax.ShapeDtypeStruct((rows, 6 * LANES), BF16),
        compiler_params=pltpu.CompilerParams(
            dimension_semantics=("arbitrary",), vmem_limit_bytes=VMEM_LIMIT),
        name="attn_ctx",
    )(sink, *ins)


def _window_block(i, tq, t, nwb):
    return jnp.clip(i * (tq // LANES) - WINDOW // LANES, 0, t // LANES - nwb)


def _attn_lat_body(l, tq, t, nwb, sink_ref, qmr_ref, qm_ref, km_ref, vm_ref, kx_ref, vx_ref,
                   qcr_ref, qc_ref, kc_ref, vc_ref, kcx_ref, vcx_ref,
                   qdr_ref, qd_ref, kdx_ref, vdx_ref, *rest):
    kd_refs, vd_refs = rest[:nwb], rest[nwb:2 * nwb]
    y_ref, kmag_scr = rest[2 * nwb:]
    i = pl.program_id(1)
    wk = nwb * LANES
    kpos = _window_block(i, tq, t, nwb) * LANES + lax.broadcasted_iota(jnp.int32, (wk, 2 * tq), 0)
    qpos = i * tq + lax.broadcasted_iota(jnp.int32, (wk, 2 * tq), 1) % tq
    valid = jnp.abs(kpos - qpos) <= WINDOW


    @pl.when(i == 0)
    def _():
        for h in range(MLA_HEADS):
            c = slice(h * LANES, (h + 1) * LANES)
            kmag_scr[h:h + 1, :] = _key_mags(km_ref[:, c])
            kmag_scr[MLA_HEADS + h:MLA_HEADS + h + 1, :] = _key_mags(kx_ref[:, c])
        kmag_scr[8:9, :] = _key_mags(kc_ref[...])
        kmag_scr[9:10, :] = _key_mags(kcx_ref[...])
        kmag_scr[10:11, :] = _key_mags(kdx_ref[...])
        kmag_scr[11:, :] = jnp.zeros((kmag_scr.shape[0] - 11, LANES), F32)

    def run(bounded):
        dens = []
        row = lambda r: kmag_scr[r:r + 1, :]

        def stage1(g):
            if g < MLA_HEADS:
                return _mla_scores(g, [qmr_ref, qm_ref], [km_ref, kx_ref],
                                   [row(g), row(MLA_HEADS + g)] if bounded else None)
            if g < MLA_HEADS + 2:
                return _gqa_scores(g % 2, [qcr_ref, qc_ref], [kc_ref, kcx_ref],
                                   [row(8), row(9)] if bounded else None)
            kw = jnp.concatenate([r[...] for r in kd_refs], axis=0)
            (s1, s2), bound = _gqa_scores(g % 2, [qdr_ref, qd_ref], [kw, kdx_ref],
                                          [_key_mags(kw), row(10)] if bounded else None)
            return [jnp.where(valid, s1, NEG_INF), s2], bound

        def stage2(g, scored):
            s, bound = scored
            j = g % 2
            hs = slice(j * HEAD_DIM, (j + 1) * HEAD_DIM)
            if g < MLA_HEADS:
                hs = slice(g * HEAD_DIM, (g + 1) * HEAD_DIM)
                o, den = _softmax_pv_t(s, [vm_ref[hs, :], vx_ref[hs, :]], True, None, bound)
            elif g < MLA_HEADS + 2:
                o, den = _softmax_pv_t(s, [vc_ref[hs, :], vcx_ref[hs, :]], j == 0, None, bound)
            else:
                vw = jnp.concatenate([r[hs, :] for r in vd_refs], axis=1)
                o, den = _softmax_pv_t(s, [vw, vdx_ref[hs, :]], j == 0,
                                       _sink_row(sink_ref, l, 2 * j, 2 * j + 1, tq), bound)
            dens.append(den)
            return o

        _store_groups(y_ref, slice(None), _run_groups(stage1, stage2, N_GROUPS), tq)
        return dens

    @pl.when(_underflowed(run(True)))
    def _():
        run(False)


def _attn_lat(l, nbatch, t, sink, qmr, qm, km, vmt, kx, vxt, qcr, qc, kc, vct, kcx, vcxt, qdr, qd, kd, vdt,
              kdx, vdxt):
    tq = 256
    nq = t // tq
    nwb = (tq + 2 * WINDOW) // LANES
    nkb = t // LANES
    rows = qm.shape[0]
    qs = lambda a: pl.BlockSpec((tq, a.shape[1]), lambda b, i: (b * nq + i, 0))
    ks = lambda a: pl.BlockSpec((t, a.shape[1]), lambda b, i: (b, 0))
    kts = lambda a: pl.BlockSpec((a.shape[0], t), lambda b, i: (0, b))
    xs = lambda a: pl.BlockSpec((None, None) + a.shape[2:], lambda b, i: (l, b, 0, 0))
    cs = lambda a: pl.BlockSpec((None, None) + a.shape[2:], lambda b, i: (b, l, 0, 0))
    wblk = lambda b, i, k: b * nkb + _window_block(i, tq, t, nwb) + k
    kd_specs = [pl.BlockSpec((LANES, LANES), lambda b, i, k=k: (wblk(b, i, k), 0)) for k in range(nwb)]
    vd_specs = [pl.BlockSpec((LANES, LANES), lambda b, i, k=k: (0, wblk(b, i, k))) for k in range(nwb)]
    in_specs = [pl.BlockSpec(memory_space=pltpu.SMEM),
                qs(qmr), qs(qm), ks(km), kts(vmt), xs(kx), xs(vxt),
                qs(qcr), qs(qc), ks(kc), kts(vct), cs(kcx), cs(vcxt),
                qs(qdr), qs(qd), cs(kdx), cs(vdxt)] + kd_specs + vd_specs
    return pl.pallas_call(
        functools.partial(_attn_lat_body, l, tq, t, nwb),
        grid=(nbatch, nq),
        in_specs=in_specs,
        out_specs=pl.BlockSpec((tq, 6 * LANES), lambda b, i: (b * nq + i, 0)),
        out_shape=jax.ShapeDtypeStruct((rows, 6 * LANES), BF16),
        scratch_shapes=[pltpu.VMEM((16, LANES), F32)],
        compiler_params=pltpu.CompilerParams(
            dimension_semantics=("arbitrary", "arbitrary"), vmem_limit_bytes=VMEM_LIMIT),
        name="attn_lat",
    )(sink, qmr, qm, km, vmt, kx, vxt, qcr, qc, kc, vct, kcx, vcxt, qdr, qd, kdx, vdxt,
      *([kd] * nwb), *([vdt] * nwb))


def _post_body(layer, seq_len, final, mod_ref, n2_ref, cwa_ref, wo_ref, cwf_ref, wup_hbm, wdn_hbm, fn_ref, *rest):
    o_ref, hbuf, gbuf, pbuf, abuf, wup_ref, wdn_ref, wsem = rest[-8:]
    ins = rest[:-8]
    r = o_ref.shape[0]
    first_step = pl.program_id(0) == 0
    up_copy = pltpu.make_async_copy(wup_hbm.at[layer], wup_ref, wsem.at[0])
    dn_copy = pltpu.make_async_copy(wdn_hbm.at[layer], wdn_ref, wsem.at[1])

    @pl.when(first_step)
    def _():
        up_copy.start()
        dn_copy.start()
    edges = seq_len > r
    assert edges or r % seq_len == 0
    if edges:
        cat = lambda k: jnp.concatenate([ins[3 * k][...], ins[3 * k + 1][...], ins[3 * k + 2][...]], axis=0)
    else:
        cat = lambda k: ins[k][...]
    halo = BF16_ROWS if edges else 0
    n = r + 2 * halo
    grp = r // 8
    nchunk = D_MODEL // LANES
    i = pl.program_id(0)

    zg = cat(1).astype(F32)
    zz, gb = zg[:, :GROUP_WIDTH], zg[:, GROUP_WIDTH:]
    pos = (i * r - halo + lax.broadcasted_iota(jnp.int32, (n, 1), 0)) % seq_len
    prev = jnp.where(pos == 0, 0.0, pltpu.roll(zz, 1, 0))
    nxt = jnp.where(pos == seq_len - 1, 0.0, pltpu.roll(zz, n - 1, 0))
    cwa = cwa_ref[...]
    ya = gb * (prev * cwa[0:1] + zz * cwa[1:2] + nxt * cwa[2:3])
    ycat = jnp.concatenate([ya.astype(BF16), cat(2)], axis=1)
    x1 = cat(0) + mod_ref[2:3, :] * _dot(ycat, wo_ref[...])
    h2 = _rms(x1, n2_ref[...], D_MODEL) * (1.0 + mod_ref[4:5, :]) + mod_ref[3:4, :]
    x1c = x1[halo:halo + r]

    for c in range(nchunk):
        for s in range(8):
            pbuf[c, pl.ds(s, grp, stride=8), :] = h2[halo + s * grp:halo + (s + 1) * grp, c * LANES:(c + 1) * LANES]
    if edges:
        hrow = lax.broadcasted_iota(jnp.int32, (halo, 1), 0)
        edge = jnp.where(hrow == 0, h2[halo - 1:halo], jnp.where(hrow == 1, h2[halo + r:halo + r + 1], 0.0))
        hbuf[0:halo, :] = edge.astype(BF16)
    hbuf[halo:, :] = jnp.concatenate([pbuf[c] for c in range(nchunk)], axis=1).astype(BF16)

    sub = lax.broadcasted_iota(jnp.int32, (8, 1), 0)
    seq_first = (i * r + sub * grp) % seq_len == 0
    seq_last = (i * r + sub * grp + grp - 1) % seq_len == seq_len - 1

    def up(j):
        ga = slice(j * FF_TILE, (j + 1) * FF_TILE)
        va = slice(D_FF + j * FF_TILE, D_FF + (j + 1) * FF_TILE)
        return _dot(hbuf[...], wup_ref[:, ga]), _dot(hbuf[...], wup_ref[:, va])

    def conv(u, cw):
        head = pltpu.roll(u[halo + r - 8:halo + r], 1, 0)
        tail = pltpu.roll(u[halo:halo + 8], 7, 0)
        if edges:
            head = jnp.where(sub == 0, u[0:8], head)
            tail = jnp.where(sub == 7, pltpu.roll(u[0:8], 6, 0), tail)
        prev = jnp.concatenate([jnp.where(seq_first, 0.0, head), u[halo:halo + r - 8]], axis=0)
        nxt = jnp.concatenate([u[halo + 8:halo + r], jnp.where(seq_last, 0.0, tail)], axis=0)
        return prev * cw[0:1] + u[halo:halo + r] * cw[1:2] + nxt * cw[2:3]

    @pl.when(first_step)
    def _():
        up_copy.wait()
        dn_copy.wait()

    nxt_u = up(0)
    for j in range(N_FF_TILES):
        ua, ub = nxt_u
        if j + 1 < N_FF_TILES:
            nxt_u = up(j + 1)
        ga = slice(j * FF_TILE, (j + 1) * FF_TILE)
        va = slice(D_FF + j * FF_TILE, D_FF + (j + 1) * FF_TILE)
        ua = conv(ua, cwf_ref[:, ga])
        ub = conv(ub, cwf_ref[:, va])
        gbuf[:, ga] = (ua * jax.nn.sigmoid(ua) * ub).astype(BF16)
    acc = _dot(gbuf[...], wdn_ref[...])

    for c in range(nchunk):
        abuf[c] = acc[:, c * LANES:(c + 1) * LANES]
    g2 = mod_ref[5:6, :]
    for c in range(nchunk):
        cs = slice(c * LANES, (c + 1) * LANES)
        for s in range(8):
            rs = slice(s * grp, (s + 1) * grp)
            o_ref[rs, cs] = x1c[rs, cs] + g2[:, cs] * abuf[c, pl.ds(s, grp, stride=8), :]
    if final:
        o_ref[...] = _rms(o_ref[...], fn_ref[...], D_MODEL)


def _halo_specs(r, w, rows):
    nblk = rows // BF16_ROWS
    per = r // BF16_ROWS
    prev = pl.BlockSpec((BF16_ROWS, w), lambda i: (jnp.maximum(i * per - 1, 0), 0))
    cur = pl.BlockSpec((r, w), lambda i: (i, 0))
    nxt = pl.BlockSpec((BF16_ROWS, w), lambda i: (jnp.minimum((i + 1) * per, nblk - 1), 0))
    return prev, cur, nxt


def _cond_spec(l, seq_len, cond_base):
    r = ROW_TILE
    tiles_per_seq = max(seq_len // r, 1)
    step = 1 if seq_len >= r else 0
    return pl.BlockSpec((None, None, 6, D_MODEL),
                        lambda i: (l, cond_base + step * (i // tiles_per_seq), 0, 0))


def _post(x, mod, lw, l, seq_len, cond_base, zg, y, final_norm, final):
    rows = x.shape[0]
    r = ROW_TILE
    full = lambda a: pl.BlockSpec((None,) + a.shape[1:], lambda i: (l,) + (0,) * (a.ndim - 1),
                                  pipeline_mode=pl.Buffered(1))
    acts = [x, zg, y]
    edges = seq_len > r
    if edges:
        act_specs = [s for a in acts for s in _halo_specs(r, a.shape[1], rows)]
        act_args = [a for a in acts for _ in range(3)]
    else:
        act_specs = [pl.BlockSpec((r, a.shape[1]), lambda i: (i, 0)) for a in acts]
        act_args = acts
    return pl.pallas_call(
        functools.partial(_post_body, l, seq_len, final),
        grid=(rows // r,),
        in_specs=[_cond_spec(l, seq_len, cond_base), full(lw['norm2']), full(lw['conv_a']), full(lw['w_out']),
                  full(lw['conv_ff']), pl.BlockSpec(memory_space=pl.ANY), pl.BlockSpec(memory_space=pl.ANY),
                  pl.BlockSpec((1, D_MODEL), lambda i: (0, 0))] + act_specs,
        out_specs=pl.BlockSpec((r, D_MODEL), lambda i: (i, 0)),
        out_shape=jax.ShapeDtypeStruct((rows, D_MODEL), F32),
        scratch_shapes=[pltpu.VMEM((r + (BF16_ROWS if edges else 0), D_MODEL), BF16), pltpu.VMEM((r, D_FF), BF16),
                        pltpu.VMEM((D_MODEL // LANES, r, LANES), F32),
                        pltpu.VMEM((D_MODEL // LANES, r, LANES), F32),
                        pltpu.VMEM(lw['w_up'].shape[1:], BF16), pltpu.VMEM(lw['w_down'].shape[1:], BF16),
                        pltpu.SemaphoreType.DMA((2,))],
        compiler_params=pltpu.CompilerParams(
            dimension_semantics=("arbitrary",), vmem_limit_bytes=VMEM_LIMIT),
        name="post",
    )(mod, lw['norm2'], lw['conv_a'], lw['w_out'], lw['conv_ff'], lw['w_up'], lw['w_down'], final_norm,
      *act_args)


def _pad_cols(a, n):
    return jnp.pad(a, [(0, 0)] * (a.ndim - 1) + [(0, n - a.shape[-1])])


def _perm_heads(a, axis):
    h = [lax.slice_in_dim(a, k * HEAD_DIM, (k + 1) * HEAD_DIM, axis=axis) for k in range(4)]
    return jnp.concatenate([h[0], h[2], h[1], h[3]], axis=axis)


def _prep_weights(w_in, mla_wq_b, mla_wkv_b, w_out, w_up, conv_ff, w_down, norm1, norm2, conv_a,
                  mla_q_norm, mla_kv_norm, gqa_q_norm, gqa_k_norm):
    L = DEPTH
    a = w_in[..., 0:768]
    cq_kpe = _pad_cols(jnp.concatenate([w_in[..., 768:960], w_in[..., 1088:1120]], axis=-1), 256)
    ckv = w_in[..., 960:1088]
    qc = _perm_heads(w_in[..., 1120:1376], 2)
    kvc = w_in[..., 1376:1632]
    qd = _perm_heads(w_in[..., 1632:1888], 2)
    kvd = w_in[..., 1888:2144]
    w_in_p = jnp.concatenate([a, cq_kpe, ckv, qc, kvc, qd, kvd], axis=-1).astype(BF16)
    assert w_in_p.shape[-1] == IN_COLS_PACKED

    wq = mla_wq_b.reshape(L, MLA_Q_LORA, MLA_HEADS, MLA_NOPE + MLA_ROPE)
    wq = jnp.concatenate([wq[..., MLA_NOPE:], wq[..., :MLA_NOPE],
                          jnp.zeros((L, MLA_Q_LORA, MLA_HEADS, LANES - MLA_NOPE - MLA_ROPE), F32)], axis=-1)
    wq = jnp.pad(wq.reshape(L, MLA_Q_LORA, MLA_HEADS * LANES), ((0, 0), (0, 256 - MLA_Q_LORA), (0, 0)))

    wkv = mla_wkv_b.reshape(L, MLA_KV_LORA, MLA_HEADS, MLA_NOPE + MLA_V)
    zk = jnp.zeros((L, MLA_KV_LORA, MLA_HEADS, MLA_ROPE), F32)
    wk = jnp.concatenate([zk, wkv[..., :MLA_NOPE], zk], axis=-1).reshape(L, MLA_KV_LORA, MLA_HEADS * LANES)
    wv = wkv[..., MLA_NOPE:].reshape(L, MLA_KV_LORA, MLA_HEADS * MLA_V)
    wkv_p = jnp.concatenate([wk, wv], axis=-1)

    wo = jnp.concatenate([w_out[:, 0:512], _perm_heads(w_out[:, 512:768], 1),
                          _perm_heads(w_out[:, 768:1024], 1)], axis=1)

    hm = jnp.asarray(np.kron(np.eye(2 * LANES // HEAD_DIM), np.ones((HEAD_DIM, HEAD_DIM))), BF16)
    return {
        'w_in': w_in_p, 'wq': wq.astype(BF16), 'wkv': wkv_p.astype(BF16), 'w_out': wo.astype(BF16),
        'w_up': w_up.astype(BF16), 'conv_ff': conv_ff, 'w_down': w_down.astype(BF16),
        'norm1': norm1.reshape(L, 1, D_MODEL), 'norm2': norm2.reshape(L, 1, D_MODEL), 'conv_a': conv_a,
        'gq': _pad_cols(mla_q_norm, 256).reshape(L, 1, 256), 'gkv': mla_kv_norm.reshape(L, 1, MLA_KV_LORA),
        'gqc': jnp.tile(gqa_q_norm, (1, 4)).reshape(L, 1, 256), 'gkc': jnp.tile(gqa_k_norm, (1, 2)).reshape(L, 1, 128),
        'hm': hm,
    }


def _rope_tables(t):
    rows = t // GRID_W
    row = np.repeat(np.arange(rows, dtype=np.float64), GRID_W)
    col = np.tile(np.arange(GRID_W, dtype=np.float64), rows)

    def tabs(dim):
        half = dim // 2
        inv = np.power(ROPE_THETA, -np.arange(0, half, 2, dtype=np.float64) / half)
        ar = row[:, None] * inv
        ac = col[:, None] * inv
        c = np.concatenate([np.cos(ar), np.cos(ar), np.cos(ac), np.cos(ac)], axis=1)
        s = np.concatenate([-np.sin(ar), np.sin(ar), -np.sin(ac), np.sin(ac)], axis=1)
        return c, s

    c64, s64 = tabs(HEAD_DIM)
    c32, s32 = tabs(MLA_ROPE)
    c64 = np.tile(c64, (1, 2))
    s64 = np.tile(s64, (1, 2))
    c32 = np.concatenate([c32, np.ones((t, LANES - MLA_ROPE))], axis=1)
    s32 = np.concatenate([s32, np.zeros((t, LANES - MLA_ROPE))], axis=1)
    return tuple(jnp.asarray(a.astype(np.float32)) for a in (c64, s64, c32, s32))


def kernel(x_prompt, x_sample, cache_mla_ckv, cache_mla_kpe, cache_gqa_k, cache_gqa_v, cache_swa_k, cache_swa_v,
           c, c_ctx, w_ada, b_ada, norm1, w_in, conv_a, mla_q_norm, mla_wq_b, mla_kv_norm, mla_wkv_b,
           gqa_q_norm, gqa_k_norm, swa_sink, w_out, norm2, w_up, conv_ff, w_down, final_norm):
    B, T, _ = x_prompt.shape
    DB, DT, _ = x_sample.shape
    past = cache_mla_ckv.shape[2]

    lw = _prep_weights(w_in, mla_wq_b, mla_wkv_b, w_out, w_up, conv_ff, w_down, norm1, norm2, conv_a,
                       mla_q_norm, mla_kv_norm, gqa_q_norm, gqa_k_norm)
    rope_tabs = _rope_tables(DT)
    fnorm = final_norm.reshape(1, D_MODEL)

    cond_t = jnp.concatenate([c_ctx[:, None], c.T, jnp.zeros((D_MODEL, 8 - 1 - DB), F32)], axis=1)
    mod = _ada(cond_t, 1 + DB, w_ada, b_ada).reshape(DEPTH, 8, 6, D_MODEL)

    kx, vx = _ctx_mla(cache_mla_ckv, _pad_cols(cache_mla_kpe, LANES), lw['wkv'])
    flat = lambda a: a.reshape(DB, DEPTH, past, 2 * HEAD_DIM).astype(BF16)
    kcx, kdx = flat(cache_gqa_k), flat(cache_swa_k)
    vcx, vdx = jnp.swapaxes(flat(cache_gqa_v), 2, 3), jnp.swapaxes(flat(cache_swa_v), 2, 3)

    xp = x_prompt.reshape(B * T, D_MODEL)
    xs = x_sample.reshape(DB * DT, D_MODEL)
    caches = None
    for l in range(DEPTH):
        final = l == DEPTH - 1
        (zg, qm, km, vm, qc, kc, vc, qd, kd, vd, *caches) = _inproj(xp, mod, lw, l, T, 0, None, caches)
        y = _attn_ctx(l, T, swa_sink, qm, km, vm, qc, kc, vc, qd, kd, vd)
        xp = _post(xp, mod, lw, l, T, 0, zg, y, fnorm, final)
        (zg, qmr, qm, km, vm, qcr, qc, kc, vc, qdr, qd, kd, vd) = _inproj(xs, mod, lw, l, DT, 1, rope_tabs)
        y = _attn_lat(l, DB, DT, swa_sink, qmr, qm, km, vm, kx, vx, qcr, qc, kc, vc, kcx, vcx,
                      qdr, qd, kd, vd, kdx, vdx)
        xs = _post(xs, mod, lw, l, DT, 1, zg, y, fnorm, final)

    heads = lambda a: a.reshape(B, DEPTH, T, 2, HEAD_DIM)
    return (xp.reshape(B, T, D_MODEL), xs.reshape(DB, DT, D_MODEL), caches[0], caches[1],
            heads(caches[2]), heads(caches[3]), heads(caches[4]), heads(caches[5]))
```

```python
import functools
import math

import jax
import jax.numpy as jnp
import numpy as np
from jax import lax
from jax.experimental import pallas as pl
from jax.experimental.pallas import tpu as pltpu

F32 = jnp.float32
BF16 = jnp.bfloat16

D_MODEL = 1024
DEPTH = 2
GRID_W = 64
HEAD_DIM = 64
GROUP_WIDTH = D_MODEL // 4
MLA_HEADS = 4
MLA_NOPE = 64
MLA_ROPE = 32
MLA_V = 64
MLA_Q_LORA = 192
MLA_KV_LORA = 128
WINDOW = 128
D_FF = 2816
ROPE_THETA = 10000.0
EPS = 1e-6
NEG_INF = -1e30
LOG2E = math.log2(math.e)
ATTN_SCALE = HEAD_DIM ** -0.5 * LOG2E
MLA_SCALE = (MLA_NOPE + MLA_ROPE) ** -0.5 * LOG2E

LANES = 128
BF16_ROWS = 16
MXU_WIDTH = 256
VMEM_PLAN = {k: v * 1024 * 1024 for k, v in
             dict(ada=20, ctx_mla=8, inproj=46, attn_ctx=32, attn_lat=56, post=46).items()}

ROW_TILE = 512
IN_ROW_TILE = 1024
IN_SUB_TILE = 256
IN_SUB_TILE_ROPE = 128
FF_TILE = MXU_WIDTH
N_FF_TILES = D_FF // FF_TILE
IN_COLS_PACKED = 2176

_NT = (((1,), (1,)), ((), ()))


def _dot(a, b):
    return jnp.dot(a, b, preferred_element_type=F32)


def _dot_nt(a, b):
    return lax.dot_general(a, b, _NT, preferred_element_type=F32)


def _rms(x, g, n):
    ms = jnp.sum(x * x, axis=-1, keepdims=True) * (1.0 / n)
    return x * lax.rsqrt(ms + EPS) * g


def _lane(shape):
    return lax.broadcasted_iota(jnp.int32, shape, len(shape) - 1)


def _rope(x, c, s, half):
    w = x.shape[-1]
    lo = (_lane(x.shape) % (2 * half)) < half
    sw = jnp.where(lo, pltpu.roll(x, w - half, 1), pltpu.roll(x, half, 1))
    return x * c + sw * s


def _ada_body(ncond, c_ref, w_ref, b_ref, o_ref):
    c = c_ref[...]
    s = c * jax.nn.sigmoid(c)
    tn = w_ref.shape[1]
    w = w_ref[...].reshape(D_MODEL // 8, 8, tn)
    rows = []
    for r in range(ncond):
        part = jnp.sum(w * s[:, r:r + 1].reshape(D_MODEL // 8, 8, 1), axis=0)
        rows.append(jnp.sum(part, axis=0, keepdims=True))
    rows.append(jnp.zeros((8 - ncond, tn), F32))
    o_ref[...] = jnp.concatenate(rows, axis=0) + b_ref[...]


def _ada(cond_t, ncond, w_ada, b_ada):
    tn = 1536
    n = 6 * D_MODEL
    return pl.pallas_call(
        functools.partial(_ada_body, ncond),
        grid=(DEPTH, n // tn),
        in_specs=[pl.BlockSpec((D_MODEL, 8), lambda l, j: (0, 0)),
                  pl.BlockSpec((None, D_MODEL, tn), lambda l, j: (l, 0, j)),
                  pl.BlockSpec((None, 1, tn), lambda l, j: (l, 0, j))],
        out_specs=pl.BlockSpec((None, 8, tn), lambda l, j: (l, 0, j)),
        out_shape=jax.ShapeDtypeStruct((DEPTH, 8, n), F32),
        compiler_params=pltpu.CompilerParams(
            dimension_semantics=("arbitrary", "arbitrary"), vmem_limit_bytes=VMEM_PLAN["ada"]),
        name="ada",
    )(cond_t, w_ada, b_ada.reshape(DEPTH, 1, n))


def _ctx_body(ckv_ref, kpe_ref, w_ref, kx_ref, vx_ref):
    kv = _dot(ckv_ref[...].astype(BF16), w_ref[...])
    kpe = kpe_ref[...]
    kx_ref[...] = (kv[:, :4 * LANES] + jnp.concatenate([kpe] * MLA_HEADS, axis=1)).astype(BF16)
    vx_ref[...] = kv[:, 4 * LANES:].T.astype(BF16)


def _ctx_mla(cache_ckv, cache_kpe_pad, wkv_p):
    b, _, s, _ = cache_ckv.shape
    return pl.pallas_call(
        _ctx_body,
        grid=(DEPTH, b),
        in_specs=[pl.BlockSpec((None, None, s, MLA_KV_LORA), lambda l, i: (i, l, 0, 0)),
                  pl.BlockSpec((None, None, s, LANES), lambda l, i: (i, l, 0, 0)),
                  pl.BlockSpec((None, MLA_KV_LORA, 6 * LANES), lambda l, i: (l, 0, 0))],
        out_specs=[pl.BlockSpec((None, None, s, 4 * LANES), lambda l, i: (l, i, 0, 0)),
                   pl.BlockSpec((None, None, 2 * LANES, s), lambda l, i: (l, i, 0, 0))],
        out_shape=[jax.ShapeDtypeStruct((DEPTH, b, s, 4 * LANES), BF16),
                   jax.ShapeDtypeStruct((DEPTH, b, 2 * LANES, s), BF16)],
        compiler_params=pltpu.CompilerParams(
            dimension_semantics=("arbitrary", "arbitrary"), vmem_limit_bytes=VMEM_PLAN["ctx_mla"]),
        name="ctx_mla",
    )(cache_ckv, cache_kpe_pad, wkv_p)


def _inproj_body(rope, n_alias, x_ref, mod_ref, n1_ref, w_ref, wq_ref, wkv_ref, gq_ref, gkv_ref, gqc_ref, gkc_ref,
                 hm_ref, *rest):
    if rope:
        c64_ref, s64_ref, c32_ref, s32_ref = rest[:4]
        outs = rest[4:]
    else:
        outs = rest[n_alias:]
    sh1 = mod_ref[0:1, :]
    sc1 = mod_ref[1:2, :]
    hm = hm_ref[...]
    sub = IN_SUB_TILE_ROPE if rope else IN_SUB_TILE
    nsub = x_ref.shape[0] // sub

    def project(s):
        x = x_ref[s * sub:(s + 1) * sub, :]
        h = _rms(x, n1_ref[...], D_MODEL) * (1.0 + sc1) + sh1
        return _dot(h.astype(BF16), w_ref[...])

    def head_rms(v, g):
        w = v.shape[-1]
        ss = _dot((v * v).astype(BF16), hm[:w, :w])
        return v * lax.rsqrt(ss * (1.0 / HEAD_DIM) + EPS) * g

    def finish(s, acc):
        rs = slice(s * sub, (s + 1) * sub)
        xa, gb, gc = acc[:, 0:256], acc[:, 256:512], acc[:, 512:768]
        z = gc * xa
        blk = acc[:, 768:1024]
        cq = jnp.where(_lane(blk.shape) < MLA_Q_LORA, blk, 0.0)
        cqn = _rms(cq, gq_ref[...], MLA_Q_LORA)
        qm = _dot(cqn.astype(BF16), wq_ref[...]) * MLA_SCALE
        ckvn = _rms(acc[:, 1024:1152], gkv_ref[...], MLA_KV_LORA)
        kv = _dot(ckvn.astype(BF16), wkv_ref[...])
        kpe = pltpu.roll(blk[:, LANES:], 2 * LANES - MLA_Q_LORA, 1)
        kpe = jnp.where(_lane(kpe.shape) < MLA_ROPE, kpe, 0.0)
        vm = kv[:, 4 * LANES:]
        qc = head_rms(acc[:, 1152:1408], gqc_ref[...]) * ATTN_SCALE
        kc = head_rms(acc[:, 1408:1536], gkc_ref[...])
        vc = acc[:, 1536:1664]
        qd = acc[:, 1664:1920] * ATTN_SCALE
        kd = acc[:, 1920:2048]
        vd = acc[:, 2048:2176]

        if not rope:
            (zg_o, qm_o, km_o, vm_o, qc_o, kc_o, vc_o, qd_o, kd_o, vd_o,
             ckv_c, kpe_c, kc_c, vc_c, kd_c, vd_c) = outs
            km = kv[:, :4 * LANES] + jnp.concatenate([kpe] * MLA_HEADS, axis=1)
            zg_o[rs, 0:256] = z.astype(BF16)
            zg_o[rs, 256:512] = gb.astype(BF16)
            qm_o[rs, :] = qm.astype(BF16)
            km_o[rs, :] = km.astype(BF16)
            vm_o[:, rs] = vm.T.astype(BF16)
            qc_o[rs, :] = qc.astype(BF16)
            kc_o[rs, :] = kc.astype(BF16)
            vc_o[:, rs] = vc.T.astype(BF16)
            qd_o[rs, :] = qd.astype(BF16)
            kd_o[rs, :] = kd.astype(BF16)
            vd_o[:, rs] = vd.T.astype(BF16)
            for o, v in ((ckv_c, ckvn), (kpe_c, kpe[:, :MLA_ROPE]), (kc_c, kc), (vc_c, vc), (kd_c, kd),
                         (vd_c, vd)):
                t = o.shape[-2]
                bs = slice(s * sub // t, (s + 1) * sub // t)
                v = v.reshape(sub // t, t, v.shape[-1])
                if n_alias:
                    o[bs] = v
                else:
                    o[bs, 0] = v
                    o[bs, 1:] = jnp.zeros((sub // t, o.shape[1] - 1) + v.shape[1:], F32)
        else:
            (zg_o, qmr_o, qm_o, km_o, vm_o, qcr_o, qc_o, kc_o, vc_o, qdr_o, qd_o, kd_o, vd_o) = outs
            c64, s64, c32, s32 = c64_ref[rs, :], s64_ref[rs, :], c32_ref[rs, :], s32_ref[rs, :]
            rope64 = lambda v: jnp.concatenate(
                [_rope(v[:, i:i + LANES], c64, s64, 16) for i in range(0, v.shape[-1], LANES)], axis=1)
            rope32 = lambda v: jnp.concatenate(
                [_rope(v[:, i:i + LANES], c32, s32, 8) for i in range(0, v.shape[-1], LANES)], axis=1)
            km = kv[:, :4 * LANES] + jnp.concatenate([_rope(kpe, c32, s32, 8)] * MLA_HEADS, axis=1)
            zg_o[rs, 0:256] = z.astype(BF16)
            zg_o[rs, 256:512] = gb.astype(BF16)
            qmr_o[rs, :] = rope32(qm).astype(BF16)
            qm_o[rs, :] = qm.astype(BF16)
            km_o[rs, :] = km.astype(BF16)
            vm_o[:, rs] = vm.T.astype(BF16)
            qcr_o[rs, :] = rope64(qc).astype(BF16)
            qc_o[rs, :] = qc.astype(BF16)
            kc_o[rs, :] = rope64(kc).astype(BF16)
            vc_o[:, rs] = vc.T.astype(BF16)
            qdr_o[rs, :] = rope64(qd).astype(BF16)
            qd_o[rs, :] = qd.astype(BF16)
            kd_o[rs, :] = rope64(kd).astype(BF16)
            vd_o[:, rs] = vd.T.astype(BF16)

    nxt = project(0)
    for s in range(nsub):
        acc = nxt
        if s + 1 < nsub:
            nxt = project(s + 1)
        finish(s, acc)


def _inproj(x, mod, lw, l, seq_len, cond_base, rope_tabs, prev_caches=None):
    rows = x.shape[0]
    r = IN_ROW_TILE
    rope = rope_tabs is not None
    aliases = {}
    tiles_per_seq = max(seq_len // r, 1)
    step = 1 if seq_len >= r else 0

    def cond_map(i):
        return (l, cond_base + step * (i // tiles_per_seq), 0, 0)

    row = lambda w: pl.BlockSpec((r, w), lambda i: (i, 0))
    full = lambda a: pl.BlockSpec((None,) + a.shape[1:], lambda i: (l,) + (0,) * (a.ndim - 1))
    in_specs = [row(D_MODEL),
                pl.BlockSpec((None, None, 6, D_MODEL), cond_map),
                full(lw['norm1']), full(lw['w_in']), full(lw['wq']), full(lw['wkv']),
                full(lw['gq']), full(lw['gkv']), full(lw['gqc']), full(lw['gkc']),
                pl.BlockSpec((2 * LANES, 2 * LANES), lambda i: (0, 0))]
    args = [x, mod, lw['norm1'], lw['w_in'], lw['wq'], lw['wkv'], lw['gq'], lw['gkv'], lw['gqc'], lw['gkc'],
            lw['hm']]
    def act(w, transposed):
        if transposed:
            return jax.ShapeDtypeStruct((w, rows), BF16), pl.BlockSpec((w, r), lambda i: (0, i))
        return jax.ShapeDtypeStruct((rows, w), BF16), row(w)

    if rope:
        tab = pl.BlockSpec((r, LANES), lambda i: (i % tiles_per_seq, 0))
        in_specs += [tab] * 4
        args += list(rope_tabs)
        widths = [(512, 0), (512, 0), (512, 0), (512, 0), (256, 1), (256, 0), (256, 0), (128, 0),
                  (128, 1), (256, 0), (256, 0), (128, 0), (128, 1)]
        out_shape, out_specs = map(list, zip(*[act(w, t) for w, t in widths]))
    else:
        widths = [(512, 0), (512, 0), (512, 0), (256, 1), (256, 0), (128, 0), (128, 1), (256, 0),
                  (128, 0), (128, 1)]
        cwidths = [128, MLA_ROPE, 128, 128, 128, 128]
        nb = r // seq_len
        out_shape, out_specs = map(list, zip(*[act(w, t) for w, t in widths]))
        out_shape += [jax.ShapeDtypeStruct((rows // seq_len, DEPTH, seq_len, w), F32) for w in cwidths]
        assert (prev_caches is None) == (l == 0)
        if l == 0:
            out_specs += [pl.BlockSpec((nb, DEPTH, seq_len, w), lambda i: (i, 0, 0, 0)) for w in cwidths]
        else:
            out_specs += [pl.BlockSpec((nb, None, seq_len, w), lambda i: (i, l, 0, 0)) for w in cwidths]
        if prev_caches is not None:
            aliases = {len(args) + k: len(widths) + k for k in range(len(cwidths))}
            in_specs += [pl.BlockSpec(memory_space=pl.ANY)] * len(cwidths)
            args += list(prev_caches)
    return pl.pallas_call(
        functools.partial(_inproj_body, rope, len(aliases)),
        grid=(rows // r,),
        in_specs=in_specs, out_specs=out_specs, out_shape=out_shape,
        input_output_aliases=aliases,
        compiler_params=pltpu.CompilerParams(
            dimension_semantics=("arbitrary",), vmem_limit_bytes=VMEM_PLAN["inproj"]),
        name="inproj_rope" if rope else "inproj",
    )(*args)


DEN_FLOOR = 2.0 ** -80


def _key_mags(k):
    return jnp.max(jnp.abs(k), axis=0, keepdims=True).astype(F32)


def _score_bound(kmag, qmag):
    col = jnp.sum(qmag.astype(F32) * kmag, axis=1, keepdims=True)
    return jnp.broadcast_to(col, qmag.shape).T[0:1]


def _softmax_pv_t(s_list, vt_list, first, extra=None, bound=None):
    if bound is None:
        m = s_list[0].max(axis=0, keepdims=True)
        for s in s_list[1:]:
            m = jnp.maximum(m, s.max(axis=0, keepdims=True))
    else:
        m = bound
    if extra is not None:
        m = jnp.maximum(m, extra)
    out = None
    for s, vt in zip(s_list, vt_list):
        ones = jnp.ones(vt.shape, BF16)
        vt = jnp.concatenate([vt, ones] if first else [ones, vt], axis=0)
        pv = _dot(vt, jnp.exp2(s - m).astype(BF16))
        out = pv if out is None else out + pv
    den = out[HEAD_DIM:HEAD_DIM + 1] if first else out[0:1]
    if extra is not None:
        den = den + jnp.exp2(extra - m)
    return out * (1.0 / den), den


def _underflowed(dens):
    width = max(d.shape[1] for d in dens)
    low = None
    for d in dens:
        d = jnp.concatenate([d] * (width // d.shape[1]), axis=1)
        low = d if low is None else jnp.minimum(low, d)
    return jnp.logical_not(jnp.min(low) >= DEN_FLOOR)


def _half_masks(q):
    lo = _lane(q.shape) < HEAD_DIM
    zero = jnp.zeros_like(q)
    return jnp.where(lo, q, zero), jnp.where(lo, zero, q)


def _merge_rows(a, b):
    r = lax.broadcasted_iota(jnp.int32, a.shape, 0)
    return jnp.where(r < HEAD_DIM, a, b)


def _sink_row(sink_ref, l, h0, h1, tq):
    c = lax.broadcasted_iota(jnp.int32, (1, 2 * tq), 1)
    return jnp.where(c < tq, sink_ref[l, h0], sink_ref[l, h1]) * LOG2E


SCORE_AHEAD = 4


def _run_groups(stage1, stage2, n, ahead=SCORE_AHEAD):
    out = []
    pending = [stage1(g) for g in range(min(ahead, n))]
    for g in range(n):
        if g + ahead < n:
            pending.append(stage1(g + ahead))
        out.append(stage2(g, pending.pop(0)))
    return out


def _scores(pairs, kmags):
    s = [_dot_nt(k, q) for k, q in pairs]
    b = None
    if kmags is not None:
        kmag, qmag = None, None
        for (_, q), km in zip(pairs, kmags):
            kmag = km if kmag is None else jnp.maximum(kmag, km)
            qmag = jnp.abs(q) if qmag is None else jnp.maximum(qmag, jnp.abs(q))
        b = _score_bound(kmag, qmag)
    return s, b


def _mla_scores(h, q_sets, k_sets, kmags):
    c = slice(h * LANES, (h + 1) * LANES)
    return _scores([(k[:, c], q[:, c]) for q, k in zip(q_sets, k_sets)], kmags)


def _gqa_scores(j, q_sets, k_sets, kmags):
    pairs = []
    for q, k in zip(q_sets, k_sets):
        qa = _half_masks(q[:, 0:LANES])[j]
        qb = _half_masks(q[:, LANES:])[j]
        pairs.append((k[...], jnp.concatenate([qa, qb], axis=0)))
    return _scores(pairs, kmags)


def _store_pair(y_ref, rows, blk, ot):
    y_ref[rows, blk * LANES:(blk + 1) * LANES] = ot.T.astype(BF16)


N_GROUPS = MLA_HEADS + 4


def _store_groups(y_ref, rows, o, tq):
    for p in range(MLA_HEADS // 2):
        _store_pair(y_ref, rows, p, jnp.concatenate([o[2 * p][:HEAD_DIM], o[2 * p + 1][:HEAD_DIM]], axis=0))
    for base, (o0, o1) in ((2, o[4:6]), (4, o[6:8])):
        _store_pair(y_ref, rows, base, _merge_rows(o0[:, :tq], o1[:, :tq]))
        _store_pair(y_ref, rows, base + 1, _merge_rows(o0[:, tq:], o1[:, tq:]))


def _attn_ctx_body(l, nb, t, sink_ref, qm_ref, km_ref, vm_ref, qc_ref, kc_ref, vc_ref, qd_ref, kd_ref, vd_ref,
                   y_ref):
    def stage1(g):
        b, k = divmod(g, N_GROUPS)
        rs = pl.ds(b * t, t)
        if k < MLA_HEADS:
            return _mla_scores(k, [qm_ref.at[rs]], [km_ref.at[rs]], None)
        q_ref, k_ref = (qc_ref, kc_ref) if k < MLA_HEADS + 2 else (qd_ref, kd_ref)
        return _gqa_scores(k % 2, [q_ref.at[rs]], [k_ref.at[rs]], None)

    def stage2(g, scored):
        s, _ = scored
        b, k = divmod(g, N_GROUPS)
        rs = slice(b * t, (b + 1) * t)
        j = k % 2
        hs = slice(j * HEAD_DIM, (j + 1) * HEAD_DIM)
        if k < MLA_HEADS:
            return _softmax_pv_t(s, [vm_ref[k * HEAD_DIM:(k + 1) * HEAD_DIM, rs]], True)[0]
        if k < MLA_HEADS + 2:
            return _softmax_pv_t(s, [vc_ref[hs, rs]], j == 0)[0]
        return _softmax_pv_t(s, [vd_ref[hs, rs]], j == 0, _sink_row(sink_ref, l, 2 * j, 2 * j + 1, t))[0]

    o = _run_groups(stage1, stage2, N_GROUPS * nb)
    for b in range(nb):
        _store_groups(y_ref, slice(b * t, (b + 1) * t), o[N_GROUPS * b:N_GROUPS * (b + 1)], t)


def _attn_ctx(l, t, sink, qm, km, vm, qc, kc, vc, qd, kd, vd):
    rows = qm.shape[0]
    nb = 8
    r = nb * t
    row = lambda a: (pl.BlockSpec((a.shape[0], r), lambda i: (0, i)) if a.shape[1] == rows
                     else pl.BlockSpec((r, a.shape[1]), lambda i: (i, 0)))
    ins = [qm, km, vm, qc, kc, vc, qd, kd, vd]
    return pl.pallas_call(
        functools.partial(_attn_ctx_body, l, nb, t),
        grid=(rows // r,),
        in_specs=[pl.BlockSpec(memory_space=pltpu.SMEM)] + [row(a) for a in ins],
        out_specs=pl.BlockSpec((r, 6 * LANES), lambda i: (i, 0)),
        out_shape=jax.ShapeDtypeStruct((rows, 6 * LANES), BF16),
        compiler_params=pltpu.CompilerParams(
            dimension_semantics=("arbitrary",), vmem_limit_bytes=VMEM_PLAN["attn_ctx"]),
        name="attn_ctx",
    )(sink, *ins)


def _window_block(i, tq, t, nwb):
    return jnp.clip(i * (tq // LANES) - WINDOW // LANES, 0, t // LANES - nwb)


def _attn_lat_body(l, tq, t, nwb, sink_ref, qmr_ref, qm_ref, km_ref, vm_ref, kx_ref, vx_ref,
                   qcr_ref, qc_ref, kc_ref, vc_ref, kcx_ref, vcx_ref,
                   qdr_ref, qd_ref, kdx_ref, vdx_ref, *rest):
    kd_refs, vd_refs = rest[:nwb], rest[nwb:2 * nwb]
    y_ref, kmag_scr = rest[2 * nwb:]
    i = pl.program_id(1)
    wk = nwb * LANES
    kpos = _window_block(i, tq, t, nwb) * LANES + lax.broadcasted_iota(jnp.int32, (wk, 2 * tq), 0)
    qpos = i * tq + lax.broadcasted_iota(jnp.int32, (wk, 2 * tq), 1) % tq
    valid = jnp.abs(kpos - qpos) <= WINDOW


    @pl.when(i == 0)
    def _():
        for h in range(MLA_HEADS):
            c = slice(h * LANES, (h + 1) * LANES)
            kmag_scr[h:h + 1, :] = _key_mags(km_ref[:, c])
            kmag_scr[MLA_HEADS + h:MLA_HEADS + h + 1, :] = _key_mags(kx_ref[:, c])
        kmag_scr[8:9, :] = _key_mags(kc_ref[...])
        kmag_scr[9:10, :] = _key_mags(kcx_ref[...])
        kmag_scr[10:11, :] = _key_mags(kdx_ref[...])
        kmag_scr[11:, :] = jnp.zeros((kmag_scr.shape[0] - 11, LANES), F32)

    def run(bounded):
        dens = []
        row = lambda r: kmag_scr[r:r + 1, :]

        def stage1(g):
            if g < MLA_HEADS:
                return _mla_scores(g, [qmr_ref, qm_ref], [km_ref, kx_ref],
                                   [row(g), row(MLA_HEADS + g)] if bounded else None)
            if g < MLA_HEADS + 2:
                return _gqa_scores(g % 2, [qcr_ref, qc_ref], [kc_ref, kcx_ref],
                                   [row(8), row(9)] if bounded else None)
            kw = jnp.concatenate([r[...] for r in kd_refs], axis=0)
            (s1, s2), bound = _gqa_scores(g % 2, [qdr_ref, qd_ref], [kw, kdx_ref],
                                          [_key_mags(kw), row(10)] if bounded else None)
            return [jnp.where(valid, s1, NEG_INF), s2], bound

        def stage2(g, scored):
            s, bound = scored
            j = g % 2
            hs = slice(j * HEAD_DIM, (j + 1) * HEAD_DIM)
            if g < MLA_HEADS:
                hs = slice(g * HEAD_DIM, (g + 1) * HEAD_DIM)
                o, den = _softmax_pv_t(s, [vm_ref[hs, :], vx_ref[hs, :]], True, None, bound)
            elif g < MLA_HEADS + 2:
                o, den = _softmax_pv_t(s, [vc_ref[hs, :], vcx_ref[hs, :]], j == 0, None, bound)
            else:
                vw = jnp.concatenate([r[hs, :] for r in vd_refs], axis=1)
                o, den = _softmax_pv_t(s, [vw, vdx_ref[hs, :]], j == 0,
                                       _sink_row(sink_ref, l, 2 * j, 2 * j + 1, tq), bound)
            dens.append(den)
            return o

        _store_groups(y_ref, slice(None), _run_groups(stage1, stage2, N_GROUPS), tq)
        return dens

    @pl.when(_underflowed(run(True)))
    def _():
        run(False)


def _attn_lat(l, nbatch, t, sink, qmr, qm, km, vmt, kx, vxt, qcr, qc, kc, vct, kcx, vcxt, qdr, qd, kd, vdt,
              kdx, vdxt):
    tq = 256
    nq = t // tq
    nwb = (tq + 2 * WINDOW) // LANES
    nkb = t // LANES
    rows = qm.shape[0]
    qs = lambda a: pl.BlockSpec((tq, a.shape[1]), lambda b, i: (b * nq + i, 0))
    ks = lambda a: pl.BlockSpec((t, a.shape[1]), lambda b, i: (b, 0))
    kts = lambda a: pl.BlockSpec((a.shape[0], t), lambda b, i: (0, b))
    xs = lambda a: pl.BlockSpec((None, None) + a.shape[2:], lambda b, i: (l, b, 0, 0))
    cs = lambda a: pl.BlockSpec((None, None) + a.shape[2:], lambda b, i: (b, l, 0, 0))
    wblk = lambda b, i, k: b * nkb + _window_block(i, tq, t, nwb) + k
    kd_specs = [pl.BlockSpec((LANES, LANES), lambda b, i, k=k: (wblk(b, i, k), 0)) for k in range(nwb)]
    vd_specs = [pl.BlockSpec((LANES, LANES), lambda b, i, k=k: (0, wblk(b, i, k))) for k in range(nwb)]
    in_specs = [pl.BlockSpec(memory_space=pltpu.SMEM),
                qs(qmr), qs(qm), ks(km), kts(vmt), xs(kx), xs(vxt),
                qs(qcr), qs(qc), ks(kc), kts(vct), cs(kcx), cs(vcxt),
                qs(qdr), qs(qd), cs(kdx), cs(vdxt)] + kd_specs + vd_specs
    return pl.pallas_call(
        functools.partial(_attn_lat_body, l, tq, t, nwb),
        grid=(nbatch, nq),
        in_specs=in_specs,
        out_specs=pl.BlockSpec((tq, 6 * LANES), lambda b, i: (b * nq + i, 0)),
        out_shape=jax.ShapeDtypeStruct((rows, 6 * LANES), BF16),
        scratch_shapes=[pltpu.VMEM((16, LANES), F32)],
        compiler_params=pltpu.CompilerParams(
            dimension_semantics=("arbitrary", "arbitrary"), vmem_limit_bytes=VMEM_PLAN["attn_lat"]),
        name="attn_lat",
    )(sink, qmr, qm, km, vmt, kx, vxt, qcr, qc, kc, vct, kcx, vcxt, qdr, qd, kdx, vdxt,
      *([kd] * nwb), *([vdt] * nwb))


def _post_body(seq_len, final, mod_ref, n2_ref, cwa_ref, wo_ref, cwf_ref, wup_ref, wdn_ref, fn_ref, *rest):
    o_ref, hbuf, gbuf, pbuf, abuf = rest[-5:]
    ins = rest[:-5]
    r = o_ref.shape[0]
    edges = seq_len > r
    assert edges or r % seq_len == 0
    if edges:
        cat = lambda k: jnp.concatenate([ins[3 * k][...], ins[3 * k + 1][...], ins[3 * k + 2][...]], axis=0)
    else:
        cat = lambda k: ins[k][...]
    halo = BF16_ROWS if edges else 0
    n = r + 2 * halo
    grp = r // 8
    nchunk = D_MODEL // LANES
    i = pl.program_id(0)

    zg = cat(1).astype(F32)
    zz, gb = zg[:, :GROUP_WIDTH], zg[:, GROUP_WIDTH:]
    pos = (i * r - halo + lax.broadcasted_iota(jnp.int32, (n, 1), 0)) % seq_len
    prev = jnp.where(pos == 0, 0.0, pltpu.roll(zz, 1, 0))
    nxt = jnp.where(pos == seq_len - 1, 0.0, pltpu.roll(zz, n - 1, 0))
    cwa = cwa_ref[...]
    ya = gb * (prev * cwa[0:1] + zz * cwa[1:2] + nxt * cwa[2:3])
    ycat = jnp.concatenate([ya.astype(BF16), cat(2)], axis=1)
    x1 = cat(0) + mod_ref[2:3, :] * _dot(ycat, wo_ref[...])
    h2 = _rms(x1, n2_ref[...], D_MODEL) * (1.0 + mod_ref[4:5, :]) + mod_ref[3:4, :]
    x1c = x1[halo:halo + r]

    for c in range(nchunk):
        for s in range(8):
            pbuf[c, pl.ds(s, grp, stride=8), :] = h2[halo + s * grp:halo + (s + 1) * grp, c * LANES:(c + 1) * LANES]
    if edges:
        hrow = lax.broadcasted_iota(jnp.int32, (halo, 1), 0)
        edge = jnp.where(hrow == 0, h2[halo - 1:halo], jnp.where(hrow == 1, h2[halo + r:halo + r + 1], 0.0))
        hbuf[0:halo, :] = edge.astype(BF16)
    hbuf[halo:, :] = jnp.concatenate([pbuf[c] for c in range(nchunk)], axis=1).astype(BF16)

    sub = lax.broadcasted_iota(jnp.int32, (8, 1), 0)
    seq_first = (i * r + sub * grp) % seq_len == 0
    seq_last = (i * r + sub * grp + grp - 1) % seq_len == seq_len - 1

    def up(j):
        ga = slice(j * FF_TILE, (j + 1) * FF_TILE)
        va = slice(D_FF + j * FF_TILE, D_FF + (j + 1) * FF_TILE)
        return _dot(hbuf[...], wup_ref[:, ga]), _dot(hbuf[...], wup_ref[:, va])

    def conv(u, cw):
        head = pltpu.roll(u[halo + r - 8:halo + r], 1, 0)
        tail = pltpu.roll(u[halo:halo + 8], 7, 0)
        if edges:
            head = jnp.where(sub == 0, u[0:8], head)
            tail = jnp.where(sub == 7, pltpu.roll(u[0:8], 6, 0), tail)
        prev = jnp.concatenate([jnp.where(seq_first, 0.0, head), u[halo:halo + r - 8]], axis=0)
        nxt = jnp.concatenate([u[halo + 8:halo + r], jnp.where(seq_last, 0.0, tail)], axis=0)
        return prev * cw[0:1] + u[halo:halo + r] * cw[1:2] + nxt * cw[2:3]

    nxt_u = up(0)
    for j in range(N_FF_TILES):
        ua, ub = nxt_u
        if j + 1 < N_FF_TILES:
            nxt_u = up(j + 1)
        ga = slice(j * FF_TILE, (j + 1) * FF_TILE)
        va = slice(D_FF + j * FF_TILE, D_FF + (j + 1) * FF_TILE)
        ua = conv(ua, cwf_ref[:, ga])
        ub = conv(ub, cwf_ref[:, va])
        gbuf[:, ga] = (ua * jax.nn.sigmoid(ua) * ub).astype(BF16)
    acc = _dot(gbuf[...], wdn_ref[...])

    for c in range(nchunk):
        abuf[c] = acc[:, c * LANES:(c + 1) * LANES]
    g2 = mod_ref[5:6, :]
    for c in range(nchunk):
        cs = slice(c * LANES, (c + 1) * LANES)
        for s in range(8):
            rs = slice(s * grp, (s + 1) * grp)
            o_ref[rs, cs] = x1c[rs, cs] + g2[:, cs] * abuf[c, pl.ds(s, grp, stride=8), :]
    if final:
        o_ref[...] = _rms(o_ref[...], fn_ref[...], D_MODEL)


def _halo_specs(r, w, rows):
    nblk = rows // BF16_ROWS
    per = r // BF16_ROWS
    prev = pl.BlockSpec((BF16_ROWS, w), lambda i: (jnp.maximum(i * per - 1, 0), 0))
    cur = pl.BlockSpec((r, w), lambda i: (i, 0))
    nxt = pl.BlockSpec((BF16_ROWS, w), lambda i: (jnp.minimum((i + 1) * per, nblk - 1), 0))
    return prev, cur, nxt


def _cond_spec(l, seq_len, cond_base):
    r = ROW_TILE
    tiles_per_seq = max(seq_len // r, 1)
    step = 1 if seq_len >= r else 0
    return pl.BlockSpec((None, None, 6, D_MODEL),
                        lambda i: (l, cond_base + step * (i // tiles_per_seq), 0, 0))


def _post(x, mod, lw, l, seq_len, cond_base, zg, y, final_norm, final):
    rows = x.shape[0]
    r = ROW_TILE
    full = lambda a: pl.BlockSpec((None,) + a.shape[1:], lambda i: (l,) + (0,) * (a.ndim - 1),
                                  pipeline_mode=pl.Buffered(1))
    acts = [x, zg, y]
    edges = seq_len > r
    if edges:
        act_specs = [s for a in acts for s in _halo_specs(r, a.shape[1], rows)]
        act_args = [a for a in acts for _ in range(3)]
    else:
        act_specs = [pl.BlockSpec((r, a.shape[1]), lambda i: (i, 0)) for a in acts]
        act_args = acts
    return pl.pallas_call(
        functools.partial(_post_body, seq_len, final),
        grid=(rows // r,),
        in_specs=[_cond_spec(l, seq_len, cond_base), full(lw['norm2']), full(lw['conv_a']), full(lw['w_out']),
                  full(lw['conv_ff']), full(lw['w_up']), full(lw['w_down']),
                  pl.BlockSpec((1, D_MODEL), lambda i: (0, 0))] + act_specs,
        out_specs=pl.BlockSpec((r, D_MODEL), lambda i: (i, 0)),
        out_shape=jax.ShapeDtypeStruct((rows, D_MODEL), F32),
        scratch_shapes=[pltpu.VMEM((r + (BF16_ROWS if edges else 0), D_MODEL), BF16), pltpu.VMEM((r, D_FF), BF16),
                        pltpu.VMEM((D_MODEL // LANES, r, LANES), F32),
                        pltpu.VMEM((D_MODEL // LANES, r, LANES), F32)],
        compiler_params=pltpu.CompilerParams(
            dimension_semantics=("arbitrary",), vmem_limit_bytes=VMEM_PLAN["post"]),
        name="post",
    )(mod, lw['norm2'], lw['conv_a'], lw['w_out'], lw['conv_ff'], lw['w_up'], lw['w_down'], final_norm,
      *act_args)


def _pad_cols(a, n):
    return jnp.pad(a, [(0, 0)] * (a.ndim - 1) + [(0, n - a.shape[-1])])


def _perm_heads(a, axis):
    h = [lax.slice_in_dim(a, k * HEAD_DIM, (k + 1) * HEAD_DIM, axis=axis) for k in range(4)]
    return jnp.concatenate([h[0], h[2], h[1], h[3]], axis=axis)


def _prep_weights(w_in, mla_wq_b, mla_wkv_b, w_out, w_up, conv_ff, w_down, norm1, norm2, conv_a,
                  mla_q_norm, mla_kv_norm, gqa_q_norm, gqa_k_norm):
    L = DEPTH
    a = w_in[..., 0:768]
    cq_kpe = _pad_cols(jnp.concatenate([w_in[..., 768:960], w_in[..., 1088:1120]], axis=-1), 256)
    ckv = w_in[..., 960:1088]
    qc = _perm_heads(w_in[..., 1120:1376], 2)
    kvc = w_in[..., 1376:1632]
    qd = _perm_heads(w_in[..., 1632:1888], 2)
    kvd = w_in[..., 1888:2144]
    w_in_p = jnp.concatenate([a, cq_kpe, ckv, qc, kvc, qd, kvd], axis=-1).astype(BF16)
    assert w_in_p.shape[-1] == IN_COLS_PACKED

    wq = mla_wq_b.reshape(L, MLA_Q_LORA, MLA_HEADS, MLA_NOPE + MLA_ROPE)
    wq = jnp.concatenate([wq[..., MLA_NOPE:], wq[..., :MLA_NOPE],
                          jnp.zeros((L, MLA_Q_LORA, MLA_HEADS, LANES - MLA_NOPE - MLA_ROPE), F32)], axis=-1)
    wq = jnp.pad(wq.reshape(L, MLA_Q_LORA, MLA_HEADS * LANES), ((0, 0), (0, 256 - MLA_Q_LORA), (0, 0)))

    wkv = mla_wkv_b.reshape(L, MLA_KV_LORA, MLA_HEADS, MLA_NOPE + MLA_V)
    zk = jnp.zeros((L, MLA_KV_LORA, MLA_HEADS, MLA_ROPE), F32)
    wk = jnp.concatenate([zk, wkv[..., :MLA_NOPE], zk], axis=-1).reshape(L, MLA_KV_LORA, MLA_HEADS * LANES)
    wv = wkv[..., MLA_NOPE:].reshape(L, MLA_KV_LORA, MLA_HEADS * MLA_V)
    wkv_p = jnp.concatenate([wk, wv], axis=-1)

    wo = jnp.concatenate([w_out[:, 0:512], _perm_heads(w_out[:, 512:768], 1),
                          _perm_heads(w_out[:, 768:1024], 1)], axis=1)

    hm = jnp.asarray(np.kron(np.eye(2 * LANES // HEAD_DIM), np.ones((HEAD_DIM, HEAD_DIM))), BF16)
    return {
        'w_in': w_in_p, 'wq': wq.astype(BF16), 'wkv': wkv_p.astype(BF16), 'w_out': wo.astype(BF16),
        'w_up': w_up.astype(BF16), 'conv_ff': conv_ff, 'w_down': w_down.astype(BF16),
        'norm1': norm1.reshape(L, 1, D_MODEL), 'norm2': norm2.reshape(L, 1, D_MODEL), 'conv_a': conv_a,
        'gq': _pad_cols(mla_q_norm, 256).reshape(L, 1, 256), 'gkv': mla_kv_norm.reshape(L, 1, MLA_KV_LORA),
        'gqc': jnp.tile(gqa_q_norm, (1, 4)).reshape(L, 1, 256), 'gkc': jnp.tile(gqa_k_norm, (1, 2)).reshape(L, 1, 128),
        'hm': hm,
    }


def _rope_tables(t):
    rows = t // GRID_W
    row = np.repeat(np.arange(rows, dtype=np.float64), GRID_W)
    col = np.tile(np.arange(GRID_W, dtype=np.float64), rows)

    def tabs(dim):
        half = dim // 2
        inv = np.power(ROPE_THETA, -np.arange(0, half, 2, dtype=np.float64) / half)
        ar = row[:, None] * inv
        ac = col[:, None] * inv
        c = np.concatenate([np.cos(ar), np.cos(ar), np.cos(ac), np.cos(ac)], axis=1)
        s = np.concatenate([-np.sin(ar), np.sin(ar), -np.sin(ac), np.sin(ac)], axis=1)
        return c, s

    c64, s64 = tabs(HEAD_DIM)
    c32, s32 = tabs(MLA_ROPE)
    c64 = np.tile(c64, (1, 2))
    s64 = np.tile(s64, (1, 2))
    c32 = np.concatenate([c32, np.ones((t, LANES - MLA_ROPE))], axis=1)
    s32 = np.concatenate([s32, np.zeros((t, LANES - MLA_ROPE))], axis=1)
    return tuple(jnp.asarray(a.astype(np.float32)) for a in (c64, s64, c32, s32))


def kernel(x_prompt, x_sample, cache_mla_ckv, cache_mla_kpe, cache_gqa_k, cache_gqa_v, cache_swa_k, cache_swa_v,
           c, c_ctx, w_ada, b_ada, norm1, w_in, conv_a, mla_q_norm, mla_wq_b, mla_kv_norm, mla_wkv_b,
           gqa_q_norm, gqa_k_norm, swa_sink, w_out, norm2, w_up, conv_ff, w_down, final_norm):
    B, T, _ = x_prompt.shape
    DB, DT, _ = x_sample.shape
    past = cache_mla_ckv.shape[2]

    lw = _prep_weights(w_in, mla_wq_b, mla_wkv_b, w_out, w_up, conv_ff, w_down, norm1, norm2, conv_a,
                       mla_q_norm, mla_kv_norm, gqa_q_norm, gqa_k_norm)
    rope_tabs = _rope_tables(DT)
    fnorm = final_norm.reshape(1, D_MODEL)

    cond_t = jnp.concatenate([c_ctx[:, None], c.T, jnp.zeros((D_MODEL, 8 - 1 - DB), F32)], axis=1)
    mod = _ada(cond_t, 1 + DB, w_ada, b_ada).reshape(DEPTH, 8, 6, D_MODEL)

    kx, vx = _ctx_mla(cache_mla_ckv, _pad_cols(cache_mla_kpe, LANES), lw['wkv'])
    flat = lambda a: a.reshape(DB, DEPTH, past, 2 * HEAD_DIM).astype(BF16)
    kcx, kdx = flat(cache_gqa_k), flat(cache_swa_k)
    vcx, vdx = jnp.swapaxes(flat(cache_gqa_v), 2, 3), jnp.swapaxes(flat(cache_swa_v), 2, 3)

    xp = x_prompt.reshape(B * T, D_MODEL)
    xs = x_sample.reshape(DB * DT, D_MODEL)
    caches = None
    for l in range(DEPTH):
        final = l == DEPTH - 1
        (zg, qm, km, vm, qc, kc, vc, qd, kd, vd, *caches) = _inproj(xp, mod, lw, l, T, 0, None, caches)
        y = _attn_ctx(l, T, swa_sink, qm, km, vm, qc, kc, vc, qd, kd, vd)
        xp = _post(xp, mod, lw, l, T, 0, zg, y, fnorm, final)
        (zg, qmr, qm, km, vm, qcr, qc, kc, vc, qdr, qd, kd, vd) = _inproj(xs, mod, lw, l, DT, 1, rope_tabs)
        y = _attn_lat(l, DB, DT, swa_sink, qmr, qm, km, vm, kx, vx, qcr, qc, kc, vc, kcx, vcx,
                      qdr, qd, kd, vd, kdx, vdx)
        xs = _post(xs, mod, lw, l, DT, 1, zg, y, fnorm, final)

    heads = lambda a: a.reshape(B, DEPTH, T, 2, HEAD_DIM)
    return (xp.reshape(B, T, D_MODEL), xs.reshape(DB, DT, D_MODEL), caches[0], caches[1],
            heads(caches[2]), heads(caches[3]), heads(caches[4]), heads(caches[5]))
```
